```python
import math
import jax
import jax.numpy as jnp
from jax import lax
import numpy as np

D_MODEL = 1024
BATCH = 1
SEQ = 16384
DEPTH = 1

CTX_LEN = 256
GRID_W = 64
HEAD_DIM = 64
ATTN_HEADS = 8
ATTN_KV_HEADS = 2
GQA_GROUP = ATTN_HEADS // ATTN_KV_HEADS
RET_HEADS = 4
RET_QK_DIM = 64
RET_V_DIM = 128
ATTN_WIDTH = ATTN_HEADS * HEAD_DIM
RET_WIDTH = RET_HEADS * RET_V_DIM
MIX_WIDTH = ATTN_WIDTH + RET_WIDTH
PROJ_SIZES = (ATTN_WIDTH, ATTN_KV_HEADS * HEAD_DIM, ATTN_KV_HEADS * HEAD_DIM,
              RET_HEADS * RET_QK_DIM, RET_HEADS * RET_QK_DIM, RET_WIDTH, RET_WIDTH)
PROJ_DIM = sum(PROJ_SIZES)
PROJ_SPLITS = tuple(int(v) for v in np.cumsum(PROJ_SIZES)[:-1])
Q_BLOCK = 128
RET_CHUNK = 128
ROPE_THETA = 10000.0
ROPE_AXIS_DIM = HEAD_DIM // 2
N_GROUPS = 4
EXPERTS_PER_GROUP = 8
N_EXPERTS = N_GROUPS * EXPERTS_PER_GROUP
TOP_K_IN_GROUP = 2
D_EXPERT = 512
MOE_BLOCK = 128
EPS = 1e-6

kernel_name = "hybrid_attn_retention_hmoe_dit"


def _rmsnorm(x, g):
    xf = x.astype(jnp.float32)
    xf = xf * lax.rsqrt(jnp.mean(xf * xf, axis=-1, keepdims=True) + EPS)
    return xf.astype(x.dtype) * g


def _modulate(xn, shift, scale):
    return xn * (1.0 + scale) + shift


def _axial_rope_tables(rows):
    row = jnp.repeat(jnp.arange(rows), GRID_W).astype(jnp.float32)
    col = jnp.tile(jnp.arange(GRID_W), rows).astype(jnp.float32)
    freqs = ROPE_THETA ** (-jnp.arange(0, ROPE_AXIS_DIM, 2, dtype=jnp.float32) / ROPE_AXIS_DIM)
    ang = jnp.concatenate([row[:, None] * freqs, col[:, None] * freqs], axis=-1)
    return jnp.cos(ang), jnp.sin(ang)


def _rope(x, cos, sin):
    B, L, H, d = x.shape
    xp = x.astype(jnp.float32).reshape(B, L, H, d // 2, 2)
    x0, x1 = xp[..., 0], xp[..., 1]
    c = cos[None, :, None, :]
    s = sin[None, :, None, :]
    out = jnp.stack([x0 * c - x1 * s, x0 * s + x1 * c], axis=-1).reshape(B, L, H, d)
    return out.astype(x.dtype)


def _attend(q, k, v):
    B, L, _, _ = q.shape
    nb = L // Q_BLOCK
    qb = jnp.moveaxis(q.reshape(B, nb, Q_BLOCK, ATTN_KV_HEADS, GQA_GROUP, HEAD_DIM), 1, 0)
    scale = HEAD_DIM ** -0.5

    def one_block(q_blk):
        s = jnp.einsum('bqhgd,bkhd->bhgqk', q_blk, k).astype(jnp.float32) * scale
        p = jax.nn.softmax(s, axis=-1)
        return jnp.einsum('bhgqk,bkhd->bqhgd', p.astype(v.dtype), v)

    o = lax.map(one_block, qb)
    return jnp.moveaxis(o, 0, 1).reshape(B, L, ATTN_WIDTH)


def _head_rms(x, g):
    xf = x.astype(jnp.float32)
    xf = xf * lax.rsqrt(jnp.mean(xf * xf, axis=-1, keepdims=True) + EPS)
    return xf.astype(x.dtype) * g


def _ret_heads(q, k, v, cos, sin):
    B, L, _ = q.shape
    q = q.reshape(B, L, RET_HEADS, RET_QK_DIM)
    k = k.reshape(B, L, RET_HEADS, RET_QK_DIM)
    if cos is not None:
        q = _rope(q, cos, sin)
        k = _rope(k, cos, sin)
    q = jnp.transpose(q, (0, 2, 1, 3)).astype(jnp.float32)
    k = jnp.transpose(k, (0, 2, 1, 3)).astype(jnp.float32) * (RET_QK_DIM ** -0.5)
    v = jnp.transpose(v.reshape(B, L, RET_HEADS, RET_V_DIM), (0, 2, 1, 3)).astype(jnp.float32)
    return q, k, v


def _retention_state(k, v, log_g, reverse):
    Lc = k.shape[2]
    m = jnp.arange(Lc, dtype=jnp.float32)
    dist = m if reverse else (Lc - 1 - m)
    w = jnp.exp(log_g[:, None] * dist[None, :])
    return jnp.einsum('bhmk,bhmv,hm->bhkv', k, v, w)


def _retention_chunked(q, k, v, log_g, s0, include_diag):
    B, H, L, dk = q.shape
    dv = v.shape[-1]
    C = RET_CHUNK
    n = L // C
    qc = q.reshape(B, H, n, C, dk)
    kc = k.reshape(B, H, n, C, dk)
    vc = v.reshape(B, H, n, C, dv)
    pos = jnp.arange(C, dtype=jnp.float32)
    diff = pos[:, None] - pos[None, :]
    mask = (diff >= 0) if include_diag else (diff > 0)
    dmat = jnp.where(mask, jnp.exp(log_g[:, None, None] * jnp.where(mask, diff, 0.0)[None]), 0.0)
    scores = jnp.einsum('bhnik,bhnjk->bhnij', qc, kc) * dmat[None, :, None]
    intra = jnp.einsum('bhnij,bhnjv->bhniv', scores, vc)
    zeta = jnp.exp(log_g[:, None] * (C - 1 - pos)[None, :])
    u = jnp.einsum('bhnjk,bhnjv,hj->bhnkv', kc, vc, zeta)
    g_chunk = jnp.exp(log_g * C)[None, :, None, None]

    def step(s, u_n):
        return g_chunk * s + u_n, s

    s_final, s_prev = lax.scan(step, s0, jnp.moveaxis(u, 2, 0))
    xi = jnp.exp(log_g[:, None] * (pos + 1.0)[None, :])
    cross = jnp.einsum('bhnik,nbhkv,hi->bhniv', qc, s_prev, xi)
    return (intra + cross).reshape(B, H, L, dv), s_final


def _retention_latent(q, k, v, k_c, v_c, log_gf, log_gb):
    s_f = _retention_state(k_c, v_c, log_gf, reverse=False)
    s_b = _retention_state(k_c, v_c, log_gb, reverse=True)
    o_f, _ = _retention_chunked(q, k, v, log_gf, s_f, True)
    o_b, _ = _retention_chunked(jnp.flip(q, 2), jnp.flip(k, 2), jnp.flip(v, 2), log_gb, s_b, False)
    return o_f + jnp.flip(o_b, 2)


def _retention_context(q, k, v, log_gf, log_gb):
    B, H, _, dk = q.shape
    s0 = jnp.zeros((B, H, dk, v.shape[-1]), jnp.float32)
    o_f, _ = _retention_chunked(q, k, v, log_gf, s0, True)
    o_b, _ = _retention_chunked(jnp.flip(q, 2), jnp.flip(k, 2), jnp.flip(v, 2), log_gb, s0, False)
    return o_f + jnp.flip(o_b, 2)


def _ret_output(o, gn_g, gn_b, gate, dtype):
    B, H, L, dv = o.shape
    mu = jnp.mean(o, axis=-1, keepdims=True)
    var = jnp.mean(jnp.square(o - mu), axis=-1, keepdims=True)
    on = (o - mu) * lax.rsqrt(var + EPS)
    on = jnp.transpose(on, (0, 2, 1, 3)).reshape(B, L, H * dv).astype(dtype)
    return (on * gn_g + gn_b) * jax.nn.silu(gate)


def _hier_moe(h, w_grp, b_grp, w_exp, b_exp, w_gate, w_up, w_down):
    T, D = h.shape
    grp_p = jax.nn.softmax((h @ w_grp).astype(jnp.float32) + b_grp, axis=-1)
    p_g, g_idx = lax.top_k(grp_p, 1)
    exp_logits = ((h @ w_exp).astype(jnp.float32) + b_exp).reshape(T, N_GROUPS, EXPERTS_PER_GROUP)
    sel = jnp.take_along_axis(exp_logits, jnp.broadcast_to(g_idx[:, :, None], (T, 1, EXPERTS_PER_GROUP)), axis=1)[:, 0]
    p_e, j_idx = lax.top_k(jax.nn.softmax(sel, axis=-1), TOP_K_IN_GROUP)
    weights = p_g * (p_e / jnp.sum(p_e, axis=-1, keepdims=True))
    eid = (g_idx * EXPERTS_PER_GROUP + j_idx).astype(jnp.int32)

    A = T * TOP_K_IN_GROUP
    eid_f = eid.reshape(A)
    tok_f = jnp.repeat(jnp.arange(T, dtype=jnp.int32), TOP_K_IN_GROUP)
    w_f = weights.reshape(A)
    order = jnp.argsort(eid_f)
    e_s, tok_s, w_s = eid_f[order], tok_f[order], w_f[order]
    counts = jnp.zeros((N_EXPERTS,), jnp.int32).at[eid_f].add(1)
    starts = jnp.cumsum(counts) - counts
    pcounts = (counts + MOE_BLOCK - 1) // MOE_BLOCK * MOE_BLOCK
    pends = jnp.cumsum(pcounts)
    pstarts = pends - pcounts
    dest = pstarts[e_s] + (jnp.arange(A, dtype=jnp.int32) - starts[e_s])
    P = A + N_EXPERTS * MOE_BLOCK
    nblk = P // MOE_BLOCK
    buf_tok = jnp.full((P,), T, jnp.int32).at[dest].set(tok_s)
    buf_w = jnp.zeros((P,), jnp.float32).at[dest].set(w_s).astype(h.dtype)
    blk_start = jnp.arange(nblk, dtype=jnp.int32) * MOE_BLOCK
    blk_e = jnp.minimum(jnp.sum(blk_start[:, None] >= pends[None, :], axis=1), N_EXPERTS - 1)
    h_pad = jnp.concatenate([h, jnp.zeros((1, D), h.dtype)], axis=0)
    xb = h_pad[buf_tok].reshape(nblk, MOE_BLOCK, D)

    def expert_block(args):
        x_blk, e = args
        return (jax.nn.silu(x_blk @ w_gate[e]) * (x_blk @ w_up[e])) @ w_down[e]

    yb = lax.map(expert_block, (xb, blk_e)).reshape(P, D)
    out = jnp.zeros((T + 1, D), h.dtype).at[buf_tok].add(yb * buf_w[:, None])
    return out[:T]


def setup_inputs(seed: int = 0) -> dict:
    key = jax.random.key(seed)
    ks = jax.random.split(key, 32)
    f32 = jnp.float32

    def nrm(k, shape, scale):
        return jax.random.normal(k, shape, f32) * scale

    base_logit = jnp.asarray(np.log(2.0 ** (5 + np.arange(RET_HEADS)) - 1.0).astype(np.float32))
    return {
        "x": nrm(ks[0], (BATCH, SEQ, D_MODEL), 1.0),
        "c": nrm(ks[1], (BATCH, D_MODEL), 1.0),
        "ctx": nrm(ks[2], (BATCH, CTX_LEN, D_MODEL), 1.0),
        "c_ctx": nrm(ks[3], (D_MODEL,), 1.0),
        "w_ada": nrm(ks[4], (DEPTH, D_MODEL, 6 * D_MODEL), 0.5 * D_MODEL ** -0.5),
        "b_ada": nrm(ks[5], (DEPTH, 6 * D_MODEL), 0.01),
        "norm1_g": 1.0 + nrm(ks[6], (DEPTH, D_MODEL), 0.05),
        "w_in": nrm(ks[7], (DEPTH, D_MODEL, PROJ_DIM), D_MODEL ** -0.5),
        "attn_q_norm": 1.0 + nrm(ks[8], (DEPTH, HEAD_DIM), 0.05),
        "attn_k_norm": 1.0 + nrm(ks[9], (DEPTH, HEAD_DIM), 0.05),
        "ret_decay_fwd": base_logit + nrm(ks[10], (DEPTH, RET_HEADS), 0.1),
        "ret_decay_bwd": base_logit + nrm(ks[11], (DEPTH, RET_HEADS), 0.1),
        "ret_gn_g": 1.0 + nrm(ks[12], (DEPTH, RET_WIDTH), 0.05),
        "ret_gn_b": nrm(ks[13], (DEPTH, RET_WIDTH), 0.01),
        "w_out": nrm(ks[14], (DEPTH, MIX_WIDTH, D_MODEL), MIX_WIDTH ** -0.5),
        "norm2_g": 1.0 + nrm(ks[15], (DEPTH, D_MODEL), 0.05),
        "moe_w_grp": nrm(ks[16], (DEPTH, D_MODEL, N_GROUPS), D_MODEL ** -0.5),
        "moe_b_grp": nrm(ks[17], (DEPTH, N_GROUPS), 0.01),
        "moe_w_exp": nrm(ks[18], (DEPTH, D_MODEL, N_EXPERTS), D_MODEL ** -0.5),
        "moe_b_exp": nrm(ks[19], (DEPTH, N_EXPERTS), 0.01),
        "moe_w_gate": nrm(ks[20], (DEPTH, N_EXPERTS, D_MODEL, D_EXPERT), D_MODEL ** -0.5),
        "moe_w_up": nrm(ks[21], (DEPTH, N_EXPERTS, D_MODEL, D_EXPERT), D_MODEL ** -0.5),
        "moe_w_down": nrm(ks[22], (DEPTH, N_EXPERTS, D_EXPERT, D_MODEL), D_EXPERT ** -0.5),
        "final_norm_g": 1.0 + nrm(ks[23], (D_MODEL,), 0.05),
    }


def reference(x, c, ctx, c_ctx, w_ada, b_ada, norm1_g, w_in, attn_q_norm, attn_k_norm,
              ret_decay_fwd, ret_decay_bwd, ret_gn_g, ret_gn_b, w_out, norm2_g,
              moe_w_grp, moe_b_grp, moe_w_exp, moe_b_exp, moe_w_gate, moe_w_up, moe_w_down,
              final_norm_g):
    B, L, D = x.shape
    CL = ctx.shape[1]
    ROWS = L // GRID_W
    cos, sin = _axial_rope_tables(ROWS)

    for l in range(DEPTH):
        mod_x = (jax.nn.silu(c) @ w_ada[l] + b_ada[l])[:, None, :]
        mod_c = (jax.nn.silu(c_ctx) @ w_ada[l] + b_ada[l])[None, None, :]
        sh1, sc1, gt1, sh2, sc2, gt2 = jnp.split(mod_x, 6, axis=-1)
        csh1, csc1, cgt1, csh2, csc2, cgt2 = jnp.split(mod_c, 6, axis=-1)
        log_gf = jax.nn.log_sigmoid(ret_decay_fwd[l].astype(jnp.float32))
        log_gb = jax.nn.log_sigmoid(ret_decay_bwd[l].astype(jnp.float32))

        hx = _modulate(_rmsnorm(x, norm1_g[l]), sh1, sc1)
        hc = _modulate(_rmsnorm(ctx, norm1_g[l]), csh1, csc1)
        qa, ka, va, qr, kr, vr, gr = jnp.split(hx @ w_in[l], PROJ_SPLITS, axis=-1)
        qa_c, ka_c, va_c, qr_c, kr_c, vr_c, gr_c = jnp.split(hc @ w_in[l], PROJ_SPLITS, axis=-1)

        qa = _rope(_head_rms(qa.reshape(B, L, ATTN_HEADS, HEAD_DIM), attn_q_norm[l]), cos, sin)
        ka = _rope(_head_rms(ka.reshape(B, L, ATTN_KV_HEADS, HEAD_DIM), attn_k_norm[l]), cos, sin)
        va = va.reshape(B, L, ATTN_KV_HEADS, HEAD_DIM)
        ka_c = _head_rms(ka_c.reshape(B, CL, ATTN_KV_HEADS, HEAD_DIM), attn_k_norm[l])
        va_c = va_c.reshape(B, CL, ATTN_KV_HEADS, HEAD_DIM)
        attn_x = _attend(qa, jnp.concatenate([ka_c, ka], axis=1), jnp.concatenate([va_c, va], axis=1))

        q_r, k_r, v_r = _ret_heads(qr, kr, vr, cos, sin)
        q_rc, k_rc, v_rc = _ret_heads(qr_c, kr_c, vr_c, None, None)
        ret_o = _retention_latent(q_r, k_r, v_r, k_rc, v_rc, log_gf, log_gb)
        ret_x = _ret_output(ret_o, ret_gn_g[l], ret_gn_b[l], gr, x.dtype)

        x_new = x + gt1 * (jnp.concatenate([attn_x, ret_x], axis=-1) @ w_out[l])

        h2 = _modulate(_rmsnorm(x_new, norm2_g[l]), sh2, sc2)
        y = _hier_moe(h2.reshape(B * L, D), moe_w_grp[l], moe_b_grp[l], moe_w_exp[l], moe_b_exp[l],
                      moe_w_gate[l], moe_w_up[l], moe_w_down[l]).reshape(B, L, D)
        x_new = x_new + gt2 * y

        if l + 1 < DEPTH:
            qa_c = _head_rms(qa_c.reshape(B, CL, ATTN_HEADS, HEAD_DIM), attn_q_norm[l])
            attn_c = _attend(qa_c, ka_c, va_c)
            ret_c = _ret_output(_retention_context(q_rc, k_rc, v_rc, log_gf, log_gb),
                                ret_gn_g[l], ret_gn_b[l], gr_c, ctx.dtype)
            ctx_new = ctx + cgt1 * (jnp.concatenate([attn_c, ret_c], axis=-1) @ w_out[l])
            h2c = _modulate(_rmsnorm(ctx_new, norm2_g[l]), csh2, csc2)
            yc = _hier_moe(h2c.reshape(B * CL, D), moe_w_grp[l], moe_b_grp[l], moe_w_exp[l], moe_b_exp[l],
                           moe_w_gate[l], moe_w_up[l], moe_w_down[l]).reshape(B, CL, D)
            ctx = ctx_new + cgt2 * yc
        x = x_new

    return _rmsnorm(x, final_norm_g)
```

```python
import functools

import jax
import jax.numpy as jnp
import numpy as np
from jax import lax
from jax.experimental import pallas as pl
from jax.experimental.pallas import tpu as pltpu

F32 = jnp.float32
BF16 = jnp.bfloat16

GRID_W = 64
HEAD_DIM = 64
ATTN_HEADS = 8
ATTN_KV_HEADS = 2
GQA_GROUP = ATTN_HEADS // ATTN_KV_HEADS
RET_HEADS = 4
RET_QK_DIM = 64
RET_V_DIM = 128
RET_CHUNK = 128
ATTN_WIDTH = ATTN_HEADS * HEAD_DIM
KV_WIDTH = ATTN_KV_HEADS * HEAD_DIM
RET_QK_WIDTH = RET_HEADS * RET_QK_DIM
RET_WIDTH = RET_HEADS * RET_V_DIM
PROJ_SIZES = (ATTN_WIDTH, KV_WIDTH, KV_WIDTH, RET_QK_WIDTH, RET_QK_WIDTH, RET_WIDTH, RET_WIDTH)
PROJ_OFFS = tuple(int(v) for v in np.cumsum((0,) + PROJ_SIZES))
PROJ_DIM = PROJ_OFFS[-1]
ROPE_THETA = 10000.0
ROPE_AXIS_DIM = HEAD_DIM // 2
N_GROUPS = 4
EXPERTS_PER_GROUP = 8
N_EXPERTS = N_GROUPS * EXPERTS_PER_GROUP
TOP_K = 2
D_EXPERT = 512
EPS = 1e-6

LANES = 128
SUBLANES = 8
VMEM_LIMIT_BYTES = 56 * 1024 * 1024

ROW_TILE = 512
ATTN_Q_TILE = 256
RET_ROW_TILE = 1024
MOE_BLOCK = 256
ROUTER_ROWS = 128
EXPERT_ROW0 = 8


def _cparams(*sem):
    return pltpu.CompilerParams(dimension_semantics=sem, vmem_limit_bytes=VMEM_LIMIT_BYTES)


def _const_spec(shape):
    nd = len(shape)
    return pl.BlockSpec(shape, lambda *_: (0,) * nd)


def _ada_kernel(s_ref, w_ref, b_ref, o_ref):
    s = s_ref[...]
    s = s / (1.0 + jnp.exp(-s))
    o_ref[...] = jnp.dot(s, w_ref[...], preferred_element_type=F32,
                         precision=lax.Precision.HIGHEST) + b_ref[...]


def _ada(cvecs, w_ada, b_ada):
    d, n = w_ada.shape
    tn = 1536
    return pl.pallas_call(
        _ada_kernel,
        grid=(n // tn,),
        in_specs=[_const_spec((SUBLANES, d)),
                  pl.BlockSpec((d, tn), lambda j: (0, j)),
                  pl.BlockSpec((1, tn), lambda j: (0, j))],
        out_specs=pl.BlockSpec((SUBLANES, tn), lambda j: (0, j)),
        out_shape=jax.ShapeDtypeStruct((SUBLANES, n), F32),
        compiler_params=_cparams("arbitrary"),
        name="ada_mod",
    )(cvecs, w_ada, b_ada.reshape(1, n))


def _rms_modulate(x, g, sh, sc):
    ms = jnp.mean(x * x, axis=-1, keepdims=True)
    return (x * lax.rsqrt(ms + EPS)) * g * (1.0 + sc) + sh


def _head_mean_sq(v, bd):
    sq = v * v
    hi = sq.astype(BF16)
    lo = (sq - hi.astype(F32)).astype(BF16)
    return (jnp.dot(hi, bd, preferred_element_type=F32) + jnp.dot(lo, bd, preferred_element_type=F32))


def _rope_chunks(v, cos, sin, even):
    outs = []
    for c in range(v.shape[1] // LANES):
        xc = v[:, c * LANES:(c + 1) * LANES]
        nxt = pltpu.roll(xc, LANES - 1, 1)
        prv = pltpu.roll(xc, 1, 1)
        outs.append(xc * cos + jnp.where(even, nxt, prv) * sin)
    return outs


def _inproj_kernel(x_ref, sh_ref, sc_ref, g_ref, w_ref, bd_ref, gq_ref, gk_ref, cos_ref, sin_ref,
                   qa_ref, ka_ref, va_ref, qr_ref, kr_ref, vr_ref, gr_ref):
    h = _rms_modulate(x_ref[...], g_ref[...], sh_ref[...], sc_ref[...])
    proj = jnp.dot(h.astype(BF16), w_ref[...], preferred_element_type=F32)
    o = PROJ_OFFS
    qa, ka, va = proj[:, o[0]:o[1]], proj[:, o[1]:o[2]], proj[:, o[2]:o[3]]
    qr, kr, vr, gr = proj[:, o[3]:o[4]], proj[:, o[4]:o[5]], proj[:, o[5]:o[6]], proj[:, o[6]:o[7]]
    cos, sin = cos_ref[...], sin_ref[...]
    even = (lax.broadcasted_iota(jnp.int32, cos.shape, 1) % 2) == 0
    bd = bd_ref[...]
    scale = HEAD_DIM ** -0.5

    qn = qa * lax.rsqrt(_head_mean_sq(qa, bd) + EPS) * gq_ref[...]
    for c, blk in enumerate(_rope_chunks(qn, cos, sin, even)):
        qa_ref[:, c * LANES:(c + 1) * LANES] = (blk * scale).astype(BF16)
    kn = ka * lax.rsqrt(_head_mean_sq(ka, bd[:KV_WIDTH, :KV_WIDTH]) + EPS) * gk_ref[...]
    ka_ref[...] = _rope_chunks(kn, cos, sin, even)[0].astype(BF16)
    va_ref[...] = va.astype(BF16)
    for c, blk in enumerate(_rope_chunks(qr, cos, sin, even)):
        qr_ref[:, c * LANES:(c + 1) * LANES] = blk.astype(BF16)
    rscale = RET_QK_DIM ** -0.5
    for c, blk in enumerate(_rope_chunks(kr, cos, sin, even)):
        kr_ref[:, c * LANES:(c + 1) * LANES] = (blk * rscale).astype(BF16)
    vr_ref[...] = vr.astype(BF16)
    gr_ref[...] = gr


def _inproj(x2, sh, sc, g1, w_bf, bd, gq, gk, cos, sin):
    rows, d = x2.shape
    tm = min(ROW_TILE, rows)
    row = lambda w: pl.BlockSpec((tm, w), lambda i: (i, 0))
    widths = PROJ_SIZES
    dtypes = (BF16,) * 6 + (F32,)
    return pl.pallas_call(
        _inproj_kernel,
        grid=(rows // tm,),
        in_specs=[row(d), _const_spec((1, d)), _const_spec((1, d)), _const_spec((1, d)),
                  _const_spec(w_bf.shape), _const_spec(bd.shape),
                  _const_spec((1, ATTN_WIDTH)), _const_spec((1, KV_WIDTH)),
                  row(LANES), row(LANES)],
        out_specs=[row(w) for w in widths],
        out_shape=[jax.ShapeDtypeStruct((rows, w), dt) for w, dt in zip(widths, dtypes)],
        compiler_params=_cparams("arbitrary"),
        name="norm_inproj",
    )(x2, sh, sc, g1, w_bf, bd, gq, gk, cos, sin)


def _attn_kernel(q_ref, k_ref, v_ref, o_ref, m_ref, l_ref, acc_ref, *, n_chunks):
    m_ref[...] = jnp.full(m_ref.shape, -jnp.inf, F32)
    l_ref[...] = jnp.zeros(l_ref.shape, F32)
    acc_ref[...] = jnp.zeros(acc_ref.shape, F32)

    def body(c, carry):
        kc = k_ref[0, c]
        vc = v_ref[0, c]
        for g in range(GQA_GROUP):
            s = jnp.dot(kc, q_ref[g], preferred_element_type=F32)
            m_old = m_ref[g]
            m_new = jnp.maximum(m_old, jnp.max(s, axis=0, keepdims=True))
            alpha = jnp.exp(m_old - m_new)
            p = jnp.exp(s - m_new)
            l_ref[g] = alpha * l_ref[g] + jnp.sum(p, axis=0, keepdims=True)
            acc_ref[g] = acc_ref[g] * alpha + jnp.dot(vc, p.astype(BF16), preferred_element_type=F32)
            m_ref[g] = m_new
        return carry

    lax.fori_loop(0, n_chunks, body, 0)
    o_ref[...] = (acc_ref[...] / l_ref[...]).astype(o_ref.dtype)


def _key_chunk(nk):
    for tk in (1280, 640, 256, 128):
        if nk % tk == 0:
            return tk
    raise ValueError(f"unsupported key count {nk}")


def _attention(q_t, k_c, v_t):
    h, d, l = q_t.shape
    kv, n_chunks, tk, _ = k_c.shape
    tq = min(ATTN_Q_TILE, l)
    return pl.pallas_call(
        functools.partial(_attn_kernel, n_chunks=n_chunks),
        grid=(kv, l // tq),
        in_specs=[pl.BlockSpec((GQA_GROUP, d, tq), lambda b, i: (b, 0, i)),
                  pl.BlockSpec((1, n_chunks, tk, d), lambda b, i: (b, 0, 0, 0)),
                  pl.BlockSpec((1, n_chunks, d, tk), lambda b, i: (b, 0, 0, 0))],
        out_specs=pl.BlockSpec((GQA_GROUP, d, tq), lambda b, i: (b, 0, i)),
        out_shape=jax.ShapeDtypeStruct((h, d, l), BF16),
        scratch_shapes=[pltpu.VMEM((GQA_GROUP, 1, tq), F32),
                        pltpu.VMEM((GQA_GROUP, 1, tq), F32),
                        pltpu.VMEM((GQA_GROUP, d, tq), F32)],
        compiler_params=_cparams("arbitrary", "arbitrary"),
        name="flash_attn",
    )(q_t, k_c, v_t)


def _ret_kernel(gc_ref, q_ref, k_ref, v_ref, kc_ref, vc_ref, tab_ref, *rest, reverse, final):
    if final:
        of_ref, gate_ref, gng_ref, gnb_ref, out_ref, s_ref = rest
    else:
        out_ref, s_ref = rest
    c_rows = RET_CHUNK
    n_local = q_ref.shape[0] // c_rows
    n_ctx = kc_ref.shape[0] // c_rows
    lane_head = lax.broadcasted_iota(jnp.int32, (c_rows, RET_QK_WIDTH), 1) // RET_QK_DIM

    def state_update(kch, vch):
        for h in range(RET_HEADS):
            vz = (vch[:, h * RET_V_DIM:(h + 1) * RET_V_DIM].astype(F32) * tab_ref[1, h]).astype(BF16)
            u = lax.dot_general(kch, vz, (((0,), (0,)), ((), ())), preferred_element_type=F32)
            s_ref[h] = gc_ref[h] * s_ref[h] + u

    @pl.when(pl.program_id(0) == 0)
    def _():
        s_ref[...] = jnp.zeros(s_ref.shape, F32)
        order = range(n_ctx - 1, -1, -1) if reverse else range(n_ctx)
        for cc in order:
            state_update(kc_ref[cc * c_rows:(cc + 1) * c_rows, :], vc_ref[cc * c_rows:(cc + 1) * c_rows, :])

    def chunk(ci, carry):
        c = (n_local - 1 - ci) if reverse else ci
        rows = pl.ds(pl.multiple_of(c * c_rows, c_rows), c_rows)
        q, k, v = q_ref[rows, :], k_ref[rows, :], v_ref[rows, :]
        for h in range(RET_HEADS):
            cols = slice(h * RET_V_DIM, (h + 1) * RET_V_DIM)
            qm = jnp.where(lane_head == h, q, jnp.zeros_like(q))
            sc = lax.dot_general(qm, k, (((1,), (1,)), ((), ())), preferred_element_type=F32)
            sc = sc * tab_ref[0, h]
            o = jnp.dot(sc.astype(BF16), v[:, cols], preferred_element_type=F32)
            o = o + jnp.dot(qm, s_ref[h].astype(BF16), preferred_element_type=F32) * tab_ref[2, h]
            if final:
                o = o + of_ref[rows, cols]
                mu = jnp.mean(o, axis=-1, keepdims=True)
                oc = o - mu
                var = jnp.mean(oc * oc, axis=-1, keepdims=True)
                on = oc * lax.rsqrt(var + EPS)
                gate = gate_ref[rows, cols]
                y = (on * gng_ref[:, cols] + gnb_ref[:, cols]) * (gate / (1.0 + jnp.exp(-gate)))
                out_ref[rows, cols] = y.astype(out_ref.dtype)
            else:
                out_ref[rows, cols] = o
        state_update(k, v)
        return carry

    lax.fori_loop(0, n_local, chunk, 0)


def _retention_pass(gc, qr, kr, vr, kr_c, vr_c, tabs, extra, *, reverse):
    l = qr.shape[0]
    tr = min(RET_ROW_TILE, l)
    nt = l // tr
    final = extra is not None
    idx = (lambda j, gc_: (nt - 1 - j, 0)) if reverse else (lambda j, gc_: (j, 0))
    const = lambda shape: pl.BlockSpec(shape, lambda j, gc_: (0,) * len(shape))
    in_specs = [pl.BlockSpec((tr, RET_QK_WIDTH), idx), pl.BlockSpec((tr, RET_QK_WIDTH), idx),
                pl.BlockSpec((tr, RET_WIDTH), idx),
                const(kr_c.shape), const(vr_c.shape), const(tabs.shape)]
    args = [qr, kr, vr, kr_c, vr_c, tabs]
    if final:
        of, gate, gng, gnb = extra
        in_specs += [pl.BlockSpec((tr, RET_WIDTH), idx), pl.BlockSpec((tr, RET_WIDTH), idx),
                     const(gng.shape), const(gnb.shape)]
        args += [of, gate, gng, gnb]
    return pl.pallas_call(
        functools.partial(_ret_kernel, reverse=reverse, final=final),
        grid_spec=pltpu.PrefetchScalarGridSpec(
            num_scalar_prefetch=1, grid=(nt,), in_specs=in_specs,
            out_specs=pl.BlockSpec((tr, RET_WIDTH), idx),
            scratch_shapes=[pltpu.VMEM((RET_HEADS, RET_QK_WIDTH, RET_V_DIM), F32)]),
        out_shape=jax.ShapeDtypeStruct((l, RET_WIDTH), BF16 if final else F32),
        compiler_params=_cparams("arbitrary"),
        name="retention_bwd" if reverse else "retention_fwd",
    )(gc, *args)


def _retention_tables(log_g, reverse):
    c = RET_CHUNK
    pos = jnp.arange(c, dtype=F32)
    diff = (pos[None, :] - pos[:, None]) if reverse else (pos[:, None] - pos[None, :])
    mask = (diff > 0) if reverse else (diff >= 0)
    lg = log_g[:, None, None]
    dmat = jnp.where(mask, jnp.exp(lg * jnp.where(mask, diff, 0.0)[None]), 0.0)
    zeta = jnp.exp(log_g[:, None] * (pos if reverse else (c - 1 - pos))[None, :])
    xi = jnp.exp(log_g[:, None] * ((c - pos) if reverse else (pos + 1.0))[None, :])
    ones = jnp.ones((1, 1, c), F32)
    tabs = jnp.stack([dmat, zeta[:, :, None] * ones, xi[:, :, None] * ones])
    return tabs, jnp.exp(log_g * c)


def _outproj_kernel(x_ref, a_ref, r_ref, w_ref, gt_ref, g2_ref, sh_ref, sc_ref, wr_ref, br_ref,
                    xn_ref, h2_ref, ids_ref, wts_ref):
    half = a_ref.shape[1]
    mix = (jnp.dot(a_ref[...], w_ref[:half, :], preferred_element_type=F32)
           + jnp.dot(r_ref[...], w_ref[half:, :], preferred_element_type=F32))
    x_new = x_ref[...] + gt_ref[...] * mix
    xn_ref[...] = x_new
    h2 = _rms_modulate(x_new, g2_ref[...], sh_ref[...], sc_ref[...])
    hi = h2.astype(BF16)
    lo = (h2 - hi.astype(F32)).astype(BF16)
    h2_ref[...] = hi

    nt = (((1,), (1,)), ((), ()))
    logits = (lax.dot_general(wr_ref[0], hi, nt, preferred_element_type=F32)
              + lax.dot_general(wr_ref[0], lo, nt, preferred_element_type=F32)
              + lax.dot_general(wr_ref[1], hi, nt, preferred_element_type=F32)) + br_ref[...]
    rows = [logits[g:g + 1, :] for g in range(N_GROUPS)]
    gmax = functools.reduce(jnp.maximum, rows)
    gidx = jnp.full(gmax.shape, N_GROUPS - 1, jnp.int32)
    for g in range(N_GROUPS - 2, -1, -1):
        gidx = jnp.where(rows[g] == gmax, g, gidx)
    p_g = 1.0 / functools.reduce(jnp.add, [jnp.exp(r - gmax) for r in rows])
    sel = logits[EXPERT_ROW0 + (N_GROUPS - 1) * EXPERTS_PER_GROUP:EXPERT_ROW0 + N_GROUPS * EXPERTS_PER_GROUP, :]
    for g in range(N_GROUPS - 2, -1, -1):
        blk = logits[EXPERT_ROW0 + g * EXPERTS_PER_GROUP:EXPERT_ROW0 + (g + 1) * EXPERTS_PER_GROUP, :]
        sel = jnp.where(gidx == g, blk, sel)
    ridx = lax.broadcasted_iota(jnp.int32, sel.shape, 0)
    m1 = jnp.max(sel, axis=0, keepdims=True)
    i1 = jnp.min(jnp.where(sel == m1, ridx, EXPERTS_PER_GROUP), axis=0, keepdims=True)
    sel2 = jnp.where(ridx == i1, -jnp.inf, sel)
    m2 = jnp.max(sel2, axis=0, keepdims=True)
    i2 = jnp.min(jnp.where(sel2 == m2, ridx, EXPERTS_PER_GROUP), axis=0, keepdims=True)
    e2 = jnp.exp(m2 - m1)
    w1 = p_g / (1.0 + e2)
    w2 = w1 * e2
    out_rows = lax.broadcasted_iota(jnp.int32, ids_ref.shape, 0)
    ids_ref[...] = jnp.where(out_rows == 0, gidx * EXPERTS_PER_GROUP + i1,
                             jnp.where(out_rows == 1, gidx * EXPERTS_PER_GROUP + i2, 0))
    wts_ref[...] = jnp.where(out_rows == 0, w1, jnp.where(out_rows == 1, w2, 0.0))


def _outproj_router(x2, attn, ret, w_out_bf, gt1, g2, sh2, sc2, wr, br):
    l, d = x2.shape
    tm = min(ROW_TILE, l)
    row = lambda w: pl.BlockSpec((tm, w), lambda i: (i, 0))
    col = pl.BlockSpec((SUBLANES, tm), lambda i: (0, i))
    return pl.pallas_call(
        _outproj_kernel,
        grid=(l // tm,),
        in_specs=[row(d), row(attn.shape[1]), row(ret.shape[1]), _const_spec(w_out_bf.shape),
                  _const_spec((1, d)), _const_spec((1, d)), _const_spec((1, d)), _const_spec((1, d)),
                  _const_spec(wr.shape), _const_spec(br.shape)],
        out_specs=[row(d), row(d), col, col],
        out_shape=[jax.ShapeDtypeStruct((l, d), F32), jax.ShapeDtypeStruct((l, d), BF16),
                   jax.ShapeDtypeStruct((SUBLANES, l), jnp.int32),
                   jax.ShapeDtypeStruct((SUBLANES, l), F32)],
        compiler_params=_cparams("arbitrary"),
        name="outproj_router",
    )(x2, attn, ret, w_out_bf, gt1, g2, sh2, sc2, wr, br)


def _moe_kernel(be_ref, na_ref, x_ref, wg_ref, wu_ref, wd_ref, y_ref, wg_s, wu_s, wd_s):
    b = pl.program_id(0)
    active = b < na_ref[0]
    changed = jnp.logical_or(b == 0, be_ref[b] != be_ref[jnp.maximum(b - 1, 0)])

    @pl.when(jnp.logical_and(active, changed))
    def _():
        wg_s[...] = wg_ref[0].astype(BF16)
        wu_s[...] = wu_ref[0].astype(BF16)
        wd_s[...] = wd_ref[0].astype(BF16)

    @pl.when(active)
    def _():
        x = x_ref[...]
        hg = jnp.dot(x, wg_s[...], preferred_element_type=F32)
        hu = jnp.dot(x, wu_s[...], preferred_element_type=F32)
        a = (hg / (1.0 + jnp.exp(-hg))) * hu
        y_ref[...] = jnp.dot(a.astype(BF16), wd_s[...], preferred_element_type=F32)

    @pl.when(jnp.logical_not(active))
    def _():
        y_ref[...] = jnp.zeros(y_ref.shape, y_ref.dtype)


def _moe_blocks(blk_e, n_active, xb, w_gate, w_up, w_down):
    p, d = xb.shape
    ne, _, de = w_gate.shape
    nblk = p // MOE_BLOCK
    xmap = lambda b, be, na: (jnp.minimum(b, na[0] - 1), 0)
    wmap = lambda b, be, na: (be[b], 0, 0)
    return pl.pallas_call(
        _moe_kernel,
        grid_spec=pltpu.PrefetchScalarGridSpec(
            num_scalar_prefetch=2, grid=(nblk,),
            in_specs=[pl.BlockSpec((MOE_BLOCK, d), xmap),
                      pl.BlockSpec((1, d, de), wmap), pl.BlockSpec((1, d, de), wmap),
                      pl.BlockSpec((1, de, d), wmap)],
            out_specs=pl.BlockSpec((MOE_BLOCK, d), lambda b, be, na: (b, 0)),
            scratch_shapes=[pltpu.VMEM((d, de), BF16), pltpu.VMEM((d, de), BF16),
                            pltpu.VMEM((de, d), BF16)]),
        out_shape=jax.ShapeDtypeStruct((p, d), F32),
        compiler_params=_cparams("arbitrary"),
        name="moe_experts",
    )(blk_e, n_active, xb, w_gate, w_up, w_down)


def _final_kernel(x_ref, y1_ref, y2_ref, w_ref, gt_ref, g_ref, o_ref):
    w = w_ref[...]
    y = y1_ref[...] * w[:, 0:1] + y2_ref[...] * w[:, 1:2]
    x = x_ref[...] + gt_ref[...] * y
    ms = jnp.mean(x * x, axis=-1, keepdims=True)
    o_ref[...] = (x * lax.rsqrt(ms + EPS)) * g_ref[...]


def _final(x_new, y1, y2, wts, gt2, gfin):
    l, d = x_new.shape
    tm = min(ROW_TILE, l)
    row = lambda w: pl.BlockSpec((tm, w), lambda i: (i, 0))
    return pl.pallas_call(
        _final_kernel,
        grid=(l // tm,),
        in_specs=[row(d), row(d), row(d), row(wts.shape[1]), _const_spec((1, d)), _const_spec((1, d))],
        out_specs=row(d),
        out_shape=jax.ShapeDtypeStruct((l, d), F32),
        compiler_params=_cparams("arbitrary"),
        name="combine_final_norm",
    )(x_new, y1, y2, wts, gt2, gfin)


def _rope_tables(rows_count):
    row = jnp.repeat(jnp.arange(rows_count), GRID_W).astype(F32)
    col = jnp.tile(jnp.arange(GRID_W), rows_count).astype(F32)
    freqs = ROPE_THETA ** (-jnp.arange(0, ROPE_AXIS_DIM, 2, dtype=F32) / ROPE_AXIS_DIM)
    ang = jnp.concatenate([row[:, None] * freqs, col[:, None] * freqs], axis=-1)
    ang = jnp.repeat(ang, 2, axis=-1)
    sign = jnp.where(jnp.arange(HEAD_DIM) % 2 == 0, -1.0, 1.0).astype(F32)
    cos = jnp.tile(jnp.cos(ang), (1, LANES // HEAD_DIM))
    sin = jnp.tile(jnp.sin(ang) * sign, (1, LANES // HEAD_DIM))
    return cos, sin


def _dispatch(ids, n_tokens):
    a = TOP_K * n_tokens
    eid_f = ids[:TOP_K].reshape(a)
    tok_f = jnp.tile(jnp.arange(n_tokens, dtype=jnp.int32), TOP_K)
    onehot = (eid_f[:, None] == jnp.arange(N_EXPERTS, dtype=jnp.int32)[None, :]).astype(jnp.int32)
    incl = jnp.cumsum(onehot, axis=0)
    counts = incl[-1]
    rank = jnp.sum((incl - onehot) * onehot, axis=1)
    pcounts = (counts + MOE_BLOCK - 1) // MOE_BLOCK * MOE_BLOCK
    pends = jnp.cumsum(pcounts)
    pstarts = pends - pcounts
    dest = jnp.sum(onehot * pstarts[None, :], axis=1) + rank
    p = a + N_EXPERTS * MOE_BLOCK
    nblk = p // MOE_BLOCK
    buf_tok = jnp.zeros((p,), jnp.int32).at[dest].set(tok_f)
    n_active = pends[-1] // MOE_BLOCK
    blk_start = jnp.arange(nblk, dtype=jnp.int32) * MOE_BLOCK
    blk_e = jnp.minimum(jnp.sum(blk_start[:, None] >= pends[None, :], axis=1), N_EXPERTS - 1)
    blk_e = jnp.where(jnp.arange(nblk) < n_active, blk_e, blk_e[n_active - 1]).astype(jnp.int32)
    return buf_tok, dest.reshape(TOP_K, n_tokens), blk_e, n_active.astype(jnp.int32).reshape(1)


def _split_bf16(w):
    hi = w.astype(BF16)
    return jnp.stack([hi, (w - hi.astype(F32)).astype(BF16)])


def kernel(x, c, ctx, c_ctx, w_ada, b_ada, norm1_g, w_in, attn_q_norm, attn_k_norm, ret_decay_fwd,
           ret_decay_bwd, ret_gn_g, ret_gn_b, w_out, norm2_g, moe_w_grp, moe_b_grp, moe_w_exp, moe_b_exp,
           moe_w_gate, moe_w_up, moe_w_down, final_norm_g):
    b, l, d = x.shape
    cl = ctx.shape[1]
    assert b == 1 and w_ada.shape[0] == 1, "single batch element, single layer"
    x2, ctx2 = x[0], ctx[0]

    cvecs = jnp.zeros((SUBLANES, d), F32).at[0].set(c[0]).at[1].set(c_ctx)
    mod = _ada(cvecs, w_ada[0], b_ada[0])
    sh1, sc1, gt1, sh2, sc2, gt2 = [mod[0:1, i * d:(i + 1) * d] for i in range(6)]
    csh1, csc1 = mod[1:2, 0:d], mod[1:2, d:2 * d]

    w_in_bf = w_in[0].astype(BF16)
    head_of = jnp.arange(ATTN_WIDTH) // HEAD_DIM
    bd = jnp.where(head_of[:, None] == head_of[None, :], 1.0 / HEAD_DIM, 0.0).astype(BF16)
    gq = jnp.tile(attn_q_norm[0], ATTN_HEADS)[None, :]
    gk = jnp.tile(attn_k_norm[0], ATTN_KV_HEADS)[None, :]
    g1 = norm1_g[0][None, :]
    cos, sin = _rope_tables(l // GRID_W)
    qa, ka, va, qr, kr, vr, gr = _inproj(x2, sh1, sc1, g1, w_in_bf, bd, gq, gk, cos, sin)
    _, ka_c, va_c, _, kr_c, vr_c, _ = _inproj(ctx2, csh1, csc1, g1, w_in_bf, bd, gq, gk,
                                              jnp.ones((cl, LANES), F32), jnp.zeros((cl, LANES), F32))

    nk = cl + l
    tk = _key_chunk(nk)
    q_t = qa.reshape(l, ATTN_HEADS, HEAD_DIM).transpose(1, 2, 0)
    k_all = jnp.concatenate([ka_c, ka], axis=0).reshape(nk // tk, tk, ATTN_KV_HEADS, HEAD_DIM)
    v_all = jnp.concatenate([va_c, va], axis=0).reshape(nk // tk, tk, ATTN_KV_HEADS, HEAD_DIM)
    o_t = _attention(q_t, k_all.transpose(2, 0, 1, 3), v_all.transpose(2, 0, 3, 1))
    attn = o_t.transpose(2, 0, 1).reshape(l, ATTN_WIDTH)

    log_gf = jax.nn.log_sigmoid(ret_decay_fwd[0].astype(F32))
    log_gb = jax.nn.log_sigmoid(ret_decay_bwd[0].astype(F32))
    tab_f, gc_f = _retention_tables(log_gf, False)
    tab_b, gc_b = _retention_tables(log_gb, True)
    o_f = _retention_pass(gc_f, qr, kr, vr, kr_c, vr_c, tab_f, None, reverse=False)
    ret = _retention_pass(gc_b, qr, kr, vr, kr_c, vr_c, tab_b,
                          (o_f, gr, ret_gn_g[0][None, :], ret_gn_b[0][None, :]), reverse=True)

    wr = jnp.zeros((ROUTER_ROWS, d), F32)
    wr = wr.at[:N_GROUPS].set(moe_w_grp[0].T).at[EXPERT_ROW0:EXPERT_ROW0 + N_EXPERTS].set(moe_w_exp[0].T)
    br = jnp.zeros((ROUTER_ROWS, 1), F32)
    br = br.at[:N_GROUPS, 0].set(moe_b_grp[0]).at[EXPERT_ROW0:EXPERT_ROW0 + N_EXPERTS, 0].set(moe_b_exp[0])
    x_new, h2, ids, wts = _outproj_router(x2, attn, ret, w_out[0].astype(BF16), gt1, norm2_g[0][None, :],
                                          sh2, sc2, _split_bf16(wr), br)

    buf_tok, dest, blk_e, n_active = _dispatch(ids, l)
    yb = _moe_blocks(blk_e, n_active, h2[buf_tok], moe_w_gate[0], moe_w_up[0], moe_w_down[0])
    out = _final(x_new, yb[dest[0]], yb[dest[1]], wts[:TOP_K].T, gt2, final_norm_g[None, :])
    return out[None]
```

```python
import functools

import jax
import jax.numpy as jnp
import numpy as np
from jax import lax
from jax.experimental import pallas as pl
from jax.experimental.pallas import tpu as pltpu

F32 = jnp.float32
BF16 = jnp.bfloat16

GRID_W = 64
HEAD_DIM = 64
ATTN_HEADS = 8
ATTN_KV_HEADS = 2
GQA_GROUP = ATTN_HEADS // ATTN_KV_HEADS
RET_HEADS = 4
RET_QK_DIM = 64
RET_V_DIM = 128
RET_CHUNK = 128
ATTN_WIDTH = ATTN_HEADS * HEAD_DIM
KV_WIDTH = ATTN_KV_HEADS * HEAD_DIM
RET_QK_WIDTH = RET_HEADS * RET_QK_DIM
RET_WIDTH = RET_HEADS * RET_V_DIM
PROJ_SIZES = (ATTN_WIDTH, KV_WIDTH, KV_WIDTH, RET_QK_WIDTH, RET_QK_WIDTH, RET_WIDTH, RET_WIDTH)
PROJ_OFFS = tuple(int(v) for v in np.cumsum((0,) + PROJ_SIZES))
PROJ_DIM = PROJ_OFFS[-1]
ROPE_THETA = 10000.0
ROPE_AXIS_DIM = HEAD_DIM // 2
N_GROUPS = 4
EXPERTS_PER_GROUP = 8
N_EXPERTS = N_GROUPS * EXPERTS_PER_GROUP
TOP_K = 2
D_EXPERT = 512
EPS = 1e-6
LOG2E = 1.4426950408889634
ATTN_AUG_DIM = 128
MAX_UNSHIFTED_BOUND = 60.0

LANES = 128
SUBLANES = 8
VMEM_LIMIT_BYTES = 56 * 1024 * 1024

ROW_TILE = 512
ATTN_Q_TILE = 256
RET_ROW_TILE = 1024
MOE_BLOCK = 256
ROUTER_ROWS = 128
EXPERT_ROW0 = 8


def _cparams(*sem):
    return pltpu.CompilerParams(dimension_semantics=sem, vmem_limit_bytes=VMEM_LIMIT_BYTES)


def _const_spec(shape):
    nd = len(shape)
    return pl.BlockSpec(shape, lambda *_: (0,) * nd)


def _ada_kernel(s_ref, w_ref, b_ref, o_ref):
    s = s_ref[...]
    s = s / (1.0 + jnp.exp(-s))
    o_ref[...] = jnp.dot(s, w_ref[...], preferred_element_type=F32,
                         precision=lax.Precision.HIGHEST) + b_ref[...]


def _ada(cvecs, w_ada, b_ada):
    d, n = w_ada.shape
    tn = 1536
    return pl.pallas_call(
        _ada_kernel,
        grid=(n // tn,),
        in_specs=[_const_spec((SUBLANES, d)),
                  pl.BlockSpec((d, tn), lambda j: (0, j)),
                  pl.BlockSpec((1, tn), lambda j: (0, j))],
        out_specs=pl.BlockSpec((SUBLANES, tn), lambda j: (0, j)),
        out_shape=jax.ShapeDtypeStruct((SUBLANES, n), F32),
        compiler_params=_cparams("arbitrary"),
        name="ada_mod",
    )(cvecs, w_ada, b_ada.reshape(1, n))


def _rms_modulate(x, g, sh, sc):
    ms = jnp.mean(x * x, axis=-1, keepdims=True)
    return (x * lax.rsqrt(ms + EPS)) * g * (1.0 + sc) + sh


def _head_mean_sq(v, bd):
    sq = v * v
    hi = sq.astype(BF16)
    lo = (sq - hi.astype(F32)).astype(BF16)
    return (jnp.dot(hi, bd, preferred_element_type=F32) + jnp.dot(lo, bd, preferred_element_type=F32))


def _rope_chunks(v, cos, sin, even):
    outs = []
    for c in range(v.shape[1] // LANES):
        xc = v[:, c * LANES:(c + 1) * LANES]
        nxt = pltpu.roll(xc, LANES - 1, 1)
        prv = pltpu.roll(xc, 1, 1)
        outs.append(xc * cos + jnp.where(even, nxt, prv) * sin)
    return outs


def _inproj_kernel(x_ref, sh_ref, sc_ref, g_ref, w_ref, bd_ref, gq_ref, gk_ref, cos_ref, sin_ref,
                   qa_ref, ka_ref, va_ref, qr_ref, kr_ref, vr_ref, gr_ref):
    h = _rms_modulate(x_ref[...], g_ref[...], sh_ref[...], sc_ref[...])
    proj = jnp.dot(h.astype(BF16), w_ref[...], preferred_element_type=F32)
    o = PROJ_OFFS
    qa, ka, va = proj[:, o[0]:o[1]], proj[:, o[1]:o[2]], proj[:, o[2]:o[3]]
    qr, kr, vr, gr = proj[:, o[3]:o[4]], proj[:, o[4]:o[5]], proj[:, o[5]:o[6]], proj[:, o[6]:o[7]]
    cos, sin = cos_ref[...], sin_ref[...]
    even = (lax.broadcasted_iota(jnp.int32, cos.shape, 1) % 2) == 0
    bd = bd_ref[...]
    scale = HEAD_DIM ** -0.5 * LOG2E

    qn = qa * lax.rsqrt(_head_mean_sq(qa, bd) + EPS) * gq_ref[...]
    for c, blk in enumerate(_rope_chunks(qn, cos, sin, even)):
        qa_ref[:, c * LANES:(c + 1) * LANES] = (blk * scale).astype(BF16)
    kn = ka * lax.rsqrt(_head_mean_sq(ka, bd[:KV_WIDTH, :KV_WIDTH]) + EPS) * gk_ref[...]
    ka_ref[...] = _rope_chunks(kn, cos, sin, even)[0].astype(BF16)
    va_ref[...] = va.astype(BF16)
    for c, blk in enumerate(_rope_chunks(qr, cos, sin, even)):
        qr_ref[:, c * LANES:(c + 1) * LANES] = blk.astype(BF16)
    rscale = RET_QK_DIM ** -0.5
    for c, blk in enumerate(_rope_chunks(kr, cos, sin, even)):
        kr_ref[:, c * LANES:(c + 1) * LANES] = (blk * rscale).astype(BF16)
    vr_ref[...] = vr.astype(BF16)
    gr_ref[...] = gr


def _inproj(x2, sh, sc, g1, w_bf, bd, gq, gk, cos, sin):
    rows, d = x2.shape
    tm = min(ROW_TILE, rows)
    row = lambda w: pl.BlockSpec((tm, w), lambda i: (i, 0))
    widths = PROJ_SIZES
    dtypes = (BF16,) * 6 + (F32,)
    return pl.pallas_call(
        _inproj_kernel,
        grid=(rows // tm,),
        in_specs=[row(d), _const_spec((1, d)), _const_spec((1, d)), _const_spec((1, d)),
                  _const_spec(w_bf.shape), _const_spec(bd.shape),
                  _const_spec((1, ATTN_WIDTH)), _const_spec((1, KV_WIDTH)),
                  row(LANES), row(LANES)],
        out_specs=[row(w) for w in widths],
        out_shape=[jax.ShapeDtypeStruct((rows, w), dt) for w, dt in zip(widths, dtypes)],
        compiler_params=_cparams("arbitrary"),
        name="norm_inproj",
    )(x2, sh, sc, g1, w_bf, bd, gq, gk, cos, sin)


def _attn_kernel(q_ref, k_ref, v_ref, o_ref, m_ref, l_ref, acc_ref, *, n_chunks, running_max):
    m_ref[...] = jnp.full(m_ref.shape, -jnp.inf, F32)
    l_ref[...] = jnp.zeros(l_ref.shape, F32)
    acc_ref[...] = jnp.zeros(acc_ref.shape, F32)

    def body(c, carry):
        kc = k_ref[0, c]
        vc = v_ref[0, c]
        s = jnp.dot(kc, q_ref[0], preferred_element_type=F32)
        if running_max:
            m_old = m_ref[...]
            m_new = jnp.maximum(m_old, jnp.max(s, axis=0, keepdims=True))
            alpha = jnp.exp2(m_old - m_new)
            p = jnp.exp2(s - m_new)
            l_ref[...] = alpha * l_ref[...] + jnp.sum(p, axis=0, keepdims=True)
            acc_ref[...] = acc_ref[...] * alpha + jnp.dot(vc, p.astype(BF16), preferred_element_type=F32)
            m_ref[...] = m_new
        else:
            p = jnp.exp2(s)
            l_ref[...] += jnp.sum(p, axis=0, keepdims=True)
            acc_ref[...] += jnp.dot(vc, p.astype(BF16), preferred_element_type=F32)
        return carry

    lax.fori_loop(0, n_chunks, body, 0)
    o_ref[0] = (acc_ref[...] / l_ref[...]).astype(o_ref.dtype)


def _key_chunk(nk):
    for tk in (3328, 1280, 640, 256, 128):
        if nk % tk == 0:
            return tk
    raise ValueError(f"unsupported key count {nk}")


def _attention(q_t, k_c, v_t, *, running_max):
    kv, da, gl = q_t.shape
    _, n_chunks, d, tk = v_t.shape
    nq = GQA_GROUP * min(ATTN_Q_TILE, gl // GQA_GROUP)
    return pl.pallas_call(
        functools.partial(_attn_kernel, n_chunks=n_chunks, running_max=running_max),
        grid=(kv, gl // nq),
        in_specs=[pl.BlockSpec((1, da, nq), lambda b, i: (b, 0, i)),
                  pl.BlockSpec((1, n_chunks, tk, da), lambda b, i: (b, 0, 0, 0)),
                  pl.BlockSpec((1, n_chunks, d, tk), lambda b, i: (b, 0, 0, 0))],
        out_specs=pl.BlockSpec((1, d, nq), lambda b, i: (b, 0, i)),
        out_shape=jax.ShapeDtypeStruct((kv, d, gl), BF16),
        scratch_shapes=[pltpu.VMEM((1, nq), F32), pltpu.VMEM((1, nq), F32), pltpu.VMEM((d, nq), F32)],
        compiler_params=_cparams("arbitrary", "arbitrary"),
        name="flash_attn",
    )(q_t, k_c, v_t)


def _ret_kernel(gc_ref, q_ref, k_ref, v_ref, kc_ref, vc_ref, tab_ref, *rest, reverse, final):
    if final:
        of_ref, gate_ref, gng_ref, gnb_ref, out_ref, s_ref = rest
    else:
        out_ref, s_ref = rest
    c_rows = RET_CHUNK
    n_local = q_ref.shape[0] // c_rows
    n_ctx = kc_ref.shape[0] // c_rows
    lane_head = lax.broadcasted_iota(jnp.int32, (c_rows, RET_QK_WIDTH), 1) // RET_QK_DIM

    def state_update(kch, vch):
        for h in range(RET_HEADS):
            vz = (vch[:, h * RET_V_DIM:(h + 1) * RET_V_DIM].astype(F32) * tab_ref[1, h]).astype(BF16)
            u = lax.dot_general(kch, vz, (((0,), (0,)), ((), ())), preferred_element_type=F32)
            s_ref[h] = gc_ref[h] * s_ref[h] + u

    @pl.when(pl.program_id(0) == 0)
    def _():
        s_ref[...] = jnp.zeros(s_ref.shape, F32)
        order = range(n_ctx - 1, -1, -1) if reverse else range(n_ctx)
        for cc in order:
            state_update(kc_ref[cc * c_rows:(cc + 1) * c_rows, :], vc_ref[cc * c_rows:(cc + 1) * c_rows, :])

    def chunk(ci, carry):
        c = (n_local - 1 - ci) if reverse else ci
        rows = pl.ds(pl.multiple_of(c * c_rows, c_rows), c_rows)
        q, k, v = q_ref[rows, :], k_ref[rows, :], v_ref[rows, :]
        for h in range(RET_HEADS):
            cols = slice(h * RET_V_DIM, (h + 1) * RET_V_DIM)
            qm = jnp.where(lane_head == h, q, jnp.zeros_like(q))
            sc = lax.dot_general(qm, k, (((1,), (1,)), ((), ())), preferred_element_type=F32)
            sc = sc * tab_ref[0, h]
            o = jnp.dot(sc.astype(BF16), v[:, cols], preferred_element_type=F32)
            o = o + jnp.dot(qm, s_ref[h].astype(BF16), preferred_element_type=F32) * tab_ref[2, h]
            if final:
                o = o + of_ref[rows, cols]
                mu = jnp.mean(o, axis=-1, keepdims=True)
                oc = o - mu
                var = jnp.mean(oc * oc, axis=-1, keepdims=True)
                on = oc * lax.rsqrt(var + EPS)
                gate = gate_ref[rows, cols]
                y = (on * gng_ref[:, cols] + gnb_ref[:, cols]) * (gate / (1.0 + jnp.exp(-gate)))
                out_ref[rows, cols] = y.astype(out_ref.dtype)
            else:
                out_ref[rows, cols] = o
        state_update(k, v)
        return carry

    lax.fori_loop(0, n_local, chunk, 0)


def _retention_pass(gc, qr, kr, vr, kr_c, vr_c, tabs, extra, *, reverse):
    l = qr.shape[0]
    tr = min(RET_ROW_TILE, l)
    nt = l // tr
    final = extra is not None
    idx = (lambda j, gc_: (nt - 1 - j, 0)) if reverse else (lambda j, gc_: (j, 0))
    const = lambda shape: pl.BlockSpec(shape, lambda j, gc_: (0,) * len(shape))
    in_specs = [pl.BlockSpec((tr, RET_QK_WIDTH), idx), pl.BlockSpec((tr, RET_QK_WIDTH), idx),
                pl.BlockSpec((tr, RET_WIDTH), idx),
                const(kr_c.shape), const(vr_c.shape), const(tabs.shape)]
    args = [qr, kr, vr, kr_c, vr_c, tabs]
    if final:
        of, gate, gng, gnb = extra
        in_specs += [pl.BlockSpec((tr, RET_WIDTH), idx), pl.BlockSpec((tr, RET_WIDTH), idx),
                     const(gng.shape), const(gnb.shape)]
        args += [of, gate, gng, gnb]
    return pl.pallas_call(
        functools.partial(_ret_kernel, reverse=reverse, final=final),
        grid_spec=pltpu.PrefetchScalarGridSpec(
            num_scalar_prefetch=1, grid=(nt,), in_specs=in_specs,
            out_specs=pl.BlockSpec((tr, RET_WIDTH), idx),
            scratch_shapes=[pltpu.VMEM((RET_HEADS, RET_QK_WIDTH, RET_V_DIM), F32)]),
        out_shape=jax.ShapeDtypeStruct((l, RET_WIDTH), BF16 if final else F32),
        compiler_params=_cparams("arbitrary"),
        name="retention_bwd" if reverse else "retention_fwd",
    )(gc, *args)


def _retention_tables(log_g, reverse):
    c = RET_CHUNK
    pos = jnp.arange(c, dtype=F32)
    diff = (pos[None, :] - pos[:, None]) if reverse else (pos[:, None] - pos[None, :])
    mask = (diff > 0) if reverse else (diff >= 0)
    lg = log_g[:, None, None]
    dmat = jnp.where(mask, jnp.exp(lg * jnp.where(mask, diff, 0.0)[None]), 0.0)
    zeta = jnp.exp(log_g[:, None] * (pos if reverse else (c - 1 - pos))[None, :])
    xi = jnp.exp(log_g[:, None] * ((c - pos) if reverse else (pos + 1.0))[None, :])
    ones = jnp.ones((1, 1, c), F32)
    tabs = jnp.stack([dmat, zeta[:, :, None] * ones, xi[:, :, None] * ones])
    return tabs, jnp.exp(log_g * c)


def _outproj_kernel(x_ref, a_ref, r_ref, w_ref, gt_ref, g2_ref, sh_ref, sc_ref, wr_ref, br_ref,
                    xn_ref, h2_ref, ids_ref, wts_ref):
    half = a_ref.shape[1]
    mix = (jnp.dot(a_ref[...], w_ref[:half, :], preferred_element_type=F32)
           + jnp.dot(r_ref[...], w_ref[half:, :], preferred_element_type=F32))
    x_new = x_ref[...] + gt_ref[...] * mix
    xn_ref[...] = x_new
    h2 = _rms_modulate(x_new, g2_ref[...], sh_ref[...], sc_ref[...])
    hi = h2.astype(BF16)
    hi32 = hi.astype(F32)
    lo = (h2 - hi32).astype(BF16)
    bits = lax.bitcast_convert_type(hi32, jnp.uint32)
    dh = bits.shape[1] // 2
    h2_ref[...] = (bits[:, dh:] & jnp.uint32(0xFFFF0000)) | (bits[:, :dh] >> 16)

    nt = (((1,), (1,)), ((), ()))
    logits = (lax.dot_general(wr_ref[0], hi, nt, preferred_element_type=F32)
              + lax.dot_general(wr_ref[0], lo, nt, preferred_element_type=F32)
              + lax.dot_general(wr_ref[1], hi, nt, preferred_element_type=F32)) + br_ref[...]
    rows = [logits[g:g + 1, :] for g in range(N_GROUPS)]
    gmax = functools.reduce(jnp.maximum, rows)
    gidx = jnp.full(gmax.shape, N_GROUPS - 1, jnp.int32)
    for g in range(N_GROUPS - 2, -1, -1):
        gidx = jnp.where(rows[g] == gmax, g, gidx)
    p_g = 1.0 / functools.reduce(jnp.add, [jnp.exp(r - gmax) for r in rows])
    sel = logits[EXPERT_ROW0 + (N_GROUPS - 1) * EXPERTS_PER_GROUP:EXPERT_ROW0 + N_GROUPS * EXPERTS_PER_GROUP, :]
    for g in range(N_GROUPS - 2, -1, -1):
        blk = logits[EXPERT_ROW0 + g * EXPERTS_PER_GROUP:EXPERT_ROW0 + (g + 1) * EXPERTS_PER_GROUP, :]
        sel = jnp.where(gidx == g, blk, sel)
    ridx = lax.broadcasted_iota(jnp.int32, sel.shape, 0)
    m1 = jnp.max(sel, axis=0, keepdims=True)
    i1 = jnp.min(jnp.where(sel == m1, ridx, EXPERTS_PER_GROUP), axis=0, keepdims=True)
    sel2 = jnp.where(ridx == i1, -jnp.inf, sel)
    m2 = jnp.max(sel2, axis=0, keepdims=True)
    i2 = jnp.min(jnp.where(sel2 == m2, ridx, EXPERTS_PER_GROUP), axis=0, keepdims=True)
    e2 = jnp.exp(m2 - m1)
    w1 = p_g / (1.0 + e2)
    w2 = w1 * e2
    out_rows = lax.broadcasted_iota(jnp.int32, ids_ref.shape, 0)
    ids_ref[...] = jnp.where(out_rows == 0, gidx * EXPERTS_PER_GROUP + i1,
                             jnp.where(out_rows == 1, gidx * EXPERTS_PER_GROUP + i2, 0))
    wts_ref[...] = jnp.where(out_rows == 0, w1, jnp.where(out_rows == 1, w2, 0.0))


def _outproj_router(x2, attn, ret, w_out_bf, gt1, g2, sh2, sc2, wr, br):
    l, d = x2.shape
    tm = min(ROW_TILE, l)
    row = lambda w: pl.BlockSpec((tm, w), lambda i: (i, 0))
    col = pl.BlockSpec((SUBLANES, tm), lambda i: (0, i))
    return pl.pallas_call(
        _outproj_kernel,
        grid=(l // tm,),
        in_specs=[row(d), row(attn.shape[1]), row(ret.shape[1]), _const_spec(w_out_bf.shape),
                  _const_spec((1, d)), _const_spec((1, d)), _const_spec((1, d)), _const_spec((1, d)),
                  _const_spec(wr.shape), _const_spec(br.shape)],
        out_specs=[row(d), row(d // 2), col, col],
        out_shape=[jax.ShapeDtypeStruct((l, d), F32), jax.ShapeDtypeStruct((l, d // 2), jnp.uint32),
                   jax.ShapeDtypeStruct((SUBLANES, l), jnp.int32),
                   jax.ShapeDtypeStruct((SUBLANES, l), F32)],
        compiler_params=_cparams("arbitrary"),
        name="outproj_router",
    )(x2, attn, ret, w_out_bf, gt1, g2, sh2, sc2, wr, br)


def _moe_kernel(be_ref, na_ref, tok_ref, h_ref, wg_ref, wu_ref, wd_ref, y_ref, wg_s, wu_s, wd_s, xg_s):
    b = pl.program_id(0)
    active = b < na_ref[0]
    changed = jnp.logical_or(b == 0, be_ref[b] != be_ref[jnp.maximum(b - 1, 0)])

    @pl.when(jnp.logical_and(active, changed))
    def _():
        wg_s[...] = wg_ref[0].astype(BF16)
        wu_s[...] = wu_ref[0].astype(BF16)
        wd_s[...] = wd_ref[0].astype(BF16)

    @pl.when(active)
    def _():
        def gather_row(r, carry):
            xg_s[pl.ds(r, 1), :] = h_ref[pl.ds(tok_ref[0, 0, r], 1), :]
            return carry

        lax.fori_loop(0, MOE_BLOCK, gather_row, 0, unroll=8)
        bits = xg_s[...]
        dh = bits.shape[1]
        x_lo = lax.bitcast_convert_type(bits << 16, F32).astype(BF16)
        x_hi = lax.bitcast_convert_type(bits & jnp.uint32(0xFFFF0000), F32).astype(BF16)
        hg = (jnp.dot(x_lo, wg_s[:dh, :], preferred_element_type=F32)
              + jnp.dot(x_hi, wg_s[dh:, :], preferred_element_type=F32))
        hu = (jnp.dot(x_lo, wu_s[:dh, :], preferred_element_type=F32)
              + jnp.dot(x_hi, wu_s[dh:, :], preferred_element_type=F32))
        a = (hg / (1.0 + jnp.exp(-hg))) * hu
        y_ref[...] = jnp.dot(a.astype(BF16), wd_s[...], preferred_element_type=F32)

    @pl.when(jnp.logical_not(active))
    def _():
        y_ref[...] = jnp.zeros(y_ref.shape, y_ref.dtype)


def _moe_blocks(blk_e, n_active, buf_tok, h2p, w_gate, w_up, w_down):
    l, dh = h2p.shape
    ne, d, de = w_gate.shape
    nblk = buf_tok.shape[0]
    wmap = lambda b, be, na: (be[b], 0, 0)
    return pl.pallas_call(
        _moe_kernel,
        grid_spec=pltpu.PrefetchScalarGridSpec(
            num_scalar_prefetch=2, grid=(nblk,),
            in_specs=[pl.BlockSpec((1, 1, MOE_BLOCK), lambda b, be, na: (b, 0, 0), memory_space=pltpu.SMEM),
                      pl.BlockSpec((l, dh), lambda b, be, na: (0, 0), pipeline_mode=pl.Buffered(1)),
                      pl.BlockSpec((1, d, de), wmap), pl.BlockSpec((1, d, de), wmap),
                      pl.BlockSpec((1, de, d), wmap)],
            out_specs=pl.BlockSpec((MOE_BLOCK, d), lambda b, be, na: (b, 0)),
            scratch_shapes=[pltpu.VMEM((d, de), BF16), pltpu.VMEM((d, de), BF16),
                            pltpu.VMEM((de, d), BF16), pltpu.VMEM((MOE_BLOCK, dh), jnp.uint32)]),
        out_shape=jax.ShapeDtypeStruct((nblk * MOE_BLOCK, d), F32),
        compiler_params=_cparams("arbitrary"),
        name="moe_experts",
    )(blk_e, n_active, buf_tok, h2p, w_gate, w_up, w_down)


def _final_kernel(x_ref, y1_ref, y2_ref, w_ref, gt_ref, g_ref, o_ref):
    w = w_ref[...]
    y = y1_ref[...] * w[:, 0:1] + y2_ref[...] * w[:, 1:2]
    x = x_ref[...] + gt_ref[...] * y
    ms = jnp.mean(x * x, axis=-1, keepdims=True)
    o_ref[...] = (x * lax.rsqrt(ms + EPS)) * g_ref[...]


def _final(x_new, y1, y2, wts, gt2, gfin):
    l, d = x_new.shape
    tm = min(ROW_TILE, l)
    row = lambda w: pl.BlockSpec((tm, w), lambda i: (i, 0))
    return pl.pallas_call(
        _final_kernel,
        grid=(l // tm,),
        in_specs=[row(d), row(d), row(d), row(wts.shape[1]), _const_spec((1, d)), _const_spec((1, d))],
        out_specs=row(d),
        out_shape=jax.ShapeDtypeStruct((l, d), F32),
        compiler_params=_cparams("arbitrary"),
        name="combine_final_norm",
    )(x_new, y1, y2, wts, gt2, gfin)


def _rope_tables(rows_count):
    row = jnp.repeat(jnp.arange(rows_count), GRID_W).astype(F32)
    col = jnp.tile(jnp.arange(GRID_W), rows_count).astype(F32)
    freqs = ROPE_THETA ** (-jnp.arange(0, ROPE_AXIS_DIM, 2, dtype=F32) / ROPE_AXIS_DIM)
    ang = jnp.concatenate([row[:, None] * freqs, col[:, None] * freqs], axis=-1)
    ang = jnp.repeat(ang, 2, axis=-1)
    sign = jnp.where(jnp.arange(HEAD_DIM) % 2 == 0, -1.0, 1.0).astype(F32)
    cos = jnp.tile(jnp.cos(ang), (1, LANES // HEAD_DIM))
    sin = jnp.tile(jnp.sin(ang) * sign, (1, LANES // HEAD_DIM))
    return cos, sin


def _dispatch(ids, n_tokens):
    a = TOP_K * n_tokens
    eid_f = ids[:TOP_K].reshape(a)
    tok_f = jnp.tile(jnp.arange(n_tokens, dtype=jnp.int32), TOP_K)
    onehot = (eid_f[:, None] == jnp.arange(N_EXPERTS, dtype=jnp.int32)[None, :]).astype(jnp.int32)
    incl = jnp.cumsum(onehot, axis=0)
    counts = incl[-1]
    rank = jnp.sum((incl - onehot) * onehot, axis=1)
    pcounts = (counts + MOE_BLOCK - 1) // MOE_BLOCK * MOE_BLOCK
    pends = jnp.cumsum(pcounts)
    pstarts = pends - pcounts
    dest = jnp.sum(onehot * pstarts[None, :], axis=1) + rank
    p = a + N_EXPERTS * MOE_BLOCK
    nblk = p // MOE_BLOCK
    buf_tok = jnp.zeros((p,), jnp.int32).at[dest].set(tok_f)
    n_active = pends[-1] // MOE_BLOCK
    blk_start = jnp.arange(nblk, dtype=jnp.int32) * MOE_BLOCK
    blk_e = jnp.minimum(jnp.sum(blk_start[:, None] >= pends[None, :], axis=1), N_EXPERTS - 1)
    blk_e = jnp.where(jnp.arange(nblk) < n_active, blk_e, blk_e[n_active - 1]).astype(jnp.int32)
    return buf_tok, dest.reshape(TOP_K, n_tokens), blk_e, n_active.astype(jnp.int32).reshape(1)


def _split_bf16(w):
    hi = w.astype(BF16)
    return jnp.stack([hi, (w - hi.astype(F32)).astype(BF16)])


def kernel(x, c, ctx, c_ctx, w_ada, b_ada, norm1_g, w_in, attn_q_norm, attn_k_norm, ret_decay_fwd,
           ret_decay_bwd, ret_gn_g, ret_gn_b, w_out, norm2_g, moe_w_grp, moe_b_grp, moe_w_exp, moe_b_exp,
           moe_w_gate, moe_w_up, moe_w_down, final_norm_g):
    b, l, d = x.shape
    cl = ctx.shape[1]
    assert b == 1 and w_ada.shape[0] == 1, "single batch element, single layer"
    x2, ctx2 = x[0], ctx[0]

    cvecs = jnp.zeros((SUBLANES, d), F32).at[0].set(c[0]).at[1].set(c_ctx)
    mod = _ada(cvecs, w_ada[0], b_ada[0])
    sh1, sc1, gt1, sh2, sc2, gt2 = [mod[0:1, i * d:(i + 1) * d] for i in range(6)]
    csh1, csc1 = mod[1:2, 0:d], mod[1:2, d:2 * d]

    w_in_bf = w_in[0].astype(BF16)
    head_of = jnp.arange(ATTN_WIDTH) // HEAD_DIM
    bd = jnp.where(head_of[:, None] == head_of[None, :], 1.0 / HEAD_DIM, 0.0).astype(BF16)
    gq = jnp.tile(attn_q_norm[0], ATTN_HEADS)[None, :]
    gk = jnp.tile(attn_k_norm[0], ATTN_KV_HEADS)[None, :]
    g1 = norm1_g[0][None, :]
    cos, sin = _rope_tables(l // GRID_W)
    qa, ka, va, qr, kr, vr, gr = _inproj(x2, sh1, sc1, g1, w_in_bf, bd, gq, gk, cos, sin)
    _, ka_c, va_c, _, kr_c, vr_c, _ = _inproj(ctx2, csh1, csc1, g1, w_in_bf, bd, gq, gk,
                                              jnp.ones((cl, LANES), F32), jnp.zeros((cl, LANES), F32))

    nk = cl + l
    tk = _key_chunk(nk)
    bound = (HEAD_DIM ** 0.5 * LOG2E * 1.01) * jnp.max(jnp.abs(attn_q_norm[0])) * jnp.max(jnp.abs(attn_k_norm[0]))
    bound = bound.astype(BF16)
    tq = min(ATTN_Q_TILE, l)
    q_t = qa.reshape(l // tq, tq, ATTN_KV_HEADS, GQA_GROUP, HEAD_DIM).transpose(2, 4, 0, 3, 1)
    q_t = q_t.reshape(ATTN_KV_HEADS, HEAD_DIM, GQA_GROUP * l)
    q_aug = jnp.zeros((ATTN_KV_HEADS, ATTN_AUG_DIM - HEAD_DIM, GQA_GROUP * l), BF16).at[:, 0, :].set(-bound)
    q_t = jnp.concatenate([q_t, q_aug], axis=1)
    k_all = jnp.concatenate([ka_c, ka], axis=0).reshape(nk // tk, tk, ATTN_KV_HEADS, HEAD_DIM)
    k_aug = jnp.zeros(k_all.shape[:3] + (ATTN_AUG_DIM - HEAD_DIM,), BF16).at[..., 0].set(1.0)
    k_all = jnp.concatenate([k_all, k_aug], axis=-1)
    v_all = jnp.concatenate([va_c, va], axis=0).reshape(nk // tk, tk, ATTN_KV_HEADS, HEAD_DIM)
    attn_args = (q_t, k_all.transpose(2, 0, 1, 3), v_all.transpose(2, 0, 3, 1))
    o_t = lax.cond(bound.astype(F32) <= MAX_UNSHIFTED_BOUND,
                   functools.partial(_attention, running_max=False),
                   functools.partial(_attention, running_max=True), *attn_args)
    attn = o_t.reshape(ATTN_KV_HEADS, HEAD_DIM, l // tq, GQA_GROUP, tq).transpose(2, 4, 0, 3, 1)
    attn = attn.reshape(l, ATTN_WIDTH)

    log_gf = jax.nn.log_sigmoid(ret_decay_fwd[0].astype(F32))
    log_gb = jax.nn.log_sigmoid(ret_decay_bwd[0].astype(F32))
    tab_f, gc_f = _retention_tables(log_gf, False)
    tab_b, gc_b = _retention_tables(log_gb, True)
    o_f = _retention_pass(gc_f, qr, kr, vr, kr_c, vr_c, tab_f, None, reverse=False)
    ret = _retention_pass(gc_b, qr, kr, vr, kr_c, vr_c, tab_b,
                          (o_f, gr, ret_gn_g[0][None, :], ret_gn_b[0][None, :]), reverse=True)

    wr = jnp.zeros((ROUTER_ROWS, d), F32)
    wr = wr.at[:N_GROUPS].set(moe_w_grp[0].T).at[EXPERT_ROW0:EXPERT_ROW0 + N_EXPERTS].set(moe_w_exp[0].T)
    br = jnp.zeros((ROUTER_ROWS, 1), F32)
    br = br.at[:N_GROUPS, 0].set(moe_b_grp[0]).at[EXPERT_ROW0:EXPERT_ROW0 + N_EXPERTS, 0].set(moe_b_exp[0])
    x_new, h2, ids, wts = _outproj_router(x2, attn, ret, w_out[0].astype(BF16), gt1, norm2_g[0][None, :],
                                          sh2, sc2, _split_bf16(wr), br)

    buf_tok, dest, blk_e, n_active = _dispatch(ids, l)
    yb = _moe_blocks(blk_e, n_active, buf_tok.reshape(-1, 1, MOE_BLOCK), h2, moe_w_gate[0], moe_w_up[0],
                     moe_w_down[0])
    out = _final(x_new, yb[dest[0]], yb[dest[1]], wts[:TOP_K].T, gt2, final_norm_g[None, :])
    return out[None]
```

```python
import functools

import jax
import jax.numpy as jnp
import numpy as np
from jax import lax
from jax.experimental import pallas as pl
from jax.experimental.pallas import tpu as pltpu

F32 = jnp.float32
BF16 = jnp.bfloat16

GRID_W = 64
HEAD_DIM = 64
ATTN_HEADS = 8
ATTN_KV_HEADS = 2
GQA_GROUP = ATTN_HEADS // ATTN_KV_HEADS
RET_HEADS = 4
RET_QK_DIM = 64
RET_V_DIM = 128
RET_CHUNK = 128
ATTN_WIDTH = ATTN_HEADS * HEAD_DIM
KV_WIDTH = ATTN_KV_HEADS * HEAD_DIM
RET_QK_WIDTH = RET_HEADS * RET_QK_DIM
RET_WIDTH = RET_HEADS * RET_V_DIM
PROJ_SIZES = (ATTN_WIDTH, KV_WIDTH, KV_WIDTH, RET_QK_WIDTH, RET_QK_WIDTH, RET_WIDTH, RET_WIDTH)
PROJ_OFFS = tuple(int(v) for v in np.cumsum((0,) + PROJ_SIZES))
PROJ_DIM = PROJ_OFFS[-1]
ROPE_THETA = 10000.0
ROPE_AXIS_DIM = HEAD_DIM // 2
N_GROUPS = 4
EXPERTS_PER_GROUP = 8
N_EXPERTS = N_GROUPS * EXPERTS_PER_GROUP
TOP_K = 2
D_EXPERT = 512
EPS = 1e-6
LOG2E = 1.4426950408889634
ATTN_AUG_DIM = 128
MAX_UNSHIFTED_BOUND = 60.0

LANES = 128
SUBLANES = 8
VMEM_LIMIT_BYTES = 56 * 1024 * 1024

ROW_TILE = 512
ATTN_Q_TILE = 512
RET_ROW_TILE = 1024
MOE_BLOCK = 256
ROUTER_ROWS = 128
EXPERT_ROW0 = 8


def _cparams(*sem):
    return pltpu.CompilerParams(dimension_semantics=sem, vmem_limit_bytes=VMEM_LIMIT_BYTES)


def _const_spec(shape):
    nd = len(shape)
    return pl.BlockSpec(shape, lambda *_: (0,) * nd)


def _ada_kernel(s_ref, w_ref, b_ref, o_ref):
    s = s_ref[...]
    s = s / (1.0 + jnp.exp(-s))
    o_ref[...] = jnp.dot(s, w_ref[...], preferred_element_type=F32,
                         precision=lax.Precision.HIGHEST) + b_ref[...]


def _ada(cvecs, w_ada, b_ada):
    d, n = w_ada.shape
    tn = 1536
    return pl.pallas_call(
        _ada_kernel,
        grid=(n // tn,),
        in_specs=[_const_spec((SUBLANES, d)),
                  pl.BlockSpec((d, tn), lambda j: (0, j)),
                  pl.BlockSpec((1, tn), lambda j: (0, j))],
        out_specs=pl.BlockSpec((SUBLANES, tn), lambda j: (0, j)),
        out_shape=jax.ShapeDtypeStruct((SUBLANES, n), F32),
        compiler_params=_cparams("arbitrary"),
        name="ada_mod",
    )(cvecs, w_ada, b_ada.reshape(1, n))


def _rms_modulate(x, g, sh, sc):
    ms = jnp.mean(x * x, axis=-1, keepdims=True)
    return (x * lax.rsqrt(ms + EPS)) * g * (1.0 + sc) + sh


def _head_mean_sq(v, bd):
    sq = v * v
    hi = sq.astype(BF16)
    lo = (sq - hi.astype(F32)).astype(BF16)
    return (jnp.dot(hi, bd, preferred_element_type=F32) + jnp.dot(lo, bd, preferred_element_type=F32))


def _rope_chunks(v, cos, sin, even):
    outs = []
    for c in range(v.shape[1] // LANES):
        xc = v[:, c * LANES:(c + 1) * LANES]
        nxt = pltpu.roll(xc, LANES - 1, 1)
        prv = pltpu.roll(xc, 1, 1)
        outs.append(xc * cos + jnp.where(even, nxt, prv) * sin)
    return outs


def _inproj_kernel(x_ref, sh_ref, sc_ref, g_ref, w_ref, bd_ref, gq_ref, gk_ref, cos_ref, sin_ref,
                   qa_ref, ka_ref, va_ref, qr_ref, kr_ref, vr_ref, gr_ref):
    h = _rms_modulate(x_ref[...], g_ref[...], sh_ref[...], sc_ref[...])
    proj = jnp.dot(h.astype(BF16), w_ref[...], preferred_element_type=F32)
    o = PROJ_OFFS
    qa, ka, va = proj[:, o[0]:o[1]], proj[:, o[1]:o[2]], proj[:, o[2]:o[3]]
    qr, kr, vr, gr = proj[:, o[3]:o[4]], proj[:, o[4]:o[5]], proj[:, o[5]:o[6]], proj[:, o[6]:o[7]]
    cos, sin = cos_ref[...], sin_ref[...]
    even = (lax.broadcasted_iota(jnp.int32, cos.shape, 1) % 2) == 0
    bd = bd_ref[...]
    scale = HEAD_DIM ** -0.5 * LOG2E

    qn = qa * lax.rsqrt(_head_mean_sq(qa, bd) + EPS) * gq_ref[...]
    for c, blk in enumerate(_rope_chunks(qn, cos, sin, even)):
        qa_ref[:, c * LANES:(c + 1) * LANES] = (blk * scale).astype(BF16)
    kn = ka * lax.rsqrt(_head_mean_sq(ka, bd[:KV_WIDTH, :KV_WIDTH]) + EPS) * gk_ref[...]
    ka_ref[...] = _rope_chunks(kn, cos, sin, even)[0].astype(BF16)
    va_ref[...] = va.astype(BF16)
    for c, blk in enumerate(_rope_chunks(qr, cos, sin, even)):
        qr_ref[:, c * LANES:(c + 1) * LANES] = blk.astype(BF16)
    rscale = RET_QK_DIM ** -0.5
    for c, blk in enumerate(_rope_chunks(kr, cos, sin, even)):
        kr_ref[:, c * LANES:(c + 1) * LANES] = (blk * rscale).astype(BF16)
    vr_ref[...] = vr.astype(BF16)
    gr_ref[...] = gr


def _inproj(x2, sh, sc, g1, w_bf, bd, gq, gk, cos, sin):
    rows, d = x2.shape
    tm = min(ROW_TILE, rows)
    row = lambda w: pl.BlockSpec((tm, w), lambda i: (i, 0))
    widths = PROJ_SIZES
    dtypes = (BF16,) * 6 + (F32,)
    return pl.pallas_call(
        _inproj_kernel,
        grid=(rows // tm,),
        in_specs=[row(d), _const_spec((1, d)), _const_spec((1, d)), _const_spec((1, d)),
                  _const_spec(w_bf.shape), _const_spec(bd.shape),
                  _const_spec((1, ATTN_WIDTH)), _const_spec((1, KV_WIDTH)),
                  row(LANES), row(LANES)],
        out_specs=[row(w) for w in widths],
        out_shape=[jax.ShapeDtypeStruct((rows, w), dt) for w, dt in zip(widths, dtypes)],
        compiler_params=_cparams("arbitrary"),
        name="norm_inproj",
    )(x2, sh, sc, g1, w_bf, bd, gq, gk, cos, sin)


def _attn_kernel(q_ref, k_ref, v_ref, o_ref, m_ref, l_ref, acc_ref, *, n_chunks, running_max):
    m_ref[...] = jnp.full(m_ref.shape, -jnp.inf, F32)
    l_ref[...] = jnp.zeros(l_ref.shape, F32)
    acc_ref[...] = jnp.zeros(acc_ref.shape, F32)

    def body(c, carry):
        kc = k_ref[0, c]
        vc = v_ref[0, c]
        s = jnp.dot(kc, q_ref[0], preferred_element_type=F32)
        if running_max:
            m_old = m_ref[...]
            m_new = jnp.maximum(m_old, jnp.max(s, axis=0, keepdims=True))
            alpha = jnp.exp2(m_old - m_new)
            p = jnp.exp2(s - m_new)
            l_ref[...] = alpha * l_ref[...] + jnp.sum(p, axis=0, keepdims=True)
            acc_ref[...] = acc_ref[...] * alpha + jnp.dot(vc, p.astype(BF16), preferred_element_type=F32)
            m_ref[...] = m_new
        else:
            p = jnp.exp2(s)
            l_ref[...] += jnp.sum(p, axis=0, keepdims=True)
            acc_ref[...] += jnp.dot(vc, p.astype(BF16), preferred_element_type=F32)
        return carry

    lax.fori_loop(0, n_chunks, body, 0)
    o_ref[0] = (acc_ref[...] / l_ref[...]).astype(o_ref.dtype)


def _key_chunk(nk):
    for tk in (3328, 1280, 640, 256, 128):
        if nk % tk == 0:
            return tk
    raise ValueError(f"unsupported key count {nk}")


def _attention(q_t, k_c, v_t, *, running_max):
    kv, da, gl = q_t.shape
    _, n_chunks, d, tk = v_t.shape
    nq = GQA_GROUP * min(ATTN_Q_TILE, gl // GQA_GROUP)
    return pl.pallas_call(
        functools.partial(_attn_kernel, n_chunks=n_chunks, running_max=running_max),
        grid=(kv, gl // nq),
        in_specs=[pl.BlockSpec((1, da, nq), lambda b, i: (b, 0, i)),
                  pl.BlockSpec((1, n_chunks, tk, da), lambda b, i: (b, 0, 0, 0)),
                  pl.BlockSpec((1, n_chunks, d, tk), lambda b, i: (b, 0, 0, 0))],
        out_specs=pl.BlockSpec((1, d, nq), lambda b, i: (b, 0, i)),
        out_shape=jax.ShapeDtypeStruct((kv, d, gl), BF16),
        scratch_shapes=[pltpu.VMEM((1, nq), F32), pltpu.VMEM((1, nq), F32), pltpu.VMEM((d, nq), F32)],
        compiler_params=_cparams("arbitrary", "arbitrary"),
        name="flash_attn",
    )(q_t, k_c, v_t)


def _ret_kernel(gc_ref, q_ref, k_ref, v_ref, kc_ref, vc_ref, tab_ref, *rest, reverse, final):
    if final:
        of_ref, gate_ref, gng_ref, gnb_ref, out_ref, s_ref = rest
    else:
        out_ref, s_ref = rest
    c_rows = RET_CHUNK
    n_local = q_ref.shape[0] // c_rows
    n_ctx = kc_ref.shape[0] // c_rows
    lane_head = lax.broadcasted_iota(jnp.int32, (c_rows, RET_QK_WIDTH), 1) // RET_QK_DIM

    def state_update(kch, vch):
        for h in range(RET_HEADS):
            vz = (vch[:, h * RET_V_DIM:(h + 1) * RET_V_DIM].astype(F32) * tab_ref[1, h]).astype(BF16)
            u = lax.dot_general(kch, vz, (((0,), (0,)), ((), ())), preferred_element_type=F32)
            s_ref[h] = gc_ref[h] * s_ref[h] + u

    @pl.when(pl.program_id(0) == 0)
    def _():
        s_ref[...] = jnp.zeros(s_ref.shape, F32)
        order = range(n_ctx - 1, -1, -1) if reverse else range(n_ctx)
        for cc in order:
            state_update(kc_ref[cc * c_rows:(cc + 1) * c_rows, :], vc_ref[cc * c_rows:(cc + 1) * c_rows, :])

    def chunk(ci, carry):
        c = (n_local - 1 - ci) if reverse else ci
        rows = pl.ds(pl.multiple_of(c * c_rows, c_rows), c_rows)
        q, k, v = q_ref[rows, :], k_ref[rows, :], v_ref[rows, :]
        for h in range(RET_HEADS):
            cols = slice(h * RET_V_DIM, (h + 1) * RET_V_DIM)
            qm = jnp.where(lane_head == h, q, jnp.zeros_like(q))
            sc = lax.dot_general(qm, k, (((1,), (1,)), ((), ())), preferred_element_type=F32)
            sc = sc * tab_ref[0, h]
            o = jnp.dot(sc.astype(BF16), v[:, cols], preferred_element_type=F32)
            o = o + jnp.dot(qm, s_ref[h].astype(BF16), preferred_element_type=F32) * tab_ref[2, h]
            if final:
                o = o + of_ref[rows, cols]
                mu = jnp.mean(o, axis=-1, keepdims=True)
                oc = o - mu
                var = jnp.mean(oc * oc, axis=-1, keepdims=True)
                on = oc * lax.rsqrt(var + EPS)
                gate = gate_ref[rows, cols]
                y = (on * gng_ref[:, cols] + gnb_ref[:, cols]) * (gate / (1.0 + jnp.exp(-gate)))
                out_ref[rows, cols] = y.astype(out_ref.dtype)
            else:
                out_ref[rows, cols] = o
        state_update(k, v)
        return carry

    lax.fori_loop(0, n_local, chunk, 0)


def _retention_pass(gc, qr, kr, vr, kr_c, vr_c, tabs, extra, *, reverse):
    l = qr.shape[0]
    tr = min(RET_ROW_TILE, l)
    nt = l // tr
    final = extra is not None
    idx = (lambda j, gc_: (nt - 1 - j, 0)) if reverse else (lambda j, gc_: (j, 0))
    const = lambda shape: pl.BlockSpec(shape, lambda j, gc_: (0,) * len(shape))
    in_specs = [pl.BlockSpec((tr, RET_QK_WIDTH), idx), pl.BlockSpec((tr, RET_QK_WIDTH), idx),
                pl.BlockSpec((tr, RET_WIDTH), idx),
                const(kr_c.shape), const(vr_c.shape), const(tabs.shape)]
    args = [qr, kr, vr, kr_c, vr_c, tabs]
    if final:
        of, gate, gng, gnb = extra
        in_specs += [pl.BlockSpec((tr, RET_WIDTH), idx), pl.BlockSpec((tr, RET_WIDTH), idx),
                     const(gng.shape), const(gnb.shape)]
        args += [of, gate, gng, gnb]
    return pl.pallas_call(
        functools.partial(_ret_kernel, reverse=reverse, final=final),
        grid_spec=pltpu.PrefetchScalarGridSpec(
            num_scalar_prefetch=1, grid=(nt,), in_specs=in_specs,
            out_specs=pl.BlockSpec((tr, RET_WIDTH), idx),
            scratch_shapes=[pltpu.VMEM((RET_HEADS, RET_QK_WIDTH, RET_V_DIM), F32)]),
        out_shape=jax.ShapeDtypeStruct((l, RET_WIDTH), BF16 if final else F32),
        compiler_params=_cparams("arbitrary"),
        name="retention_bwd" if reverse else "retention_fwd",
    )(gc, *args)


def _retention_tables(log_g, reverse):
    c = RET_CHUNK
    pos = jnp.arange(c, dtype=F32)
    diff = (pos[None, :] - pos[:, None]) if reverse else (pos[:, None] - pos[None, :])
    mask = (diff > 0) if reverse else (diff >= 0)
    lg = log_g[:, None, None]
    dmat = jnp.where(mask, jnp.exp(lg * jnp.where(mask, diff, 0.0)[None]), 0.0)
    zeta = jnp.exp(log_g[:, None] * (pos if reverse else (c - 1 - pos))[None, :])
    xi = jnp.exp(log_g[:, None] * ((c - pos) if reverse else (pos + 1.0))[None, :])
    ones = jnp.ones((1, 1, c), F32)
    tabs = jnp.stack([dmat, zeta[:, :, None] * ones, xi[:, :, None] * ones])
    return tabs, jnp.exp(log_g * c)


def _outproj_kernel(x_ref, a_ref, r_ref, w_ref, gt_ref, g2_ref, sh_ref, sc_ref, wr_ref, br_ref,
                    xn_ref, h2_ref, ids_ref, wts_ref):
    half = a_ref.shape[1]
    mix = (jnp.dot(a_ref[...], w_ref[:half, :], preferred_element_type=F32)
           + jnp.dot(r_ref[...], w_ref[half:, :], preferred_element_type=F32))
    x_new = x_ref[...] + gt_ref[...] * mix
    xn_ref[...] = x_new
    h2 = _rms_modulate(x_new, g2_ref[...], sh_ref[...], sc_ref[...])
    hi = h2.astype(BF16)
    hi32 = hi.astype(F32)
    lo = (h2 - hi32).astype(BF16)
    bits = lax.bitcast_convert_type(hi32, jnp.uint32)
    dh = bits.shape[1] // 2
    h2_ref[...] = (bits[:, dh:] & jnp.uint32(0xFFFF0000)) | (bits[:, :dh] >> 16)

    nt = (((1,), (1,)), ((), ()))
    logits = (lax.dot_general(wr_ref[0], hi, nt, preferred_element_type=F32)
              + lax.dot_general(wr_ref[0], lo, nt, preferred_element_type=F32)
              + lax.dot_general(wr_ref[1], hi, nt, preferred_element_type=F32)) + br_ref[...]
    rows = [logits[g:g + 1, :] for g in range(N_GROUPS)]
    gmax = functools.reduce(jnp.maximum, rows)
    gidx = jnp.full(gmax.shape, N_GROUPS - 1, jnp.int32)
    for g in range(N_GROUPS - 2, -1, -1):
        gidx = jnp.where(rows[g] == gmax, g, gidx)
    p_g = 1.0 / functools.reduce(jnp.add, [jnp.exp(r - gmax) for r in rows])
    sel = logits[EXPERT_ROW0 + (N_GROUPS - 1) * EXPERTS_PER_GROUP:EXPERT_ROW0 + N_GROUPS * EXPERTS_PER_GROUP, :]
    for g in range(N_GROUPS - 2, -1, -1):
        blk = logits[EXPERT_ROW0 + g * EXPERTS_PER_GROUP:EXPERT_ROW0 + (g + 1) * EXPERTS_PER_GROUP, :]
        sel = jnp.where(gidx == g, blk, sel)
    ridx = lax.broadcasted_iota(jnp.int32, sel.shape, 0)
    m1 = jnp.max(sel, axis=0, keepdims=True)
    i1 = jnp.min(jnp.where(sel == m1, ridx, EXPERTS_PER_GROUP), axis=0, keepdims=True)
    sel2 = jnp.where(ridx == i1, -jnp.inf, sel)
    m2 = jnp.max(sel2, axis=0, keepdims=True)
    i2 = jnp.min(jnp.where(sel2 == m2, ridx, EXPERTS_PER_GROUP), axis=0, keepdims=True)
    e2 = jnp.exp(m2 - m1)
    w1 = p_g / (1.0 + e2)
    w2 = w1 * e2
    out_rows = lax.broadcasted_iota(jnp.int32, ids_ref.shape, 0)
    ids_ref[...] = jnp.where(out_rows == 0, gidx * EXPERTS_PER_GROUP + i1,
                             jnp.where(out_rows == 1, gidx * EXPERTS_PER_GROUP + i2, 0))
    wts_ref[...] = jnp.where(out_rows == 0, w1, jnp.where(out_rows == 1, w2, 0.0))


def _outproj_router(x2, attn, ret, w_out_bf, gt1, g2, sh2, sc2, wr, br):
    l, d = x2.shape
    tm = min(ROW_TILE, l)
    row = lambda w: pl.BlockSpec((tm, w), lambda i: (i, 0))
    col = pl.BlockSpec((SUBLANES, tm), lambda i: (0, i))
    return pl.pallas_call(
        _outproj_kernel,
        grid=(l // tm,),
        in_specs=[row(d), row(attn.shape[1]), row(ret.shape[1]), _const_spec(w_out_bf.shape),
                  _const_spec((1, d)), _const_spec((1, d)), _const_spec((1, d)), _const_spec((1, d)),
                  _const_spec(wr.shape), _const_spec(br.shape)],
        out_specs=[row(d), row(d // 2), col, col],
        out_shape=[jax.ShapeDtypeStruct((l, d), F32), jax.ShapeDtypeStruct((l, d // 2), jnp.uint32),
                   jax.ShapeDtypeStruct((SUBLANES, l), jnp.int32),
                   jax.ShapeDtypeStruct((SUBLANES, l), F32)],
        compiler_params=_cparams("arbitrary"),
        name="outproj_router",
    )(x2, attn, ret, w_out_bf, gt1, g2, sh2, sc2, wr, br)


def _gather_token_rows(tok_ref, h_ref, xg_s, slot, r0):
    rows = [h_ref[pl.ds(tok_ref[0, 0, r0 + j], 1), :] for j in range(SUBLANES)]
    xg_s[slot, pl.ds(r0, SUBLANES), :] = jnp.concatenate(rows, axis=0)


def _moe_kernel(be_ref, na_ref, tok0_ref, tokn_ref, h_ref, wg_ref, wu_ref, wd_ref, y_ref,
                wg_s, wu_s, wd_s, xg_s):
    b = pl.program_id(0)
    slot = b % 2
    active = b < na_ref[0]
    changed = jnp.logical_or(b == 0, be_ref[b] != be_ref[jnp.maximum(b - 1, 0)])

    @pl.when(b == 0)
    def _():
        def first_block(i, carry):
            _gather_token_rows(tok0_ref, h_ref, xg_s, 0, pl.multiple_of(i * SUBLANES, SUBLANES))
            return carry

        lax.fori_loop(0, MOE_BLOCK // SUBLANES, first_block, 0, unroll=2)

    @pl.when(jnp.logical_and(active, changed))
    def _():
        wg_s[...] = wg_ref[0].astype(BF16)
        wu_s[...] = wu_ref[0].astype(BF16)
        wd_s[...] = wd_ref[0].astype(BF16)

    @pl.when(active)
    def _():
        bits = xg_s[slot]
        dh = bits.shape[1]
        x_lo = lax.bitcast_convert_type(bits << 16, F32).astype(BF16)
        x_hi = lax.bitcast_convert_type(bits & jnp.uint32(0xFFFF0000), F32).astype(BF16)
        hg = (jnp.dot(x_lo, wg_s[:dh, :], preferred_element_type=F32)
              + jnp.dot(x_hi, wg_s[dh:, :], preferred_element_type=F32))
        hu = (jnp.dot(x_lo, wu_s[:dh, :], preferred_element_type=F32)
              + jnp.dot(x_hi, wu_s[dh:, :], preferred_element_type=F32))
        a = (hg / (1.0 + jnp.exp(-hg))) * hu
        y_ref[...] = jnp.dot(a.astype(BF16), wd_s[...], preferred_element_type=F32)
        for g in range(MOE_BLOCK // SUBLANES):
            _gather_token_rows(tokn_ref, h_ref, xg_s, 1 - slot, g * SUBLANES)

    @pl.when(jnp.logical_not(active))
    def _():
        y_ref[...] = jnp.zeros(y_ref.shape, y_ref.dtype)


def _moe_blocks(blk_e, n_active, buf_tok, h2p, w_gate, w_up, w_down):
    l, dh = h2p.shape
    ne, d, de = w_gate.shape
    nblk = buf_tok.shape[0]
    wmap = lambda b, be, na: (be[b], 0, 0)
    return pl.pallas_call(
        _moe_kernel,
        grid_spec=pltpu.PrefetchScalarGridSpec(
            num_scalar_prefetch=2, grid=(nblk,),
            in_specs=[pl.BlockSpec((1, 1, MOE_BLOCK), lambda b, be, na: (0, 0, 0), memory_space=pltpu.SMEM),
                      pl.BlockSpec((1, 1, MOE_BLOCK), lambda b, be, na: (jnp.minimum(b + 1, nblk - 1), 0, 0),
                                   memory_space=pltpu.SMEM),
                      pl.BlockSpec((l, dh), lambda b, be, na: (0, 0), pipeline_mode=pl.Buffered(1)),
                      pl.BlockSpec((1, d, de), wmap), pl.BlockSpec((1, d, de), wmap),
                      pl.BlockSpec((1, de, d), wmap)],
            out_specs=pl.BlockSpec((MOE_BLOCK, d), lambda b, be, na: (b, 0)),
            scratch_shapes=[pltpu.VMEM((d, de), BF16), pltpu.VMEM((d, de), BF16),
                            pltpu.VMEM((de, d), BF16), pltpu.VMEM((2, MOE_BLOCK, dh), jnp.uint32)]),
        out_shape=jax.ShapeDtypeStruct((nblk * MOE_BLOCK, d), F32),
        compiler_params=_cparams("arbitrary"),
        name="moe_experts",
    )(blk_e, n_active, buf_tok, buf_tok, h2p, w_gate, w_up, w_down)


def _final_kernel(x_ref, y1_ref, y2_ref, w_ref, gt_ref, g_ref, o_ref):
    w = w_ref[...]
    y = y1_ref[...] * w[:, 0:1] + y2_ref[...] * w[:, 1:2]
    x = x_ref[...] + gt_ref[...] * y
    ms = jnp.mean(x * x, axis=-1, keepdims=True)
    o_ref[...] = (x * lax.rsqrt(ms + EPS)) * g_ref[...]


def _final(x_new, y1, y2, wts, gt2, gfin):
    l, d = x_new.shape
    tm = min(ROW_TILE, l)
    row = lambda w: pl.BlockSpec((tm, w), lambda i: (i, 0))
    return pl.pallas_call(
        _final_kernel,
        grid=(l // tm,),
        in_specs=[row(d), row(d), row(d), row(wts.shape[1]), _const_spec((1, d)), _const_spec((1, d))],
        out_specs=row(d),
        out_shape=jax.ShapeDtypeStruct((l, d), F32),
        compiler_params=_cparams("arbitrary"),
        name="combine_final_norm",
    )(x_new, y1, y2, wts, gt2, gfin)


def _rope_tables(rows_count):
    row = jnp.repeat(jnp.arange(rows_count), GRID_W).astype(F32)
    col = jnp.tile(jnp.arange(GRID_W), rows_count).astype(F32)
    freqs = ROPE_THETA ** (-jnp.arange(0, ROPE_AXIS_DIM, 2, dtype=F32) / ROPE_AXIS_DIM)
    ang = jnp.concatenate([row[:, None] * freqs, col[:, None] * freqs], axis=-1)
    ang = jnp.repeat(ang, 2, axis=-1)
    sign = jnp.where(jnp.arange(HEAD_DIM) % 2 == 0, -1.0, 1.0).astype(F32)
    cos = jnp.tile(jnp.cos(ang), (1, LANES // HEAD_DIM))
    sin = jnp.tile(jnp.sin(ang) * sign, (1, LANES // HEAD_DIM))
    return cos, sin


def _dispatch(ids, n_tokens):
    a = TOP_K * n_tokens
    eid_f = ids[:TOP_K].reshape(a)
    tok_f = jnp.tile(jnp.arange(n_tokens, dtype=jnp.int32), TOP_K)
    onehot = (eid_f[:, None] == jnp.arange(N_EXPERTS, dtype=jnp.int32)[None, :]).astype(jnp.int32)
    incl = jnp.cumsum(onehot, axis=0)
    counts = incl[-1]
    rank = jnp.sum((incl - onehot) * onehot, axis=1)
    pcounts = (counts + MOE_BLOCK - 1) // MOE_BLOCK * MOE_BLOCK
    pends = jnp.cumsum(pcounts)
    pstarts = pends - pcounts
    dest = jnp.sum(onehot * pstarts[None, :], axis=1) + rank
    p = a + N_EXPERTS * MOE_BLOCK
    nblk = p // MOE_BLOCK
    buf_tok = jnp.zeros((p,), jnp.int32).at[dest].set(tok_f)
    n_active = pends[-1] // MOE_BLOCK
    blk_start = jnp.arange(nblk, dtype=jnp.int32) * MOE_BLOCK
    blk_e = jnp.minimum(jnp.sum(blk_start[:, None] >= pends[None, :], axis=1), N_EXPERTS - 1)
    blk_e = jnp.where(jnp.arange(nblk) < n_active, blk_e, blk_e[n_active - 1]).astype(jnp.int32)
    return buf_tok, dest.reshape(TOP_K, n_tokens), blk_e, n_active.astype(jnp.int32).reshape(1)


def _split_bf16(w):
    hi = w.astype(BF16)
    return jnp.stack([hi, (w - hi.astype(F32)).astype(BF16)])


def kernel(x, c, ctx, c_ctx, w_ada, b_ada, norm1_g, w_in, attn_q_norm, attn_k_norm, ret_decay_fwd,
           ret_decay_bwd, ret_gn_g, ret_gn_b, w_out, norm2_g, moe_w_grp, moe_b_grp, moe_w_exp, moe_b_exp,
           moe_w_gate, moe_w_up, moe_w_down, final_norm_g):
    b, l, d = x.shape
    cl = ctx.shape[1]
    assert b == 1 and w_ada.shape[0] == 1, "single batch element, single layer"
    x2, ctx2 = x[0], ctx[0]

    cvecs = jnp.zeros((SUBLANES, d), F32).at[0].set(c[0]).at[1].set(c_ctx)
    mod = _ada(cvecs, w_ada[0], b_ada[0])
    sh1, sc1, gt1, sh2, sc2, gt2 = [mod[0:1, i * d:(i + 1) * d] for i in range(6)]
    csh1, csc1 = mod[1:2, 0:d], mod[1:2, d:2 * d]

    w_in_bf = w_in[0].astype(BF16)
    head_of = jnp.arange(ATTN_WIDTH) // HEAD_DIM
    bd = jnp.where(head_of[:, None] == head_of[None, :], 1.0 / HEAD_DIM, 0.0).astype(BF16)
    gq = jnp.tile(attn_q_norm[0], ATTN_HEADS)[None, :]
    gk = jnp.tile(attn_k_norm[0], ATTN_KV_HEADS)[None, :]
    g1 = norm1_g[0][None, :]
    cos, sin = _rope_tables(l // GRID_W)
    qa, ka, va, qr, kr, vr, gr = _inproj(x2, sh1, sc1, g1, w_in_bf, bd, gq, gk, cos, sin)
    _, ka_c, va_c, _, kr_c, vr_c, _ = _inproj(ctx2, csh1, csc1, g1, w_in_bf, bd, gq, gk,
                                              jnp.ones((cl, LANES), F32), jnp.zeros((cl, LANES), F32))

    nk = cl + l
    tk = _key_chunk(nk)
    bound = (HEAD_DIM ** 0.5 * LOG2E * 1.01) * jnp.max(jnp.abs(attn_q_norm[0])) * jnp.max(jnp.abs(attn_k_norm[0]))
    bound = bound.astype(BF16)
    tq = min(ATTN_Q_TILE, l)
    q_t = qa.reshape(l // tq, tq, ATTN_KV_HEADS, GQA_GROUP, HEAD_DIM).transpose(2, 4, 0, 3, 1)
    q_t = q_t.reshape(ATTN_KV_HEADS, HEAD_DIM, GQA_GROUP * l)
    q_aug = jnp.zeros((ATTN_KV_HEADS, ATTN_AUG_DIM - HEAD_DIM, GQA_GROUP * l), BF16).at[:, 0, :].set(-bound)
    q_t = jnp.concatenate([q_t, q_aug], axis=1)
    k_all = jnp.concatenate([ka_c, ka], axis=0).reshape(nk // tk, tk, ATTN_KV_HEADS, HEAD_DIM)
    k_aug = jnp.zeros(k_all.shape[:3] + (ATTN_AUG_DIM - HEAD_DIM,), BF16).at[..., 0].set(1.0)
    k_all = jnp.concatenate([k_all, k_aug], axis=-1)
    v_all = jnp.concatenate([va_c, va], axis=0).reshape(nk // tk, tk, ATTN_KV_HEADS, HEAD_DIM)
    attn_args = (q_t, k_all.transpose(2, 0, 1, 3), v_all.transpose(2, 0, 3, 1))
    o_t = lax.cond(bound.astype(F32) <= MAX_UNSHIFTED_BOUND,
                   functools.partial(_attention, running_max=False),
                   functools.partial(_attention, running_max=True), *attn_args)
    attn = o_t.reshape(ATTN_KV_HEADS, HEAD_DIM, l // tq, GQA_GROUP, tq).transpose(2, 4, 0, 3, 1)
    attn = attn.reshape(l, ATTN_WIDTH)

    log_gf = jax.nn.log_sigmoid(ret_decay_fwd[0].astype(F32))
    log_gb = jax.nn.log_sigmoid(ret_decay_bwd[0].astype(F32))
    tab_f, gc_f = _retention_tables(log_gf, False)
    tab_b, gc_b = _retention_tables(log_gb, True)
    o_f = _retention_pass(gc_f, qr, kr, vr, kr_c, vr_c, tab_f, None, reverse=False)
    ret = _retention_pass(gc_b, qr, kr, vr, kr_c, vr_c, tab_b,
                          (o_f, gr, ret_gn_g[0][None, :], ret_gn_b[0][None, :]), reverse=True)

    wr = jnp.zeros((ROUTER_ROWS, d), F32)
    wr = wr.at[:N_GROUPS].set(moe_w_grp[0].T).at[EXPERT_ROW0:EXPERT_ROW0 + N_EXPERTS].set(moe_w_exp[0].T)
    br = jnp.zeros((ROUTER_ROWS, 1), F32)
    br = br.at[:N_GROUPS, 0].set(moe_b_grp[0]).at[EXPERT_ROW0:EXPERT_ROW0 + N_EXPERTS, 0].set(moe_b_exp[0])
    x_new, h2, ids, wts = _outproj_router(x2, attn, ret, w_out[0].astype(BF16), gt1, norm2_g[0][None, :],
                                          sh2, sc2, _split_bf16(wr), br)

    buf_tok, dest, blk_e, n_active = _dispatch(ids, l)
    yb = _moe_blocks(blk_e, n_active, buf_tok.reshape(-1, 1, MOE_BLOCK), h2, moe_w_gate[0], moe_w_up[0],
                     moe_w_down[0])
    out = _final(x_new, yb[dest[0]], yb[dest[1]], wts[:TOP_K].T, gt2, final_norm_g[None, :])
    return out[None]
```

```python
import functools

import jax
import jax.numpy as jnp
import numpy as np
from jax import lax
from jax.experimental import pallas as pl
from jax.experimental.pallas import tpu as pltpu

F32 = jnp.float32
BF16 = jnp.bfloat16

GRID_W = 64
HEAD_DIM = 64
ATTN_HEADS = 8
ATTN_KV_HEADS = 2
GQA_GROUP = ATTN_HEADS // ATTN_KV_HEADS
RET_HEADS = 4
RET_QK_DIM = 64
RET_V_DIM = 128
RET_CHUNK = 128
ATTN_WIDTH = ATTN_HEADS * HEAD_DIM
KV_WIDTH = ATTN_KV_HEADS * HEAD_DIM
RET_QK_WIDTH = RET_HEADS * RET_QK_DIM
RET_WIDTH = RET_HEADS * RET_V_DIM
PROJ_SIZES = (ATTN_WIDTH, KV_WIDTH, KV_WIDTH, RET_QK_WIDTH, RET_QK_WIDTH, RET_WIDTH, RET_WIDTH)
PROJ_OFFS = tuple(int(v) for v in np.cumsum((0,) + PROJ_SIZES))
PROJ_DIM = PROJ_OFFS[-1]
ROPE_THETA = 10000.0
ROPE_AXIS_DIM = HEAD_DIM // 2
N_GROUPS = 4
EXPERTS_PER_GROUP = 8
N_EXPERTS = N_GROUPS * EXPERTS_PER_GROUP
TOP_K = 2
D_EXPERT = 512
EPS = 1e-6
LOG2E = 1.4426950408889634
ATTN_AUG_DIM = 128
MAX_UNSHIFTED_BOUND = 60.0

LANES = 128
SUBLANES = 8
VMEM_LIMIT_BYTES = 56 * 1024 * 1024

ROW_TILE = 512
ATTN_Q_TILE = 512
RET_ROW_TILE = 1024
MOE_BLOCK = 256
ROUTER_ROWS = 128
EXPERT_ROW0 = 8


def _cparams(*sem):
    return pltpu.CompilerParams(dimension_semantics=sem, vmem_limit_bytes=VMEM_LIMIT_BYTES)


def _const_spec(shape):
    nd = len(shape)
    return pl.BlockSpec(shape, lambda *_: (0,) * nd)


def _ada_kernel(s_ref, w_ref, b_ref, o_ref):
    s = s_ref[...]
    s = s / (1.0 + jnp.exp(-s))
    o_ref[...] = jnp.dot(s, w_ref[...], preferred_element_type=F32,
                         precision=lax.Precision.HIGHEST) + b_ref[...]


def _ada(cvecs, w_ada, b_ada):
    d, n = w_ada.shape
    tn = 1536
    return pl.pallas_call(
        _ada_kernel,
        grid=(n // tn,),
        in_specs=[_const_spec((SUBLANES, d)),
                  pl.BlockSpec((d, tn), lambda j: (0, j)),
                  pl.BlockSpec((1, tn), lambda j: (0, j))],
        out_specs=pl.BlockSpec((SUBLANES, tn), lambda j: (0, j)),
        out_shape=jax.ShapeDtypeStruct((SUBLANES, n), F32),
        compiler_params=_cparams("arbitrary"),
        name="ada_mod",
    )(cvecs, w_ada, b_ada.reshape(1, n))


def _rms_modulate(x, g, sh, sc):
    ms = jnp.mean(x * x, axis=-1, keepdims=True)
    return (x * lax.rsqrt(ms + EPS)) * g * (1.0 + sc) + sh


def _head_mean_sq(v, bd):
    sq = v * v
    hi = sq.astype(BF16)
    lo = (sq - hi.astype(F32)).astype(BF16)
    return (jnp.dot(hi, bd, preferred_element_type=F32) + jnp.dot(lo, bd, preferred_element_type=F32))


def _rope_chunks(v, cos, sin, even):
    outs = []
    for c in range(v.shape[1] // LANES):
        xc = v[:, c * LANES:(c + 1) * LANES]
        nxt = pltpu.roll(xc, LANES - 1, 1)
        prv = pltpu.roll(xc, 1, 1)
        outs.append(xc * cos + jnp.where(even, nxt, prv) * sin)
    return outs


def _inproj_kernel(x_ref, sh_ref, sc_ref, g_ref, w_ref, bd_ref, gq_ref, gk_ref, cos_ref, sin_ref, nb_ref,
                   qt_ref, ka_ref, va_ref, qr_ref, kr_ref, vr_ref, gr_ref):
    h = _rms_modulate(x_ref[...], g_ref[...], sh_ref[...], sc_ref[...])
    proj = jnp.dot(h.astype(BF16), w_ref[...], preferred_element_type=F32)
    o = PROJ_OFFS
    qa, ka, va = proj[:, o[0]:o[1]], proj[:, o[1]:o[2]], proj[:, o[2]:o[3]]
    qr, kr, vr, gr = proj[:, o[3]:o[4]], proj[:, o[4]:o[5]], proj[:, o[5]:o[6]], proj[:, o[6]:o[7]]
    cos, sin = cos_ref[...], sin_ref[...]
    even = (lax.broadcasted_iota(jnp.int32, cos.shape, 1) % 2) == 0
    bd = bd_ref[...]
    scale = HEAD_DIM ** -0.5 * LOG2E

    qn = qa * lax.rsqrt(_head_mean_sq(qa, bd) + EPS) * gq_ref[...]
    tm = qa.shape[0]
    for c, blk in enumerate(_rope_chunks(qn, cos, sin, even)):
        blk_t = (blk * scale).T
        for j in range(LANES // HEAD_DIM):
            kvh, grp = divmod(c * (LANES // HEAD_DIM) + j, GQA_GROUP)
            qt_ref[kvh, 0:HEAD_DIM, grp * tm:(grp + 1) * tm] = blk_t[j * HEAD_DIM:(j + 1) * HEAD_DIM, :].astype(BF16)
    aug_shape = (ATTN_AUG_DIM - HEAD_DIM, GQA_GROUP * tm)
    aug_row = lax.broadcasted_iota(jnp.int32, aug_shape, 0)
    aug = jnp.where(aug_row == 0, nb_ref[0:1, 0:1], 0.0).astype(BF16)
    for kvh in range(ATTN_KV_HEADS):
        qt_ref[kvh, HEAD_DIM:ATTN_AUG_DIM, :] = aug
    kn = ka * lax.rsqrt(_head_mean_sq(ka, bd[:KV_WIDTH, :KV_WIDTH]) + EPS) * gk_ref[...]
    ka_ref[...] = _rope_chunks(kn, cos, sin, even)[0].astype(BF16)
    va_ref[...] = va.astype(BF16)
    for c, blk in enumerate(_rope_chunks(qr, cos, sin, even)):
        qr_ref[:, c * LANES:(c + 1) * LANES] = blk.astype(BF16)
    rscale = RET_QK_DIM ** -0.5
    for c, blk in enumerate(_rope_chunks(kr, cos, sin, even)):
        kr_ref[:, c * LANES:(c + 1) * LANES] = (blk * rscale).astype(BF16)
    vr_ref[...] = vr.astype(BF16)
    gr_ref[...] = gr


def _inproj(x2, sh, sc, g1, w_bf, bd, gq, gk, cos, sin, neg_bound):
    rows, d = x2.shape
    tm = min(ROW_TILE, rows)
    row = lambda w: pl.BlockSpec((tm, w), lambda i: (i, 0))
    widths = PROJ_SIZES[1:]
    dtypes = (BF16,) * 5 + (F32,)
    qt_shape = (ATTN_KV_HEADS, ATTN_AUG_DIM, GQA_GROUP * rows)
    return pl.pallas_call(
        _inproj_kernel,
        grid=(rows // tm,),
        in_specs=[row(d), _const_spec((1, d)), _const_spec((1, d)), _const_spec((1, d)),
                  _const_spec(w_bf.shape), _const_spec(bd.shape),
                  _const_spec((1, ATTN_WIDTH)), _const_spec((1, KV_WIDTH)),
                  row(LANES), row(LANES), _const_spec((1, LANES))],
        out_specs=[pl.BlockSpec((ATTN_KV_HEADS, ATTN_AUG_DIM, GQA_GROUP * tm), lambda i: (0, 0, i))]
        + [row(w) for w in widths],
        out_shape=[jax.ShapeDtypeStruct(qt_shape, BF16)]
        + [jax.ShapeDtypeStruct((rows, w), dt) for w, dt in zip(widths, dtypes)],
        compiler_params=_cparams("arbitrary"),
        name="norm_inproj",
    )(x2, sh, sc, g1, w_bf, bd, gq, gk, cos, sin, neg_bound)


def _attn_kernel(q_ref, k_ref, v_ref, o_ref, m_ref, l_ref, acc_ref, *, n_chunks, running_max):
    m_ref[...] = jnp.full(m_ref.shape, -jnp.inf, F32)
    l_ref[...] = jnp.zeros(l_ref.shape, F32)
    acc_ref[...] = jnp.zeros(acc_ref.shape, F32)

    def body(c, carry):
        kc = k_ref[0, c]
        vc = v_ref[0, c]
        s = jnp.dot(kc, q_ref[0], preferred_element_type=F32)
        if running_max:
            m_old = m_ref[...]
            m_new = jnp.maximum(m_old, jnp.max(s, axis=0, keepdims=True))
            alpha = jnp.exp2(m_old - m_new)
            p = jnp.exp2(s - m_new)
            l_ref[...] = alpha * l_ref[...] + jnp.sum(p, axis=0, keepdims=True)
            acc_ref[...] = acc_ref[...] * alpha + jnp.dot(vc, p.astype(BF16), preferred_element_type=F32)
            m_ref[...] = m_new
        else:
            p = jnp.exp2(s)
            l_ref[...] += jnp.sum(p, axis=0, keepdims=True)
            acc_ref[...] += jnp.dot(vc, p.astype(BF16), preferred_element_type=F32)
        return carry

    lax.fori_loop(0, n_chunks, body, 0)
    o_ref[0] = (acc_ref[...] / l_ref[...]).astype(o_ref.dtype)


def _key_chunk(nk):
    for tk in (3328, 1280, 640, 256, 128):
        if nk % tk == 0:
            return tk
    raise ValueError(f"unsupported key count {nk}")


def _attention(q_t, k_c, v_t, *, running_max):
    kv, da, gl = q_t.shape
    _, n_chunks, d, tk = v_t.shape
    nq = GQA_GROUP * min(ATTN_Q_TILE, gl // GQA_GROUP)
    return pl.pallas_call(
        functools.partial(_attn_kernel, n_chunks=n_chunks, running_max=running_max),
        grid=(kv, gl // nq),
        in_specs=[pl.BlockSpec((1, da, nq), lambda b, i: (b, 0, i)),
                  pl.BlockSpec((1, n_chunks, tk, da), lambda b, i: (b, 0, 0, 0)),
                  pl.BlockSpec((1, n_chunks, d, tk), lambda b, i: (b, 0, 0, 0))],
        out_specs=pl.BlockSpec((1, d, nq), lambda b, i: (b, 0, i)),
        out_shape=jax.ShapeDtypeStruct((kv, d, gl), BF16),
        scratch_shapes=[pltpu.VMEM((1, nq), F32), pltpu.VMEM((1, nq), F32), pltpu.VMEM((d, nq), F32)],
        compiler_params=_cparams("arbitrary", "arbitrary"),
        name="flash_attn",
    )(q_t, k_c, v_t)


def _ret_kernel(gc_ref, q_ref, k_ref, v_ref, kc_ref, vc_ref, tab_ref, *rest, reverse, final):
    if final:
        of_ref, gate_ref, gng_ref, gnb_ref, out_ref, s_ref = rest
    else:
        out_ref, s_ref = rest
    c_rows = RET_CHUNK
    n_local = q_ref.shape[0] // c_rows
    n_ctx = kc_ref.shape[0] // c_rows
    lane_head = lax.broadcasted_iota(jnp.int32, (c_rows, RET_QK_WIDTH), 1) // RET_QK_DIM

    def state_update(kch, vch):
        for h in range(RET_HEADS):
            vz = (vch[:, h * RET_V_DIM:(h + 1) * RET_V_DIM].astype(F32) * tab_ref[1, h]).astype(BF16)
            u = lax.dot_general(kch, vz, (((0,), (0,)), ((), ())), preferred_element_type=F32)
            s_ref[h] = gc_ref[h] * s_ref[h] + u

    @pl.when(pl.program_id(0) == 0)
    def _():
        s_ref[...] = jnp.zeros(s_ref.shape, F32)
        order = range(n_ctx - 1, -1, -1) if reverse else range(n_ctx)
        for cc in order:
            state_update(kc_ref[cc * c_rows:(cc + 1) * c_rows, :], vc_ref[cc * c_rows:(cc + 1) * c_rows, :])

    def chunk(ci, carry):
        c = (n_local - 1 - ci) if reverse else ci
        rows = pl.ds(pl.multiple_of(c * c_rows, c_rows), c_rows)
        q, k, v = q_ref[rows, :], k_ref[rows, :], v_ref[rows, :]
        for h in range(RET_HEADS):
            cols = slice(h * RET_V_DIM, (h + 1) * RET_V_DIM)
            qm = jnp.where(lane_head == h, q, jnp.zeros_like(q))
            sc = lax.dot_general(qm, k, (((1,), (1,)), ((), ())), preferred_element_type=F32)
            sc = sc * tab_ref[0, h]
            o = jnp.dot(sc.astype(BF16), v[:, cols], preferred_element_type=F32)
            o = o + jnp.dot(qm, s_ref[h].astype(BF16), preferred_element_type=F32) * tab_ref[2, h]
            if final:
                o = o + of_ref[rows, cols]
                mu = jnp.mean(o, axis=-1, keepdims=True)
                oc = o - mu
                var = jnp.mean(oc * oc, axis=-1, keepdims=True)
                on = oc * lax.rsqrt(var + EPS)
                gate = gate_ref[rows, cols]
                y = (on * gng_ref[:, cols] + gnb_ref[:, cols]) * (gate / (1.0 + jnp.exp(-gate)))
                out_ref[rows, cols] = y.astype(out_ref.dtype)
            else:
                out_ref[rows, cols] = o
        state_update(k, v)
        return carry

    lax.fori_loop(0, n_local, chunk, 0)


def _retention_pass(gc, qr, kr, vr, kr_c, vr_c, tabs, extra, *, reverse):
    l = qr.shape[0]
    tr = min(RET_ROW_TILE, l)
    nt = l // tr
    final = extra is not None
    idx = (lambda j, gc_: (nt - 1 - j, 0)) if reverse else (lambda j, gc_: (j, 0))
    const = lambda shape: pl.BlockSpec(shape, lambda j, gc_: (0,) * len(shape))
    in_specs = [pl.BlockSpec((tr, RET_QK_WIDTH), idx), pl.BlockSpec((tr, RET_QK_WIDTH), idx),
                pl.BlockSpec((tr, RET_WIDTH), idx),
                const(kr_c.shape), const(vr_c.shape), const(tabs.shape)]
    args = [qr, kr, vr, kr_c, vr_c, tabs]
    if final:
        of, gate, gng, gnb = extra
        in_specs += [pl.BlockSpec((tr, RET_WIDTH), idx), pl.BlockSpec((tr, RET_WIDTH), idx),
                     const(gng.shape), const(gnb.shape)]
        args += [of, gate, gng, gnb]
    return pl.pallas_call(
        functools.partial(_ret_kernel, reverse=reverse, final=final),
        grid_spec=pltpu.PrefetchScalarGridSpec(
            num_scalar_prefetch=1, grid=(nt,), in_specs=in_specs,
            out_specs=pl.BlockSpec((tr, RET_WIDTH), idx),
            scratch_shapes=[pltpu.VMEM((RET_HEADS, RET_QK_WIDTH, RET_V_DIM), F32)]),
        out_shape=jax.ShapeDtypeStruct((l, RET_WIDTH), BF16 if final else F32),
        compiler_params=_cparams("arbitrary"),
        name="retention_bwd" if reverse else "retention_fwd",
    )(gc, *args)


def _retention_tables(log_g, reverse):
    c = RET_CHUNK
    pos = jnp.arange(c, dtype=F32)
    diff = (pos[None, :] - pos[:, None]) if reverse else (pos[:, None] - pos[None, :])
    mask = (diff > 0) if reverse else (diff >= 0)
    lg = log_g[:, None, None]
    dmat = jnp.where(mask, jnp.exp(lg * jnp.where(mask, diff, 0.0)[None]), 0.0)
    zeta = jnp.exp(log_g[:, None] * (pos if reverse else (c - 1 - pos))[None, :])
    xi = jnp.exp(log_g[:, None] * ((c - pos) if reverse else (pos + 1.0))[None, :])
    ones = jnp.ones((1, 1, c), F32)
    tabs = jnp.stack([dmat, zeta[:, :, None] * ones, xi[:, :, None] * ones])
    return tabs, jnp.exp(log_g * c)


def _outproj_kernel(x_ref, a_ref, r_ref, w_ref, gt_ref, g2_ref, sh_ref, sc_ref, wr_ref, br_ref,
                    xn_ref, h2_ref, ids_ref, wts_ref):
    tm = x_ref.shape[0]
    mix = jnp.dot(r_ref[...], w_ref[ATTN_WIDTH:, :], preferred_element_type=F32)
    per = LANES // HEAD_DIM
    for pair in range(ATTN_HEADS // per):
        pieces = []
        for j in range(per):
            kvh, grp = divmod(pair * per + j, GQA_GROUP)
            pieces.append(a_ref[kvh, :, grp * tm:(grp + 1) * tm])
        o_t = jnp.concatenate(pieces, axis=0)
        mix = mix + lax.dot_general(o_t, w_ref[pair * LANES:(pair + 1) * LANES, :],
                                    (((0,), (0,)), ((), ())), preferred_element_type=F32)
    x_new = x_ref[...] + gt_ref[...] * mix
    xn_ref[...] = x_new
    h2 = _rms_modulate(x_new, g2_ref[...], sh_ref[...], sc_ref[...])
    hi = h2.astype(BF16)
    hi32 = hi.astype(F32)
    lo = (h2 - hi32).astype(BF16)
    bits = lax.bitcast_convert_type(hi32, jnp.uint32)
    dh = bits.shape[1] // 2
    h2_ref[...] = (bits[:, dh:] & jnp.uint32(0xFFFF0000)) | (bits[:, :dh] >> 16)

    nt = (((1,), (1,)), ((), ()))
    logits = (lax.dot_general(wr_ref[0], hi, nt, preferred_element_type=F32)
              + lax.dot_general(wr_ref[0], lo, nt, preferred_element_type=F32)
              + lax.dot_general(wr_ref[1], hi, nt, preferred_element_type=F32)) + br_ref[...]
    rows = [logits[g:g + 1, :] for g in range(N_GROUPS)]
    gmax = functools.reduce(jnp.maximum, rows)
    gidx = jnp.full(gmax.shape, N_GROUPS - 1, jnp.int32)
    for g in range(N_GROUPS - 2, -1, -1):
        gidx = jnp.where(rows[g] == gmax, g, gidx)
    p_g = 1.0 / functools.reduce(jnp.add, [jnp.exp(r - gmax) for r in rows])
    sel = logits[EXPERT_ROW0 + (N_GROUPS - 1) * EXPERTS_PER_GROUP:EXPERT_ROW0 + N_GROUPS * EXPERTS_PER_GROUP, :]
    for g in range(N_GROUPS - 2, -1, -1):
        blk = logits[EXPERT_ROW0 + g * EXPERTS_PER_GROUP:EXPERT_ROW0 + (g + 1) * EXPERTS_PER_GROUP, :]
        sel = jnp.where(gidx == g, blk, sel)
    ridx = lax.broadcasted_iota(jnp.int32, sel.shape, 0)
    m1 = jnp.max(sel, axis=0, keepdims=True)
    i1 = jnp.min(jnp.where(sel == m1, ridx, EXPERTS_PER_GROUP), axis=0, keepdims=True)
    sel2 = jnp.where(ridx == i1, -jnp.inf, sel)
    m2 = jnp.max(sel2, axis=0, keepdims=True)
    i2 = jnp.min(jnp.where(sel2 == m2, ridx, EXPERTS_PER_GROUP), axis=0, keepdims=True)
    e2 = jnp.exp(m2 - m1)
    w1 = p_g / (1.0 + e2)
    w2 = w1 * e2
    out_rows = lax.broadcasted_iota(jnp.int32, ids_ref.shape, 0)
    ids_ref[...] = jnp.where(out_rows == 0, gidx * EXPERTS_PER_GROUP + i1,
                             jnp.where(out_rows == 1, gidx * EXPERTS_PER_GROUP + i2, 0))
    wts_ref[...] = jnp.where(out_rows == 0, w1, jnp.where(out_rows == 1, w2, 0.0))


def _outproj_router(x2, attn, ret, w_out_bf, gt1, g2, sh2, sc2, wr, br):
    l, d = x2.shape
    tm = min(ROW_TILE, l)
    row = lambda w: pl.BlockSpec((tm, w), lambda i: (i, 0))
    col = pl.BlockSpec((SUBLANES, tm), lambda i: (0, i))
    return pl.pallas_call(
        _outproj_kernel,
        grid=(l // tm,),
        in_specs=[row(d),
                  pl.BlockSpec((ATTN_KV_HEADS, HEAD_DIM, GQA_GROUP * tm), lambda i: (0, 0, i)),
                  row(ret.shape[1]), _const_spec(w_out_bf.shape),
                  _const_spec((1, d)), _const_spec((1, d)), _const_spec((1, d)), _const_spec((1, d)),
                  _const_spec(wr.shape), _const_spec(br.shape)],
        out_specs=[row(d), row(d // 2), col, col],
        out_shape=[jax.ShapeDtypeStruct((l, d), F32), jax.ShapeDtypeStruct((l, d // 2), jnp.uint32),
                   jax.ShapeDtypeStruct((SUBLANES, l), jnp.int32),
                   jax.ShapeDtypeStruct((SUBLANES, l), F32)],
        compiler_params=_cparams("arbitrary"),
        name="outproj_router",
    )(x2, attn, ret, w_out_bf, gt1, g2, sh2, sc2, wr, br)


def _gather_token_rows(tok_ref, h_ref, xg_s, slot, r0):
    rows = [h_ref[pl.ds(tok_ref[0, 0, r0 + j], 1), :] for j in range(SUBLANES)]
    xg_s[slot, pl.ds(r0, SUBLANES), :] = jnp.concatenate(rows, axis=0)


def _moe_kernel(be_ref, na_ref, tok0_ref, tokn_ref, h_ref, wg_ref, wu_ref, wd_ref, y_ref,
                wg_s, wu_s, wd_s, xg_s):
    b = pl.program_id(0)
    slot = b % 2
    active = b < na_ref[0]
    changed = jnp.logical_or(b == 0, be_ref[b] != be_ref[jnp.maximum(b - 1, 0)])

    @pl.when(b == 0)
    def _():
        def first_block(i, carry):
            _gather_token_rows(tok0_ref, h_ref, xg_s, 0, pl.multiple_of(i * SUBLANES, SUBLANES))
            return carry

        lax.fori_loop(0, MOE_BLOCK // SUBLANES, first_block, 0, unroll=2)

    @pl.when(jnp.logical_and(active, changed))
    def _():
        wg_s[...] = wg_ref[0].astype(BF16)
        wu_s[...] = wu_ref[0].astype(BF16)
        wd_s[...] = wd_ref[0].astype(BF16)

    @pl.when(active)
    def _():
        bits = xg_s[slot]
        dh = bits.shape[1]
        x_lo = lax.bitcast_convert_type(bits << 16, F32).astype(BF16)
        x_hi = lax.bitcast_convert_type(bits & jnp.uint32(0xFFFF0000), F32).astype(BF16)
        hg = (jnp.dot(x_lo, wg_s[:dh, :], preferred_element_type=F32)
              + jnp.dot(x_hi, wg_s[dh:, :], preferred_element_type=F32))
        hu = (jnp.dot(x_lo, wu_s[:dh, :], preferred_element_type=F32)
              + jnp.dot(x_hi, wu_s[dh:, :], preferred_element_type=F32))
        a = (hg / (1.0 + jnp.exp(-hg))) * hu
        y_ref[...] = jnp.dot(a.astype(BF16), wd_s[...], preferred_element_type=F32)
        for g in range(MOE_BLOCK // SUBLANES):
            _gather_token_rows(tokn_ref, h_ref, xg_s, 1 - slot, g * SUBLANES)

    @pl.when(jnp.logical_not(active))
    def _():
        y_ref[...] = jnp.zeros(y_ref.shape, y_ref.dtype)


def _moe_blocks(blk_e, n_active, buf_tok, h2p, w_gate, w_up, w_down):
    l, dh = h2p.shape
    ne, d, de = w_gate.shape
    nblk = buf_tok.shape[0]
    wmap = lambda b, be, na: (be[b], 0, 0)
    return pl.pallas_call(
        _moe_kernel,
        grid_spec=pltpu.PrefetchScalarGridSpec(
            num_scalar_prefetch=2, grid=(nblk,),
            in_specs=[pl.BlockSpec((1, 1, MOE_BLOCK), lambda b, be, na: (0, 0, 0), memory_space=pltpu.SMEM),
                      pl.BlockSpec((1, 1, MOE_BLOCK), lambda b, be, na: (jnp.minimum(b + 1, nblk - 1), 0, 0),
                                   memory_space=pltpu.SMEM),
                      pl.BlockSpec((l, dh), lambda b, be, na: (0, 0), pipeline_mode=pl.Buffered(1)),
                      pl.BlockSpec((1, d, de), wmap), pl.BlockSpec((1, d, de), wmap),
                      pl.BlockSpec((1, de, d), wmap)],
            out_specs=pl.BlockSpec((MOE_BLOCK, d), lambda b, be, na: (b, 0)),
            scratch_shapes=[pltpu.VMEM((d, de), BF16), pltpu.VMEM((d, de), BF16),
                            pltpu.VMEM((de, d), BF16), pltpu.VMEM((2, MOE_BLOCK, dh), jnp.uint32)]),
        out_shape=jax.ShapeDtypeStruct((nblk * MOE_BLOCK, d), F32),
        compiler_params=_cparams("arbitrary"),
        name="moe_experts",
    )(blk_e, n_active, buf_tok, buf_tok, h2p, w_gate, w_up, w_down)


def _final_kernel(x_ref, y1_ref, y2_ref, w_ref, gt_ref, g_ref, o_ref):
    w = w_ref[...]
    y = y1_ref[...] * w[:, 0:1] + y2_ref[...] * w[:, 1:2]
    x = x_ref[...] + gt_ref[...] * y
    ms = jnp.mean(x * x, axis=-1, keepdims=True)
    o_ref[...] = (x * lax.rsqrt(ms + EPS)) * g_ref[...]


def _final(x_new, y1, y2, wts, gt2, gfin):
    l, d = x_new.shape
    tm = min(ROW_TILE, l)
    row = lambda w: pl.BlockSpec((tm, w), lambda i: (i, 0))
    return pl.pallas_call(
        _final_kernel,
        grid=(l // tm,),
        in_specs=[row(d), row(d), row(d), row(wts.shape[1]), _const_spec((1, d)), _const_spec((1, d))],
        out_specs=row(d),
        out_shape=jax.ShapeDtypeStruct((l, d), F32),
        compiler_params=_cparams("arbitrary"),
        name="combine_final_norm",
    )(x_new, y1, y2, wts, gt2, gfin)


def _rope_tables(rows_count):
    row = jnp.repeat(jnp.arange(rows_count), GRID_W).astype(F32)
    col = jnp.tile(jnp.arange(GRID_W), rows_count).astype(F32)
    freqs = ROPE_THETA ** (-jnp.arange(0, ROPE_AXIS_DIM, 2, dtype=F32) / ROPE_AXIS_DIM)
    ang = jnp.concatenate([row[:, None] * freqs, col[:, None] * freqs], axis=-1)
    ang = jnp.repeat(ang, 2, axis=-1)
    sign = jnp.where(jnp.arange(HEAD_DIM) % 2 == 0, -1.0, 1.0).astype(F32)
    cos = jnp.tile(jnp.cos(ang), (1, LANES // HEAD_DIM))
    sin = jnp.tile(jnp.sin(ang) * sign, (1, LANES // HEAD_DIM))
    return cos, sin


def _dispatch(ids, n_tokens):
    a = TOP_K * n_tokens
    eid_f = ids[:TOP_K].reshape(a)
    tok_f = jnp.tile(jnp.arange(n_tokens, dtype=jnp.int32), TOP_K)
    onehot = (eid_f[:, None] == jnp.arange(N_EXPERTS, dtype=jnp.int32)[None, :]).astype(jnp.int32)
    incl = jnp.cumsum(onehot, axis=0)
    counts = incl[-1]
    rank = jnp.sum((incl - onehot) * onehot, axis=1)
    pcounts = (counts + MOE_BLOCK - 1) // MOE_BLOCK * MOE_BLOCK
    pends = jnp.cumsum(pcounts)
    pstarts = pends - pcounts
    dest = jnp.sum(onehot * pstarts[None, :], axis=1) + rank
    p = a + N_EXPERTS * MOE_BLOCK
    nblk = p // MOE_BLOCK
    buf_tok = jnp.zeros((p,), jnp.int32).at[dest].set(tok_f)
    n_active = pends[-1] // MOE_BLOCK
    blk_start = jnp.arange(nblk, dtype=jnp.int32) * MOE_BLOCK
    blk_e = jnp.minimum(jnp.sum(blk_start[:, None] >= pends[None, :], axis=1), N_EXPERTS - 1)
    blk_e = jnp.where(jnp.arange(nblk) < n_active, blk_e, blk_e[n_active - 1]).astype(jnp.int32)
    return buf_tok, dest.reshape(TOP_K, n_tokens), blk_e, n_active.astype(jnp.int32).reshape(1)


def _split_bf16(w):
    hi = w.astype(BF16)
    return jnp.stack([hi, (w - hi.astype(F32)).astype(BF16)])


def kernel(x, c, ctx, c_ctx, w_ada, b_ada, norm1_g, w_in, attn_q_norm, attn_k_norm, ret_decay_fwd,
           ret_decay_bwd, ret_gn_g, ret_gn_b, w_out, norm2_g, moe_w_grp, moe_b_grp, moe_w_exp, moe_b_exp,
           moe_w_gate, moe_w_up, moe_w_down, final_norm_g):
    b, l, d = x.shape
    cl = ctx.shape[1]
    assert b == 1 and w_ada.shape[0] == 1, "single batch element, single layer"
    x2, ctx2 = x[0], ctx[0]

    cvecs = jnp.zeros((SUBLANES, d), F32).at[0].set(c[0]).at[1].set(c_ctx)
    mod = _ada(cvecs, w_ada[0], b_ada[0])
    sh1, sc1, gt1, sh2, sc2, gt2 = [mod[0:1, i * d:(i + 1) * d] for i in range(6)]
    csh1, csc1 = mod[1:2, 0:d], mod[1:2, d:2 * d]

    w_in_bf = w_in[0].astype(BF16)
    head_of = jnp.arange(ATTN_WIDTH) // HEAD_DIM
    bd = jnp.where(head_of[:, None] == head_of[None, :], 1.0 / HEAD_DIM, 0.0).astype(BF16)
    gq = jnp.tile(attn_q_norm[0], ATTN_HEADS)[None, :]
    gk = jnp.tile(attn_k_norm[0], ATTN_KV_HEADS)[None, :]
    g1 = norm1_g[0][None, :]
    cos, sin = _rope_tables(l // GRID_W)
    bound = (HEAD_DIM ** 0.5 * LOG2E * 1.01) * jnp.max(jnp.abs(attn_q_norm[0])) * jnp.max(jnp.abs(attn_k_norm[0]))
    bound = bound.astype(BF16).astype(F32)
    neg_bound = jnp.full((1, LANES), -1.0, F32) * bound
    q_t, ka, va, qr, kr, vr, gr = _inproj(x2, sh1, sc1, g1, w_in_bf, bd, gq, gk, cos, sin, neg_bound)
    _, ka_c, va_c, _, kr_c, vr_c, _ = _inproj(ctx2, csh1, csc1, g1, w_in_bf, bd, gq, gk,
                                              jnp.ones((cl, LANES), F32), jnp.zeros((cl, LANES), F32), neg_bound)

    nk = cl + l
    tk = _key_chunk(nk)
    k_all = jnp.concatenate([ka_c, ka], axis=0).reshape(nk // tk, tk, ATTN_KV_HEADS, HEAD_DIM)
    k_aug = jnp.zeros(k_all.shape[:3] + (ATTN_AUG_DIM - HEAD_DIM,), BF16).at[..., 0].set(1.0)
    k_all = jnp.concatenate([k_all, k_aug], axis=-1)
    v_all = jnp.concatenate([va_c, va], axis=0).reshape(nk // tk, tk, ATTN_KV_HEADS, HEAD_DIM)
    attn_args = (q_t, k_all.transpose(2, 0, 1, 3), v_all.transpose(2, 0, 3, 1))
    o_t = lax.cond(bound <= MAX_UNSHIFTED_BOUND,
                   functools.partial(_attention, running_max=False),
                   functools.partial(_attention, running_max=True), *attn_args)

    log_gf = jax.nn.log_sigmoid(ret_decay_fwd[0].astype(F32))
    log_gb = jax.nn.log_sigmoid(ret_decay_bwd[0].astype(F32))
    tab_f, gc_f = _retention_tables(log_gf, False)
    tab_b, gc_b = _retention_tables(log_gb, True)
    o_f = _retention_pass(gc_f, qr, kr, vr, kr_c, vr_c, tab_f, None, reverse=False)
    ret = _retention_pass(gc_b, qr, kr, vr, kr_c, vr_c, tab_b,
                          (o_f, gr, ret_gn_g[0][None, :], ret_gn_b[0][None, :]), reverse=True)

    wr = jnp.zeros((ROUTER_ROWS, d), F32)
    wr = wr.at[:N_GROUPS].set(moe_w_grp[0].T).at[EXPERT_ROW0:EXPERT_ROW0 + N_EXPERTS].set(moe_w_exp[0].T)
    br = jnp.zeros((ROUTER_ROWS, 1), F32)
    br = br.at[:N_GROUPS, 0].set(moe_b_grp[0]).at[EXPERT_ROW0:EXPERT_ROW0 + N_EXPERTS, 0].set(moe_b_exp[0])
    x_new, h2, ids, wts = _outproj_router(x2, o_t, ret, w_out[0].astype(BF16), gt1, norm2_g[0][None, :],
                                          sh2, sc2, _split_bf16(wr), br)

    buf_tok, dest, blk_e, n_active = _dispatch(ids, l)
    yb = _moe_blocks(blk_e, n_active, buf_tok.reshape(-1, 1, MOE_BLOCK), h2, moe_w_gate[0], moe_w_up[0],
                     moe_w_down[0])
    out = _final(x_new, yb[dest[0]], yb[dest[1]], wts[:TOP_K].T, gt2, final_norm_g[None, :])
    return out[None]
```

```python
import functools

import jax
import jax.numpy as jnp
import numpy as np
from jax import lax
from jax.experimental import pallas as pl
from jax.experimental.pallas import tpu as pltpu

F32 = jnp.float32
BF16 = jnp.bfloat16

GRID_W = 64
HEAD_DIM = 64
ATTN_HEADS = 8
ATTN_KV_HEADS = 2
GQA_GROUP = ATTN_HEADS // ATTN_KV_HEADS
RET_HEADS = 4
RET_QK_DIM = 64
RET_V_DIM = 128
RET_CHUNK = 256
ATTN_WIDTH = ATTN_HEADS * HEAD_DIM
KV_WIDTH = ATTN_KV_HEADS * HEAD_DIM
RET_QK_WIDTH = RET_HEADS * RET_QK_DIM
RET_WIDTH = RET_HEADS * RET_V_DIM
PROJ_SIZES = (ATTN_WIDTH, KV_WIDTH, KV_WIDTH, RET_QK_WIDTH, RET_QK_WIDTH, RET_WIDTH, RET_WIDTH)
PROJ_OFFS = tuple(int(v) for v in np.cumsum((0,) + PROJ_SIZES))
PROJ_DIM = PROJ_OFFS[-1]
ROPE_THETA = 10000.0
ROPE_AXIS_DIM = HEAD_DIM // 2
N_GROUPS = 4
EXPERTS_PER_GROUP = 8
N_EXPERTS = N_GROUPS * EXPERTS_PER_GROUP
TOP_K = 2
D_EXPERT = 512
EPS = 1e-6
LOG2E = 1.4426950408889634
ATTN_AUG_DIM = 128
MAX_UNSHIFTED_BOUND = 60.0

LANES = 128
SUBLANES = 8
VMEM_LIMIT_BYTES = 56 * 1024 * 1024

ROW_TILE = 512
ATTN_Q_TILE = 512
RET_ROW_TILE = 1024
RET_UNROLL = 4
MOE_BLOCK = 256
FINAL_TILE = 256
ROUTER_ROWS = 128
EXPERT_ROW0 = 8


def _cparams(*sem):
    return pltpu.CompilerParams(dimension_semantics=sem, vmem_limit_bytes=VMEM_LIMIT_BYTES)


def _const_spec(shape):
    nd = len(shape)
    return pl.BlockSpec(shape, lambda *_: (0,) * nd)


def _ada_kernel(s_ref, w_ref, b_ref, o_ref):
    s = s_ref[...]
    s = s / (1.0 + jnp.exp(-s))
    o_ref[...] = jnp.dot(s, w_ref[...], preferred_element_type=F32,
                         precision=lax.Precision.HIGHEST) + b_ref[...]


def _ada(cvecs, w_ada, b_ada):
    d, n = w_ada.shape
    tn = 1536
    return pl.pallas_call(
        _ada_kernel,
        grid=(n // tn,),
        in_specs=[_const_spec((SUBLANES, d)),
                  pl.BlockSpec((d, tn), lambda j: (0, j)),
                  pl.BlockSpec((1, tn), lambda j: (0, j))],
        out_specs=pl.BlockSpec((SUBLANES, tn), lambda j: (0, j)),
        out_shape=jax.ShapeDtypeStruct((SUBLANES, n), F32),
        compiler_params=_cparams("arbitrary"),
        name="ada_mod",
    )(cvecs, w_ada, b_ada.reshape(1, n))


def _rms_modulate(x, g, sh, sc):
    ms = jnp.mean(x * x, axis=-1, keepdims=True)
    return (x * lax.rsqrt(ms + EPS)) * g * (1.0 + sc) + sh


def _head_mean_sq(v, bd):
    sq = v * v
    hi = sq.astype(BF16)
    lo = (sq - hi.astype(F32)).astype(BF16)
    return (jnp.dot(hi, bd, preferred_element_type=F32) + jnp.dot(lo, bd, preferred_element_type=F32))


def _rope_chunks(v, cos, sin, even):
    outs = []
    for c in range(v.shape[1] // LANES):
        xc = v[:, c * LANES:(c + 1) * LANES]
        nxt = pltpu.roll(xc, LANES - 1, 1)
        prv = pltpu.roll(xc, 1, 1)
        outs.append(xc * cos + jnp.where(even, nxt, prv) * sin)
    return outs


def _inproj_kernel(x_ref, sh_ref, sc_ref, g_ref, w_ref, bd_ref, gq_ref, gk_ref, cos_ref, sin_ref, nb_ref,
                   qt_ref, ka_ref, va_ref, qr_ref, kr_ref, vr_ref, gr_ref):
    h = _rms_modulate(x_ref[...], g_ref[...], sh_ref[...], sc_ref[...])
    proj = jnp.dot(h.astype(BF16), w_ref[...], preferred_element_type=F32)
    o = PROJ_OFFS
    qa, ka, va = proj[:, o[0]:o[1]], proj[:, o[1]:o[2]], proj[:, o[2]:o[3]]
    qr, kr, vr, gr = proj[:, o[3]:o[4]], proj[:, o[4]:o[5]], proj[:, o[5]:o[6]], proj[:, o[6]:o[7]]
    cos, sin = cos_ref[...], sin_ref[...]
    even = (lax.broadcasted_iota(jnp.int32, cos.shape, 1) % 2) == 0
    bd = bd_ref[...]
    scale = HEAD_DIM ** -0.5 * LOG2E

    qn = qa * lax.rsqrt(_head_mean_sq(qa, bd) + EPS) * gq_ref[...]
    tm = qa.shape[0]
    for c, blk in enumerate(_rope_chunks(qn, cos, sin, even)):
        blk_t = (blk * scale).T
        for j in range(LANES // HEAD_DIM):
            kvh, grp = divmod(c * (LANES // HEAD_DIM) + j, GQA_GROUP)
            qt_ref[kvh, 0:HEAD_DIM, grp * tm:(grp + 1) * tm] = blk_t[j * HEAD_DIM:(j + 1) * HEAD_DIM, :].astype(BF16)
    aug_shape = (ATTN_AUG_DIM - HEAD_DIM, GQA_GROUP * tm)
    aug_row = lax.broadcasted_iota(jnp.int32, aug_shape, 0)
    aug = jnp.where(aug_row == 0, nb_ref[0:1, 0:1], 0.0).astype(BF16)
    for kvh in range(ATTN_KV_HEADS):
        qt_ref[kvh, HEAD_DIM:ATTN_AUG_DIM, :] = aug
    kn = ka * lax.rsqrt(_head_mean_sq(ka, bd[:KV_WIDTH, :KV_WIDTH]) + EPS) * gk_ref[...]
    ka_ref[...] = _rope_chunks(kn, cos, sin, even)[0].astype(BF16)
    va_ref[...] = va.astype(BF16)
    for c, blk in enumerate(_rope_chunks(qr, cos, sin, even)):
        qr_ref[:, c * LANES:(c + 1) * LANES] = blk.astype(BF16)
    rscale = RET_QK_DIM ** -0.5
    for c, blk in enumerate(_rope_chunks(kr, cos, sin, even)):
        kr_ref[:, c * LANES:(c + 1) * LANES] = (blk * rscale).astype(BF16)
    vr_ref[...] = vr.astype(BF16)
    gr_ref[...] = gr


def _inproj(x2, sh, sc, g1, w_bf, bd, gq, gk, cos, sin, neg_bound):
    rows, d = x2.shape
    tm = min(ROW_TILE, rows)
    row = lambda w: pl.BlockSpec((tm, w), lambda i: (i, 0))
    widths = PROJ_SIZES[1:]
    dtypes = (BF16,) * 5 + (F32,)
    qt_shape = (ATTN_KV_HEADS, ATTN_AUG_DIM, GQA_GROUP * rows)
    return pl.pallas_call(
        _inproj_kernel,
        grid=(rows // tm,),
        in_specs=[row(d), _const_spec((1, d)), _const_spec((1, d)), _const_spec((1, d)),
                  _const_spec(w_bf.shape), _const_spec(bd.shape),
                  _const_spec((1, ATTN_WIDTH)), _const_spec((1, KV_WIDTH)),
                  row(LANES), row(LANES), _const_spec((1, LANES))],
        out_specs=[pl.BlockSpec((ATTN_KV_HEADS, ATTN_AUG_DIM, GQA_GROUP * tm), lambda i: (0, 0, i))]
        + [row(w) for w in widths],
        out_shape=[jax.ShapeDtypeStruct(qt_shape, BF16)]
        + [jax.ShapeDtypeStruct((rows, w), dt) for w, dt in zip(widths, dtypes)],
        compiler_params=_cparams("arbitrary"),
        name="norm_inproj",
    )(x2, sh, sc, g1, w_bf, bd, gq, gk, cos, sin, neg_bound)


def _attn_kernel(q_ref, k_ref, v_ref, o_ref, m_ref, l_ref, acc_ref, *, n_chunks, running_max):
    m_ref[...] = jnp.full(m_ref.shape, -jnp.inf, F32)
    l_ref[...] = jnp.zeros(l_ref.shape, F32)
    acc_ref[...] = jnp.zeros(acc_ref.shape, F32)

    def body(c, carry):
        kc = k_ref[0, c]
        vc = v_ref[0, c]
        s = jnp.dot(kc, q_ref[0], preferred_element_type=F32)
        if running_max:
            m_old = m_ref[...]
            m_new = jnp.maximum(m_old, jnp.max(s, axis=0, keepdims=True))
            alpha = jnp.exp2(m_old - m_new)
            p = jnp.exp2(s - m_new)
            l_ref[...] = alpha * l_ref[...] + jnp.sum(p, axis=0, keepdims=True)
            acc_ref[...] = acc_ref[...] * alpha + jnp.dot(vc, p.astype(BF16), preferred_element_type=F32)
            m_ref[...] = m_new
        else:
            p = jnp.exp2(s)
            l_ref[...] += jnp.sum(p, axis=0, keepdims=True)
            acc_ref[...] += jnp.dot(vc, p.astype(BF16), preferred_element_type=F32)
        return carry

    lax.fori_loop(0, n_chunks, body, 0)
    o_ref[0] = (acc_ref[...] / l_ref[...]).astype(o_ref.dtype)


def _key_chunk(nk):
    for tk in (3328, 1280, 640, 256, 128):
        if nk % tk == 0:
            return tk
    raise ValueError(f"unsupported key count {nk}")


def _attention(q_t, k_c, v_t, *, running_max):
    kv, da, gl = q_t.shape
    _, n_chunks, d, tk = v_t.shape
    nq = GQA_GROUP * min(ATTN_Q_TILE, gl // GQA_GROUP)
    return pl.pallas_call(
        functools.partial(_attn_kernel, n_chunks=n_chunks, running_max=running_max),
        grid=(kv, gl // nq),
        in_specs=[pl.BlockSpec((1, da, nq), lambda b, i: (b, 0, i)),
                  pl.BlockSpec((1, n_chunks, tk, da), lambda b, i: (b, 0, 0, 0)),
                  pl.BlockSpec((1, n_chunks, d, tk), lambda b, i: (b, 0, 0, 0))],
        out_specs=pl.BlockSpec((1, d, nq), lambda b, i: (b, 0, i)),
        out_shape=jax.ShapeDtypeStruct((kv, d, gl), BF16),
        scratch_shapes=[pltpu.VMEM((1, nq), F32), pltpu.VMEM((1, nq), F32), pltpu.VMEM((d, nq), F32)],
        compiler_params=_cparams("arbitrary", "arbitrary"),
        name="flash_attn",
    )(q_t, k_c, v_t)


def _ret_kernel(gc_ref, q_ref, k_ref, v_ref, kc_ref, vc_ref, tab_ref, *rest, reverse, final):
    if final:
        of_ref, gate_ref, gng_ref, gnb_ref, out_ref, s_ref = rest
    else:
        out_ref, s_ref = rest
    c_rows = RET_CHUNK
    n_local = q_ref.shape[0] // c_rows
    n_ctx = kc_ref.shape[0] // c_rows
    lane_head = lax.broadcasted_iota(jnp.int32, (c_rows, RET_QK_WIDTH), 1) // RET_QK_DIM

    def state_update(kch, vch):
        for h in range(RET_HEADS):
            vz = (vch[:, h * RET_V_DIM:(h + 1) * RET_V_DIM].astype(F32)
                  * tab_ref[1, h, :, 0:RET_V_DIM]).astype(BF16)
            u = lax.dot_general(kch, vz, (((0,), (0,)), ((), ())), preferred_element_type=F32)
            s_ref[h] = gc_ref[h] * s_ref[h] + u

    @pl.when(pl.program_id(0) == 0)
    def _():
        s_ref[...] = jnp.zeros(s_ref.shape, F32)
        order = range(n_ctx - 1, -1, -1) if reverse else range(n_ctx)
        for cc in order:
            state_update(kc_ref[cc * c_rows:(cc + 1) * c_rows, :], vc_ref[cc * c_rows:(cc + 1) * c_rows, :])

    def chunk(ci, carry):
        c = (n_local - 1 - ci) if reverse else ci
        rows = pl.ds(pl.multiple_of(c * c_rows, c_rows), c_rows)
        q, k, v = q_ref[rows, :], k_ref[rows, :], v_ref[rows, :]
        for h in range(RET_HEADS):
            cols = slice(h * RET_V_DIM, (h + 1) * RET_V_DIM)
            qm = jnp.where(lane_head == h, q, jnp.zeros_like(q))
            sc = lax.dot_general(qm, k, (((1,), (1,)), ((), ())), preferred_element_type=F32)
            sc = sc * tab_ref[0, h]
            o = jnp.dot(sc.astype(BF16), v[:, cols], preferred_element_type=F32)
            o = o + (jnp.dot(qm, s_ref[h].astype(BF16), preferred_element_type=F32)
                     * tab_ref[2, h, :, 0:RET_V_DIM])
            if final:
                o = o + of_ref[rows, cols]
                mu = jnp.mean(o, axis=-1, keepdims=True)
                oc = o - mu
                var = jnp.mean(oc * oc, axis=-1, keepdims=True)
                on = oc * lax.rsqrt(var + EPS)
                gate = gate_ref[rows, cols]
                y = (on * gng_ref[:, cols] + gnb_ref[:, cols]) * (gate / (1.0 + jnp.exp(-gate)))
                out_ref[rows, cols] = y.astype(out_ref.dtype)
            else:
                out_ref[rows, cols] = o
        state_update(k, v)
        return carry

    lax.fori_loop(0, n_local, chunk, 0, unroll=RET_UNROLL)


def _retention_pass(gc, qr, kr, vr, kr_c, vr_c, tabs, extra, *, reverse):
    l = qr.shape[0]
    tr = min(RET_ROW_TILE, l)
    nt = l // tr
    final = extra is not None
    idx = (lambda j, gc_: (nt - 1 - j, 0)) if reverse else (lambda j, gc_: (j, 0))
    const = lambda shape: pl.BlockSpec(shape, lambda j, gc_: (0,) * len(shape))
    in_specs = [pl.BlockSpec((tr, RET_QK_WIDTH), idx), pl.BlockSpec((tr, RET_QK_WIDTH), idx),
                pl.BlockSpec((tr, RET_WIDTH), idx),
                const(kr_c.shape), const(vr_c.shape), const(tabs.shape)]
    args = [qr, kr, vr, kr_c, vr_c, tabs]
    if final:
        of, gate, gng, gnb = extra
        in_specs += [pl.BlockSpec((tr, RET_WIDTH), idx), pl.BlockSpec((tr, RET_WIDTH), idx),
                     const(gng.shape), const(gnb.shape)]
        args += [of, gate, gng, gnb]
    return pl.pallas_call(
        functools.partial(_ret_kernel, reverse=reverse, final=final),
        grid_spec=pltpu.PrefetchScalarGridSpec(
            num_scalar_prefetch=1, grid=(nt,), in_specs=in_specs,
            out_specs=pl.BlockSpec((tr, RET_WIDTH), idx),
            scratch_shapes=[pltpu.VMEM((RET_HEADS, RET_QK_WIDTH, RET_V_DIM), F32)]),
        out_shape=jax.ShapeDtypeStruct((l, RET_WIDTH), BF16 if final else F32),
        compiler_params=_cparams("arbitrary"),
        name="retention_bwd" if reverse else "retention_fwd",
    )(gc, *args)


def _retention_tables(log_g, reverse):
    c = RET_CHUNK
    pos = jnp.arange(c, dtype=F32)
    diff = (pos[None, :] - pos[:, None]) if reverse else (pos[:, None] - pos[None, :])
    mask = (diff > 0) if reverse else (diff >= 0)
    lg = log_g[:, None, None]
    dmat = jnp.where(mask, jnp.exp(lg * jnp.where(mask, diff, 0.0)[None]), 0.0)
    zeta = jnp.exp(log_g[:, None] * (pos if reverse else (c - 1 - pos))[None, :])
    xi = jnp.exp(log_g[:, None] * ((c - pos) if reverse else (pos + 1.0))[None, :])
    ones = jnp.ones((1, 1, c), F32)
    tabs = jnp.stack([dmat, zeta[:, :, None] * ones, xi[:, :, None] * ones])
    return tabs, jnp.exp(log_g * c)


def _outproj_kernel(x_ref, a_ref, r_ref, w_ref, gt_ref, g2_ref, sh_ref, sc_ref, wr_ref, br_ref,
                    xn_ref, h2_ref, ids_ref, wts_ref):
    tm = x_ref.shape[0]
    mix = jnp.dot(r_ref[...], w_ref[ATTN_WIDTH:, :], preferred_element_type=F32)
    per = LANES // HEAD_DIM
    for pair in range(ATTN_HEADS // per):
        pieces = []
        for j in range(per):
            kvh, grp = divmod(pair * per + j, GQA_GROUP)
            pieces.append(a_ref[kvh, :, grp * tm:(grp + 1) * tm])
        o_t = jnp.concatenate(pieces, axis=0)
        mix = mix + lax.dot_general(o_t, w_ref[pair * LANES:(pair + 1) * LANES, :],
                                    (((0,), (0,)), ((), ())), preferred_element_type=F32)
    x_new = x_ref[...] + gt_ref[...] * mix
    xn_ref[...] = x_new
    h2 = _rms_modulate(x_new, g2_ref[...], sh_ref[...], sc_ref[...])
    h2_ref[...] = h2
    hi = h2.astype(BF16)
    lo = (h2 - hi.astype(F32)).astype(BF16)

    nt = (((1,), (1,)), ((), ()))
    logits = (lax.dot_general(wr_ref[0], hi, nt, preferred_element_type=F32)
              + lax.dot_general(wr_ref[0], lo, nt, preferred_element_type=F32)
              + lax.dot_general(wr_ref[1], hi, nt, preferred_element_type=F32)) + br_ref[...]
    rows = [logits[g:g + 1, :] for g in range(N_GROUPS)]
    gmax = functools.reduce(jnp.maximum, rows)
    gidx = jnp.full(gmax.shape, N_GROUPS - 1, jnp.int32)
    for g in range(N_GROUPS - 2, -1, -1):
        gidx = jnp.where(rows[g] == gmax, g, gidx)
    p_g = 1.0 / functools.reduce(jnp.add, [jnp.exp(r - gmax) for r in rows])
    sel = logits[EXPERT_ROW0 + (N_GROUPS - 1) * EXPERTS_PER_GROUP:EXPERT_ROW0 + N_GROUPS * EXPERTS_PER_GROUP, :]
    for g in range(N_GROUPS - 2, -1, -1):
        blk = logits[EXPERT_ROW0 + g * EXPERTS_PER_GROUP:EXPERT_ROW0 + (g + 1) * EXPERTS_PER_GROUP, :]
        sel = jnp.where(gidx == g, blk, sel)
    ridx = lax.broadcasted_iota(jnp.int32, sel.shape, 0)
    m1 = jnp.max(sel, axis=0, keepdims=True)
    i1 = jnp.min(jnp.where(sel == m1, ridx, EXPERTS_PER_GROUP), axis=0, keepdims=True)
    sel2 = jnp.where(ridx == i1, -jnp.inf, sel)
    m2 = jnp.max(sel2, axis=0, keepdims=True)
    i2 = jnp.min(jnp.where(sel2 == m2, ridx, EXPERTS_PER_GROUP), axis=0, keepdims=True)
    e2 = jnp.exp(m2 - m1)
    w1 = p_g / (1.0 + e2)
    w2 = w1 * e2
    out_rows = lax.broadcasted_iota(jnp.int32, ids_ref.shape, 0)
    ids_ref[...] = jnp.where(out_rows == 0, gidx * EXPERTS_PER_GROUP + i1,
                             jnp.where(out_rows == 1, gidx * EXPERTS_PER_GROUP + i2, 0))
    wts_ref[...] = jnp.where(out_rows == 0, w1, jnp.where(out_rows == 1, w2, 0.0))


def _outproj_router(x2, attn, ret, w_out_bf, gt1, g2, sh2, sc2, wr, br):
    l, d = x2.shape
    tm = min(ROW_TILE, l)
    row = lambda w: pl.BlockSpec((tm, w), lambda i: (i, 0))
    col = pl.BlockSpec((SUBLANES, tm), lambda i: (0, i))
    return pl.pallas_call(
        _outproj_kernel,
        grid=(l // tm,),
        in_specs=[row(d),
                  pl.BlockSpec((ATTN_KV_HEADS, HEAD_DIM, GQA_GROUP * tm), lambda i: (0, 0, i)),
                  row(ret.shape[1]), _const_spec(w_out_bf.shape),
                  _const_spec((1, d)), _const_spec((1, d)), _const_spec((1, d)), _const_spec((1, d)),
                  _const_spec(wr.shape), _const_spec(br.shape)],
        out_specs=[row(d), row(d), col, col],
        out_shape=[jax.ShapeDtypeStruct((l, d), F32), jax.ShapeDtypeStruct((l, d), F32),
                   jax.ShapeDtypeStruct((SUBLANES, l), jnp.int32),
                   jax.ShapeDtypeStruct((SUBLANES, l), F32)],
        compiler_params=_cparams("arbitrary"),
        name="outproj_router",
    )(x2, attn, ret, w_out_bf, gt1, g2, sh2, sc2, wr, br)


def _row_copy(src_hbm, src_row, dst_ref, sem):
    return pltpu.make_async_copy(src_hbm.at[pl.ds(src_row, 1), :], dst_ref, sem)


def _moe_kernel(be_ref, na_ref, tok0_ref, tokn_ref, h_hbm, wg_ref, wu_ref, wd_ref, y_ref,
                wg_s, wu_s, wd_s, xg_s, sem):
    b = pl.program_id(0)
    slot = b % 2
    n_active = na_ref[0]
    active = b < n_active
    changed = jnp.logical_or(b == 0, be_ref[b] != be_ref[jnp.maximum(b - 1, 0)])

    def wait_rows():
        def body(r, carry):
            _row_copy(h_hbm, 0, xg_s.at[slot, pl.ds(r, 1), :], sem.at[slot]).wait()
            return carry

        lax.fori_loop(0, MOE_BLOCK, body, 0, unroll=8)

    @pl.when(b == 0)
    def _():
        def body(r, carry):
            _row_copy(h_hbm, tok0_ref[0, 0, r], xg_s.at[0, pl.ds(r, 1), :], sem.at[0]).start()
            return carry

        lax.fori_loop(0, MOE_BLOCK, body, 0, unroll=8)

    @pl.when(jnp.logical_and(active, changed))
    def _():
        wg_s[...] = wg_ref[0].astype(BF16)
        wu_s[...] = wu_ref[0].astype(BF16)
        wd_s[...] = wd_ref[0].astype(BF16)

    @pl.when(active)
    def _():
        wait_rows()
        x = xg_s[slot].astype(BF16)
        hg = jnp.dot(x, wg_s[...], preferred_element_type=F32)
        hu = jnp.dot(x, wu_s[...], preferred_element_type=F32)
        a = (hg / (1.0 + jnp.exp(-hg))) * hu
        y_ref[...] = jnp.dot(a.astype(BF16), wd_s[...], preferred_element_type=F32)
        for r in range(MOE_BLOCK):
            _row_copy(h_hbm, tokn_ref[0, 0, r], xg_s.at[1 - slot, pl.ds(r, 1), :],
                      sem.at[1 - slot]).start(priority=r % 2)

    @pl.when(b == n_active)
    def _():
        wait_rows()

    @pl.when(jnp.logical_not(active))
    def _():
        y_ref[...] = jnp.zeros(y_ref.shape, y_ref.dtype)


def _moe_blocks(blk_e, n_active, buf_tok, h2, w_gate, w_up, w_down):
    ne, d, de = w_gate.shape
    nblk = buf_tok.shape[0]
    wmap = lambda b, be, na: (be[b], 0, 0)
    return pl.pallas_call(
        _moe_kernel,
        grid_spec=pltpu.PrefetchScalarGridSpec(
            num_scalar_prefetch=2, grid=(nblk,),
            in_specs=[pl.BlockSpec((1, 1, MOE_BLOCK), lambda b, be, na: (0, 0, 0), memory_space=pltpu.SMEM),
                      pl.BlockSpec((1, 1, MOE_BLOCK), lambda b, be, na: (jnp.minimum(b + 1, nblk - 1), 0, 0),
                                   memory_space=pltpu.SMEM),
                      pl.BlockSpec(memory_space=pl.ANY),
                      pl.BlockSpec((1, d, de), wmap), pl.BlockSpec((1, d, de), wmap),
                      pl.BlockSpec((1, de, d), wmap)],
            out_specs=pl.BlockSpec((MOE_BLOCK, d), lambda b, be, na: (b, 0)),
            scratch_shapes=[pltpu.VMEM((d, de), BF16), pltpu.VMEM((d, de), BF16),
                            pltpu.VMEM((de, d), BF16), pltpu.VMEM((2, MOE_BLOCK, d), F32),
                            pltpu.SemaphoreType.DMA((2,))]),
        out_shape=jax.ShapeDtypeStruct((nblk * MOE_BLOCK, d), F32),
        compiler_params=_cparams("arbitrary"),
        name="moe_experts",
    )(blk_e, n_active, buf_tok, buf_tok, h2, w_gate, w_up, w_down)


def _final_kernel(d0_ref, dn_ref, x_ref, w_ref, gt_ref, g_ref, y_hbm, o_ref, yg_s, sem):
    i = pl.program_id(0)
    slot = i % 2
    tm = x_ref.shape[0]

    def start_tile(d_ref, slot_):
        def body(r, carry):
            for k in range(TOP_K):
                _row_copy(y_hbm, d_ref[0, k, r], yg_s.at[slot_, k, pl.ds(r, 1), :],
                          sem.at[slot_]).start(priority=k)
            return carry

        lax.fori_loop(0, tm, body, 0, unroll=8)

    @pl.when(i == 0)
    def _():
        start_tile(d0_ref, 0)

    @pl.when(i + 1 < pl.num_programs(0))
    def _():
        start_tile(dn_ref, 1 - slot)

    def wait_row(r, carry):
        for k in range(TOP_K):
            _row_copy(y_hbm, 0, yg_s.at[slot, k, pl.ds(r, 1), :], sem.at[slot]).wait()
        return carry

    lax.fori_loop(0, tm, wait_row, 0, unroll=8)
    w = w_ref[...]
    y = yg_s[slot, 0] * w[:, 0:1] + yg_s[slot, 1] * w[:, 1:2]
    x = x_ref[...] + gt_ref[...] * y
    ms = jnp.mean(x * x, axis=-1, keepdims=True)
    o_ref[...] = (x * lax.rsqrt(ms + EPS)) * g_ref[...]


def _final(x_new, yb, dest, wts, gt2, gfin):
    l, d = x_new.shape
    n, _, tm = dest.shape
    row = lambda w: pl.BlockSpec((tm, w), lambda i: (i, 0))
    return pl.pallas_call(
        _final_kernel,
        grid=(n,),
        in_specs=[pl.BlockSpec((1, TOP_K, tm), lambda i: (0, 0, 0), memory_space=pltpu.SMEM),
                  pl.BlockSpec((1, TOP_K, tm), lambda i: (jnp.minimum(i + 1, n - 1), 0, 0),
                               memory_space=pltpu.SMEM),
                  row(d), row(wts.shape[1]), _const_spec((1, d)), _const_spec((1, d)),
                  pl.BlockSpec(memory_space=pl.ANY)],
        out_specs=row(d),
        out_shape=jax.ShapeDtypeStruct((l, d), F32),
        scratch_shapes=[pltpu.VMEM((2, TOP_K, tm, d), F32), pltpu.SemaphoreType.DMA((2,))],
        compiler_params=_cparams("arbitrary"),
        name="combine_final_norm",
    )(dest, dest, x_new, wts, gt2, gfin, yb)


def _rope_tables(rows_count):
    row = jnp.repeat(jnp.arange(rows_count), GRID_W).astype(F32)
    col = jnp.tile(jnp.arange(GRID_W), rows_count).astype(F32)
    freqs = ROPE_THETA ** (-jnp.arange(0, ROPE_AXIS_DIM, 2, dtype=F32) / ROPE_AXIS_DIM)
    ang = jnp.concatenate([row[:, None] * freqs, col[:, None] * freqs], axis=-1)
    ang = jnp.repeat(ang, 2, axis=-1)
    sign = jnp.where(jnp.arange(HEAD_DIM) % 2 == 0, -1.0, 1.0).astype(F32)
    cos = jnp.tile(jnp.cos(ang), (1, LANES // HEAD_DIM))
    sin = jnp.tile(jnp.sin(ang) * sign, (1, LANES // HEAD_DIM))
    return cos, sin


def _dispatch(ids, n_tokens):
    a = TOP_K * n_tokens
    eid_f = ids[:TOP_K].reshape(a)
    tok_f = jnp.tile(jnp.arange(n_tokens, dtype=jnp.int32), TOP_K)
    onehot = (eid_f[:, None] == jnp.arange(N_EXPERTS, dtype=jnp.int32)[None, :]).astype(jnp.int32)
    incl = jnp.cumsum(onehot, axis=0)
    counts = incl[-1]
    rank = jnp.sum((incl - onehot) * onehot, axis=1)
    pcounts = (counts + MOE_BLOCK - 1) // MOE_BLOCK * MOE_BLOCK
    pends = jnp.cumsum(pcounts)
    pstarts = pends - pcounts
    dest = jnp.sum(onehot * pstarts[None, :], axis=1) + rank
    p = a + N_EXPERTS * MOE_BLOCK
    nblk = p // MOE_BLOCK
    buf_tok = jnp.zeros((p,), jnp.int32).at[dest].set(tok_f)
    n_active = pends[-1] // MOE_BLOCK
    blk_start = jnp.arange(nblk, dtype=jnp.int32) * MOE_BLOCK
    blk_e = jnp.minimum(jnp.sum(blk_start[:, None] >= pends[None, :], axis=1), N_EXPERTS - 1)
    blk_e = jnp.where(jnp.arange(nblk) < n_active, blk_e, blk_e[n_active - 1]).astype(jnp.int32)
    return buf_tok, dest.reshape(TOP_K, n_tokens), blk_e, n_active.astype(jnp.int32).reshape(1)


def _split_bf16(w):
    hi = w.astype(BF16)
    return jnp.stack([hi, (w - hi.astype(F32)).astype(BF16)])


def kernel(x, c, ctx, c_ctx, w_ada, b_ada, norm1_g, w_in, attn_q_norm, attn_k_norm, ret_decay_fwd,
           ret_decay_bwd, ret_gn_g, ret_gn_b, w_out, norm2_g, moe_w_grp, moe_b_grp, moe_w_exp, moe_b_exp,
           moe_w_gate, moe_w_up, moe_w_down, final_norm_g):
    b, l, d = x.shape
    cl = ctx.shape[1]
    assert b == 1 and w_ada.shape[0] == 1, "single batch element, single layer"
    x2, ctx2 = x[0], ctx[0]

    cvecs = jnp.zeros((SUBLANES, d), F32).at[0].set(c[0]).at[1].set(c_ctx)
    mod = _ada(cvecs, w_ada[0], b_ada[0])
    sh1, sc1, gt1, sh2, sc2, gt2 = [mod[0:1, i * d:(i + 1) * d] for i in range(6)]
    csh1, csc1 = mod[1:2, 0:d], mod[1:2, d:2 * d]

    w_in_bf = w_in[0].astype(BF16)
    head_of = jnp.arange(ATTN_WIDTH) // HEAD_DIM
    bd = jnp.where(head_of[:, None] == head_of[None, :], 1.0 / HEAD_DIM, 0.0).astype(BF16)
    gq = jnp.tile(attn_q_norm[0], ATTN_HEADS)[None, :]
    gk = jnp.tile(attn_k_norm[0], ATTN_KV_HEADS)[None, :]
    g1 = norm1_g[0][None, :]
    cos, sin = _rope_tables(l // GRID_W)
    bound = (HEAD_DIM ** 0.5 * LOG2E * 1.01) * jnp.max(jnp.abs(attn_q_norm[0])) * jnp.max(jnp.abs(attn_k_norm[0]))
    bound = bound.astype(BF16).astype(F32)
    neg_bound = jnp.full((1, LANES), -1.0, F32) * bound
    q_t, ka, va, qr, kr, vr, gr = _inproj(x2, sh1, sc1, g1, w_in_bf, bd, gq, gk, cos, sin, neg_bound)
    _, ka_c, va_c, _, kr_c, vr_c, _ = _inproj(ctx2, csh1, csc1, g1, w_in_bf, bd, gq, gk,
                                              jnp.ones((cl, LANES), F32), jnp.zeros((cl, LANES), F32), neg_bound)

    nk = cl + l
    tk = _key_chunk(nk)
    k_all = jnp.concatenate([ka_c, ka], axis=0).reshape(nk // tk, tk, ATTN_KV_HEADS, HEAD_DIM)
    k_aug = jnp.zeros(k_all.shape[:3] + (ATTN_AUG_DIM - HEAD_DIM,), BF16).at[..., 0].set(1.0)
    k_all = jnp.concatenate([k_all, k_aug], axis=-1)
    v_all = jnp.concatenate([va_c, va], axis=0).reshape(nk // tk, tk, ATTN_KV_HEADS, HEAD_DIM)
    attn_args = (q_t, k_all.transpose(2, 0, 1, 3), v_all.transpose(2, 0, 3, 1))
    o_t = lax.cond(bound <= MAX_UNSHIFTED_BOUND,
                   functools.partial(_attention, running_max=False),
                   functools.partial(_attention, running_max=True), *attn_args)

    log_gf = jax.nn.log_sigmoid(ret_decay_fwd[0].astype(F32))
    log_gb = jax.nn.log_sigmoid(ret_decay_bwd[0].astype(F32))
    tab_f, gc_f = _retention_tables(log_gf, False)
    tab_b, gc_b = _retention_tables(log_gb, True)
    o_f = _retention_pass(gc_f, qr, kr, vr, kr_c, vr_c, tab_f, None, reverse=False)
    ret = _retention_pass(gc_b, qr, kr, vr, kr_c, vr_c, tab_b,
                          (o_f, gr, ret_gn_g[0][None, :], ret_gn_b[0][None, :]), reverse=True)

    wr = jnp.zeros((ROUTER_ROWS, d), F32)
    wr = wr.at[:N_GROUPS].set(moe_w_grp[0].T).at[EXPERT_ROW0:EXPERT_ROW0 + N_EXPERTS].set(moe_w_exp[0].T)
    br = jnp.zeros((ROUTER_ROWS, 1), F32)
    br = br.at[:N_GROUPS, 0].set(moe_b_grp[0]).at[EXPERT_ROW0:EXPERT_ROW0 + N_EXPERTS, 0].set(moe_b_exp[0])
    x_new, h2, ids, wts = _outproj_router(x2, o_t, ret, w_out[0].astype(BF16), gt1, norm2_g[0][None, :],
                                          sh2, sc2, _split_bf16(wr), br)

    buf_tok, dest, blk_e, n_active = _dispatch(ids, l)
    yb = _moe_blocks(blk_e, n_active, buf_tok.reshape(-1, 1, MOE_BLOCK), h2, moe_w_gate[0], moe_w_up[0],
                     moe_w_down[0])
    tf = min(FINAL_TILE, l)
    dest_t = dest.reshape(TOP_K, l // tf, tf).transpose(1, 0, 2)
    out = _final(x_new, yb, dest_t, wts[:TOP_K].T, gt2, final_norm_g[None, :])
    return out[None]
```

```python
import functools

import jax
import jax.numpy as jnp
import numpy as np
from jax import lax
from jax.experimental import pallas as pl
from jax.experimental.pallas import tpu as pltpu

F32 = jnp.float32
BF16 = jnp.bfloat16

GRID_W = 64
HEAD_DIM = 64
ATTN_HEADS = 8
ATTN_KV_HEADS = 2
GQA_GROUP = ATTN_HEADS // ATTN_KV_HEADS
RET_HEADS = 4
RET_QK_DIM = 64
RET_V_DIM = 128
RET_CHUNK = 256
ATTN_WIDTH = ATTN_HEADS * HEAD_DIM
KV_WIDTH = ATTN_KV_HEADS * HEAD_DIM
RET_QK_WIDTH = RET_HEADS * RET_QK_DIM
RET_WIDTH = RET_HEADS * RET_V_DIM
PROJ_SIZES = (ATTN_WIDTH, KV_WIDTH, KV_WIDTH, RET_QK_WIDTH, RET_QK_WIDTH, RET_WIDTH, RET_WIDTH)
PROJ_OFFS = tuple(int(v) for v in np.cumsum((0,) + PROJ_SIZES))
PROJ_DIM = PROJ_OFFS[-1]
ROPE_THETA = 10000.0
ROPE_AXIS_DIM = HEAD_DIM // 2
N_GROUPS = 4
EXPERTS_PER_GROUP = 8
N_EXPERTS = N_GROUPS * EXPERTS_PER_GROUP
TOP_K = 2
D_EXPERT = 512
EPS = 1e-6
LOG2E = 1.4426950408889634
ATTN_AUG_DIM = 128
MAX_UNSHIFTED_BOUND = 60.0

LANES = 128
SUBLANES = 8
VMEM_LIMIT_BYTES = 56 * 1024 * 1024

ROW_TILE = 512
ATTN_Q_TILE = 512
RET_ROW_TILE = 1024
RET_UNROLL = 4
MOE_BLOCK = 256
FINAL_TILE = 256
ROUTER_ROWS = 128
EXPERT_ROW0 = 8


def _cparams(*sem):
    return pltpu.CompilerParams(dimension_semantics=sem, vmem_limit_bytes=VMEM_LIMIT_BYTES)


def _const_spec(shape):
    nd = len(shape)
    return pl.BlockSpec(shape, lambda *_: (0,) * nd)


def _store_row_tiles(ref, idx, rows):
    n, w = rows.shape[0], rows.shape[1] // SUBLANES
    for s in range(SUBLANES):
        ref[idx + (pl.ds(s, n, stride=SUBLANES), slice(None))] = rows[:, s * w:(s + 1) * w]


def _load_row_tiles(ref, idx, n):
    return jnp.concatenate([ref[idx + (pl.ds(s, n, stride=SUBLANES), slice(None))] for s in range(SUBLANES)],
                           axis=1)


def _ada_kernel(s_ref, w_ref, b_ref, o_ref):
    s = s_ref[...]
    s = s / (1.0 + jnp.exp(-s))
    o_ref[...] = jnp.dot(s, w_ref[...], preferred_element_type=F32,
                         precision=lax.Precision.HIGHEST) + b_ref[...]


def _ada(cvecs, w_ada, b_ada):
    d, n = w_ada.shape
    tn = 1536
    return pl.pallas_call(
        _ada_kernel,
        grid=(n // tn,),
        in_specs=[_const_spec((SUBLANES, d)),
                  pl.BlockSpec((d, tn), lambda j: (0, j)),
                  pl.BlockSpec((1, tn), lambda j: (0, j))],
        out_specs=pl.BlockSpec((SUBLANES, tn), lambda j: (0, j)),
        out_shape=jax.ShapeDtypeStruct((SUBLANES, n), F32),
        compiler_params=_cparams("arbitrary"),
        name="ada_mod",
    )(cvecs, w_ada, b_ada.reshape(1, n))


def _rms_modulate(x, g, sh, sc):
    ms = jnp.mean(x * x, axis=-1, keepdims=True)
    return (x * lax.rsqrt(ms + EPS)) * g * (1.0 + sc) + sh


def _head_mean_sq(v, bd):
    sq = v * v
    hi = sq.astype(BF16)
    lo = (sq - hi.astype(F32)).astype(BF16)
    return (jnp.dot(hi, bd, preferred_element_type=F32) + jnp.dot(lo, bd, preferred_element_type=F32))


def _rope_chunks(v, cos, sin, even):
    outs = []
    for c in range(v.shape[1] // LANES):
        xc = v[:, c * LANES:(c + 1) * LANES]
        nxt = pltpu.roll(xc, LANES - 1, 1)
        prv = pltpu.roll(xc, 1, 1)
        outs.append(xc * cos + jnp.where(even, nxt, prv) * sin)
    return outs


def _inproj_kernel(x_ref, sh_ref, sc_ref, g_ref, w_ref, bd_ref, gq_ref, gk_ref, cos_ref, sin_ref, nb_ref,
                   qt_ref, ka_ref, va_ref, qr_ref, kr_ref, vr_ref, gr_ref):
    h = _rms_modulate(x_ref[...], g_ref[...], sh_ref[...], sc_ref[...])
    proj = jnp.dot(h.astype(BF16), w_ref[...], preferred_element_type=F32)
    o = PROJ_OFFS
    qa, ka, va = proj[:, o[0]:o[1]], proj[:, o[1]:o[2]], proj[:, o[2]:o[3]]
    qr, kr, vr, gr = proj[:, o[3]:o[4]], proj[:, o[4]:o[5]], proj[:, o[5]:o[6]], proj[:, o[6]:o[7]]
    cos, sin = cos_ref[...], sin_ref[...]
    even = (lax.broadcasted_iota(jnp.int32, cos.shape, 1) % 2) == 0
    bd = bd_ref[...]
    scale = HEAD_DIM ** -0.5 * LOG2E

    qn = qa * lax.rsqrt(_head_mean_sq(qa, bd) + EPS) * gq_ref[...]
    tm = qa.shape[0]
    for c, blk in enumerate(_rope_chunks(qn, cos, sin, even)):
        blk_t = (blk * scale).T
        for j in range(LANES // HEAD_DIM):
            kvh, grp = divmod(c * (LANES // HEAD_DIM) + j, GQA_GROUP)
            qt_ref[kvh, 0:HEAD_DIM, grp * tm:(grp + 1) * tm] = blk_t[j * HEAD_DIM:(j + 1) * HEAD_DIM, :].astype(BF16)
    aug_shape = (ATTN_AUG_DIM - HEAD_DIM, GQA_GROUP * tm)
    aug_row = lax.broadcasted_iota(jnp.int32, aug_shape, 0)
    aug = jnp.where(aug_row == 0, nb_ref[0:1, 0:1], 0.0).astype(BF16)
    for kvh in range(ATTN_KV_HEADS):
        qt_ref[kvh, HEAD_DIM:ATTN_AUG_DIM, :] = aug
    kn = ka * lax.rsqrt(_head_mean_sq(ka, bd[:KV_WIDTH, :KV_WIDTH]) + EPS) * gk_ref[...]
    ka_ref[...] = _rope_chunks(kn, cos, sin, even)[0].astype(BF16)
    va_ref[...] = va.astype(BF16)
    for c, blk in enumerate(_rope_chunks(qr, cos, sin, even)):
        qr_ref[:, c * LANES:(c + 1) * LANES] = blk.astype(BF16)
    rscale = RET_QK_DIM ** -0.5
    for c, blk in enumerate(_rope_chunks(kr, cos, sin, even)):
        kr_ref[:, c * LANES:(c + 1) * LANES] = (blk * rscale).astype(BF16)
    vr_ref[...] = vr.astype(BF16)
    gr_ref[...] = gr


def _inproj(x2, sh, sc, g1, w_bf, bd, gq, gk, cos, sin, neg_bound):
    rows, d = x2.shape
    tm = min(ROW_TILE, rows)
    row = lambda w: pl.BlockSpec((tm, w), lambda i: (i, 0))
    widths = PROJ_SIZES[1:]
    dtypes = (BF16,) * 5 + (F32,)
    qt_shape = (ATTN_KV_HEADS, ATTN_AUG_DIM, GQA_GROUP * rows)
    return pl.pallas_call(
        _inproj_kernel,
        grid=(rows // tm,),
        in_specs=[row(d), _const_spec((1, d)), _const_spec((1, d)), _const_spec((1, d)),
                  _const_spec(w_bf.shape), _const_spec(bd.shape),
                  _const_spec((1, ATTN_WIDTH)), _const_spec((1, KV_WIDTH)),
                  row(LANES), row(LANES), _const_spec((1, LANES))],
        out_specs=[pl.BlockSpec((ATTN_KV_HEADS, ATTN_AUG_DIM, GQA_GROUP * tm), lambda i: (0, 0, i))]
        + [row(w) for w in widths],
        out_shape=[jax.ShapeDtypeStruct(qt_shape, BF16)]
        + [jax.ShapeDtypeStruct((rows, w), dt) for w, dt in zip(widths, dtypes)],
        compiler_params=_cparams("arbitrary"),
        name="norm_inproj",
    )(x2, sh, sc, g1, w_bf, bd, gq, gk, cos, sin, neg_bound)


def _attn_kernel(q_ref, k_ref, v_ref, o_ref, m_ref, l_ref, acc_ref, *, n_chunks, running_max):
    m_ref[...] = jnp.full(m_ref.shape, -jnp.inf, F32)
    l_ref[...] = jnp.zeros(l_ref.shape, F32)
    acc_ref[...] = jnp.zeros(acc_ref.shape, F32)

    def body(c, carry):
        kc = k_ref[0, c]
        vc = v_ref[0, c]
        s = jnp.dot(kc, q_ref[0], preferred_element_type=F32)
        if running_max:
            m_old = m_ref[...]
            m_new = jnp.maximum(m_old, jnp.max(s, axis=0, keepdims=True))
            alpha = jnp.exp2(m_old - m_new)
            p = jnp.exp2(s - m_new)
            l_ref[...] = alpha * l_ref[...] + jnp.sum(p, axis=0, keepdims=True)
            acc_ref[...] = acc_ref[...] * alpha + jnp.dot(vc, p.astype(BF16), preferred_element_type=F32)
            m_ref[...] = m_new
        else:
            p = jnp.exp2(s)
            l_ref[...] += jnp.sum(p, axis=0, keepdims=True)
            acc_ref[...] += jnp.dot(vc, p.astype(BF16), preferred_element_type=F32)
        return carry

    lax.fori_loop(0, n_chunks, body, 0)
    o_ref[0] = (acc_ref[...] / l_ref[...]).astype(o_ref.dtype)


def _key_chunk(nk):
    for tk in (3328, 1280, 640, 256, 128):
        if nk % tk == 0:
            return tk
    raise ValueError(f"unsupported key count {nk}")


def _attention(q_t, k_c, v_t, *, running_max):
    kv, da, gl = q_t.shape
    _, n_chunks, d, tk = v_t.shape
    nq = GQA_GROUP * min(ATTN_Q_TILE, gl // GQA_GROUP)
    return pl.pallas_call(
        functools.partial(_attn_kernel, n_chunks=n_chunks, running_max=running_max),
        grid=(kv, gl // nq),
        in_specs=[pl.BlockSpec((1, da, nq), lambda b, i: (b, 0, i)),
                  pl.BlockSpec((1, n_chunks, tk, da), lambda b, i: (b, 0, 0, 0)),
                  pl.BlockSpec((1, n_chunks, d, tk), lambda b, i: (b, 0, 0, 0))],
        out_specs=pl.BlockSpec((1, d, nq), lambda b, i: (b, 0, i)),
        out_shape=jax.ShapeDtypeStruct((kv, d, gl), BF16),
        scratch_shapes=[pltpu.VMEM((1, nq), F32), pltpu.VMEM((1, nq), F32), pltpu.VMEM((d, nq), F32)],
        compiler_params=_cparams("arbitrary", "arbitrary"),
        name="flash_attn",
    )(q_t, k_c, v_t)


def _ret_kernel(gc_ref, q_ref, k_ref, v_ref, kc_ref, vc_ref, tab_ref, *rest, reverse, final):
    if final:
        of_ref, gate_ref, gng_ref, gnb_ref, out_ref, s_ref = rest
    else:
        out_ref, s_ref = rest
    c_rows = RET_CHUNK
    n_local = q_ref.shape[0] // c_rows
    n_ctx = kc_ref.shape[0] // c_rows
    lane_head = lax.broadcasted_iota(jnp.int32, (c_rows, RET_QK_WIDTH), 1) // RET_QK_DIM

    def state_update(kch, vch):
        for h in range(RET_HEADS):
            vz = (vch[:, h * RET_V_DIM:(h + 1) * RET_V_DIM].astype(F32)
                  * tab_ref[1, h, :, 0:RET_V_DIM]).astype(BF16)
            u = lax.dot_general(kch, vz, (((0,), (0,)), ((), ())), preferred_element_type=F32)
            s_ref[h] = gc_ref[h] * s_ref[h] + u

    @pl.when(pl.program_id(0) == 0)
    def _():
        s_ref[...] = jnp.zeros(s_ref.shape, F32)
        order = range(n_ctx - 1, -1, -1) if reverse else range(n_ctx)
        for cc in order:
            state_update(kc_ref[cc * c_rows:(cc + 1) * c_rows, :], vc_ref[cc * c_rows:(cc + 1) * c_rows, :])

    def chunk(ci, carry):
        c = (n_local - 1 - ci) if reverse else ci
        rows = pl.ds(pl.multiple_of(c * c_rows, c_rows), c_rows)
        q, k, v = q_ref[rows, :], k_ref[rows, :], v_ref[rows, :]
        for h in range(RET_HEADS):
            cols = slice(h * RET_V_DIM, (h + 1) * RET_V_DIM)
            qm = jnp.where(lane_head == h, q, jnp.zeros_like(q))
            sc = lax.dot_general(qm, k, (((1,), (1,)), ((), ())), preferred_element_type=F32)
            sc = sc * tab_ref[0, h]
            o = jnp.dot(sc.astype(BF16), v[:, cols], preferred_element_type=F32)
            o = o + (jnp.dot(qm, s_ref[h].astype(BF16), preferred_element_type=F32)
                     * tab_ref[2, h, :, 0:RET_V_DIM])
            if final:
                o = o + of_ref[rows, cols]
                mu = jnp.mean(o, axis=-1, keepdims=True)
                oc = o - mu
                var = jnp.mean(oc * oc, axis=-1, keepdims=True)
                on = oc * lax.rsqrt(var + EPS)
                gate = gate_ref[rows, cols]
                y = (on * gng_ref[:, cols] + gnb_ref[:, cols]) * (gate / (1.0 + jnp.exp(-gate)))
                out_ref[rows, cols] = y.astype(out_ref.dtype)
            else:
                out_ref[rows, cols] = o
        state_update(k, v)
        return carry

    lax.fori_loop(0, n_local, chunk, 0, unroll=RET_UNROLL)


def _retention_pass(gc, qr, kr, vr, kr_c, vr_c, tabs, extra, *, reverse):
    l = qr.shape[0]
    tr = min(RET_ROW_TILE, l)
    nt = l // tr
    final = extra is not None
    idx = (lambda j, gc_: (nt - 1 - j, 0)) if reverse else (lambda j, gc_: (j, 0))
    const = lambda shape: pl.BlockSpec(shape, lambda j, gc_: (0,) * len(shape))
    in_specs = [pl.BlockSpec((tr, RET_QK_WIDTH), idx), pl.BlockSpec((tr, RET_QK_WIDTH), idx),
                pl.BlockSpec((tr, RET_WIDTH), idx),
                const(kr_c.shape), const(vr_c.shape), const(tabs.shape)]
    args = [qr, kr, vr, kr_c, vr_c, tabs]
    if final:
        of, gate, gng, gnb = extra
        in_specs += [pl.BlockSpec((tr, RET_WIDTH), idx), pl.BlockSpec((tr, RET_WIDTH), idx),
                     const(gng.shape), const(gnb.shape)]
        args += [of, gate, gng, gnb]
    return pl.pallas_call(
        functools.partial(_ret_kernel, reverse=reverse, final=final),
        grid_spec=pltpu.PrefetchScalarGridSpec(
            num_scalar_prefetch=1, grid=(nt,), in_specs=in_specs,
            out_specs=pl.BlockSpec((tr, RET_WIDTH), idx),
            scratch_shapes=[pltpu.VMEM((RET_HEADS, RET_QK_WIDTH, RET_V_DIM), F32)]),
        out_shape=jax.ShapeDtypeStruct((l, RET_WIDTH), BF16 if final else F32),
        compiler_params=_cparams("arbitrary"),
        name="retention_bwd" if reverse else "retention_fwd",
    )(gc, *args)


def _retention_tables(log_g, reverse):
    c = RET_CHUNK
    pos = jnp.arange(c, dtype=F32)
    diff = (pos[None, :] - pos[:, None]) if reverse else (pos[:, None] - pos[None, :])
    mask = (diff > 0) if reverse else (diff >= 0)
    lg = log_g[:, None, None]
    dmat = jnp.where(mask, jnp.exp(lg * jnp.where(mask, diff, 0.0)[None]), 0.0)
    zeta = jnp.exp(log_g[:, None] * (pos if reverse else (c - 1 - pos))[None, :])
    xi = jnp.exp(log_g[:, None] * ((c - pos) if reverse else (pos + 1.0))[None, :])
    ones = jnp.ones((1, 1, c), F32)
    tabs = jnp.stack([dmat, zeta[:, :, None] * ones, xi[:, :, None] * ones])
    return tabs, jnp.exp(log_g * c)


def _outproj_kernel(x_ref, a_ref, r_ref, w_ref, gt_ref, g2_ref, sh_ref, sc_ref, wr_ref, br_ref,
                    xn_ref, h2_ref, ids_ref, wts_ref):
    tm = x_ref.shape[0]
    mix = jnp.dot(r_ref[...], w_ref[ATTN_WIDTH:, :], preferred_element_type=F32)
    per = LANES // HEAD_DIM
    for pair in range(ATTN_HEADS // per):
        pieces = []
        for j in range(per):
            kvh, grp = divmod(pair * per + j, GQA_GROUP)
            pieces.append(a_ref[kvh, :, grp * tm:(grp + 1) * tm])
        o_t = jnp.concatenate(pieces, axis=0)
        mix = mix + lax.dot_general(o_t, w_ref[pair * LANES:(pair + 1) * LANES, :],
                                    (((0,), (0,)), ((), ())), preferred_element_type=F32)
    x_new = x_ref[...] + gt_ref[...] * mix
    xn_ref[...] = x_new
    h2 = _rms_modulate(x_new, g2_ref[...], sh_ref[...], sc_ref[...])
    _store_row_tiles(h2_ref, (), h2)
    hi = h2.astype(BF16)
    lo = (h2 - hi.astype(F32)).astype(BF16)

    nt = (((1,), (1,)), ((), ()))
    logits = (lax.dot_general(wr_ref[0], hi, nt, preferred_element_type=F32)
              + lax.dot_general(wr_ref[0], lo, nt, preferred_element_type=F32)
              + lax.dot_general(wr_ref[1], hi, nt, preferred_element_type=F32)) + br_ref[...]
    rows = [logits[g:g + 1, :] for g in range(N_GROUPS)]
    gmax = functools.reduce(jnp.maximum, rows)
    gidx = jnp.full(gmax.shape, N_GROUPS - 1, jnp.int32)
    for g in range(N_GROUPS - 2, -1, -1):
        gidx = jnp.where(rows[g] == gmax, g, gidx)
    p_g = 1.0 / functools.reduce(jnp.add, [jnp.exp(r - gmax) for r in rows])
    sel = logits[EXPERT_ROW0 + (N_GROUPS - 1) * EXPERTS_PER_GROUP:EXPERT_ROW0 + N_GROUPS * EXPERTS_PER_GROUP, :]
    for g in range(N_GROUPS - 2, -1, -1):
        blk = logits[EXPERT_ROW0 + g * EXPERTS_PER_GROUP:EXPERT_ROW0 + (g + 1) * EXPERTS_PER_GROUP, :]
        sel = jnp.where(gidx == g, blk, sel)
    ridx = lax.broadcasted_iota(jnp.int32, sel.shape, 0)
    m1 = jnp.max(sel, axis=0, keepdims=True)
    i1 = jnp.min(jnp.where(sel == m1, ridx, EXPERTS_PER_GROUP), axis=0, keepdims=True)
    sel2 = jnp.where(ridx == i1, -jnp.inf, sel)
    m2 = jnp.max(sel2, axis=0, keepdims=True)
    i2 = jnp.min(jnp.where(sel2 == m2, ridx, EXPERTS_PER_GROUP), axis=0, keepdims=True)
    e2 = jnp.exp(m2 - m1)
    w1 = p_g / (1.0 + e2)
    w2 = w1 * e2
    out_rows = lax.broadcasted_iota(jnp.int32, ids_ref.shape, 0)
    ids_ref[...] = jnp.where(out_rows == 0, gidx * EXPERTS_PER_GROUP + i1,
                             jnp.where(out_rows == 1, gidx * EXPERTS_PER_GROUP + i2, 0))
    wts_ref[...] = jnp.where(out_rows == 0, w1, jnp.where(out_rows == 1, w2, 0.0))


def _outproj_router(x2, attn, ret, w_out_bf, gt1, g2, sh2, sc2, wr, br):
    l, d = x2.shape
    tm = min(ROW_TILE, l)
    row = lambda w: pl.BlockSpec((tm, w), lambda i: (i, 0))
    col = pl.BlockSpec((SUBLANES, tm), lambda i: (0, i))
    return pl.pallas_call(
        _outproj_kernel,
        grid=(l // tm,),
        in_specs=[row(d),
                  pl.BlockSpec((ATTN_KV_HEADS, HEAD_DIM, GQA_GROUP * tm), lambda i: (0, 0, i)),
                  row(ret.shape[1]), _const_spec(w_out_bf.shape),
                  _const_spec((1, d)), _const_spec((1, d)), _const_spec((1, d)), _const_spec((1, d)),
                  _const_spec(wr.shape), _const_spec(br.shape)],
        out_specs=[row(d), pl.BlockSpec((tm * SUBLANES, d // SUBLANES), lambda i: (i, 0)), col, col],
        out_shape=[jax.ShapeDtypeStruct((l, d), F32), jax.ShapeDtypeStruct((l * SUBLANES, d // SUBLANES), F32),
                   jax.ShapeDtypeStruct((SUBLANES, l), jnp.int32),
                   jax.ShapeDtypeStruct((SUBLANES, l), F32)],
        compiler_params=_cparams("arbitrary"),
        name="outproj_router",
    )(x2, attn, ret, w_out_bf, gt1, g2, sh2, sc2, wr, br)


def _row_copy(src_hbm, src_row, dst_ref, sem):
    return pltpu.make_async_copy(src_hbm.at[pl.ds(src_row * SUBLANES, SUBLANES), :], dst_ref, sem)


def _moe_kernel(be_ref, na_ref, tok0_ref, tokn_ref, h_hbm, wg_ref, wu_ref, wd_ref, y_ref,
                wg_s, wu_s, wd_s, xg_s, sem):
    b = pl.program_id(0)
    slot = b % 2
    n_active = na_ref[0]
    active = b < n_active
    changed = jnp.logical_or(b == 0, be_ref[b] != be_ref[jnp.maximum(b - 1, 0)])

    def wait_rows():
        def body(r, carry):
            _row_copy(h_hbm, 0, xg_s.at[slot, pl.ds(r * SUBLANES, SUBLANES), :], sem.at[slot]).wait()
            return carry

        lax.fori_loop(0, MOE_BLOCK, body, 0, unroll=8)

    @pl.when(b == 0)
    def _():
        def body(r, carry):
            _row_copy(h_hbm, tok0_ref[0, 0, r], xg_s.at[0, pl.ds(r * SUBLANES, SUBLANES), :], sem.at[0]).start()
            return carry

        lax.fori_loop(0, MOE_BLOCK, body, 0, unroll=8)

    @pl.when(jnp.logical_and(active, changed))
    def _():
        wg_s[...] = wg_ref[0].astype(BF16)
        wu_s[...] = wu_ref[0].astype(BF16)
        wd_s[...] = wd_ref[0].astype(BF16)

    @pl.when(active)
    def _():
        wait_rows()
        x = _load_row_tiles(xg_s, (slot,), MOE_BLOCK).astype(BF16)
        hg = jnp.dot(x, wg_s[...], preferred_element_type=F32)
        hu = jnp.dot(x, wu_s[...], preferred_element_type=F32)
        a = (hg / (1.0 + jnp.exp(-hg))) * hu
        _store_row_tiles(y_ref, (), jnp.dot(a.astype(BF16), wd_s[...], preferred_element_type=F32))
        for r in range(MOE_BLOCK):
            _row_copy(h_hbm, tokn_ref[0, 0, r], xg_s.at[1 - slot, pl.ds(r * SUBLANES, SUBLANES), :],
                      sem.at[1 - slot]).start(priority=r % 2)

    @pl.when(b == n_active)
    def _():
        wait_rows()

    @pl.when(jnp.logical_not(active))
    def _():
        y_ref[...] = jnp.zeros(y_ref.shape, y_ref.dtype)


def _moe_blocks(blk_e, n_active, buf_tok, h2, w_gate, w_up, w_down):
    ne, d, de = w_gate.shape
    tiled = (MOE_BLOCK * SUBLANES, d // SUBLANES)
    nblk = buf_tok.shape[0]
    wmap = lambda b, be, na: (be[b], 0, 0)
    return pl.pallas_call(
        _moe_kernel,
        grid_spec=pltpu.PrefetchScalarGridSpec(
            num_scalar_prefetch=2, grid=(nblk,),
            in_specs=[pl.BlockSpec((1, 1, MOE_BLOCK), lambda b, be, na: (0, 0, 0), memory_space=pltpu.SMEM),
                      pl.BlockSpec((1, 1, MOE_BLOCK), lambda b, be, na: (jnp.minimum(b + 1, nblk - 1), 0, 0),
                                   memory_space=pltpu.SMEM),
                      pl.BlockSpec(memory_space=pl.ANY),
                      pl.BlockSpec((1, d, de), wmap), pl.BlockSpec((1, d, de), wmap),
                      pl.BlockSpec((1, de, d), wmap)],
            out_specs=pl.BlockSpec(tiled, lambda b, be, na: (b, 0)),
            scratch_shapes=[pltpu.VMEM((d, de), BF16), pltpu.VMEM((d, de), BF16),
                            pltpu.VMEM((de, d), BF16), pltpu.VMEM((2,) + tiled, F32),
                            pltpu.SemaphoreType.DMA((2,))]),
        out_shape=jax.ShapeDtypeStruct((nblk * tiled[0], tiled[1]), F32),
        compiler_params=_cparams("arbitrary"),
        name="moe_experts",
    )(blk_e, n_active, buf_tok, buf_tok, h2, w_gate, w_up, w_down)


def _final_kernel(d0_ref, dn_ref, x_ref, w_ref, gt_ref, g_ref, y_hbm, o_ref, yg_s, sem):
    i = pl.program_id(0)
    slot = i % 2
    tm = x_ref.shape[0]

    def start_tile(d_ref, slot_):
        def body(r, carry):
            for k in range(TOP_K):
                _row_copy(y_hbm, d_ref[0, k, r], yg_s.at[slot_, k, pl.ds(r * SUBLANES, SUBLANES), :],
                          sem.at[slot_]).start(priority=k)
            return carry

        lax.fori_loop(0, tm, body, 0, unroll=8)

    @pl.when(i == 0)
    def _():
        start_tile(d0_ref, 0)

    @pl.when(i + 1 < pl.num_programs(0))
    def _():
        start_tile(dn_ref, 1 - slot)

    def wait_row(r, carry):
        for k in range(TOP_K):
            _row_copy(y_hbm, 0, yg_s.at[slot, k, pl.ds(r * SUBLANES, SUBLANES), :], sem.at[slot]).wait()
        return carry

    lax.fori_loop(0, tm, wait_row, 0, unroll=8)
    w = w_ref[...]
    y = _load_row_tiles(yg_s, (slot, 0), tm) * w[:, 0:1] + _load_row_tiles(yg_s, (slot, 1), tm) * w[:, 1:2]
    x = x_ref[...] + gt_ref[...] * y
    ms = jnp.mean(x * x, axis=-1, keepdims=True)
    o_ref[...] = (x * lax.rsqrt(ms + EPS)) * g_ref[...]


def _final(x_new, yb, dest, wts, gt2, gfin):
    l, d = x_new.shape
    n, _, tm = dest.shape
    row = lambda w: pl.BlockSpec((tm, w), lambda i: (i, 0))
    return pl.pallas_call(
        _final_kernel,
        grid=(n,),
        in_specs=[pl.BlockSpec((1, TOP_K, tm), lambda i: (0, 0, 0), memory_space=pltpu.SMEM),
                  pl.BlockSpec((1, TOP_K, tm), lambda i: (jnp.minimum(i + 1, n - 1), 0, 0),
                               memory_space=pltpu.SMEM),
                  row(d), row(wts.shape[1]), _const_spec((1, d)), _const_spec((1, d)),
                  pl.BlockSpec(memory_space=pl.ANY)],
        out_specs=row(d),
        out_shape=jax.ShapeDtypeStruct((l, d), F32),
        scratch_shapes=[pltpu.VMEM((2, TOP_K, tm * SUBLANES, d // SUBLANES), F32), pltpu.SemaphoreType.DMA((2,))],
        compiler_params=_cparams("arbitrary"),
        name="combine_final_norm",
    )(dest, dest, x_new, wts, gt2, gfin, yb)


def _rope_tables(rows_count):
    row = jnp.repeat(jnp.arange(rows_count), GRID_W).astype(F32)
    col = jnp.tile(jnp.arange(GRID_W), rows_count).astype(F32)
    freqs = ROPE_THETA ** (-jnp.arange(0, ROPE_AXIS_DIM, 2, dtype=F32) / ROPE_AXIS_DIM)
    ang = jnp.concatenate([row[:, None] * freqs, col[:, None] * freqs], axis=-1)
    ang = jnp.repeat(ang, 2, axis=-1)
    sign = jnp.where(jnp.arange(HEAD_DIM) % 2 == 0, -1.0, 1.0).astype(F32)
    cos = jnp.tile(jnp.cos(ang), (1, LANES // HEAD_DIM))
    sin = jnp.tile(jnp.sin(ang) * sign, (1, LANES // HEAD_DIM))
    return cos, sin


def _dispatch(ids, n_tokens):
    a = TOP_K * n_tokens
    eid_f = ids[:TOP_K].reshape(a)
    tok_f = jnp.tile(jnp.arange(n_tokens, dtype=jnp.int32), TOP_K)
    onehot = (eid_f[:, None] == jnp.arange(N_EXPERTS, dtype=jnp.int32)[None, :]).astype(jnp.int32)
    incl = jnp.cumsum(onehot, axis=0)
    counts = incl[-1]
    rank = jnp.sum((incl - onehot) * onehot, axis=1)
    pcounts = (counts + MOE_BLOCK - 1) // MOE_BLOCK * MOE_BLOCK
    pends = jnp.cumsum(pcounts)
    pstarts = pends - pcounts
    dest = jnp.sum(onehot * pstarts[None, :], axis=1) + rank
    p = a + N_EXPERTS * MOE_BLOCK
    nblk = p // MOE_BLOCK
    buf_tok = jnp.zeros((p,), jnp.int32).at[dest].set(tok_f)
    n_active = pends[-1] // MOE_BLOCK
    blk_start = jnp.arange(nblk, dtype=jnp.int32) * MOE_BLOCK
    blk_e = jnp.minimum(jnp.sum(blk_start[:, None] >= pends[None, :], axis=1), N_EXPERTS - 1)
    blk_e = jnp.where(jnp.arange(nblk) < n_active, blk_e, blk_e[n_active - 1]).astype(jnp.int32)
    return buf_tok, dest.reshape(TOP_K, n_tokens), blk_e, n_active.astype(jnp.int32).reshape(1)


def _split_bf16(w):
    hi = w.astype(BF16)
    return jnp.stack([hi, (w - hi.astype(F32)).astype(BF16)])


def kernel(x, c, ctx, c_ctx, w_ada, b_ada, norm1_g, w_in, attn_q_norm, attn_k_norm, ret_decay_fwd,
           ret_decay_bwd, ret_gn_g, ret_gn_b, w_out, norm2_g, moe_w_grp, moe_b_grp, moe_w_exp, moe_b_exp,
           moe_w_gate, moe_w_up, moe_w_down, final_norm_g):
    b, l, d = x.shape
    cl = ctx.shape[1]
    assert b == 1 and w_ada.shape[0] == 1, "single batch element, single layer"
    x2, ctx2 = x[0], ctx[0]

    cvecs = jnp.zeros((SUBLANES, d), F32).at[0].set(c[0]).at[1].set(c_ctx)
    mod = _ada(cvecs, w_ada[0], b_ada[0])
    sh1, sc1, gt1, sh2, sc2, gt2 = [mod[0:1, i * d:(i + 1) * d] for i in range(6)]
    csh1, csc1 = mod[1:2, 0:d], mod[1:2, d:2 * d]

    w_in_bf = w_in[0].astype(BF16)
    head_of = jnp.arange(ATTN_WIDTH) // HEAD_DIM
    bd = jnp.where(head_of[:, None] == head_of[None, :], 1.0 / HEAD_DIM, 0.0).astype(BF16)
    gq = jnp.tile(attn_q_norm[0], ATTN_HEADS)[None, :]
    gk = jnp.tile(attn_k_norm[0], ATTN_KV_HEADS)[None, :]
    g1 = norm1_g[0][None, :]
    cos, sin = _rope_tables(l // GRID_W)
    bound = (HEAD_DIM ** 0.5 * LOG2E * 1.01) * jnp.max(jnp.abs(attn_q_norm[0])) * jnp.max(jnp.abs(attn_k_norm[0]))
    bound = bound.astype(BF16).astype(F32)
    neg_bound = jnp.full((1, LANES), -1.0, F32) * bound
    q_t, ka, va, qr, kr, vr, gr = _inproj(x2, sh1, sc1, g1, w_in_bf, bd, gq, gk, cos, sin, neg_bound)
    _, ka_c, va_c, _, kr_c, vr_c, _ = _inproj(ctx2, csh1, csc1, g1, w_in_bf, bd, gq, gk,
                                              jnp.ones((cl, LANES), F32), jnp.zeros((cl, LANES), F32), neg_bound)

    nk = cl + l
    tk = _key_chunk(nk)
    k_all = jnp.concatenate([ka_c, ka], axis=0).reshape(nk // tk, tk, ATTN_KV_HEADS, HEAD_DIM)
    k_aug = jnp.zeros(k_all.shape[:3] + (ATTN_AUG_DIM - HEAD_DIM,), BF16).at[..., 0].set(1.0)
    k_all = jnp.concatenate([k_all, k_aug], axis=-1)
    v_all = jnp.concatenate([va_c, va], axis=0).reshape(nk // tk, tk, ATTN_KV_HEADS, HEAD_DIM)
    attn_args = (q_t, k_all.transpose(2, 0, 1, 3), v_all.transpose(2, 0, 3, 1))
    o_t = lax.cond(bound <= MAX_UNSHIFTED_BOUND,
                   functools.partial(_attention, running_max=False),
                   functools.partial(_attention, running_max=True), *attn_args)

    log_gf = jax.nn.log_sigmoid(ret_decay_fwd[0].astype(F32))
    log_gb = jax.nn.log_sigmoid(ret_decay_bwd[0].astype(F32))
    tab_f, gc_f = _retention_tables(log_gf, False)
    tab_b, gc_b = _retention_tables(log_gb, True)
    o_f = _retention_pass(gc_f, qr, kr, vr, kr_c, vr_c, tab_f, None, reverse=False)
    ret = _retention_pass(gc_b, qr, kr, vr, kr_c, vr_c, tab_b,
                          (o_f, gr, ret_gn_g[0][None, :], ret_gn_b[0][None, :]), reverse=True)

    wr = jnp.zeros((ROUTER_ROWS, d), F32)
    wr = wr.at[:N_GROUPS].set(moe_w_grp[0].T).at[EXPERT_ROW0:EXPERT_ROW0 + N_EXPERTS].set(moe_w_exp[0].T)
    br = jnp.zeros((ROUTER_ROWS, 1), F32)
    br = br.at[:N_GROUPS, 0].set(moe_b_grp[0]).at[EXPERT_ROW0:EXPERT_ROW0 + N_EXPERTS, 0].set(moe_b_exp[0])
    x_new, h2, ids, wts = _outproj_router(x2, o_t, ret, w_out[0].astype(BF16), gt1, norm2_g[0][None, :],
                                          sh2, sc2, _split_bf16(wr), br)

    buf_tok, dest, blk_e, n_active = _dispatch(ids, l)
    yb = _moe_blocks(blk_e, n_active, buf_tok.reshape(-1, 1, MOE_BLOCK), h2, moe_w_gate[0], moe_w_up[0],
                     moe_w_down[0])
    tf = min(FINAL_TILE, l)
    dest_t = dest.reshape(TOP_K, l // tf, tf).transpose(1, 0, 2)
    out = _final(x_new, yb, dest_t, wts[:TOP_K].T, gt2, final_norm_g[None, :])
    return out[None]
```

```python
import functools

import jax
import jax.numpy as jnp
import numpy as np
from jax import lax
from jax.experimental import pallas as pl
from jax.experimental.pallas import tpu as pltpu

F32 = jnp.float32
BF16 = jnp.bfloat16

GRID_W = 64
HEAD_DIM = 64
ATTN_HEADS = 8
ATTN_KV_HEADS = 2
GQA_GROUP = ATTN_HEADS // ATTN_KV_HEADS
RET_HEADS = 4
RET_QK_DIM = 64
RET_V_DIM = 128
RET_CHUNK = 256
ATTN_WIDTH = ATTN_HEADS * HEAD_DIM
KV_WIDTH = ATTN_KV_HEADS * HEAD_DIM
RET_QK_WIDTH = RET_HEADS * RET_QK_DIM
RET_WIDTH = RET_HEADS * RET_V_DIM
PROJ_SIZES = (ATTN_WIDTH, KV_WIDTH, KV_WIDTH, RET_QK_WIDTH, RET_QK_WIDTH, RET_WIDTH, RET_WIDTH)
PROJ_OFFS = tuple(int(v) for v in np.cumsum((0,) + PROJ_SIZES))
PROJ_DIM = PROJ_OFFS[-1]
ROPE_THETA = 10000.0
ROPE_AXIS_DIM = HEAD_DIM // 2
N_GROUPS = 4
EXPERTS_PER_GROUP = 8
N_EXPERTS = N_GROUPS * EXPERTS_PER_GROUP
TOP_K = 2
D_EXPERT = 512
EPS = 1e-6
LOG2E = 1.4426950408889634
ATTN_AUG_DIM = 128
MAX_UNSHIFTED_BOUND = 60.0

LANES = 128
SUBLANES = 8
VMEM_LIMIT_BYTES = 56 * 1024 * 1024

ROW_TILE = 512
ATTN_Q_TILE = 512
RET_ROW_TILE = 1024
RET_UNROLL = 4
MOE_BLOCK = 256
MOE_SLOTS = 3
FINAL_TILE = 256
ROUTER_ROWS = 128
EXPERT_ROW0 = 8


def _cparams(*sem):
    return pltpu.CompilerParams(dimension_semantics=sem, vmem_limit_bytes=VMEM_LIMIT_BYTES)


def _const_spec(shape):
    nd = len(shape)
    return pl.BlockSpec(shape, lambda *_: (0,) * nd)


def _store_row_tiles(ref, idx, rows):
    n, w = rows.shape[0], rows.shape[1] // SUBLANES
    for s in range(SUBLANES):
        ref[idx + (pl.ds(s, n, stride=SUBLANES), slice(None))] = rows[:, s * w:(s + 1) * w]


def _load_row_tiles(ref, idx, n):
    return jnp.concatenate([ref[idx + (pl.ds(s, n, stride=SUBLANES), slice(None))] for s in range(SUBLANES)],
                           axis=1)


def _ada_kernel(s_ref, w_ref, b_ref, o_ref):
    s = s_ref[...]
    s = s / (1.0 + jnp.exp(-s))
    o_ref[...] = jnp.dot(s, w_ref[...], preferred_element_type=F32,
                         precision=lax.Precision.HIGHEST) + b_ref[...]


def _ada(cvecs, w_ada, b_ada):
    d, n = w_ada.shape
    tn = 1536
    return pl.pallas_call(
        _ada_kernel,
        grid=(n // tn,),
        in_specs=[_const_spec((SUBLANES, d)),
                  pl.BlockSpec((d, tn), lambda j: (0, j)),
                  pl.BlockSpec((1, tn), lambda j: (0, j))],
        out_specs=pl.BlockSpec((SUBLANES, tn), lambda j: (0, j)),
        out_shape=jax.ShapeDtypeStruct((SUBLANES, n), F32),
        compiler_params=_cparams("arbitrary"),
        name="ada_mod",
    )(cvecs, w_ada, b_ada.reshape(1, n))


def _rms_modulate(x, g, sh, sc):
    ms = jnp.mean(x * x, axis=-1, keepdims=True)
    return (x * lax.rsqrt(ms + EPS)) * g * (1.0 + sc) + sh


def _head_mean_sq(v, bd):
    sq = v * v
    hi = sq.astype(BF16)
    lo = (sq - hi.astype(F32)).astype(BF16)
    return (jnp.dot(hi, bd, preferred_element_type=F32) + jnp.dot(lo, bd, preferred_element_type=F32))


def _rope_chunks(v, cos, sin, even):
    outs = []
    for c in range(v.shape[1] // LANES):
        xc = v[:, c * LANES:(c + 1) * LANES]
        nxt = pltpu.roll(xc, LANES - 1, 1)
        prv = pltpu.roll(xc, 1, 1)
        outs.append(xc * cos + jnp.where(even, nxt, prv) * sin)
    return outs


def _inproj_kernel(x_ref, sh_ref, sc_ref, g_ref, w_ref, bd_ref, gq_ref, gk_ref, cos_ref, sin_ref, nb_ref,
                   qt_ref, ka_ref, va_ref, qr_ref, kr_ref, vr_ref, gr_ref):
    h = _rms_modulate(x_ref[...], g_ref[...], sh_ref[...], sc_ref[...])
    proj = jnp.dot(h.astype(BF16), w_ref[...], preferred_element_type=F32)
    o = PROJ_OFFS
    qa, ka, va = proj[:, o[0]:o[1]], proj[:, o[1]:o[2]], proj[:, o[2]:o[3]]
    qr, kr, vr, gr = proj[:, o[3]:o[4]], proj[:, o[4]:o[5]], proj[:, o[5]:o[6]], proj[:, o[6]:o[7]]
    cos, sin = cos_ref[...], sin_ref[...]
    even = (lax.broadcasted_iota(jnp.int32, cos.shape, 1) % 2) == 0
    bd = bd_ref[...]
    scale = HEAD_DIM ** -0.5 * LOG2E

    qn = qa * lax.rsqrt(_head_mean_sq(qa, bd) + EPS) * gq_ref[...]
    tm = qa.shape[0]
    for c, blk in enumerate(_rope_chunks(qn, cos, sin, even)):
        blk_t = (blk * scale).T
        for j in range(LANES // HEAD_DIM):
            kvh, grp = divmod(c * (LANES // HEAD_DIM) + j, GQA_GROUP)
            qt_ref[kvh, 0:HEAD_DIM, grp * tm:(grp + 1) * tm] = blk_t[j * HEAD_DIM:(j + 1) * HEAD_DIM, :].astype(BF16)
    aug_shape = (ATTN_AUG_DIM - HEAD_DIM, GQA_GROUP * tm)
    aug_row = lax.broadcasted_iota(jnp.int32, aug_shape, 0)
    aug = jnp.where(aug_row == 0, nb_ref[0:1, 0:1], 0.0).astype(BF16)
    for kvh in range(ATTN_KV_HEADS):
        qt_ref[kvh, HEAD_DIM:ATTN_AUG_DIM, :] = aug
    kn = ka * lax.rsqrt(_head_mean_sq(ka, bd[:KV_WIDTH, :KV_WIDTH]) + EPS) * gk_ref[...]
    ka_ref[...] = _rope_chunks(kn, cos, sin, even)[0].astype(BF16)
    va_ref[...] = va.astype(BF16)
    for c, blk in enumerate(_rope_chunks(qr, cos, sin, even)):
        qr_ref[:, c * LANES:(c + 1) * LANES] = blk.astype(BF16)
    rscale = RET_QK_DIM ** -0.5
    for c, blk in enumerate(_rope_chunks(kr, cos, sin, even)):
        kr_ref[:, c * LANES:(c + 1) * LANES] = (blk * rscale).astype(BF16)
    vr_ref[...] = vr.astype(BF16)
    gr_ref[...] = gr


def _inproj(x2, sh, sc, g1, w_bf, bd, gq, gk, cos, sin, neg_bound):
    rows, d = x2.shape
    tm = min(ROW_TILE, rows)
    row = lambda w: pl.BlockSpec((tm, w), lambda i: (i, 0))
    widths = PROJ_SIZES[1:]
    dtypes = (BF16,) * 5 + (F32,)
    qt_shape = (ATTN_KV_HEADS, ATTN_AUG_DIM, GQA_GROUP * rows)
    return pl.pallas_call(
        _inproj_kernel,
        grid=(rows // tm,),
        in_specs=[row(d), _const_spec((1, d)), _const_spec((1, d)), _const_spec((1, d)),
                  _const_spec(w_bf.shape), _const_spec(bd.shape),
                  _const_spec((1, ATTN_WIDTH)), _const_spec((1, KV_WIDTH)),
                  row(LANES), row(LANES), _const_spec((1, LANES))],
        out_specs=[pl.BlockSpec((ATTN_KV_HEADS, ATTN_AUG_DIM, GQA_GROUP * tm), lambda i: (0, 0, i))]
        + [row(w) for w in widths],
        out_shape=[jax.ShapeDtypeStruct(qt_shape, BF16)]
        + [jax.ShapeDtypeStruct((rows, w), dt) for w, dt in zip(widths, dtypes)],
        compiler_params=_cparams("arbitrary"),
        name="norm_inproj",
    )(x2, sh, sc, g1, w_bf, bd, gq, gk, cos, sin, neg_bound)


def _attn_kernel(q_ref, k_ref, v_ref, o_ref, m_ref, l_ref, acc_ref, *, n_chunks, running_max):
    m_ref[...] = jnp.full(m_ref.shape, -jnp.inf, F32)
    l_ref[...] = jnp.zeros(l_ref.shape, F32)
    acc_ref[...] = jnp.zeros(acc_ref.shape, F32)

    def body(c, carry):
        kc = k_ref[0, c]
        vc = v_ref[0, c]
        s = jnp.dot(kc, q_ref[0], preferred_element_type=F32)
        if running_max:
            m_old = m_ref[...]
            m_new = jnp.maximum(m_old, jnp.max(s, axis=0, keepdims=True))
            alpha = jnp.exp2(m_old - m_new)
            p = jnp.exp2(s - m_new)
            l_ref[...] = alpha * l_ref[...] + jnp.sum(p, axis=0, keepdims=True)
            acc_ref[...] = acc_ref[...] * alpha + jnp.dot(vc, p.astype(BF16), preferred_element_type=F32)
            m_ref[...] = m_new
        else:
            p = jnp.exp2(s)
            l_ref[...] += jnp.sum(p, axis=0, keepdims=True)
            acc_ref[...] += jnp.dot(vc, p.astype(BF16), preferred_element_type=F32)
        return carry

    lax.fori_loop(0, n_chunks, body, 0)
    o_ref[0] = (acc_ref[...] / l_ref[...]).astype(o_ref.dtype)


def _key_chunk(nk):
    for tk in (3328, 1280, 640, 256, 128):
        if nk % tk == 0:
            return tk
    raise ValueError(f"unsupported key count {nk}")


def _attention(q_t, k_c, v_t, *, running_max):
    kv, da, gl = q_t.shape
    _, n_chunks, d, tk = v_t.shape
    nq = GQA_GROUP * min(ATTN_Q_TILE, gl // GQA_GROUP)
    return pl.pallas_call(
        functools.partial(_attn_kernel, n_chunks=n_chunks, running_max=running_max),
        grid=(kv, gl // nq),
        in_specs=[pl.BlockSpec((1, da, nq), lambda b, i: (b, 0, i)),
                  pl.BlockSpec((1, n_chunks, tk, da), lambda b, i: (b, 0, 0, 0)),
                  pl.BlockSpec((1, n_chunks, d, tk), lambda b, i: (b, 0, 0, 0))],
        out_specs=pl.BlockSpec((1, d, nq), lambda b, i: (b, 0, i)),
        out_shape=jax.ShapeDtypeStruct((kv, d, gl), BF16),
        scratch_shapes=[pltpu.VMEM((1, nq), F32), pltpu.VMEM((1, nq), F32), pltpu.VMEM((d, nq), F32)],
        compiler_params=_cparams("arbitrary", "arbitrary"),
        name="flash_attn",
    )(q_t, k_c, v_t)


def _ret_kernel(gc_ref, q_ref, k_ref, v_ref, kc_ref, vc_ref, tab_ref, *rest, reverse, final):
    if final:
        of_ref, gate_ref, gng_ref, gnb_ref, out_ref, s_ref = rest
    else:
        out_ref, s_ref = rest
    c_rows = RET_CHUNK
    n_local = q_ref.shape[0] // c_rows
    n_ctx = kc_ref.shape[0] // c_rows
    lane_head = lax.broadcasted_iota(jnp.int32, (c_rows, RET_QK_WIDTH), 1) // RET_QK_DIM

    def state_update(kch, vch):
        for h in range(RET_HEADS):
            vz = (vch[:, h * RET_V_DIM:(h + 1) * RET_V_DIM].astype(F32)
                  * tab_ref[1, h, :, 0:RET_V_DIM]).astype(BF16)
            u = lax.dot_general(kch, vz, (((0,), (0,)), ((), ())), preferred_element_type=F32)
            s_ref[h] = gc_ref[h] * s_ref[h] + u

    @pl.when(pl.program_id(0) == 0)
    def _():
        s_ref[...] = jnp.zeros(s_ref.shape, F32)
        order = range(n_ctx - 1, -1, -1) if reverse else range(n_ctx)
        for cc in order:
            state_update(kc_ref[cc * c_rows:(cc + 1) * c_rows, :], vc_ref[cc * c_rows:(cc + 1) * c_rows, :])

    def chunk(ci, carry):
        c = (n_local - 1 - ci) if reverse else ci
        rows = pl.ds(pl.multiple_of(c * c_rows, c_rows), c_rows)
        q, k, v = q_ref[rows, :], k_ref[rows, :], v_ref[rows, :]
        for h in range(RET_HEADS):
            cols = slice(h * RET_V_DIM, (h + 1) * RET_V_DIM)
            qm = jnp.where(lane_head == h, q, jnp.zeros_like(q))
            sc = lax.dot_general(qm, k, (((1,), (1,)), ((), ())), preferred_element_type=F32)
            sc = sc * tab_ref[0, h]
            o = jnp.dot(sc.astype(BF16), v[:, cols], preferred_element_type=F32)
            o = o + (jnp.dot(qm, s_ref[h].astype(BF16), preferred_element_type=F32)
                     * tab_ref[2, h, :, 0:RET_V_DIM])
            if final:
                o = o + of_ref[rows, cols]
                mu = jnp.mean(o, axis=-1, keepdims=True)
                oc = o - mu
                var = jnp.mean(oc * oc, axis=-1, keepdims=True)
                on = oc * lax.rsqrt(var + EPS)
                gate = gate_ref[rows, cols]
                y = (on * gng_ref[:, cols] + gnb_ref[:, cols]) * (gate / (1.0 + jnp.exp(-gate)))
                out_ref[rows, cols] = y.astype(out_ref.dtype)
            else:
                out_ref[rows, cols] = o
        state_update(k, v)
        return carry

    lax.fori_loop(0, n_local, chunk, 0, unroll=RET_UNROLL)


def _retention_pass(gc, qr, kr, vr, kr_c, vr_c, tabs, extra, *, reverse):
    l = qr.shape[0]
    tr = min(RET_ROW_TILE, l)
    nt = l // tr
    final = extra is not None
    idx = (lambda j, gc_: (nt - 1 - j, 0)) if reverse else (lambda j, gc_: (j, 0))
    const = lambda shape: pl.BlockSpec(shape, lambda j, gc_: (0,) * len(shape))
    in_specs = [pl.BlockSpec((tr, RET_QK_WIDTH), idx), pl.BlockSpec((tr, RET_QK_WIDTH), idx),
                pl.BlockSpec((tr, RET_WIDTH), idx),
                const(kr_c.shape), const(vr_c.shape), const(tabs.shape)]
    args = [qr, kr, vr, kr_c, vr_c, tabs]
    if final:
        of, gate, gng, gnb = extra
        in_specs += [pl.BlockSpec((tr, RET_WIDTH), idx), pl.BlockSpec((tr, RET_WIDTH), idx),
                     const(gng.shape), const(gnb.shape)]
        args += [of, gate, gng, gnb]
    return pl.pallas_call(
        functools.partial(_ret_kernel, reverse=reverse, final=final),
        grid_spec=pltpu.PrefetchScalarGridSpec(
            num_scalar_prefetch=1, grid=(nt,), in_specs=in_specs,
            out_specs=pl.BlockSpec((tr, RET_WIDTH), idx),
            scratch_shapes=[pltpu.VMEM((RET_HEADS, RET_QK_WIDTH, RET_V_DIM), F32)]),
        out_shape=jax.ShapeDtypeStruct((l, RET_WIDTH), BF16 if final else F32),
        compiler_params=_cparams("arbitrary"),
        name="retention_bwd" if reverse else "retention_fwd",
    )(gc, *args)


def _retention_tables(log_g, reverse):
    c = RET_CHUNK
    pos = jnp.arange(c, dtype=F32)
    diff = (pos[None, :] - pos[:, None]) if reverse else (pos[:, None] - pos[None, :])
    mask = (diff > 0) if reverse else (diff >= 0)
    lg = log_g[:, None, None]
    dmat = jnp.where(mask, jnp.exp(lg * jnp.where(mask, diff, 0.0)[None]), 0.0)
    zeta = jnp.exp(log_g[:, None] * (pos if reverse else (c - 1 - pos))[None, :])
    xi = jnp.exp(log_g[:, None] * ((c - pos) if reverse else (pos + 1.0))[None, :])
    ones = jnp.ones((1, 1, c), F32)
    tabs = jnp.stack([dmat, zeta[:, :, None] * ones, xi[:, :, None] * ones])
    return tabs, jnp.exp(log_g * c)


def _outproj_kernel(x_ref, a_ref, r_ref, w_ref, gt_ref, g2_ref, sh_ref, sc_ref, wr_ref, br_ref,
                    xn_ref, h2_ref, ids_ref, wts_ref):
    tm = x_ref.shape[0]
    mix = jnp.dot(r_ref[...], w_ref[ATTN_WIDTH:, :], preferred_element_type=F32)
    per = LANES // HEAD_DIM
    for pair in range(ATTN_HEADS // per):
        pieces = []
        for j in range(per):
            kvh, grp = divmod(pair * per + j, GQA_GROUP)
            pieces.append(a_ref[kvh, :, grp * tm:(grp + 1) * tm])
        o_t = jnp.concatenate(pieces, axis=0)
        mix = mix + lax.dot_general(o_t, w_ref[pair * LANES:(pair + 1) * LANES, :],
                                    (((0,), (0,)), ((), ())), preferred_element_type=F32)
    x_new = x_ref[...] + gt_ref[...] * mix
    xn_ref[...] = x_new
    h2 = _rms_modulate(x_new, g2_ref[...], sh_ref[...], sc_ref[...])
    _store_row_tiles(h2_ref, (), h2)
    hi = h2.astype(BF16)
    lo = (h2 - hi.astype(F32)).astype(BF16)

    nt = (((1,), (1,)), ((), ()))
    logits = (lax.dot_general(wr_ref[0], hi, nt, preferred_element_type=F32)
              + lax.dot_general(wr_ref[0], lo, nt, preferred_element_type=F32)
              + lax.dot_general(wr_ref[1], hi, nt, preferred_element_type=F32)) + br_ref[...]
    rows = [logits[g:g + 1, :] for g in range(N_GROUPS)]
    gmax = functools.reduce(jnp.maximum, rows)
    gidx = jnp.full(gmax.shape, N_GROUPS - 1, jnp.int32)
    for g in range(N_GROUPS - 2, -1, -1):
        gidx = jnp.where(rows[g] == gmax, g, gidx)
    p_g = 1.0 / functools.reduce(jnp.add, [jnp.exp(r - gmax) for r in rows])
    sel = logits[EXPERT_ROW0 + (N_GROUPS - 1) * EXPERTS_PER_GROUP:EXPERT_ROW0 + N_GROUPS * EXPERTS_PER_GROUP, :]
    for g in range(N_GROUPS - 2, -1, -1):
        blk = logits[EXPERT_ROW0 + g * EXPERTS_PER_GROUP:EXPERT_ROW0 + (g + 1) * EXPERTS_PER_GROUP, :]
        sel = jnp.where(gidx == g, blk, sel)
    ridx = lax.broadcasted_iota(jnp.int32, sel.shape, 0)
    m1 = jnp.max(sel, axis=0, keepdims=True)
    i1 = jnp.min(jnp.where(sel == m1, ridx, EXPERTS_PER_GROUP), axis=0, keepdims=True)
    sel2 = jnp.where(ridx == i1, -jnp.inf, sel)
    m2 = jnp.max(sel2, axis=0, keepdims=True)
    i2 = jnp.min(jnp.where(sel2 == m2, ridx, EXPERTS_PER_GROUP), axis=0, keepdims=True)
    e2 = jnp.exp(m2 - m1)
    w1 = p_g / (1.0 + e2)
    w2 = w1 * e2
    out_rows = lax.broadcasted_iota(jnp.int32, ids_ref.shape, 0)
    ids_ref[...] = jnp.where(out_rows == 0, gidx * EXPERTS_PER_GROUP + i1,
                             jnp.where(out_rows == 1, gidx * EXPERTS_PER_GROUP + i2, 0))
    wts_ref[...] = jnp.where(out_rows == 0, w1, jnp.where(out_rows == 1, w2, 0.0))


def _outproj_router(x2, attn, ret, w_out_bf, gt1, g2, sh2, sc2, wr, br):
    l, d = x2.shape
    tm = min(ROW_TILE, l)
    row = lambda w: pl.BlockSpec((tm, w), lambda i: (i, 0))
    col = pl.BlockSpec((SUBLANES, tm), lambda i: (0, i))
    return pl.pallas_call(
        _outproj_kernel,
        grid=(l // tm,),
        in_specs=[row(d),
                  pl.BlockSpec((ATTN_KV_HEADS, HEAD_DIM, GQA_GROUP * tm), lambda i: (0, 0, i)),
                  row(ret.shape[1]), _const_spec(w_out_bf.shape),
                  _const_spec((1, d)), _const_spec((1, d)), _const_spec((1, d)), _const_spec((1, d)),
                  _const_spec(wr.shape), _const_spec(br.shape)],
        out_specs=[row(d), pl.BlockSpec((tm * SUBLANES, d // SUBLANES), lambda i: (i, 0)), col, col],
        out_shape=[jax.ShapeDtypeStruct((l, d), F32), jax.ShapeDtypeStruct((l * SUBLANES, d // SUBLANES), F32),
                   jax.ShapeDtypeStruct((SUBLANES, l), jnp.int32),
                   jax.ShapeDtypeStruct((SUBLANES, l), F32)],
        compiler_params=_cparams("arbitrary"),
        name="outproj_router",
    )(x2, attn, ret, w_out_bf, gt1, g2, sh2, sc2, wr, br)


def _row_copy(src_hbm, src_row, dst_ref, sem):
    return pltpu.make_async_copy(src_hbm.at[pl.ds(src_row * SUBLANES, SUBLANES), :], dst_ref, sem)


def _moe_kernel(be_ref, na_ref, tok0_ref, tok1_ref, tokn_ref, h_hbm, wg_ref, wu_ref, wd_ref, y_ref,
                wg_s, wu_s, wd_s, xg_s, sem):
    b = pl.program_id(0)
    slot = lax.rem(b, MOE_SLOTS)
    nxt = lax.rem(b + 2, MOE_SLOTS)
    n_active = na_ref[0]
    active = b < n_active
    last = be_ref.shape[0] - 1
    changed = jnp.logical_or(b == 0, be_ref[jnp.minimum(b, last)] != be_ref[jnp.clip(b - 1, 0, last)])

    def wait_rows():
        def body(r, carry):
            _row_copy(h_hbm, 0, xg_s.at[slot, pl.ds(r * SUBLANES, SUBLANES), :], sem.at[slot]).wait()
            return carry

        lax.fori_loop(0, MOE_BLOCK, body, 0, unroll=8)

    @pl.when(b == 0)
    def _():
        def body(r, carry):
            _row_copy(h_hbm, tok0_ref[0, 0, r], xg_s.at[0, pl.ds(r * SUBLANES, SUBLANES), :], sem.at[0]).start()
            _row_copy(h_hbm, tok1_ref[0, 0, r], xg_s.at[1, pl.ds(r * SUBLANES, SUBLANES), :], sem.at[1]).start()
            return carry

        lax.fori_loop(0, MOE_BLOCK, body, 0, unroll=8)

    @pl.when(jnp.logical_and(active, changed))
    def _():
        wg_s[...] = wg_ref[0].astype(BF16)
        wu_s[...] = wu_ref[0].astype(BF16)
        wd_s[...] = wd_ref[0].astype(BF16)

    @pl.when(active)
    def _():
        wait_rows()
        x = _load_row_tiles(xg_s, (slot,), MOE_BLOCK).astype(BF16)
        hg = jnp.dot(x, wg_s[...], preferred_element_type=F32)
        hu = jnp.dot(x, wu_s[...], preferred_element_type=F32)
        a = (hg / (1.0 + jnp.exp(-hg))) * hu
        _store_row_tiles(y_ref, (), jnp.dot(a.astype(BF16), wd_s[...], preferred_element_type=F32))
        for r in range(MOE_BLOCK):
            _row_copy(h_hbm, tokn_ref[0, 0, r], xg_s.at[nxt, pl.ds(r * SUBLANES, SUBLANES), :],
                      sem.at[nxt]).start(priority=r % 2)

    @pl.when(jnp.logical_or(b == n_active, b == n_active + 1))
    def _():
        wait_rows()

    @pl.when(jnp.logical_not(active))
    def _():
        y_ref[...] = jnp.zeros(y_ref.shape, y_ref.dtype)


def _moe_blocks(blk_e, n_active, buf_tok, h2, w_gate, w_up, w_down):
    ne, d, de = w_gate.shape
    tiled = (MOE_BLOCK * SUBLANES, d // SUBLANES)
    nblk = buf_tok.shape[0]
    blk = lambda b: jnp.minimum(b, nblk - 1)
    wmap = lambda b, be, na: (be[blk(b)], 0, 0)
    tokmap = lambda off: (lambda b, be, na: (blk(b + off), 0, 0))
    smem_tok = lambda off: pl.BlockSpec((1, 1, MOE_BLOCK), tokmap(off), memory_space=pltpu.SMEM)
    return pl.pallas_call(
        _moe_kernel,
        grid_spec=pltpu.PrefetchScalarGridSpec(
            num_scalar_prefetch=2, grid=(nblk + 1,),
            in_specs=[pl.BlockSpec((1, 1, MOE_BLOCK), lambda b, be, na: (0, 0, 0), memory_space=pltpu.SMEM),
                      pl.BlockSpec((1, 1, MOE_BLOCK), lambda b, be, na: (1, 0, 0), memory_space=pltpu.SMEM),
                      smem_tok(2),
                      pl.BlockSpec(memory_space=pl.ANY),
                      pl.BlockSpec((1, d, de), wmap), pl.BlockSpec((1, d, de), wmap),
                      pl.BlockSpec((1, de, d), wmap)],
            out_specs=pl.BlockSpec(tiled, lambda b, be, na: (blk(b), 0)),
            scratch_shapes=[pltpu.VMEM((d, de), BF16), pltpu.VMEM((d, de), BF16),
                            pltpu.VMEM((de, d), BF16), pltpu.VMEM((MOE_SLOTS,) + tiled, F32),
                            pltpu.SemaphoreType.DMA((MOE_SLOTS,))]),
        out_shape=jax.ShapeDtypeStruct((nblk * tiled[0], tiled[1]), F32),
        compiler_params=_cparams("arbitrary"),
        name="moe_experts",
    )(blk_e, n_active, buf_tok, buf_tok, buf_tok, h2, w_gate, w_up, w_down)


def _final_kernel(d0_ref, dn_ref, x_ref, w_ref, gt_ref, g_ref, y_hbm, o_ref, yg_s, sem):
    i = pl.program_id(0)
    slot = i % 2
    tm = x_ref.shape[0]

    def start_tile(d_ref, slot_):
        def body(r, carry):
            for k in range(TOP_K):
                _row_copy(y_hbm, d_ref[0, k, r], yg_s.at[slot_, k, pl.ds(r * SUBLANES, SUBLANES), :],
                          sem.at[slot_]).start(priority=k)
            return carry

        lax.fori_loop(0, tm, body, 0, unroll=8)

    @pl.when(i == 0)
    def _():
        start_tile(d0_ref, 0)

    @pl.when(i + 1 < pl.num_programs(0))
    def _():
        start_tile(dn_ref, 1 - slot)

    def wait_row(r, carry):
        for k in range(TOP_K):
            _row_copy(y_hbm, 0, yg_s.at[slot, k, pl.ds(r * SUBLANES, SUBLANES), :], sem.at[slot]).wait()
        return carry

    lax.fori_loop(0, tm, wait_row, 0, unroll=8)
    w = w_ref[...]
    y = _load_row_tiles(yg_s, (slot, 0), tm) * w[:, 0:1] + _load_row_tiles(yg_s, (slot, 1), tm) * w[:, 1:2]
    x = x_ref[...] + gt_ref[...] * y
    ms = jnp.mean(x * x, axis=-1, keepdims=True)
    o_ref[...] = (x * lax.rsqrt(ms + EPS)) * g_ref[...]


def _final(x_new, yb, dest, wts, gt2, gfin):
    l, d = x_new.shape
    n, _, tm = dest.shape
    row = lambda w: pl.BlockSpec((tm, w), lambda i: (i, 0))
    return pl.pallas_call(
        _final_kernel,
        grid=(n,),
        in_specs=[pl.BlockSpec((1, TOP_K, tm), lambda i: (0, 0, 0), memory_space=pltpu.SMEM),
                  pl.BlockSpec((1, TOP_K, tm), lambda i: (jnp.minimum(i + 1, n - 1), 0, 0),
                               memory_space=pltpu.SMEM),
                  row(d), row(wts.shape[1]), _const_spec((1, d)), _const_spec((1, d)),
                  pl.BlockSpec(memory_space=pl.ANY)],
        out_specs=row(d),
        out_shape=jax.ShapeDtypeStruct((l, d), F32),
        scratch_shapes=[pltpu.VMEM((2, TOP_K, tm * SUBLANES, d // SUBLANES), F32), pltpu.SemaphoreType.DMA((2,))],
        compiler_params=_cparams("arbitrary"),
        name="combine_final_norm",
    )(dest, dest, x_new, wts, gt2, gfin, yb)


def _rope_tables(rows_count):
    row = jnp.repeat(jnp.arange(rows_count), GRID_W).astype(F32)
    col = jnp.tile(jnp.arange(GRID_W), rows_count).astype(F32)
    freqs = ROPE_THETA ** (-jnp.arange(0, ROPE_AXIS_DIM, 2, dtype=F32) / ROPE_AXIS_DIM)
    ang = jnp.concatenate([row[:, None] * freqs, col[:, None] * freqs], axis=-1)
    ang = jnp.repeat(ang, 2, axis=-1)
    sign = jnp.where(jnp.arange(HEAD_DIM) % 2 == 0, -1.0, 1.0).astype(F32)
    cos = jnp.tile(jnp.cos(ang), (1, LANES // HEAD_DIM))
    sin = jnp.tile(jnp.sin(ang) * sign, (1, LANES // HEAD_DIM))
    return cos, sin


def _dispatch(ids, n_tokens):
    a = TOP_K * n_tokens
    eid_f = ids[:TOP_K].reshape(a)
    tok_f = jnp.tile(jnp.arange(n_tokens, dtype=jnp.int32), TOP_K)
    onehot = (eid_f[:, None] == jnp.arange(N_EXPERTS, dtype=jnp.int32)[None, :]).astype(jnp.int32)
    incl = jnp.cumsum(onehot, axis=0)
    counts = incl[-1]
    rank = jnp.sum((incl - onehot) * onehot, axis=1)
    pcounts = (counts + MOE_BLOCK - 1) // MOE_BLOCK * MOE_BLOCK
    pends = jnp.cumsum(pcounts)
    pstarts = pends - pcounts
    dest = jnp.sum(onehot * pstarts[None, :], axis=1) + rank
    p = a + N_EXPERTS * MOE_BLOCK
    nblk = p // MOE_BLOCK
    buf_tok = jnp.zeros((p,), jnp.int32).at[dest].set(tok_f)
    n_active = pends[-1] // MOE_BLOCK
    blk_start = jnp.arange(nblk, dtype=jnp.int32) * MOE_BLOCK
    blk_e = jnp.minimum(jnp.sum(blk_start[:, None] >= pends[None, :], axis=1), N_EXPERTS - 1)
    blk_e = jnp.where(jnp.arange(nblk) < n_active, blk_e, blk_e[n_active - 1]).astype(jnp.int32)
    return buf_tok, dest.reshape(TOP_K, n_tokens), blk_e, n_active.astype(jnp.int32).reshape(1)


def _split_bf16(w):
    hi = w.astype(BF16)
    return jnp.stack([hi, (w - hi.astype(F32)).astype(BF16)])


def kernel(x, c, ctx, c_ctx, w_ada, b_ada, norm1_g, w_in, attn_q_norm, attn_k_norm, ret_decay_fwd,
           ret_decay_bwd, ret_gn_g, ret_gn_b, w_out, norm2_g, moe_w_grp, moe_b_grp, moe_w_exp, moe_b_exp,
           moe_w_gate, moe_w_up, moe_w_down, final_norm_g):
    b, l, d = x.shape
    cl = ctx.shape[1]
    assert b == 1 and w_ada.shape[0] == 1, "single batch element, single layer"
    x2, ctx2 = x[0], ctx[0]

    cvecs = jnp.zeros((SUBLANES, d), F32).at[0].set(c[0]).at[1].set(c_ctx)
    mod = _ada(cvecs, w_ada[0], b_ada[0])
    sh1, sc1, gt1, sh2, sc2, gt2 = [mod[0:1, i * d:(i + 1) * d] for i in range(6)]
    csh1, csc1 = mod[1:2, 0:d], mod[1:2, d:2 * d]

    w_in_bf = w_in[0].astype(BF16)
    head_of = jnp.arange(ATTN_WIDTH) // HEAD_DIM
    bd = jnp.where(head_of[:, None] == head_of[None, :], 1.0 / HEAD_DIM, 0.0).astype(BF16)
    gq = jnp.tile(attn_q_norm[0], ATTN_HEADS)[None, :]
    gk = jnp.tile(attn_k_norm[0], ATTN_KV_HEADS)[None, :]
    g1 = norm1_g[0][None, :]
    cos, sin = _rope_tables(l // GRID_W)
    bound = (HEAD_DIM ** 0.5 * LOG2E * 1.01) * jnp.max(jnp.abs(attn_q_norm[0])) * jnp.max(jnp.abs(attn_k_norm[0]))
    bound = bound.astype(BF16).astype(F32)
    neg_bound = jnp.full((1, LANES), -1.0, F32) * bound
    q_t, ka, va, qr, kr, vr, gr = _inproj(x2, sh1, sc1, g1, w_in_bf, bd, gq, gk, cos, sin, neg_bound)
    _, ka_c, va_c, _, kr_c, vr_c, _ = _inproj(ctx2, csh1, csc1, g1, w_in_bf, bd, gq, gk,
                                              jnp.ones((cl, LANES), F32), jnp.zeros((cl, LANES), F32), neg_bound)

    nk = cl + l
    tk = _key_chunk(nk)
    k_all = jnp.concatenate([ka_c, ka], axis=0).reshape(nk // tk, tk, ATTN_KV_HEADS, HEAD_DIM)
    k_aug = jnp.zeros(k_all.shape[:3] + (ATTN_AUG_DIM - HEAD_DIM,), BF16).at[..., 0].set(1.0)
    k_all = jnp.concatenate([k_all, k_aug], axis=-1)
    v_all = jnp.concatenate([va_c, va], axis=0).reshape(nk // tk, tk, ATTN_KV_HEADS, HEAD_DIM)
    attn_args = (q_t, k_all.transpose(2, 0, 1, 3), v_all.transpose(2, 0, 3, 1))
    o_t = lax.cond(bound <= MAX_UNSHIFTED_BOUND,
                   functools.partial(_attention, running_max=False),
                   functools.partial(_attention, running_max=True), *attn_args)

    log_gf = jax.nn.log_sigmoid(ret_decay_fwd[0].astype(F32))
    log_gb = jax.nn.log_sigmoid(ret_decay_bwd[0].astype(F32))
    tab_f, gc_f = _retention_tables(log_gf, False)
    tab_b, gc_b = _retention_tables(log_gb, True)
    o_f = _retention_pass(gc_f, qr, kr, vr, kr_c, vr_c, tab_f, None, reverse=False)
    ret = _retention_pass(gc_b, qr, kr, vr, kr_c, vr_c, tab_b,
                          (o_f, gr, ret_gn_g[0][None, :], ret_gn_b[0][None, :]), reverse=True)

    wr = jnp.zeros((ROUTER_ROWS, d), F32)
    wr = wr.at[:N_GROUPS].set(moe_w_grp[0].T).at[EXPERT_ROW0:EXPERT_ROW0 + N_EXPERTS].set(moe_w_exp[0].T)
    br = jnp.zeros((ROUTER_ROWS, 1), F32)
    br = br.at[:N_GROUPS, 0].set(moe_b_grp[0]).at[EXPERT_ROW0:EXPERT_ROW0 + N_EXPERTS, 0].set(moe_b_exp[0])
    x_new, h2, ids, wts = _outproj_router(x2, o_t, ret, w_out[0].astype(BF16), gt1, norm2_g[0][None, :],
                                          sh2, sc2, _split_bf16(wr), br)

    buf_tok, dest, blk_e, n_active = _dispatch(ids, l)
    yb = _moe_blocks(blk_e, n_active, buf_tok.reshape(-1, 1, MOE_BLOCK), h2, moe_w_gate[0], moe_w_up[0],
                     moe_w_down[0])
    tf = min(FINAL_TILE, l)
    dest_t = dest.reshape(TOP_K, l // tf, tf).transpose(1, 0, 2)
    out = _final(x_new, yb, dest_t, wts[:TOP_K].T, gt2, final_norm_g[None, :])
    return out[None]
```

```python
import functools

import jax
import jax.numpy as jnp
import numpy as np
from jax import lax
from jax.experimental import pallas as pl
from jax.experimental.pallas import tpu as pltpu

F32 = jnp.float32
BF16 = jnp.bfloat16

GRID_W = 64
HEAD_DIM = 64
ATTN_HEADS = 8
ATTN_KV_HEADS = 2
GQA_GROUP = ATTN_HEADS // ATTN_KV_HEADS
RET_HEADS = 4
RET_QK_DIM = 64
RET_V_DIM = 128
RET_CHUNK = 256
ATTN_WIDTH = ATTN_HEADS * HEAD_DIM
KV_WIDTH = ATTN_KV_HEADS * HEAD_DIM
RET_QK_WIDTH = RET_HEADS * RET_QK_DIM
RET_WIDTH = RET_HEADS * RET_V_DIM
PROJ_SIZES = (ATTN_WIDTH, KV_WIDTH, KV_WIDTH, RET_QK_WIDTH, RET_QK_WIDTH, RET_WIDTH, RET_WIDTH)
PROJ_OFFS = tuple(int(v) for v in np.cumsum((0,) + PROJ_SIZES))
PROJ_DIM = PROJ_OFFS[-1]
ROPE_THETA = 10000.0
ROPE_AXIS_DIM = HEAD_DIM // 2
N_GROUPS = 4
EXPERTS_PER_GROUP = 8
N_EXPERTS = N_GROUPS * EXPERTS_PER_GROUP
TOP_K = 2
D_EXPERT = 512
EPS = 1e-6
LOG2E = 1.4426950408889634
ATTN_AUG_DIM = 128
MAX_UNSHIFTED_BOUND = 60.0

LANES = 128
SUBLANES = 8
VMEM_LIMIT_BYTES = 56 * 1024 * 1024

ROW_TILE = 512
ATTN_Q_TILE = 512
RET_ROW_TILE = 1024
RET_UNROLL = 4
MOE_BLOCK = 256
MOE_SLOTS = 3
FINAL_TILE = 256
ROUTER_ROWS = 128
EXPERT_ROW0 = 8


def _cparams(*sem):
    return pltpu.CompilerParams(dimension_semantics=sem, vmem_limit_bytes=VMEM_LIMIT_BYTES)


def _const_spec(shape):
    nd = len(shape)
    return pl.BlockSpec(shape, lambda *_: (0,) * nd)


def _store_row_tiles(ref, idx, rows):
    n, w = rows.shape[0], rows.shape[1] // SUBLANES
    for s in range(SUBLANES):
        ref[idx + (pl.ds(s, n, stride=SUBLANES), slice(None))] = rows[:, s * w:(s + 1) * w]


def _load_row_tiles(ref, idx, n):
    return jnp.concatenate([ref[idx + (pl.ds(s, n, stride=SUBLANES), slice(None))] for s in range(SUBLANES)],
                           axis=1)


def _ada_kernel(s_ref, w_ref, b_ref, o_ref):
    s = s_ref[...]
    s = s / (1.0 + jnp.exp(-s))
    o_ref[...] = jnp.dot(s, w_ref[...], preferred_element_type=F32,
                         precision=lax.Precision.HIGHEST) + b_ref[...]


def _ada(cvecs, w_ada, b_ada):
    d, n = w_ada.shape
    tn = 1536
    return pl.pallas_call(
        _ada_kernel,
        grid=(n // tn,),
        in_specs=[_const_spec((SUBLANES, d)),
                  pl.BlockSpec((d, tn), lambda j: (0, j)),
                  pl.BlockSpec((1, tn), lambda j: (0, j))],
        out_specs=pl.BlockSpec((SUBLANES, tn), lambda j: (0, j)),
        out_shape=jax.ShapeDtypeStruct((SUBLANES, n), F32),
        compiler_params=_cparams("arbitrary"),
        name="ada_mod",
    )(cvecs, w_ada, b_ada.reshape(1, n))


def _rms_modulate(x, g, sh, sc):
    ms = jnp.mean(x * x, axis=-1, keepdims=True)
    return (x * lax.rsqrt(ms + EPS)) * g * (1.0 + sc) + sh


def _head_mean_sq(v, bd):
    sq = v * v
    hi = sq.astype(BF16)
    lo = (sq - hi.astype(F32)).astype(BF16)
    return (jnp.dot(hi, bd, preferred_element_type=F32) + jnp.dot(lo, bd, preferred_element_type=F32))


def _rope_chunks(v, cos, sin, even):
    outs = []
    for c in range(v.shape[1] // LANES):
        xc = v[:, c * LANES:(c + 1) * LANES]
        nxt = pltpu.roll(xc, LANES - 1, 1)
        prv = pltpu.roll(xc, 1, 1)
        outs.append(xc * cos + jnp.where(even, nxt, prv) * sin)
    return outs


def _inproj_kernel(x_ref, sh_ref, sc_ref, g_ref, w_ref, bd_ref, gq_ref, gk_ref, cos_ref, sin_ref, nb_ref,
                   qt_ref, ka_ref, va_ref, qr_ref, kr_ref, vr_ref, gr_ref):
    h = _rms_modulate(x_ref[...], g_ref[...], sh_ref[...], sc_ref[...])
    proj = jnp.dot(h.astype(BF16), w_ref[...], preferred_element_type=F32)
    o = PROJ_OFFS
    qa, ka, va = proj[:, o[0]:o[1]], proj[:, o[1]:o[2]], proj[:, o[2]:o[3]]
    qr, kr, vr, gr = proj[:, o[3]:o[4]], proj[:, o[4]:o[5]], proj[:, o[5]:o[6]], proj[:, o[6]:o[7]]
    cos, sin = cos_ref[...], sin_ref[...]
    even = (lax.broadcasted_iota(jnp.int32, cos.shape, 1) % 2) == 0
    bd = bd_ref[...]
    scale = HEAD_DIM ** -0.5 * LOG2E

    qn = qa * lax.rsqrt(_head_mean_sq(qa, bd) + EPS) * gq_ref[...]
    tm = qa.shape[0]
    for c, blk in enumerate(_rope_chunks(qn, cos, sin, even)):
        blk_t = (blk * scale).T
        for j in range(LANES // HEAD_DIM):
            kvh, grp = divmod(c * (LANES // HEAD_DIM) + j, GQA_GROUP)
            qt_ref[kvh, 0:HEAD_DIM, grp * tm:(grp + 1) * tm] = blk_t[j * HEAD_DIM:(j + 1) * HEAD_DIM, :].astype(BF16)
    aug_shape = (ATTN_AUG_DIM - HEAD_DIM, GQA_GROUP * tm)
    aug_row = lax.broadcasted_iota(jnp.int32, aug_shape, 0)
    aug = jnp.where(aug_row == 0, nb_ref[0:1, 0:1], 0.0).astype(BF16)
    for kvh in range(ATTN_KV_HEADS):
        qt_ref[kvh, HEAD_DIM:ATTN_AUG_DIM, :] = aug
    kn = ka * lax.rsqrt(_head_mean_sq(ka, bd[:KV_WIDTH, :KV_WIDTH]) + EPS) * gk_ref[...]
    ka_ref[...] = _rope_chunks(kn, cos, sin, even)[0].astype(BF16)
    va_ref[...] = va.astype(BF16)
    for c, blk in enumerate(_rope_chunks(qr, cos, sin, even)):
        qr_ref[:, c * LANES:(c + 1) * LANES] = blk.astype(BF16)
    rscale = RET_QK_DIM ** -0.5
    for c, blk in enumerate(_rope_chunks(kr, cos, sin, even)):
        kr_ref[:, c * LANES:(c + 1) * LANES] = (blk * rscale).astype(BF16)
    vr_ref[...] = vr.astype(BF16)
    gr_ref[...] = gr


def _inproj(x2, sh, sc, g1, w_bf, bd, gq, gk, cos, sin, neg_bound):
    rows, d = x2.shape
    tm = min(ROW_TILE, rows)
    row = lambda w: pl.BlockSpec((tm, w), lambda i: (i, 0))
    widths = PROJ_SIZES[1:]
    dtypes = (BF16,) * 5 + (F32,)
    qt_shape = (ATTN_KV_HEADS, ATTN_AUG_DIM, GQA_GROUP * rows)
    return pl.pallas_call(
        _inproj_kernel,
        grid=(rows // tm,),
        in_specs=[row(d), _const_spec((1, d)), _const_spec((1, d)), _const_spec((1, d)),
                  _const_spec(w_bf.shape), _const_spec(bd.shape),
                  _const_spec((1, ATTN_WIDTH)), _const_spec((1, KV_WIDTH)),
                  row(LANES), row(LANES), _const_spec((1, LANES))],
        out_specs=[pl.BlockSpec((ATTN_KV_HEADS, ATTN_AUG_DIM, GQA_GROUP * tm), lambda i: (0, 0, i))]
        + [row(w) for w in widths],
        out_shape=[jax.ShapeDtypeStruct(qt_shape, BF16)]
        + [jax.ShapeDtypeStruct((rows, w), dt) for w, dt in zip(widths, dtypes)],
        compiler_params=_cparams("arbitrary"),
        name="norm_inproj",
    )(x2, sh, sc, g1, w_bf, bd, gq, gk, cos, sin, neg_bound)


def _attn_kernel(q_ref, k_ref, v_ref, o_ref, m_ref, l_ref, acc_ref, *, n_chunks, running_max):
    m_ref[...] = jnp.full(m_ref.shape, -jnp.inf, F32)
    l_ref[...] = jnp.zeros(l_ref.shape, F32)
    acc_ref[...] = jnp.zeros(acc_ref.shape, F32)

    def body(c, carry):
        kc = k_ref[0, c]
        vc = v_ref[0, c]
        s = jnp.dot(kc, q_ref[0], preferred_element_type=F32)
        if running_max:
            m_old = m_ref[...]
            m_new = jnp.maximum(m_old, jnp.max(s, axis=0, keepdims=True))
            alpha = jnp.exp2(m_old - m_new)
            p = jnp.exp2(s - m_new)
            l_ref[...] = alpha * l_ref[...] + jnp.sum(p, axis=0, keepdims=True)
            acc_ref[...] = acc_ref[...] * alpha + jnp.dot(vc, p.astype(BF16), preferred_element_type=F32)
            m_ref[...] = m_new
        else:
            p = jnp.exp2(s)
            l_ref[...] += jnp.sum(p, axis=0, keepdims=True)
            acc_ref[...] += jnp.dot(vc, p.astype(BF16), preferred_element_type=F32)
        return carry

    lax.fori_loop(0, n_chunks, body, 0)
    o_ref[0] = (acc_ref[...] / l_ref[...]).astype(o_ref.dtype)


def _key_chunk(nk):
    for tk in (3328, 1280, 640, 256, 128):
        if nk % tk == 0:
            return tk
    raise ValueError(f"unsupported key count {nk}")


def _attention(q_t, k_c, v_t, *, running_max):
    kv, da, gl = q_t.shape
    _, n_chunks, d, tk = v_t.shape
    nq = GQA_GROUP * min(ATTN_Q_TILE, gl // GQA_GROUP)
    return pl.pallas_call(
        functools.partial(_attn_kernel, n_chunks=n_chunks, running_max=running_max),
        grid=(kv, gl // nq),
        in_specs=[pl.BlockSpec((1, da, nq), lambda b, i: (b, 0, i)),
                  pl.BlockSpec((1, n_chunks, tk, da), lambda b, i: (b, 0, 0, 0)),
                  pl.BlockSpec((1, n_chunks, d, tk), lambda b, i: (b, 0, 0, 0))],
        out_specs=pl.BlockSpec((1, d, nq), lambda b, i: (b, 0, i)),
        out_shape=jax.ShapeDtypeStruct((kv, d, gl), BF16),
        scratch_shapes=[pltpu.VMEM((1, nq), F32), pltpu.VMEM((1, nq), F32), pltpu.VMEM((d, nq), F32)],
        compiler_params=_cparams("arbitrary", "arbitrary"),
        name="flash_attn",
    )(q_t, k_c, v_t)


def _ret_kernel(gc_ref, q_ref, k_ref, v_ref, kc_ref, vc_ref, tab_ref, *rest, reverse, final):
    if final:
        of_ref, gate_ref, gng_ref, gnb_ref, out_ref, s_ref = rest
    else:
        out_ref, s_ref = rest
    c_rows = RET_CHUNK
    n_local = q_ref.shape[0] // c_rows
    n_ctx = kc_ref.shape[0] // c_rows
    lane_head = lax.broadcasted_iota(jnp.int32, (c_rows, RET_QK_WIDTH), 1) // RET_QK_DIM

    def state_update(kch, vch):
        for h in range(RET_HEADS):
            vz = (vch[:, h * RET_V_DIM:(h + 1) * RET_V_DIM].astype(F32)
                  * tab_ref[1, h, :, 0:RET_V_DIM]).astype(BF16)
            u = lax.dot_general(kch, vz, (((0,), (0,)), ((), ())), preferred_element_type=F32)
            s_ref[h] = gc_ref[h] * s_ref[h] + u

    @pl.when(pl.program_id(0) == 0)
    def _():
        s_ref[...] = jnp.zeros(s_ref.shape, F32)
        order = range(n_ctx - 1, -1, -1) if reverse else range(n_ctx)
        for cc in order:
            state_update(kc_ref[cc * c_rows:(cc + 1) * c_rows, :], vc_ref[cc * c_rows:(cc + 1) * c_rows, :])

    def chunk(ci, carry):
        c = (n_local - 1 - ci) if reverse else ci
        rows = pl.ds(pl.multiple_of(c * c_rows, c_rows), c_rows)
        q, k, v = q_ref[rows, :], k_ref[rows, :], v_ref[rows, :]
        for h in range(RET_HEADS):
            cols = slice(h * RET_V_DIM, (h + 1) * RET_V_DIM)
            qm = jnp.where(lane_head == h, q, jnp.zeros_like(q))
            sc = lax.dot_general(qm, k, (((1,), (1,)), ((), ())), preferred_element_type=F32)
            sc = sc * tab_ref[0, h]
            o = jnp.dot(sc.astype(BF16), v[:, cols], preferred_element_type=F32)
            o = o + (jnp.dot(qm, s_ref[h].astype(BF16), preferred_element_type=F32)
                     * tab_ref[2, h, :, 0:RET_V_DIM])
            if final:
                o = o + of_ref[rows, cols]
                mu = jnp.mean(o, axis=-1, keepdims=True)
                oc = o - mu
                var = jnp.mean(oc * oc, axis=-1, keepdims=True)
                on = oc * lax.rsqrt(var + EPS)
                gate = gate_ref[rows, cols]
                y = (on * gng_ref[:, cols] + gnb_ref[:, cols]) * (gate / (1.0 + jnp.exp(-gate)))
                out_ref[rows, cols] = y.astype(out_ref.dtype)
            else:
                out_ref[rows, cols] = o
        state_update(k, v)
        return carry

    lax.fori_loop(0, n_local, chunk, 0, unroll=RET_UNROLL)


def _retention_pass(gc, qr, kr, vr, kr_c, vr_c, tabs, extra, *, reverse):
    l = qr.shape[0]
    tr = min(RET_ROW_TILE, l)
    nt = l // tr
    final = extra is not None
    idx = (lambda j, gc_: (nt - 1 - j, 0)) if reverse else (lambda j, gc_: (j, 0))
    const = lambda shape: pl.BlockSpec(shape, lambda j, gc_: (0,) * len(shape))
    in_specs = [pl.BlockSpec((tr, RET_QK_WIDTH), idx), pl.BlockSpec((tr, RET_QK_WIDTH), idx),
                pl.BlockSpec((tr, RET_WIDTH), idx),
                const(kr_c.shape), const(vr_c.shape), const(tabs.shape)]
    args = [qr, kr, vr, kr_c, vr_c, tabs]
    if final:
        of, gate, gng, gnb = extra
        in_specs += [pl.BlockSpec((tr, RET_WIDTH), idx), pl.BlockSpec((tr, RET_WIDTH), idx),
                     const(gng.shape), const(gnb.shape)]
        args += [of, gate, gng, gnb]
    return pl.pallas_call(
        functools.partial(_ret_kernel, reverse=reverse, final=final),
        grid_spec=pltpu.PrefetchScalarGridSpec(
            num_scalar_prefetch=1, grid=(nt,), in_specs=in_specs,
            out_specs=pl.BlockSpec((tr, RET_WIDTH), idx),
            scratch_shapes=[pltpu.VMEM((RET_HEADS, RET_QK_WIDTH, RET_V_DIM), F32)]),
        out_shape=jax.ShapeDtypeStruct((l, RET_WIDTH), BF16 if final else F32),
        compiler_params=_cparams("arbitrary"),
        name="retention_bwd" if reverse else "retention_fwd",
    )(gc, *args)


def _retention_tables(log_g, reverse):
    c = RET_CHUNK
    pos = jnp.arange(c, dtype=F32)
    diff = (pos[None, :] - pos[:, None]) if reverse else (pos[:, None] - pos[None, :])
    mask = (diff > 0) if reverse else (diff >= 0)
    lg = log_g[:, None, None]
    dmat = jnp.where(mask, jnp.exp(lg * jnp.where(mask, diff, 0.0)[None]), 0.0)
    zeta = jnp.exp(log_g[:, None] * (pos if reverse else (c - 1 - pos))[None, :])
    xi = jnp.exp(log_g[:, None] * ((c - pos) if reverse else (pos + 1.0))[None, :])
    ones = jnp.ones((1, 1, c), F32)
    tabs = jnp.stack([dmat, zeta[:, :, None] * ones, xi[:, :, None] * ones])
    return tabs, jnp.exp(log_g * c)


def _outproj_kernel(x_ref, a_ref, r_ref, w_ref, gt_ref, g2_ref, sh_ref, sc_ref, wr_ref, br_ref,
                    xn_ref, h2_ref, ids_ref, wts_ref):
    tm = x_ref.shape[0]
    mix = jnp.dot(r_ref[...], w_ref[ATTN_WIDTH:, :], preferred_element_type=F32)
    per = LANES // HEAD_DIM
    for pair in range(ATTN_HEADS // per):
        pieces = []
        for j in range(per):
            kvh, grp = divmod(pair * per + j, GQA_GROUP)
            pieces.append(a_ref[kvh, :, grp * tm:(grp + 1) * tm])
        o_t = jnp.concatenate(pieces, axis=0)
        mix = mix + lax.dot_general(o_t, w_ref[pair * LANES:(pair + 1) * LANES, :],
                                    (((0,), (0,)), ((), ())), preferred_element_type=F32)
    x_new = x_ref[...] + gt_ref[...] * mix
    xn_ref[...] = x_new
    h2 = _rms_modulate(x_new, g2_ref[...], sh_ref[...], sc_ref[...])
    _store_row_tiles(h2_ref, (), h2)
    hi = h2.astype(BF16)
    lo = (h2 - hi.astype(F32)).astype(BF16)

    nt = (((1,), (1,)), ((), ()))
    logits = (lax.dot_general(wr_ref[0], hi, nt, preferred_element_type=F32)
              + lax.dot_general(wr_ref[0], lo, nt, preferred_element_type=F32)
              + lax.dot_general(wr_ref[1], hi, nt, preferred_element_type=F32)) + br_ref[...]
    rows = [logits[g:g + 1, :] for g in range(N_GROUPS)]
    gmax = functools.reduce(jnp.maximum, rows)
    gidx = jnp.full(gmax.shape, N_GROUPS - 1, jnp.int32)
    for g in range(N_GROUPS - 2, -1, -1):
        gidx = jnp.where(rows[g] == gmax, g, gidx)
    p_g = 1.0 / functools.reduce(jnp.add, [jnp.exp(r - gmax) for r in rows])
    sel = logits[EXPERT_ROW0 + (N_GROUPS - 1) * EXPERTS_PER_GROUP:EXPERT_ROW0 + N_GROUPS * EXPERTS_PER_GROUP, :]
    for g in range(N_GROUPS - 2, -1, -1):
        blk = logits[EXPERT_ROW0 + g * EXPERTS_PER_GROUP:EXPERT_ROW0 + (g + 1) * EXPERTS_PER_GROUP, :]
        sel = jnp.where(gidx == g, blk, sel)
    ridx = lax.broadcasted_iota(jnp.int32, sel.shape, 0)
    m1 = jnp.max(sel, axis=0, keepdims=True)
    i1 = jnp.min(jnp.where(sel == m1, ridx, EXPERTS_PER_GROUP), axis=0, keepdims=True)
    sel2 = jnp.where(ridx == i1, -jnp.inf, sel)
    m2 = jnp.max(sel2, axis=0, keepdims=True)
    i2 = jnp.min(jnp.where(sel2 == m2, ridx, EXPERTS_PER_GROUP), axis=0, keepdims=True)
    e2 = jnp.exp(m2 - m1)
    w1 = p_g / (1.0 + e2)
    w2 = w1 * e2
    out_rows = lax.broadcasted_iota(jnp.int32, ids_ref.shape, 0)
    ids_ref[...] = jnp.where(out_rows == 0, gidx * EXPERTS_PER_GROUP + i1,
                             jnp.where(out_rows == 1, gidx * EXPERTS_PER_GROUP + i2, 0))
    wts_ref[...] = jnp.where(out_rows == 0, w1, jnp.where(out_rows == 1, w2, 0.0))


def _outproj_router(x2, attn, ret, w_out_bf, gt1, g2, sh2, sc2, wr, br):
    l, d = x2.shape
    tm = min(ROW_TILE, l)
    row = lambda w: pl.BlockSpec((tm, w), lambda i: (i, 0))
    col = pl.BlockSpec((SUBLANES, tm), lambda i: (0, i))
    return pl.pallas_call(
        _outproj_kernel,
        grid=(l // tm,),
        in_specs=[row(d),
                  pl.BlockSpec((ATTN_KV_HEADS, HEAD_DIM, GQA_GROUP * tm), lambda i: (0, 0, i)),
                  row(ret.shape[1]), _const_spec(w_out_bf.shape),
                  _const_spec((1, d)), _const_spec((1, d)), _const_spec((1, d)), _const_spec((1, d)),
                  _const_spec(wr.shape), _const_spec(br.shape)],
        out_specs=[row(d), pl.BlockSpec((tm * SUBLANES, d // SUBLANES), lambda i: (i, 0)), col, col],
        out_shape=[jax.ShapeDtypeStruct((l, d), F32), jax.ShapeDtypeStruct((l * SUBLANES, d // SUBLANES), F32),
                   jax.ShapeDtypeStruct((SUBLANES, l), jnp.int32),
                   jax.ShapeDtypeStruct((SUBLANES, l), F32)],
        compiler_params=_cparams("arbitrary"),
        name="outproj_router",
    )(x2, attn, ret, w_out_bf, gt1, g2, sh2, sc2, wr, br)


def _row_copy(src_hbm, src_row, dst_ref, sem):
    return pltpu.make_async_copy(src_hbm.at[pl.ds(src_row * SUBLANES, SUBLANES), :], dst_ref, sem)


def _moe_kernel(be_ref, na_ref, tok0_ref, tok1_ref, tokn_ref, h_hbm, wg_ref, wu_ref, wd_ref, y_ref,
                wg_s, wu_s, wd_s, xg_s, sem):
    b = pl.program_id(0)
    slot = lax.rem(b, MOE_SLOTS)
    nxt = lax.rem(b + 2, MOE_SLOTS)
    n_active = na_ref[0]
    active = b < n_active
    last = be_ref.shape[0] - 1
    changed = jnp.logical_or(b == 0, be_ref[jnp.minimum(b, last)] != be_ref[jnp.clip(b - 1, 0, last)])

    def wait_rows():
        pltpu.make_async_copy(h_hbm.at[pl.ds(0, MOE_BLOCK * SUBLANES), :], xg_s.at[slot], sem.at[slot]).wait()

    @pl.when(b == 0)
    def _():
        def body(r, carry):
            _row_copy(h_hbm, tok0_ref[0, 0, r], xg_s.at[0, pl.ds(r * SUBLANES, SUBLANES), :], sem.at[0]).start()
            _row_copy(h_hbm, tok1_ref[0, 0, r], xg_s.at[1, pl.ds(r * SUBLANES, SUBLANES), :], sem.at[1]).start()
            return carry

        lax.fori_loop(0, MOE_BLOCK, body, 0, unroll=8)

    @pl.when(jnp.logical_and(active, changed))
    def _():
        wg_s[...] = wg_ref[0].astype(BF16)
        wu_s[...] = wu_ref[0].astype(BF16)
        wd_s[...] = wd_ref[0].astype(BF16)

    @pl.when(active)
    def _():
        wait_rows()
        x = _load_row_tiles(xg_s, (slot,), MOE_BLOCK).astype(BF16)
        hg = jnp.dot(x, wg_s[...], preferred_element_type=F32)
        hu = jnp.dot(x, wu_s[...], preferred_element_type=F32)
        a = (hg / (1.0 + jnp.exp(-hg))) * hu
        _store_row_tiles(y_ref, (), jnp.dot(a.astype(BF16), wd_s[...], preferred_element_type=F32))
        for r in range(MOE_BLOCK):
            _row_copy(h_hbm, tokn_ref[0, 0, r], xg_s.at[nxt, pl.ds(r * SUBLANES, SUBLANES), :],
                      sem.at[nxt]).start(priority=r % 2)

    @pl.when(jnp.logical_or(b == n_active, b == n_active + 1))
    def _():
        wait_rows()

    @pl.when(jnp.logical_not(active))
    def _():
        y_ref[...] = jnp.zeros(y_ref.shape, y_ref.dtype)


def _moe_blocks(blk_e, n_active, buf_tok, h2, w_gate, w_up, w_down):
    ne, d, de = w_gate.shape
    tiled = (MOE_BLOCK * SUBLANES, d // SUBLANES)
    nblk = buf_tok.shape[0]
    blk = lambda b: jnp.minimum(b, nblk - 1)
    wmap = lambda b, be, na: (be[blk(b)], 0, 0)
    tokmap = lambda off: (lambda b, be, na: (blk(b + off), 0, 0))
    smem_tok = lambda off: pl.BlockSpec((1, 1, MOE_BLOCK), tokmap(off), memory_space=pltpu.SMEM)
    return pl.pallas_call(
        _moe_kernel,
        grid_spec=pltpu.PrefetchScalarGridSpec(
            num_scalar_prefetch=2, grid=(nblk + 1,),
            in_specs=[pl.BlockSpec((1, 1, MOE_BLOCK), lambda b, be, na: (0, 0, 0), memory_space=pltpu.SMEM),
                      pl.BlockSpec((1, 1, MOE_BLOCK), lambda b, be, na: (1, 0, 0), memory_space=pltpu.SMEM),
                      smem_tok(2),
                      pl.BlockSpec(memory_space=pl.ANY),
                      pl.BlockSpec((1, d, de), wmap), pl.BlockSpec((1, d, de), wmap),
                      pl.BlockSpec((1, de, d), wmap)],
            out_specs=pl.BlockSpec(tiled, lambda b, be, na: (blk(b), 0)),
            scratch_shapes=[pltpu.VMEM((d, de), BF16), pltpu.VMEM((d, de), BF16),
                            pltpu.VMEM((de, d), BF16), pltpu.VMEM((MOE_SLOTS,) + tiled, F32),
                            pltpu.SemaphoreType.DMA((MOE_SLOTS,))]),
        out_shape=jax.ShapeDtypeStruct((nblk * tiled[0], tiled[1]), F32),
        compiler_params=_cparams("arbitrary"),
        name="moe_experts",
    )(blk_e, n_active, buf_tok, buf_tok, buf_tok, h2, w_gate, w_up, w_down)


def _final_kernel(d0_ref, dn_ref, x_ref, w_ref, gt_ref, g_ref, y_hbm, o_ref, yg_s, sem):
    i = pl.program_id(0)
    slot = i % 2
    tm = x_ref.shape[0]

    def start_tile(d_ref, slot_):
        def body(r, carry):
            for k in range(TOP_K):
                _row_copy(y_hbm, d_ref[0, k, r], yg_s.at[slot_, k, pl.ds(r * SUBLANES, SUBLANES), :],
                          sem.at[slot_]).start(priority=k)
            return carry

        lax.fori_loop(0, tm, body, 0, unroll=8)

    @pl.when(i == 0)
    def _():
        start_tile(d0_ref, 0)

    @pl.when(i + 1 < pl.num_programs(0))
    def _():
        start_tile(dn_ref, 1 - slot)

    for k in range(TOP_K):
        pltpu.make_async_copy(y_hbm.at[pl.ds(0, tm * SUBLANES), :], yg_s.at[slot, k], sem.at[slot]).wait()
    w = w_ref[...]
    y = _load_row_tiles(yg_s, (slot, 0), tm) * w[:, 0:1] + _load_row_tiles(yg_s, (slot, 1), tm) * w[:, 1:2]
    x = x_ref[...] + gt_ref[...] * y
    ms = jnp.mean(x * x, axis=-1, keepdims=True)
    o_ref[...] = (x * lax.rsqrt(ms + EPS)) * g_ref[...]


def _final(x_new, yb, dest, wts, gt2, gfin):
    l, d = x_new.shape
    n, _, tm = dest.shape
    row = lambda w: pl.BlockSpec((tm, w), lambda i: (i, 0))
    return pl.pallas_call(
        _final_kernel,
        grid=(n,),
        in_specs=[pl.BlockSpec((1, TOP_K, tm), lambda i: (0, 0, 0), memory_space=pltpu.SMEM),
                  pl.BlockSpec((1, TOP_K, tm), lambda i: (jnp.minimum(i + 1, n - 1), 0, 0),
                               memory_space=pltpu.SMEM),
                  row(d), row(wts.shape[1]), _const_spec((1, d)), _const_spec((1, d)),
                  pl.BlockSpec(memory_space=pl.ANY)],
        out_specs=row(d),
        out_shape=jax.ShapeDtypeStruct((l, d), F32),
        scratch_shapes=[pltpu.VMEM((2, TOP_K, tm * SUBLANES, d // SUBLANES), F32), pltpu.SemaphoreType.DMA((2,))],
        compiler_params=_cparams("arbitrary"),
        name="combine_final_norm",
    )(dest, dest, x_new, wts, gt2, gfin, yb)


def _rope_tables(rows_count):
    row = jnp.repeat(jnp.arange(rows_count), GRID_W).astype(F32)
    col = jnp.tile(jnp.arange(GRID_W), rows_count).astype(F32)
    freqs = ROPE_THETA ** (-jnp.arange(0, ROPE_AXIS_DIM, 2, dtype=F32) / ROPE_AXIS_DIM)
    ang = jnp.concatenate([row[:, None] * freqs, col[:, None] * freqs], axis=-1)
    ang = jnp.repeat(ang, 2, axis=-1)
    sign = jnp.where(jnp.arange(HEAD_DIM) % 2 == 0, -1.0, 1.0).astype(F32)
    cos = jnp.tile(jnp.cos(ang), (1, LANES // HEAD_DIM))
    sin = jnp.tile(jnp.sin(ang) * sign, (1, LANES // HEAD_DIM))
    return cos, sin


def _dispatch(ids, n_tokens):
    a = TOP_K * n_tokens
    eid_f = ids[:TOP_K].reshape(a)
    tok_f = jnp.tile(jnp.arange(n_tokens, dtype=jnp.int32), TOP_K)
    onehot = (eid_f[:, None] == jnp.arange(N_EXPERTS, dtype=jnp.int32)[None, :]).astype(jnp.int32)
    incl = jnp.cumsum(onehot, axis=0)
    counts = incl[-1]
    rank = jnp.sum((incl - onehot) * onehot, axis=1)
    pcounts = (counts + MOE_BLOCK - 1) // MOE_BLOCK * MOE_BLOCK
    pends = jnp.cumsum(pcounts)
    pstarts = pends - pcounts
    dest = jnp.sum(onehot * pstarts[None, :], axis=1) + rank
    p = a + N_EXPERTS * MOE_BLOCK
    nblk = p // MOE_BLOCK
    buf_tok = jnp.zeros((p,), jnp.int32).at[dest].set(tok_f)
    n_active = pends[-1] // MOE_BLOCK
    blk_start = jnp.arange(nblk, dtype=jnp.int32) * MOE_BLOCK
    blk_e = jnp.minimum(jnp.sum(blk_start[:, None] >= pends[None, :], axis=1), N_EXPERTS - 1)
    blk_e = jnp.where(jnp.arange(nblk) < n_active, blk_e, blk_e[n_active - 1]).astype(jnp.int32)
    return buf_tok, dest.reshape(TOP_K, n_tokens), blk_e, n_active.astype(jnp.int32).reshape(1)


def _split_bf16(w):
    hi = w.astype(BF16)
    return jnp.stack([hi, (w - hi.astype(F32)).astype(BF16)])


def kernel(x, c, ctx, c_ctx, w_ada, b_ada, norm1_g, w_in, attn_q_norm, attn_k_norm, ret_decay_fwd,
           ret_decay_bwd, ret_gn_g, ret_gn_b, w_out, norm2_g, moe_w_grp, moe_b_grp, moe_w_exp, moe_b_exp,
           moe_w_gate, moe_w_up, moe_w_down, final_norm_g):
    b, l, d = x.shape
    cl = ctx.shape[1]
    assert b == 1 and w_ada.shape[0] == 1, "single batch element, single layer"
    x2, ctx2 = x[0], ctx[0]

    cvecs = jnp.zeros((SUBLANES, d), F32).at[0].set(c[0]).at[1].set(c_ctx)
    mod = _ada(cvecs, w_ada[0], b_ada[0])
    sh1, sc1, gt1, sh2, sc2, gt2 = [mod[0:1, i * d:(i + 1) * d] for i in range(6)]
    csh1, csc1 = mod[1:2, 0:d], mod[1:2, d:2 * d]

    w_in_bf = w_in[0].astype(BF16)
    head_of = jnp.arange(ATTN_WIDTH) // HEAD_DIM
    bd = jnp.where(head_of[:, None] == head_of[None, :], 1.0 / HEAD_DIM, 0.0).astype(BF16)
    gq = jnp.tile(attn_q_norm[0], ATTN_HEADS)[None, :]
    gk = jnp.tile(attn_k_norm[0], ATTN_KV_HEADS)[None, :]
    g1 = norm1_g[0][None, :]
    cos, sin = _rope_tables(l // GRID_W)
    bound = (HEAD_DIM ** 0.5 * LOG2E * 1.01) * jnp.max(jnp.abs(attn_q_norm[0])) * jnp.max(jnp.abs(attn_k_norm[0]))
    bound = bound.astype(BF16).astype(F32)
    neg_bound = jnp.full((1, LANES), -1.0, F32) * bound
    q_t, ka, va, qr, kr, vr, gr = _inproj(x2, sh1, sc1, g1, w_in_bf, bd, gq, gk, cos, sin, neg_bound)
    _, ka_c, va_c, _, kr_c, vr_c, _ = _inproj(ctx2, csh1, csc1, g1, w_in_bf, bd, gq, gk,
                                              jnp.ones((cl, LANES), F32), jnp.zeros((cl, LANES), F32), neg_bound)

    nk = cl + l
    tk = _key_chunk(nk)
    k_all = jnp.concatenate([ka_c, ka], axis=0).reshape(nk // tk, tk, ATTN_KV_HEADS, HEAD_DIM)
    k_aug = jnp.zeros(k_all.shape[:3] + (ATTN_AUG_DIM - HEAD_DIM,), BF16).at[..., 0].set(1.0)
    k_all = jnp.concatenate([k_all, k_aug], axis=-1)
    v_all = jnp.concatenate([va_c, va], axis=0).reshape(nk // tk, tk, ATTN_KV_HEADS, HEAD_DIM)
    attn_args = (q_t, k_all.transpose(2, 0, 1, 3), v_all.transpose(2, 0, 3, 1))
    o_t = lax.cond(bound <= MAX_UNSHIFTED_BOUND,
                   functools.partial(_attention, running_max=False),
                   functools.partial(_attention, running_max=True), *attn_args)

    log_gf = jax.nn.log_sigmoid(ret_decay_fwd[0].astype(F32))
    log_gb = jax.nn.log_sigmoid(ret_decay_bwd[0].astype(F32))
    tab_f, gc_f = _retention_tables(log_gf, False)
    tab_b, gc_b = _retention_tables(log_gb, True)
    o_f = _retention_pass(gc_f, qr, kr, vr, kr_c, vr_c, tab_f, None, reverse=False)
    ret = _retention_pass(gc_b, qr, kr, vr, kr_c, vr_c, tab_b,
                          (o_f, gr, ret_gn_g[0][None, :], ret_gn_b[0][None, :]), reverse=True)

    wr = jnp.zeros((ROUTER_ROWS, d), F32)
    wr = wr.at[:N_GROUPS].set(moe_w_grp[0].T).at[EXPERT_ROW0:EXPERT_ROW0 + N_EXPERTS].set(moe_w_exp[0].T)
    br = jnp.zeros((ROUTER_ROWS, 1), F32)
    br = br.at[:N_GROUPS, 0].set(moe_b_grp[0]).at[EXPERT_ROW0:EXPERT_ROW0 + N_EXPERTS, 0].set(moe_b_exp[0])
    x_new, h2, ids, wts = _outproj_router(x2, o_t, ret, w_out[0].astype(BF16), gt1, norm2_g[0][None, :],
                                          sh2, sc2, _split_bf16(wr), br)

    buf_tok, dest, blk_e, n_active = _dispatch(ids, l)
    yb = _moe_blocks(blk_e, n_active, buf_tok.reshape(-1, 1, MOE_BLOCK), h2, moe_w_gate[0], moe_w_up[0],
                     moe_w_down[0])
    tf = min(FINAL_TILE, l)
    dest_t = dest.reshape(TOP_K, l // tf, tf).transpose(1, 0, 2)
    out = _final(x_new, yb, dest_t, wts[:TOP_K].T, gt2, final_norm_g[None, :])
    return out[None]
```

```python
import functools

import jax
import jax.numpy as jnp
import numpy as np
from jax import lax
from jax.experimental import pallas as pl
from jax.experimental.pallas import tpu as pltpu

F32 = jnp.float32
BF16 = jnp.bfloat16

GRID_W = 64
HEAD_DIM = 64
ATTN_HEADS = 8
ATTN_KV_HEADS = 2
GQA_GROUP = ATTN_HEADS // ATTN_KV_HEADS
RET_HEADS = 4
RET_QK_DIM = 64
RET_V_DIM = 128
RET_CHUNK = 256
ATTN_WIDTH = ATTN_HEADS * HEAD_DIM
KV_WIDTH = ATTN_KV_HEADS * HEAD_DIM
RET_QK_WIDTH = RET_HEADS * RET_QK_DIM
RET_WIDTH = RET_HEADS * RET_V_DIM
PROJ_SIZES = (ATTN_WIDTH, KV_WIDTH, KV_WIDTH, RET_QK_WIDTH, RET_QK_WIDTH, RET_WIDTH, RET_WIDTH)
PROJ_OFFS = tuple(int(v) for v in np.cumsum((0,) + PROJ_SIZES))
PROJ_DIM = PROJ_OFFS[-1]
ROPE_THETA = 10000.0
ROPE_AXIS_DIM = HEAD_DIM // 2
N_GROUPS = 4
EXPERTS_PER_GROUP = 8
N_EXPERTS = N_GROUPS * EXPERTS_PER_GROUP
TOP_K = 2
D_EXPERT = 512
EPS = 1e-6
LOG2E = 1.4426950408889634
ATTN_AUG_DIM = 128
MAX_UNSHIFTED_BOUND = 60.0

LANES = 128
SUBLANES = 8
VMEM_LIMIT_BYTES = 56 * 1024 * 1024

ROW_TILE = 512
INPROJ_SUBTILE = 256
ATTN_Q_TILE = 512
RET_ROW_TILE = 1024
RET_UNROLL = 4
MOE_BLOCK = 512
MOE_SLOTS = 3
FINAL_TILE = 256
ROUTER_ROWS = 128
EXPERT_ROW0 = 8


def _cparams(*sem):
    return pltpu.CompilerParams(dimension_semantics=sem, vmem_limit_bytes=VMEM_LIMIT_BYTES)


def _const_spec(shape):
    nd = len(shape)
    return pl.BlockSpec(shape, lambda *_: (0,) * nd)


def _store_row_tiles(ref, idx, rows):
    n, w = rows.shape[0], rows.shape[1] // SUBLANES
    for s in range(SUBLANES):
        ref[idx + (pl.ds(s, n, stride=SUBLANES), slice(None))] = rows[:, s * w:(s + 1) * w]


def _load_row_tiles(ref, idx, n):
    return jnp.concatenate([ref[idx + (pl.ds(s, n, stride=SUBLANES), slice(None))] for s in range(SUBLANES)],
                           axis=1)


def _ada_kernel(s_ref, w_ref, b_ref, o_ref):
    s = s_ref[...]
    s = s / (1.0 + jnp.exp(-s))
    o_ref[...] = jnp.dot(s, w_ref[...], preferred_element_type=F32,
                         precision=lax.Precision.HIGHEST) + b_ref[...]


def _ada(cvecs, w_ada, b_ada):
    d, n = w_ada.shape
    tn = 1536
    return pl.pallas_call(
        _ada_kernel,
        grid=(n // tn,),
        in_specs=[_const_spec((SUBLANES, d)),
                  pl.BlockSpec((d, tn), lambda j: (0, j)),
                  pl.BlockSpec((1, tn), lambda j: (0, j))],
        out_specs=pl.BlockSpec((SUBLANES, tn), lambda j: (0, j)),
        out_shape=jax.ShapeDtypeStruct((SUBLANES, n), F32),
        compiler_params=_cparams("arbitrary"),
        name="ada_mod",
    )(cvecs, w_ada, b_ada.reshape(1, n))


def _rms_modulate(x, g, sh, sc):
    ms = jnp.mean(x * x, axis=-1, keepdims=True)
    return (x * lax.rsqrt(ms + EPS)) * g * (1.0 + sc) + sh


def _head_mean_sq(v, bd):
    sq = v * v
    hi = sq.astype(BF16)
    lo = (sq - hi.astype(F32)).astype(BF16)
    return (jnp.dot(hi, bd, preferred_element_type=F32) + jnp.dot(lo, bd, preferred_element_type=F32))


def _rope_chunks(v, cos, sin, even):
    outs = []
    for c in range(v.shape[1] // LANES):
        xc = v[:, c * LANES:(c + 1) * LANES]
        nxt = pltpu.roll(xc, LANES - 1, 1)
        prv = pltpu.roll(xc, 1, 1)
        outs.append(xc * cos + jnp.where(even, nxt, prv) * sin)
    return outs


def _inproj_kernel(x_ref, sh_ref, sc_ref, g_ref, w_ref, bd_ref, gq_ref, gk_ref, cos_ref, sin_ref, nb_ref,
                   qt_ref, ka_ref, va_ref, qr_ref, kr_ref, vr_ref, gr_ref):
    tm = x_ref.shape[0]
    sub = min(INPROJ_SUBTILE, tm)
    bd = bd_ref[...]
    scale = HEAD_DIM ** -0.5 * LOG2E
    rscale = RET_QK_DIM ** -0.5
    o = PROJ_OFFS
    per = LANES // HEAD_DIM
    for j0 in range(0, tm, sub):
        rows = slice(j0, j0 + sub)
        h = _rms_modulate(x_ref[rows, :], g_ref[...], sh_ref[...], sc_ref[...])
        proj = jnp.dot(h.astype(BF16), w_ref[...], preferred_element_type=F32)
        qa, ka, va = proj[:, o[0]:o[1]], proj[:, o[1]:o[2]], proj[:, o[2]:o[3]]
        qr, kr, vr, gr = proj[:, o[3]:o[4]], proj[:, o[4]:o[5]], proj[:, o[5]:o[6]], proj[:, o[6]:o[7]]
        cos, sin = cos_ref[rows, :], sin_ref[rows, :]
        even = (lax.broadcasted_iota(jnp.int32, cos.shape, 1) % 2) == 0

        qn = qa * lax.rsqrt(_head_mean_sq(qa, bd) + EPS) * gq_ref[...]
        for c, blk in enumerate(_rope_chunks(qn, cos, sin, even)):
            blk_t = (blk * scale).T
            for j in range(per):
                kvh, grp = divmod(c * per + j, GQA_GROUP)
                qt_ref[kvh, 0:HEAD_DIM, grp * tm + j0:grp * tm + j0 + sub] = (
                    blk_t[j * HEAD_DIM:(j + 1) * HEAD_DIM, :].astype(BF16))
        kn = ka * lax.rsqrt(_head_mean_sq(ka, bd[:KV_WIDTH, :KV_WIDTH]) + EPS) * gk_ref[...]
        ka_ref[rows, :] = _rope_chunks(kn, cos, sin, even)[0].astype(BF16)
        va_ref[rows, :] = va.astype(BF16)
        for c, blk in enumerate(_rope_chunks(qr, cos, sin, even)):
            qr_ref[rows, c * LANES:(c + 1) * LANES] = blk.astype(BF16)
        for c, blk in enumerate(_rope_chunks(kr, cos, sin, even)):
            kr_ref[rows, c * LANES:(c + 1) * LANES] = (blk * rscale).astype(BF16)
        vr_ref[rows, :] = vr.astype(BF16)
        gr_ref[rows, :] = gr
    aug_shape = (ATTN_AUG_DIM - HEAD_DIM, GQA_GROUP * tm)
    aug_row = lax.broadcasted_iota(jnp.int32, aug_shape, 0)
    aug = jnp.where(aug_row == 0, nb_ref[0:1, 0:1], 0.0).astype(BF16)
    for kvh in range(ATTN_KV_HEADS):
        qt_ref[kvh, HEAD_DIM:ATTN_AUG_DIM, :] = aug


def _inproj(x2, sh, sc, g1, w_bf, bd, gq, gk, cos, sin, neg_bound):
    rows, d = x2.shape
    tm = min(ROW_TILE, rows)
    row = lambda w: pl.BlockSpec((tm, w), lambda i: (i, 0))
    widths = PROJ_SIZES[1:]
    dtypes = (BF16,) * 5 + (F32,)
    qt_shape = (ATTN_KV_HEADS, ATTN_AUG_DIM, GQA_GROUP * rows)
    return pl.pallas_call(
        _inproj_kernel,
        grid=(rows // tm,),
        in_specs=[row(d), _const_spec((1, d)), _const_spec((1, d)), _const_spec((1, d)),
                  _const_spec(w_bf.shape), _const_spec(bd.shape),
                  _const_spec((1, ATTN_WIDTH)), _const_spec((1, KV_WIDTH)),
                  row(LANES), row(LANES), _const_spec((1, LANES))],
        out_specs=[pl.BlockSpec((ATTN_KV_HEADS, ATTN_AUG_DIM, GQA_GROUP * tm), lambda i: (0, 0, i))]
        + [row(w) for w in widths],
        out_shape=[jax.ShapeDtypeStruct(qt_shape, BF16)]
        + [jax.ShapeDtypeStruct((rows, w), dt) for w, dt in zip(widths, dtypes)],
        compiler_params=_cparams("arbitrary"),
        name="norm_inproj",
    )(x2, sh, sc, g1, w_bf, bd, gq, gk, cos, sin, neg_bound)


def _attn_kernel(q_ref, k_ref, v_ref, o_ref, m_ref, l_ref, acc_ref, *, n_chunks, running_max):
    m_ref[...] = jnp.full(m_ref.shape, -jnp.inf, F32)
    l_ref[...] = jnp.zeros(l_ref.shape, F32)
    acc_ref[...] = jnp.zeros(acc_ref.shape, F32)

    def body(c, carry):
        kc = k_ref[0, c]
        vc = v_ref[0, c]
        s = jnp.dot(kc, q_ref[0], preferred_element_type=F32)
        if running_max:
            m_old = m_ref[...]
            m_new = jnp.maximum(m_old, jnp.max(s, axis=0, keepdims=True))
            alpha = jnp.exp2(m_old - m_new)
            p = jnp.exp2(s - m_new)
            l_ref[...] = alpha * l_ref[...] + jnp.sum(p, axis=0, keepdims=True)
            acc_ref[...] = acc_ref[...] * alpha + jnp.dot(vc, p.astype(BF16), preferred_element_type=F32)
            m_ref[...] = m_new
        else:
            p = jnp.exp2(s)
            l_ref[...] += jnp.sum(p, axis=0, keepdims=True)
            acc_ref[...] += jnp.dot(vc, p.astype(BF16), preferred_element_type=F32)
        return carry

    lax.fori_loop(0, n_chunks, body, 0)
    o_ref[0] = (acc_ref[...] / l_ref[...]).astype(o_ref.dtype)


def _key_chunk(nk):
    for tk in (3328, 1280, 640, 256, 128):
        if nk % tk == 0:
            return tk
    raise ValueError(f"unsupported key count {nk}")


def _attention(q_t, k_c, v_t, *, running_max):
    kv, da, gl = q_t.shape
    _, n_chunks, d, tk = v_t.shape
    nq = GQA_GROUP * min(ATTN_Q_TILE, gl // GQA_GROUP)
    return pl.pallas_call(
        functools.partial(_attn_kernel, n_chunks=n_chunks, running_max=running_max),
        grid=(kv, gl // nq),
        in_specs=[pl.BlockSpec((1, da, nq), lambda b, i: (b, 0, i)),
                  pl.BlockSpec((1, n_chunks, tk, da), lambda b, i: (b, 0, 0, 0)),
                  pl.BlockSpec((1, n_chunks, d, tk), lambda b, i: (b, 0, 0, 0))],
        out_specs=pl.BlockSpec((1, d, nq), lambda b, i: (b, 0, i)),
        out_shape=jax.ShapeDtypeStruct((kv, d, gl), BF16),
        scratch_shapes=[pltpu.VMEM((1, nq), F32), pltpu.VMEM((1, nq), F32), pltpu.VMEM((d, nq), F32)],
        compiler_params=_cparams("arbitrary", "arbitrary"),
        name="flash_attn",
    )(q_t, k_c, v_t)


def _ret_kernel(gc_ref, q_ref, k_ref, v_ref, kc_ref, vc_ref, tab_ref, *rest, reverse, final):
    if final:
        of_ref, gate_ref, gng_ref, gnb_ref, out_ref, s_ref = rest
    else:
        out_ref, s_ref = rest
    c_rows = RET_CHUNK
    n_local = q_ref.shape[0] // c_rows
    n_ctx = kc_ref.shape[0] // c_rows
    lane_head = lax.broadcasted_iota(jnp.int32, (c_rows, RET_QK_WIDTH), 1) // RET_QK_DIM

    def state_update(kch, vch):
        for h in range(RET_HEADS):
            vz = (vch[:, h * RET_V_DIM:(h + 1) * RET_V_DIM].astype(F32)
                  * tab_ref[1, h, :, 0:RET_V_DIM]).astype(BF16)
            u = lax.dot_general(kch, vz, (((0,), (0,)), ((), ())), preferred_element_type=F32)
            s_ref[h] = gc_ref[h] * s_ref[h] + u

    @pl.when(pl.program_id(0) == 0)
    def _():
        s_ref[...] = jnp.zeros(s_ref.shape, F32)
        order = range(n_ctx - 1, -1, -1) if reverse else range(n_ctx)
        for cc in order:
            state_update(kc_ref[cc * c_rows:(cc + 1) * c_rows, :], vc_ref[cc * c_rows:(cc + 1) * c_rows, :])

    def chunk(ci, carry):
        c = (n_local - 1 - ci) if reverse else ci
        rows = pl.ds(pl.multiple_of(c * c_rows, c_rows), c_rows)
        q, k, v = q_ref[rows, :], k_ref[rows, :], v_ref[rows, :]
        for h in range(RET_HEADS):
            cols = slice(h * RET_V_DIM, (h + 1) * RET_V_DIM)
            qm = jnp.where(lane_head == h, q, jnp.zeros_like(q))
            sc = lax.dot_general(qm, k, (((1,), (1,)), ((), ())), preferred_element_type=F32)
            sc = sc * tab_ref[0, h]
            o = jnp.dot(sc.astype(BF16), v[:, cols], preferred_element_type=F32)
            o = o + (jnp.dot(qm, s_ref[h].astype(BF16), preferred_element_type=F32)
                     * tab_ref[2, h, :, 0:RET_V_DIM])
            if final:
                o = o + of_ref[rows, cols]
                mu = jnp.mean(o, axis=-1, keepdims=True)
                oc = o - mu
                var = jnp.mean(oc * oc, axis=-1, keepdims=True)
                on = oc * lax.rsqrt(var + EPS)
                gate = gate_ref[rows, cols]
                y = (on * gng_ref[:, cols] + gnb_ref[:, cols]) * (gate / (1.0 + jnp.exp(-gate)))
                out_ref[rows, cols] = y.astype(out_ref.dtype)
            else:
                out_ref[rows, cols] = o
        state_update(k, v)
        return carry

    lax.fori_loop(0, n_local, chunk, 0, unroll=RET_UNROLL)


def _retention_pass(gc, qr, kr, vr, kr_c, vr_c, tabs, extra, *, reverse):
    l = qr.shape[0]
    tr = min(RET_ROW_TILE, l)
    nt = l // tr
    final = extra is not None
    idx = (lambda j, gc_: (nt - 1 - j, 0)) if reverse else (lambda j, gc_: (j, 0))
    const = lambda shape: pl.BlockSpec(shape, lambda j, gc_: (0,) * len(shape))
    in_specs = [pl.BlockSpec((tr, RET_QK_WIDTH), idx), pl.BlockSpec((tr, RET_QK_WIDTH), idx),
                pl.BlockSpec((tr, RET_WIDTH), idx),
                const(kr_c.shape), const(vr_c.shape), const(tabs.shape)]
    args = [qr, kr, vr, kr_c, vr_c, tabs]
    if final:
        of, gate, gng, gnb = extra
        in_specs += [pl.BlockSpec((tr, RET_WIDTH), idx), pl.BlockSpec((tr, RET_WIDTH), idx),
                     const(gng.shape), const(gnb.shape)]
        args += [of, gate, gng, gnb]
    return pl.pallas_call(
        functools.partial(_ret_kernel, reverse=reverse, final=final),
        grid_spec=pltpu.PrefetchScalarGridSpec(
            num_scalar_prefetch=1, grid=(nt,), in_specs=in_specs,
            out_specs=pl.BlockSpec((tr, RET_WIDTH), idx),
            scratch_shapes=[pltpu.VMEM((RET_HEADS, RET_QK_WIDTH, RET_V_DIM), F32)]),
        out_shape=jax.ShapeDtypeStruct((l, RET_WIDTH), BF16 if final else F32),
        compiler_params=_cparams("arbitrary"),
        name="retention_bwd" if reverse else "retention_fwd",
    )(gc, *args)


def _retention_tables(log_g, reverse):
    c = RET_CHUNK
    pos = jnp.arange(c, dtype=F32)
    diff = (pos[None, :] - pos[:, None]) if reverse else (pos[:, None] - pos[None, :])
    mask = (diff > 0) if reverse else (diff >= 0)
    lg = log_g[:, None, None]
    dmat = jnp.where(mask, jnp.exp(lg * jnp.where(mask, diff, 0.0)[None]), 0.0)
    zeta = jnp.exp(log_g[:, None] * (pos if reverse else (c - 1 - pos))[None, :])
    xi = jnp.exp(log_g[:, None] * ((c - pos) if reverse else (pos + 1.0))[None, :])
    ones = jnp.ones((1, 1, c), F32)
    tabs = jnp.stack([dmat, zeta[:, :, None] * ones, xi[:, :, None] * ones])
    return tabs, jnp.exp(log_g * c)


def _outproj_kernel(x_ref, a_ref, r_ref, w_ref, gt_ref, g2_ref, sh_ref, sc_ref, wr_ref, br_ref,
                    xn_ref, h2_ref, ids_ref, wts_ref):
    tm = x_ref.shape[0]
    mix = jnp.dot(r_ref[...], w_ref[ATTN_WIDTH:, :], preferred_element_type=F32)
    per = LANES // HEAD_DIM
    for pair in range(ATTN_HEADS // per):
        pieces = []
        for j in range(per):
            kvh, grp = divmod(pair * per + j, GQA_GROUP)
            pieces.append(a_ref[kvh, :, grp * tm:(grp + 1) * tm])
        o_t = jnp.concatenate(pieces, axis=0)
        mix = mix + lax.dot_general(o_t, w_ref[pair * LANES:(pair + 1) * LANES, :],
                                    (((0,), (0,)), ((), ())), preferred_element_type=F32)
    x_new = x_ref[...] + gt_ref[...] * mix
    xn_ref[...] = x_new
    h2 = _rms_modulate(x_new, g2_ref[...], sh_ref[...], sc_ref[...])
    _store_row_tiles(h2_ref, (), h2)
    hi = h2.astype(BF16)
    lo = (h2 - hi.astype(F32)).astype(BF16)

    nt = (((1,), (1,)), ((), ()))
    logits = (lax.dot_general(wr_ref[0], hi, nt, preferred_element_type=F32)
              + lax.dot_general(wr_ref[0], lo, nt, preferred_element_type=F32)
              + lax.dot_general(wr_ref[1], hi, nt, preferred_element_type=F32)) + br_ref[...]
    rows = [logits[g:g + 1, :] for g in range(N_GROUPS)]
    gmax = functools.reduce(jnp.maximum, rows)
    gidx = jnp.full(gmax.shape, N_GROUPS - 1, jnp.int32)
    for g in range(N_GROUPS - 2, -1, -1):
        gidx = jnp.where(rows[g] == gmax, g, gidx)
    p_g = 1.0 / functools.reduce(jnp.add, [jnp.exp(r - gmax) for r in rows])
    sel = logits[EXPERT_ROW0 + (N_GROUPS - 1) * EXPERTS_PER_GROUP:EXPERT_ROW0 + N_GROUPS * EXPERTS_PER_GROUP, :]
    for g in range(N_GROUPS - 2, -1, -1):
        blk = logits[EXPERT_ROW0 + g * EXPERTS_PER_GROUP:EXPERT_ROW0 + (g + 1) * EXPERTS_PER_GROUP, :]
        sel = jnp.where(gidx == g, blk, sel)
    ridx = lax.broadcasted_iota(jnp.int32, sel.shape, 0)
    m1 = jnp.max(sel, axis=0, keepdims=True)
    i1 = jnp.min(jnp.where(sel == m1, ridx, EXPERTS_PER_GROUP), axis=0, keepdims=True)
    sel2 = jnp.where(ridx == i1, -jnp.inf, sel)
    m2 = jnp.max(sel2, axis=0, keepdims=True)
    i2 = jnp.min(jnp.where(sel2 == m2, ridx, EXPERTS_PER_GROUP), axis=0, keepdims=True)
    e2 = jnp.exp(m2 - m1)
    w1 = p_g / (1.0 + e2)
    w2 = w1 * e2
    out_rows = lax.broadcasted_iota(jnp.int32, ids_ref.shape, 0)
    ids_ref[...] = jnp.where(out_rows == 0, gidx * EXPERTS_PER_GROUP + i1,
                             jnp.where(out_rows == 1, gidx * EXPERTS_PER_GROUP + i2, 0))
    wts_ref[...] = jnp.where(out_rows == 0, w1, jnp.where(out_rows == 1, w2, 0.0))


def _outproj_router(x2, attn, ret, w_out_bf, gt1, g2, sh2, sc2, wr, br):
    l, d = x2.shape
    tm = min(ROW_TILE, l)
    row = lambda w: pl.BlockSpec((tm, w), lambda i: (i, 0))
    col = pl.BlockSpec((SUBLANES, tm), lambda i: (0, i))
    return pl.pallas_call(
        _outproj_kernel,
        grid=(l // tm,),
        in_specs=[row(d),
                  pl.BlockSpec((ATTN_KV_HEADS, HEAD_DIM, GQA_GROUP * tm), lambda i: (0, 0, i)),
                  row(ret.shape[1]), _const_spec(w_out_bf.shape),
                  _const_spec((1, d)), _const_spec((1, d)), _const_spec((1, d)), _const_spec((1, d)),
                  _const_spec(wr.shape), _const_spec(br.shape)],
        out_specs=[row(d), pl.BlockSpec((tm * SUBLANES, d // SUBLANES), lambda i: (i, 0)), col, col],
        out_shape=[jax.ShapeDtypeStruct((l, d), F32), jax.ShapeDtypeStruct((l * SUBLANES, d // SUBLANES), F32),
                   jax.ShapeDtypeStruct((SUBLANES, l), jnp.int32),
                   jax.ShapeDtypeStruct((SUBLANES, l), F32)],
        compiler_params=_cparams("arbitrary"),
        name="outproj_router",
    )(x2, attn, ret, w_out_bf, gt1, g2, sh2, sc2, wr, br)


def _row_copy(src_hbm, src_row, dst_ref, sem):
    return pltpu.make_async_copy(src_hbm.at[pl.ds(src_row * SUBLANES, SUBLANES), :], dst_ref, sem)


def _moe_kernel(be_ref, na_ref, tok0_ref, tok1_ref, tokn_ref, h_hbm, wg_ref, wu_ref, wd_ref, y_ref,
                wg_s, wu_s, wd_s, xg_s, sem):
    b = pl.program_id(0)
    slot = lax.rem(b, MOE_SLOTS)
    nxt = lax.rem(b + 2, MOE_SLOTS)
    n_active = na_ref[0]
    active = b < n_active
    last = be_ref.shape[0] - 1
    changed = jnp.logical_or(b == 0, be_ref[jnp.minimum(b, last)] != be_ref[jnp.clip(b - 1, 0, last)])

    def wait_rows():
        pltpu.make_async_copy(h_hbm.at[pl.ds(0, MOE_BLOCK * SUBLANES), :], xg_s.at[slot], sem.at[slot]).wait()

    @pl.when(b == 0)
    def _():
        def body(r, carry):
            _row_copy(h_hbm, tok0_ref[0, 0, r], xg_s.at[0, pl.ds(r * SUBLANES, SUBLANES), :], sem.at[0]).start()
            _row_copy(h_hbm, tok1_ref[0, 0, r], xg_s.at[1, pl.ds(r * SUBLANES, SUBLANES), :], sem.at[1]).start()
            return carry

        lax.fori_loop(0, MOE_BLOCK, body, 0, unroll=8)

    @pl.when(jnp.logical_and(active, changed))
    def _():
        wg_s[...] = wg_ref[0].astype(BF16)
        wu_s[...] = wu_ref[0].astype(BF16)
        wd_s[...] = wd_ref[0].astype(BF16)

    @pl.when(active)
    def _():
        wait_rows()
        x = _load_row_tiles(xg_s, (slot,), MOE_BLOCK).astype(BF16)
        hg = jnp.dot(x, wg_s[...], preferred_element_type=F32)
        hu = jnp.dot(x, wu_s[...], preferred_element_type=F32)
        a = (hg / (1.0 + jnp.exp(-hg))) * hu
        _store_row_tiles(y_ref, (), jnp.dot(a.astype(BF16), wd_s[...], preferred_element_type=F32))
        for r in range(MOE_BLOCK):
            _row_copy(h_hbm, tokn_ref[0, 0, r], xg_s.at[nxt, pl.ds(r * SUBLANES, SUBLANES), :],
                      sem.at[nxt]).start(priority=r % 2)

    @pl.when(jnp.logical_or(b == n_active, b == n_active + 1))
    def _():
        wait_rows()

    @pl.when(jnp.logical_not(active))
    def _():
        y_ref[...] = jnp.zeros(y_ref.shape, y_ref.dtype)


def _moe_blocks(blk_e, n_active, buf_tok, h2, w_gate, w_up, w_down):
    ne, d, de = w_gate.shape
    tiled = (MOE_BLOCK * SUBLANES, d // SUBLANES)
    nblk = buf_tok.shape[0]
    blk = lambda b: jnp.minimum(b, nblk - 1)
    wmap = lambda b, be, na: (be[blk(b)], 0, 0)
    tokmap = lambda off: (lambda b, be, na: (blk(b + off), 0, 0))
    smem_tok = lambda off: pl.BlockSpec((1, 1, MOE_BLOCK), tokmap(off), memory_space=pltpu.SMEM)
    return pl.pallas_call(
        _moe_kernel,
        grid_spec=pltpu.PrefetchScalarGridSpec(
            num_scalar_prefetch=2, grid=(nblk + 1,),
            in_specs=[pl.BlockSpec((1, 1, MOE_BLOCK), lambda b, be, na: (0, 0, 0), memory_space=pltpu.SMEM),
                      pl.BlockSpec((1, 1, MOE_BLOCK), lambda b, be, na: (1, 0, 0), memory_space=pltpu.SMEM),
                      smem_tok(2),
                      pl.BlockSpec(memory_space=pl.ANY),
                      pl.BlockSpec((1, d, de), wmap), pl.BlockSpec((1, d, de), wmap),
                      pl.BlockSpec((1, de, d), wmap)],
            out_specs=pl.BlockSpec(tiled, lambda b, be, na: (blk(b), 0)),
            scratch_shapes=[pltpu.VMEM((d, de), BF16), pltpu.VMEM((d, de), BF16),
                            pltpu.VMEM((de, d), BF16), pltpu.VMEM((MOE_SLOTS,) + tiled, F32),
                            pltpu.SemaphoreType.DMA((MOE_SLOTS,))]),
        out_shape=jax.ShapeDtypeStruct((nblk * tiled[0], tiled[1]), F32),
        compiler_params=_cparams("arbitrary"),
        name="moe_experts",
    )(blk_e, n_active, buf_tok, buf_tok, buf_tok, h2, w_gate, w_up, w_down)


def _final_kernel(d0_ref, dn_ref, x_ref, w_ref, gt_ref, g_ref, y_hbm, o_ref, yg_s, sem):
    i = pl.program_id(0)
    slot = i % 2
    tm = x_ref.shape[0]

    def start_tile(d_ref, slot_):
        def body(r, carry):
            for k in range(TOP_K):
                _row_copy(y_hbm, d_ref[0, k, r], yg_s.at[slot_, k, pl.ds(r * SUBLANES, SUBLANES), :],
                          sem.at[slot_]).start(priority=k)
            return carry

        lax.fori_loop(0, tm, body, 0, unroll=8)

    @pl.when(i == 0)
    def _():
        start_tile(d0_ref, 0)

    @pl.when(i + 1 < pl.num_programs(0))
    def _():
        start_tile(dn_ref, 1 - slot)

    for k in range(TOP_K):
        pltpu.make_async_copy(y_hbm.at[pl.ds(0, tm * SUBLANES), :], yg_s.at[slot, k], sem.at[slot]).wait()
    w = w_ref[...]
    y = _load_row_tiles(yg_s, (slot, 0), tm) * w[:, 0:1] + _load_row_tiles(yg_s, (slot, 1), tm) * w[:, 1:2]
    x = x_ref[...] + gt_ref[...] * y
    ms = jnp.mean(x * x, axis=-1, keepdims=True)
    o_ref[...] = (x * lax.rsqrt(ms + EPS)) * g_ref[...]


def _final(x_new, yb, dest, wts, gt2, gfin):
    l, d = x_new.shape
    n, _, tm = dest.shape
    row = lambda w: pl.BlockSpec((tm, w), lambda i: (i, 0))
    return pl.pallas_call(
        _final_kernel,
        grid=(n,),
        in_specs=[pl.BlockSpec((1, TOP_K, tm), lambda i: (0, 0, 0), memory_space=pltpu.SMEM),
                  pl.BlockSpec((1, TOP_K, tm), lambda i: (jnp.minimum(i + 1, n - 1), 0, 0),
                               memory_space=pltpu.SMEM),
                  row(d), row(wts.shape[1]), _const_spec((1, d)), _const_spec((1, d)),
                  pl.BlockSpec(memory_space=pl.ANY)],
        out_specs=row(d),
        out_shape=jax.ShapeDtypeStruct((l, d), F32),
        scratch_shapes=[pltpu.VMEM((2, TOP_K, tm * SUBLANES, d // SUBLANES), F32), pltpu.SemaphoreType.DMA((2,))],
        compiler_params=_cparams("arbitrary"),
        name="combine_final_norm",
    )(dest, dest, x_new, wts, gt2, gfin, yb)


def _rope_tables(rows_count):
    row = jnp.repeat(jnp.arange(rows_count), GRID_W).astype(F32)
    col = jnp.tile(jnp.arange(GRID_W), rows_count).astype(F32)
    freqs = ROPE_THETA ** (-jnp.arange(0, ROPE_AXIS_DIM, 2, dtype=F32) / ROPE_AXIS_DIM)
    ang = jnp.concatenate([row[:, None] * freqs, col[:, None] * freqs], axis=-1)
    ang = jnp.repeat(ang, 2, axis=-1)
    sign = jnp.where(jnp.arange(HEAD_DIM) % 2 == 0, -1.0, 1.0).astype(F32)
    cos = jnp.tile(jnp.cos(ang), (1, LANES // HEAD_DIM))
    sin = jnp.tile(jnp.sin(ang) * sign, (1, LANES // HEAD_DIM))
    return cos, sin


def _dispatch(ids, n_tokens):
    a = TOP_K * n_tokens
    eid_f = ids[:TOP_K].reshape(a)
    tok_f = jnp.tile(jnp.arange(n_tokens, dtype=jnp.int32), TOP_K)
    onehot = (eid_f[:, None] == jnp.arange(N_EXPERTS, dtype=jnp.int32)[None, :]).astype(jnp.int32)
    incl = jnp.cumsum(onehot, axis=0)
    counts = incl[-1]
    rank = jnp.sum((incl - onehot) * onehot, axis=1)
    pcounts = (counts + MOE_BLOCK - 1) // MOE_BLOCK * MOE_BLOCK
    pends = jnp.cumsum(pcounts)
    pstarts = pends - pcounts
    dest = jnp.sum(onehot * pstarts[None, :], axis=1) + rank
    p = a + N_EXPERTS * MOE_BLOCK
    nblk = p // MOE_BLOCK
    n_fill = p - a
    fill_ends = jnp.cumsum(pcounts - counts)
    fill_idx = jnp.arange(n_fill, dtype=jnp.int32)
    fill_e = jnp.sum(fill_idx[:, None] >= fill_ends[None, :], axis=1).astype(jnp.int32)
    idx_bits = max(a, n_fill).bit_length()
    real_keys = (eid_f << (idx_bits + 1)) | jnp.arange(a, dtype=jnp.int32)
    fill_keys = (fill_e << (idx_bits + 1)) | (1 << idx_bits) | fill_idx
    keys = jnp.sort(jnp.concatenate([real_keys, fill_keys]))
    is_real = ((keys >> idx_bits) & 1) == 0
    buf_tok = jnp.where(is_real, (keys & ((1 << idx_bits) - 1)) % n_tokens, 0)
    del tok_f
    n_active = pends[-1] // MOE_BLOCK
    blk_start = jnp.arange(nblk, dtype=jnp.int32) * MOE_BLOCK
    blk_e = jnp.minimum(jnp.sum(blk_start[:, None] >= pends[None, :], axis=1), N_EXPERTS - 1)
    blk_e = jnp.where(jnp.arange(nblk) < n_active, blk_e, blk_e[n_active - 1]).astype(jnp.int32)
    return buf_tok, dest.reshape(TOP_K, n_tokens), blk_e, n_active.astype(jnp.int32).reshape(1)


def _split_bf16(w):
    hi = w.astype(BF16)
    return jnp.stack([hi, (w - hi.astype(F32)).astype(BF16)])


def kernel(x, c, ctx, c_ctx, w_ada, b_ada, norm1_g, w_in, attn_q_norm, attn_k_norm, ret_decay_fwd,
           ret_decay_bwd, ret_gn_g, ret_gn_b, w_out, norm2_g, moe_w_grp, moe_b_grp, moe_w_exp, moe_b_exp,
           moe_w_gate, moe_w_up, moe_w_down, final_norm_g):
    b, l, d = x.shape
    cl = ctx.shape[1]
    assert b == 1 and w_ada.shape[0] == 1, "single batch element, single layer"
    x2, ctx2 = x[0], ctx[0]

    cvecs = jnp.zeros((SUBLANES, d), F32).at[0].set(c[0]).at[1].set(c_ctx)
    mod = _ada(cvecs, w_ada[0], b_ada[0])
    sh1, sc1, gt1, sh2, sc2, gt2 = [mod[0:1, i * d:(i + 1) * d] for i in range(6)]
    csh1, csc1 = mod[1:2, 0:d], mod[1:2, d:2 * d]

    w_in_bf = w_in[0].astype(BF16)
    head_of = jnp.arange(ATTN_WIDTH) // HEAD_DIM
    bd = jnp.where(head_of[:, None] == head_of[None, :], 1.0 / HEAD_DIM, 0.0).astype(BF16)
    gq = jnp.tile(attn_q_norm[0], ATTN_HEADS)[None, :]
    gk = jnp.tile(attn_k_norm[0], ATTN_KV_HEADS)[None, :]
    g1 = norm1_g[0][None, :]
    cos, sin = _rope_tables(l // GRID_W)
    bound = (HEAD_DIM ** 0.5 * LOG2E * 1.01) * jnp.max(jnp.abs(attn_q_norm[0])) * jnp.max(jnp.abs(attn_k_norm[0]))
    bound = bound.astype(BF16).astype(F32)
    neg_bound = jnp.full((1, LANES), -1.0, F32) * bound
    q_t, ka, va, qr, kr, vr, gr = _inproj(x2, sh1, sc1, g1, w_in_bf, bd, gq, gk, cos, sin, neg_bound)
    _, ka_c, va_c, _, kr_c, vr_c, _ = _inproj(ctx2, csh1, csc1, g1, w_in_bf, bd, gq, gk,
                                              jnp.ones((cl, LANES), F32), jnp.zeros((cl, LANES), F32), neg_bound)

    nk = cl + l
    tk = _key_chunk(nk)
    k_all = jnp.concatenate([ka_c, ka], axis=0).reshape(nk // tk, tk, ATTN_KV_HEADS, HEAD_DIM)
    k_aug = jnp.zeros(k_all.shape[:3] + (ATTN_AUG_DIM - HEAD_DIM,), BF16).at[..., 0].set(1.0)
    k_all = jnp.concatenate([k_all, k_aug], axis=-1)
    v_all = jnp.concatenate([va_c, va], axis=0).reshape(nk // tk, tk, ATTN_KV_HEADS, HEAD_DIM)
    attn_args = (q_t, k_all.transpose(2, 0, 1, 3), v_all.transpose(2, 0, 3, 1))
    o_t = lax.cond(bound <= MAX_UNSHIFTED_BOUND,
                   functools.partial(_attention, running_max=False),
                   functools.partial(_attention, running_max=True), *attn_args)

    log_gf = jax.nn.log_sigmoid(ret_decay_fwd[0].astype(F32))
    log_gb = jax.nn.log_sigmoid(ret_decay_bwd[0].astype(F32))
    tab_f, gc_f = _retention_tables(log_gf, False)
    tab_b, gc_b = _retention_tables(log_gb, True)
    o_f = _retention_pass(gc_f, qr, kr, vr, kr_c, vr_c, tab_f, None, reverse=False)
    ret = _retention_pass(gc_b, qr, kr, vr, kr_c, vr_c, tab_b,
                          (o_f, gr, ret_gn_g[0][None, :], ret_gn_b[0][None, :]), reverse=True)

    wr = jnp.zeros((ROUTER_ROWS, d), F32)
    wr = wr.at[:N_GROUPS].set(moe_w_grp[0].T).at[EXPERT_ROW0:EXPERT_ROW0 + N_EXPERTS].set(moe_w_exp[0].T)
    br = jnp.zeros((ROUTER_ROWS, 1), F32)
    br = br.at[:N_GROUPS, 0].set(moe_b_grp[0]).at[EXPERT_ROW0:EXPERT_ROW0 + N_EXPERTS, 0].set(moe_b_exp[0])
    x_new, h2, ids, wts = _outproj_router(x2, o_t, ret, w_out[0].astype(BF16), gt1, norm2_g[0][None, :],
                                          sh2, sc2, _split_bf16(wr), br)

    buf_tok, dest, blk_e, n_active = _dispatch(ids, l)
    yb = _moe_blocks(blk_e, n_active, buf_tok.reshape(-1, 1, MOE_BLOCK), h2, moe_w_gate[0], moe_w_up[0],
                     moe_w_down[0])
    tf = min(FINAL_TILE, l)
    dest_t = dest.reshape(TOP_K, l // tf, tf).transpose(1, 0, 2)
    out = _final(x_new, yb, dest_t, wts[:TOP_K].T, gt2, final_norm_g[None, :])
    return out[None]
```

```python
import functools

import jax
import jax.numpy as jnp
import numpy as np
from jax import lax
from jax.experimental import pallas as pl
from jax.experimental.pallas import tpu as pltpu

F32 = jnp.float32
BF16 = jnp.bfloat16

GRID_W = 64
HEAD_DIM = 64
ATTN_HEADS = 8
ATTN_KV_HEADS = 2
GQA_GROUP = ATTN_HEADS // ATTN_KV_HEADS
RET_HEADS = 4
RET_QK_DIM = 64
RET_V_DIM = 128
RET_CHUNK = 256
ATTN_WIDTH = ATTN_HEADS * HEAD_DIM
KV_WIDTH = ATTN_KV_HEADS * HEAD_DIM
RET_QK_WIDTH = RET_HEADS * RET_QK_DIM
RET_WIDTH = RET_HEADS * RET_V_DIM
PROJ_SIZES = (ATTN_WIDTH, KV_WIDTH, KV_WIDTH, RET_QK_WIDTH, RET_QK_WIDTH, RET_WIDTH, RET_WIDTH)
PROJ_OFFS = tuple(int(v) for v in np.cumsum((0,) + PROJ_SIZES))
PROJ_DIM = PROJ_OFFS[-1]
ROPE_THETA = 10000.0
ROPE_AXIS_DIM = HEAD_DIM // 2
N_GROUPS = 4
EXPERTS_PER_GROUP = 8
N_EXPERTS = N_GROUPS * EXPERTS_PER_GROUP
TOP_K = 2
D_EXPERT = 512
EPS = 1e-6
LOG2E = 1.4426950408889634
ATTN_AUG_DIM = 128
MAX_UNSHIFTED_BOUND = 60.0

LANES = 128
SUBLANES = 8
VMEM_LIMIT_BYTES = 56 * 1024 * 1024

ROW_TILE = 512
INPROJ_SUBTILE = 256
ATTN_Q_TILE = 512
RET_ROW_TILE = 1024
RET_UNROLL = 4
MOE_BLOCK = 256
MOE_SLOTS = 3
FINAL_TILE = 256
ROUTER_ROWS = 128
EXPERT_ROW0 = 8


def _cparams(*sem):
    return pltpu.CompilerParams(dimension_semantics=sem, vmem_limit_bytes=VMEM_LIMIT_BYTES)


def _const_spec(shape):
    nd = len(shape)
    return pl.BlockSpec(shape, lambda *_: (0,) * nd)


def _store_row_tiles(ref, idx, rows):
    n, w = rows.shape[0], rows.shape[1] // SUBLANES
    for s in range(SUBLANES):
        ref[idx + (pl.ds(s, n, stride=SUBLANES), slice(None))] = rows[:, s * w:(s + 1) * w]


def _load_row_tiles(ref, idx, n):
    return jnp.concatenate([ref[idx + (pl.ds(s, n, stride=SUBLANES), slice(None))] for s in range(SUBLANES)],
                           axis=1)


def _ada_kernel(s_ref, w_ref, b_ref, o_ref):
    s = s_ref[...]
    s = s / (1.0 + jnp.exp(-s))
    o_ref[...] = jnp.dot(s, w_ref[...], preferred_element_type=F32,
                         precision=lax.Precision.HIGHEST) + b_ref[...]


def _ada(cvecs, w_ada, b_ada):
    d, n = w_ada.shape
    tn = 1536
    return pl.pallas_call(
        _ada_kernel,
        grid=(n // tn,),
        in_specs=[_const_spec((SUBLANES, d)),
                  pl.BlockSpec((d, tn), lambda j: (0, j)),
                  pl.BlockSpec((1, tn), lambda j: (0, j))],
        out_specs=pl.BlockSpec((SUBLANES, tn), lambda j: (0, j)),
        out_shape=jax.ShapeDtypeStruct((SUBLANES, n), F32),
        compiler_params=_cparams("arbitrary"),
        name="ada_mod",
    )(cvecs, w_ada, b_ada.reshape(1, n))


def _rms_modulate(x, g, sh, sc):
    ms = jnp.mean(x * x, axis=-1, keepdims=True)
    return (x * lax.rsqrt(ms + EPS)) * g * (1.0 + sc) + sh


def _head_mean_sq(v, bd):
    sq = v * v
    hi = sq.astype(BF16)
    lo = (sq - hi.astype(F32)).astype(BF16)
    return (jnp.dot(hi, bd, preferred_element_type=F32) + jnp.dot(lo, bd, preferred_element_type=F32))


def _rope_chunks(v, cos, sin, even):
    outs = []
    for c in range(v.shape[1] // LANES):
        xc = v[:, c * LANES:(c + 1) * LANES]
        nxt = pltpu.roll(xc, LANES - 1, 1)
        prv = pltpu.roll(xc, 1, 1)
        outs.append(xc * cos + jnp.where(even, nxt, prv) * sin)
    return outs


def _inproj_kernel(x_ref, sh_ref, sc_ref, g_ref, w_ref, bd_ref, gq_ref, gk_ref, cos_ref, sin_ref, nb_ref,
                   qt_ref, ka_ref, va_ref, qr_ref, kr_ref, vr_ref, gr_ref):
    tm = x_ref.shape[0]
    sub = min(INPROJ_SUBTILE, tm)
    bd = bd_ref[...]
    scale = HEAD_DIM ** -0.5 * LOG2E
    rscale = RET_QK_DIM ** -0.5
    o = PROJ_OFFS
    per = LANES // HEAD_DIM
    for j0 in range(0, tm, sub):
        rows = slice(j0, j0 + sub)
        h = _rms_modulate(x_ref[rows, :], g_ref[...], sh_ref[...], sc_ref[...])
        proj = jnp.dot(h.astype(BF16), w_ref[...], preferred_element_type=F32)
        qa, ka, va = proj[:, o[0]:o[1]], proj[:, o[1]:o[2]], proj[:, o[2]:o[3]]
        qr, kr, vr, gr = proj[:, o[3]:o[4]], proj[:, o[4]:o[5]], proj[:, o[5]:o[6]], proj[:, o[6]:o[7]]
        cos, sin = cos_ref[rows, :], sin_ref[rows, :]
        even = (lax.broadcasted_iota(jnp.int32, cos.shape, 1) % 2) == 0

        qn = qa * lax.rsqrt(_head_mean_sq(qa, bd) + EPS) * gq_ref[...]
        for c, blk in enumerate(_rope_chunks(qn, cos, sin, even)):
            blk_t = (blk * scale).T
            for j in range(per):
                kvh, grp = divmod(c * per + j, GQA_GROUP)
                qt_ref[kvh, 0:HEAD_DIM, grp * tm + j0:grp * tm + j0 + sub] = (
                    blk_t[j * HEAD_DIM:(j + 1) * HEAD_DIM, :].astype(BF16))
        kn = ka * lax.rsqrt(_head_mean_sq(ka, bd[:KV_WIDTH, :KV_WIDTH]) + EPS) * gk_ref[...]
        ka_ref[rows, :] = _rope_chunks(kn, cos, sin, even)[0].astype(BF16)
        va_ref[rows, :] = va.astype(BF16)
        for c, blk in enumerate(_rope_chunks(qr, cos, sin, even)):
            qr_ref[rows, c * LANES:(c + 1) * LANES] = blk.astype(BF16)
        for c, blk in enumerate(_rope_chunks(kr, cos, sin, even)):
            kr_ref[rows, c * LANES:(c + 1) * LANES] = (blk * rscale).astype(BF16)
        vr_ref[rows, :] = vr.astype(BF16)
        gr_ref[rows, :] = gr
    aug_shape = (ATTN_AUG_DIM - HEAD_DIM, GQA_GROUP * tm)
    aug_row = lax.broadcasted_iota(jnp.int32, aug_shape, 0)
    aug = jnp.where(aug_row == 0, nb_ref[0:1, 0:1], 0.0).astype(BF16)
    for kvh in range(ATTN_KV_HEADS):
        qt_ref[kvh, HEAD_DIM:ATTN_AUG_DIM, :] = aug


def _inproj(x2, sh, sc, g1, w_bf, bd, gq, gk, cos, sin, neg_bound):
    rows, d = x2.shape
    tm = min(ROW_TILE, rows)
    row = lambda w: pl.BlockSpec((tm, w), lambda i: (i, 0))
    widths = PROJ_SIZES[1:]
    dtypes = (BF16,) * 5 + (F32,)
    qt_shape = (ATTN_KV_HEADS, ATTN_AUG_DIM, GQA_GROUP * rows)
    return pl.pallas_call(
        _inproj_kernel,
        grid=(rows // tm,),
        in_specs=[row(d), _const_spec((1, d)), _const_spec((1, d)), _const_spec((1, d)),
                  _const_spec(w_bf.shape), _const_spec(bd.shape),
                  _const_spec((1, ATTN_WIDTH)), _const_spec((1, KV_WIDTH)),
                  row(LANES), row(LANES), _const_spec((1, LANES))],
        out_specs=[pl.BlockSpec((ATTN_KV_HEADS, ATTN_AUG_DIM, GQA_GROUP * tm), lambda i: (0, 0, i))]
        + [row(w) for w in widths],
        out_shape=[jax.ShapeDtypeStruct(qt_shape, BF16)]
        + [jax.ShapeDtypeStruct((rows, w), dt) for w, dt in zip(widths, dtypes)],
        compiler_params=_cparams("arbitrary"),
        name="norm_inproj",
    )(x2, sh, sc, g1, w_bf, bd, gq, gk, cos, sin, neg_bound)


def _attn_kernel(q_ref, k_ref, v_ref, o_ref, m_ref, l_ref, acc_ref, *, n_chunks, running_max):
    m_ref[...] = jnp.full(m_ref.shape, -jnp.inf, F32)
    l_ref[...] = jnp.zeros(l_ref.shape, F32)
    acc_ref[...] = jnp.zeros(acc_ref.shape, F32)

    def body(c, carry):
        kc = k_ref[0, c]
        vc = v_ref[0, c]
        s = jnp.dot(kc, q_ref[0], preferred_element_type=F32)
        if running_max:
            m_old = m_ref[...]
            m_new = jnp.maximum(m_old, jnp.max(s, axis=0, keepdims=True))
            alpha = jnp.exp2(m_old - m_new)
            p = jnp.exp2(s - m_new)
            l_ref[...] = alpha * l_ref[...] + jnp.sum(p, axis=0, keepdims=True)
            acc_ref[...] = acc_ref[...] * alpha + jnp.dot(vc, p.astype(BF16), preferred_element_type=F32)
            m_ref[...] = m_new
        else:
            p = jnp.exp2(s)
            l_ref[...] += jnp.sum(p, axis=0, keepdims=True)
            acc_ref[...] += jnp.dot(vc, p.astype(BF16), preferred_element_type=F32)
        return carry

    lax.fori_loop(0, n_chunks, body, 0)
    o_ref[0] = (acc_ref[...] / l_ref[...]).astype(o_ref.dtype)


def _key_chunk(nk):
    for tk in (3328, 1280, 640, 256, 128):
        if nk % tk == 0:
            return tk
    raise ValueError(f"unsupported key count {nk}")


def _attention(q_t, k_c, v_t, *, running_max):
    kv, da, gl = q_t.shape
    _, n_chunks, d, tk = v_t.shape
    nq = GQA_GROUP * min(ATTN_Q_TILE, gl // GQA_GROUP)
    return pl.pallas_call(
        functools.partial(_attn_kernel, n_chunks=n_chunks, running_max=running_max),
        grid=(kv, gl // nq),
        in_specs=[pl.BlockSpec((1, da, nq), lambda b, i: (b, 0, i)),
                  pl.BlockSpec((1, n_chunks, tk, da), lambda b, i: (b, 0, 0, 0)),
                  pl.BlockSpec((1, n_chunks, d, tk), lambda b, i: (b, 0, 0, 0))],
        out_specs=pl.BlockSpec((1, d, nq), lambda b, i: (b, 0, i)),
        out_shape=jax.ShapeDtypeStruct((kv, d, gl), BF16),
        scratch_shapes=[pltpu.VMEM((1, nq), F32), pltpu.VMEM((1, nq), F32), pltpu.VMEM((d, nq), F32)],
        compiler_params=_cparams("arbitrary", "arbitrary"),
        name="flash_attn",
    )(q_t, k_c, v_t)


def _ret_kernel(gc_ref, q_ref, k_ref, v_ref, kc_ref, vc_ref, tab_ref, *rest, reverse, final):
    if final:
        of_ref, gate_ref, gng_ref, gnb_ref, out_ref, s_ref = rest
    else:
        out_ref, s_ref = rest
    c_rows = RET_CHUNK
    n_local = q_ref.shape[0] // c_rows
    n_ctx = kc_ref.shape[0] // c_rows
    lane_head = lax.broadcasted_iota(jnp.int32, (c_rows, RET_QK_WIDTH), 1) // RET_QK_DIM

    def state_update(kch, vch):
        for h in range(RET_HEADS):
            vz = (vch[:, h * RET_V_DIM:(h + 1) * RET_V_DIM].astype(F32)
                  * tab_ref[1, h, :, 0:RET_V_DIM]).astype(BF16)
            u = lax.dot_general(kch, vz, (((0,), (0,)), ((), ())), preferred_element_type=F32)
            s_ref[h] = gc_ref[h] * s_ref[h] + u

    @pl.when(pl.program_id(0) == 0)
    def _():
        s_ref[...] = jnp.zeros(s_ref.shape, F32)
        order = range(n_ctx - 1, -1, -1) if reverse else range(n_ctx)
        for cc in order:
            state_update(kc_ref[cc * c_rows:(cc + 1) * c_rows, :], vc_ref[cc * c_rows:(cc + 1) * c_rows, :])

    def chunk(ci, carry):
        c = (n_local - 1 - ci) if reverse else ci
        rows = pl.ds(pl.multiple_of(c * c_rows, c_rows), c_rows)
        q, k, v = q_ref[rows, :], k_ref[rows, :], v_ref[rows, :]
        for h in range(RET_HEADS):
            cols = slice(h * RET_V_DIM, (h + 1) * RET_V_DIM)
            qm = jnp.where(lane_head == h, q, jnp.zeros_like(q))
            sc = lax.dot_general(qm, k, (((1,), (1,)), ((), ())), preferred_element_type=F32)
            sc = sc * tab_ref[0, h]
            o = jnp.dot(sc.astype(BF16), v[:, cols], preferred_element_type=F32)
            o = o + (jnp.dot(qm, s_ref[h].astype(BF16), preferred_element_type=F32)
                     * tab_ref[2, h, :, 0:RET_V_DIM])
            if final:
                o = o + of_ref[rows, cols]
                mu = jnp.mean(o, axis=-1, keepdims=True)
                oc = o - mu
                var = jnp.mean(oc * oc, axis=-1, keepdims=True)
                on = oc * lax.rsqrt(var + EPS)
                gate = gate_ref[rows, cols]
                y = (on * gng_ref[:, cols] + gnb_ref[:, cols]) * (gate / (1.0 + jnp.exp(-gate)))
                out_ref[rows, cols] = y.astype(out_ref.dtype)
            else:
                out_ref[rows, cols] = o
        state_update(k, v)
        return carry

    lax.fori_loop(0, n_local, chunk, 0, unroll=RET_UNROLL)


def _retention_pass(gc, qr, kr, vr, kr_c, vr_c, tabs, extra, *, reverse):
    l = qr.shape[0]
    tr = min(RET_ROW_TILE, l)
    nt = l // tr
    final = extra is not None
    idx = (lambda j, gc_: (nt - 1 - j, 0)) if reverse else (lambda j, gc_: (j, 0))
    const = lambda shape: pl.BlockSpec(shape, lambda j, gc_: (0,) * len(shape))
    in_specs = [pl.BlockSpec((tr, RET_QK_WIDTH), idx), pl.BlockSpec((tr, RET_QK_WIDTH), idx),
                pl.BlockSpec((tr, RET_WIDTH), idx),
                const(kr_c.shape), const(vr_c.shape), const(tabs.shape)]
    args = [qr, kr, vr, kr_c, vr_c, tabs]
    if final:
        of, gate, gng, gnb = extra
        in_specs += [pl.BlockSpec((tr, RET_WIDTH), idx), pl.BlockSpec((tr, RET_WIDTH), idx),
                     const(gng.shape), const(gnb.shape)]
        args += [of, gate, gng, gnb]
    return pl.pallas_call(
        functools.partial(_ret_kernel, reverse=reverse, final=final),
        grid_spec=pltpu.PrefetchScalarGridSpec(
            num_scalar_prefetch=1, grid=(nt,), in_specs=in_specs,
            out_specs=pl.BlockSpec((tr, RET_WIDTH), idx),
            scratch_shapes=[pltpu.VMEM((RET_HEADS, RET_QK_WIDTH, RET_V_DIM), F32)]),
        out_shape=jax.ShapeDtypeStruct((l, RET_WIDTH), BF16 if final else F32),
        compiler_params=_cparams("arbitrary"),
        name="retention_bwd" if reverse else "retention_fwd",
    )(gc, *args)


def _retention_tables(log_g, reverse):
    c = RET_CHUNK
    pos = jnp.arange(c, dtype=F32)
    diff = (pos[None, :] - pos[:, None]) if reverse else (pos[:, None] - pos[None, :])
    mask = (diff > 0) if reverse else (diff >= 0)
    lg = log_g[:, None, None]
    dmat = jnp.where(mask, jnp.exp(lg * jnp.where(mask, diff, 0.0)[None]), 0.0)
    zeta = jnp.exp(log_g[:, None] * (pos if reverse else (c - 1 - pos))[None, :])
    xi = jnp.exp(log_g[:, None] * ((c - pos) if reverse else (pos + 1.0))[None, :])
    ones = jnp.ones((1, 1, c), F32)
    tabs = jnp.stack([dmat, zeta[:, :, None] * ones, xi[:, :, None] * ones])
    return tabs, jnp.exp(log_g * c)


def _outproj_kernel(x_ref, a_ref, r_ref, w_ref, gt_ref, g2_ref, sh_ref, sc_ref, wr_ref, br_ref,
                    xn_ref, h2_ref, ids_ref, wts_ref):
    tm = x_ref.shape[0]
    mix = jnp.dot(r_ref[...], w_ref[ATTN_WIDTH:, :], preferred_element_type=F32)
    per = LANES // HEAD_DIM
    for pair in range(ATTN_HEADS // per):
        pieces = []
        for j in range(per):
            kvh, grp = divmod(pair * per + j, GQA_GROUP)
            pieces.append(a_ref[kvh, :, grp * tm:(grp + 1) * tm])
        o_t = jnp.concatenate(pieces, axis=0)
        mix = mix + lax.dot_general(o_t, w_ref[pair * LANES:(pair + 1) * LANES, :],
                                    (((0,), (0,)), ((), ())), preferred_element_type=F32)
    x_new = x_ref[...] + gt_ref[...] * mix
    xn_ref[...] = x_new
    h2 = _rms_modulate(x_new, g2_ref[...], sh_ref[...], sc_ref[...])
    _store_row_tiles(h2_ref, (), h2)
    hi = h2.astype(BF16)
    lo = (h2 - hi.astype(F32)).astype(BF16)

    nt = (((1,), (1,)), ((), ()))
    logits = (lax.dot_general(wr_ref[0], hi, nt, preferred_element_type=F32)
              + lax.dot_general(wr_ref[0], lo, nt, preferred_element_type=F32)
              + lax.dot_general(wr_ref[1], hi, nt, preferred_element_type=F32)) + br_ref[...]
    rows = [logits[g:g + 1, :] for g in range(N_GROUPS)]
    gmax = functools.reduce(jnp.maximum, rows)
    gidx = jnp.full(gmax.shape, N_GROUPS - 1, jnp.int32)
    for g in range(N_GROUPS - 2, -1, -1):
        gidx = jnp.where(rows[g] == gmax, g, gidx)
    p_g = 1.0 / functools.reduce(jnp.add, [jnp.exp(r - gmax) for r in rows])
    sel = logits[EXPERT_ROW0 + (N_GROUPS - 1) * EXPERTS_PER_GROUP:EXPERT_ROW0 + N_GROUPS * EXPERTS_PER_GROUP, :]
    for g in range(N_GROUPS - 2, -1, -1):
        blk = logits[EXPERT_ROW0 + g * EXPERTS_PER_GROUP:EXPERT_ROW0 + (g + 1) * EXPERTS_PER_GROUP, :]
        sel = jnp.where(gidx == g, blk, sel)
    ridx = lax.broadcasted_iota(jnp.int32, sel.shape, 0)
    m1 = jnp.max(sel, axis=0, keepdims=True)
    i1 = jnp.min(jnp.where(sel == m1, ridx, EXPERTS_PER_GROUP), axis=0, keepdims=True)
    sel2 = jnp.where(ridx == i1, -jnp.inf, sel)
    m2 = jnp.max(sel2, axis=0, keepdims=True)
    i2 = jnp.min(jnp.where(sel2 == m2, ridx, EXPERTS_PER_GROUP), axis=0, keepdims=True)
    e2 = jnp.exp(m2 - m1)
    w1 = p_g / (1.0 + e2)
    w2 = w1 * e2
    out_rows = lax.broadcasted_iota(jnp.int32, ids_ref.shape, 0)
    ids_ref[...] = jnp.where(out_rows == 0, gidx * EXPERTS_PER_GROUP + i1,
                             jnp.where(out_rows == 1, gidx * EXPERTS_PER_GROUP + i2, 0))
    wts_ref[...] = jnp.where(out_rows == 0, w1, jnp.where(out_rows == 1, w2, 0.0))


def _outproj_router(x2, attn, ret, w_out_bf, gt1, g2, sh2, sc2, wr, br):
    l, d = x2.shape
    tm = min(ROW_TILE, l)
    row = lambda w: pl.BlockSpec((tm, w), lambda i: (i, 0))
    col = pl.BlockSpec((SUBLANES, tm), lambda i: (0, i))
    return pl.pallas_call(
        _outproj_kernel,
        grid=(l // tm,),
        in_specs=[row(d),
                  pl.BlockSpec((ATTN_KV_HEADS, HEAD_DIM, GQA_GROUP * tm), lambda i: (0, 0, i)),
                  row(ret.shape[1]), _const_spec(w_out_bf.shape),
                  _const_spec((1, d)), _const_spec((1, d)), _const_spec((1, d)), _const_spec((1, d)),
                  _const_spec(wr.shape), _const_spec(br.shape)],
        out_specs=[row(d), pl.BlockSpec((tm * SUBLANES, d // SUBLANES), lambda i: (i, 0)), col, col],
        out_shape=[jax.ShapeDtypeStruct((l, d), F32), jax.ShapeDtypeStruct((l * SUBLANES, d // SUBLANES), F32),
                   jax.ShapeDtypeStruct((SUBLANES, l), jnp.int32),
                   jax.ShapeDtypeStruct((SUBLANES, l), F32)],
        compiler_params=_cparams("arbitrary"),
        name="outproj_router",
    )(x2, attn, ret, w_out_bf, gt1, g2, sh2, sc2, wr, br)


def _row_copy(src_hbm, src_row, dst_ref, sem):
    return pltpu.make_async_copy(src_hbm.at[pl.ds(src_row * SUBLANES, SUBLANES), :], dst_ref, sem)


def _moe_kernel(be_ref, na_ref, tok0_ref, tok1_ref, tokn_ref, h_hbm, wg_ref, wu_ref, wd_ref, y_ref,
                wg_s, wu_s, wd_s, xg_s, sem):
    b = pl.program_id(0)
    slot = lax.rem(b, MOE_SLOTS)
    nxt = lax.rem(b + 2, MOE_SLOTS)
    n_active = na_ref[0]
    active = b < n_active
    last = be_ref.shape[0] - 1
    changed = jnp.logical_or(b == 0, be_ref[jnp.minimum(b, last)] != be_ref[jnp.clip(b - 1, 0, last)])

    def wait_rows():
        rows = pl.ds(0, MOE_BLOCK * SUBLANES)
        pltpu.make_async_copy(h_hbm.at[rows, :], xg_s.at[slot, rows, :], sem.at[slot]).wait()

    @pl.when(b == 0)
    def _():
        def body(r, carry):
            _row_copy(h_hbm, tok0_ref[0, 0, r], xg_s.at[0, pl.ds(r * SUBLANES, SUBLANES), :], sem.at[0]).start()
            _row_copy(h_hbm, tok1_ref[0, 0, r], xg_s.at[1, pl.ds(r * SUBLANES, SUBLANES), :], sem.at[1]).start()
            return carry

        lax.fori_loop(0, MOE_BLOCK, body, 0, unroll=8)

    @pl.when(jnp.logical_and(active, changed))
    def _():
        wg_s[...] = wg_ref[0].astype(BF16)
        wu_s[...] = wu_ref[0].astype(BF16)
        wd_s[...] = wd_ref[0].astype(BF16)

    @pl.when(active)
    def _():
        wait_rows()
        x = _load_row_tiles(xg_s, (slot,), MOE_BLOCK).astype(BF16)
        hg = jnp.dot(x, wg_s[...], preferred_element_type=F32)
        hu = jnp.dot(x, wu_s[...], preferred_element_type=F32)
        a = (hg / (1.0 + jnp.exp(-hg))) * hu
        y = jnp.dot(a.astype(BF16), wd_s[...], preferred_element_type=F32)
        _store_row_tiles(y_ref, (), y)
        half, quarter, w = MOE_BLOCK // 2, MOE_BLOCK // 4, xg_s.shape[2]
        marks = [hg[0:SUBLANES, 0:w], hg[half:half + SUBLANES, 0:w],
                 hu[0:SUBLANES, 0:w], hu[half:half + SUBLANES, 0:w]]
        marks += [y[i * quarter:i * quarter + SUBLANES, 0:w] for i in range(4)]
        per = MOE_BLOCK // len(marks)
        for gi, mark in enumerate(marks):
            xg_s[nxt, pl.ds(MOE_BLOCK * SUBLANES, SUBLANES), :] = mark
            for r in range(gi * per, (gi + 1) * per):
                _row_copy(h_hbm, tokn_ref[0, 0, r], xg_s.at[nxt, pl.ds(r * SUBLANES, SUBLANES), :],
                          sem.at[nxt]).start(priority=r % 2)

    @pl.when(jnp.logical_or(b == n_active, b == n_active + 1))
    def _():
        wait_rows()

    @pl.when(jnp.logical_not(active))
    def _():
        y_ref[...] = jnp.zeros(y_ref.shape, y_ref.dtype)


def _moe_blocks(blk_e, n_active, buf_tok, h2, w_gate, w_up, w_down):
    ne, d, de = w_gate.shape
    tiled = (MOE_BLOCK * SUBLANES, d // SUBLANES)
    nblk = buf_tok.shape[0]
    blk = lambda b: jnp.minimum(b, nblk - 1)
    wmap = lambda b, be, na: (be[blk(b)], 0, 0)
    tokmap = lambda off: (lambda b, be, na: (blk(b + off), 0, 0))
    smem_tok = lambda off: pl.BlockSpec((1, 1, MOE_BLOCK), tokmap(off), memory_space=pltpu.SMEM)
    return pl.pallas_call(
        _moe_kernel,
        grid_spec=pltpu.PrefetchScalarGridSpec(
            num_scalar_prefetch=2, grid=(nblk + 1,),
            in_specs=[pl.BlockSpec((1, 1, MOE_BLOCK), lambda b, be, na: (0, 0, 0), memory_space=pltpu.SMEM),
                      pl.BlockSpec((1, 1, MOE_BLOCK), lambda b, be, na: (1, 0, 0), memory_space=pltpu.SMEM),
                      smem_tok(2),
                      pl.BlockSpec(memory_space=pl.ANY),
                      pl.BlockSpec((1, d, de), wmap), pl.BlockSpec((1, d, de), wmap),
                      pl.BlockSpec((1, de, d), wmap)],
            out_specs=pl.BlockSpec(tiled, lambda b, be, na: (blk(b), 0)),
            scratch_shapes=[pltpu.VMEM((d, de), BF16), pltpu.VMEM((d, de), BF16),
                            pltpu.VMEM((de, d), BF16),
                            pltpu.VMEM((MOE_SLOTS, tiled[0] + SUBLANES, tiled[1]), F32),
                            pltpu.SemaphoreType.DMA((MOE_SLOTS,))]),
        out_shape=jax.ShapeDtypeStruct((nblk * tiled[0], tiled[1]), F32),
        compiler_params=_cparams("arbitrary"),
        name="moe_experts",
    )(blk_e, n_active, buf_tok, buf_tok, buf_tok, h2, w_gate, w_up, w_down)


def _final_kernel(d0_ref, dn_ref, x_ref, w_ref, gt_ref, g_ref, y_hbm, o_ref, yg_s, sem):
    i = pl.program_id(0)
    slot = i % 2
    tm = x_ref.shape[0]

    def start_tile(d_ref, slot_):
        def body(r, carry):
            for k in range(TOP_K):
                _row_copy(y_hbm, d_ref[0, k, r], yg_s.at[slot_, k, pl.ds(r * SUBLANES, SUBLANES), :],
                          sem.at[slot_]).start(priority=k)
            return carry

        lax.fori_loop(0, tm, body, 0, unroll=8)

    @pl.when(i == 0)
    def _():
        start_tile(d0_ref, 0)

    @pl.when(i + 1 < pl.num_programs(0))
    def _():
        start_tile(dn_ref, 1 - slot)

    for k in range(TOP_K):
        pltpu.make_async_copy(y_hbm.at[pl.ds(0, tm * SUBLANES), :], yg_s.at[slot, k], sem.at[slot]).wait()
    w = w_ref[...]
    y = _load_row_tiles(yg_s, (slot, 0), tm) * w[:, 0:1] + _load_row_tiles(yg_s, (slot, 1), tm) * w[:, 1:2]
    x = x_ref[...] + gt_ref[...] * y
    ms = jnp.mean(x * x, axis=-1, keepdims=True)
    o_ref[...] = (x * lax.rsqrt(ms + EPS)) * g_ref[...]


def _final(x_new, yb, dest, wts, gt2, gfin):
    l, d = x_new.shape
    n, _, tm = dest.shape
    row = lambda w: pl.BlockSpec((tm, w), lambda i: (i, 0))
    return pl.pallas_call(
        _final_kernel,
        grid=(n,),
        in_specs=[pl.BlockSpec((1, TOP_K, tm), lambda i: (0, 0, 0), memory_space=pltpu.SMEM),
                  pl.BlockSpec((1, TOP_K, tm), lambda i: (jnp.minimum(i + 1, n - 1), 0, 0),
                               memory_space=pltpu.SMEM),
                  row(d), row(wts.shape[1]), _const_spec((1, d)), _const_spec((1, d)),
                  pl.BlockSpec(memory_space=pl.ANY)],
        out_specs=row(d),
        out_shape=jax.ShapeDtypeStruct((l, d), F32),
        scratch_shapes=[pltpu.VMEM((2, TOP_K, tm * SUBLANES, d // SUBLANES), F32), pltpu.SemaphoreType.DMA((2,))],
        compiler_params=_cparams("arbitrary"),
        name="combine_final_norm",
    )(dest, dest, x_new, wts, gt2, gfin, yb)


def _rope_tables(rows_count):
    row = jnp.repeat(jnp.arange(rows_count), GRID_W).astype(F32)
    col = jnp.tile(jnp.arange(GRID_W), rows_count).astype(F32)
    freqs = ROPE_THETA ** (-jnp.arange(0, ROPE_AXIS_DIM, 2, dtype=F32) / ROPE_AXIS_DIM)
    ang = jnp.concatenate([row[:, None] * freqs, col[:, None] * freqs], axis=-1)
    ang = jnp.repeat(ang, 2, axis=-1)
    sign = jnp.where(jnp.arange(HEAD_DIM) % 2 == 0, -1.0, 1.0).astype(F32)
    cos = jnp.tile(jnp.cos(ang), (1, LANES // HEAD_DIM))
    sin = jnp.tile(jnp.sin(ang) * sign, (1, LANES // HEAD_DIM))
    return cos, sin


def _dispatch(ids, n_tokens):
    a = TOP_K * n_tokens
    eid_f = ids[:TOP_K].reshape(a)
    tok_f = jnp.tile(jnp.arange(n_tokens, dtype=jnp.int32), TOP_K)
    onehot = (eid_f[:, None] == jnp.arange(N_EXPERTS, dtype=jnp.int32)[None, :]).astype(jnp.int32)
    incl = jnp.cumsum(onehot, axis=0)
    counts = incl[-1]
    rank = jnp.sum((incl - onehot) * onehot, axis=1)
    pcounts = (counts + MOE_BLOCK - 1) // MOE_BLOCK * MOE_BLOCK
    pends = jnp.cumsum(pcounts)
    pstarts = pends - pcounts
    dest = jnp.sum(onehot * pstarts[None, :], axis=1) + rank
    p = a + N_EXPERTS * MOE_BLOCK
    nblk = p // MOE_BLOCK
    n_fill = p - a
    fill_ends = jnp.cumsum(pcounts - counts)
    fill_idx = jnp.arange(n_fill, dtype=jnp.int32)
    fill_e = jnp.sum(fill_idx[:, None] >= fill_ends[None, :], axis=1).astype(jnp.int32)
    idx_bits = max(a, n_fill).bit_length()
    real_keys = (eid_f << (idx_bits + 1)) | jnp.arange(a, dtype=jnp.int32)
    fill_keys = (fill_e << (idx_bits + 1)) | (1 << idx_bits) | fill_idx
    keys = jnp.sort(jnp.concatenate([real_keys, fill_keys]))
    is_real = ((keys >> idx_bits) & 1) == 0
    buf_tok = jnp.where(is_real, (keys & ((1 << idx_bits) - 1)) % n_tokens, 0)
    del tok_f
    n_active = pends[-1] // MOE_BLOCK
    blk_start = jnp.arange(nblk, dtype=jnp.int32) * MOE_BLOCK
    blk_e = jnp.minimum(jnp.sum(blk_start[:, None] >= pends[None, :], axis=1), N_EXPERTS - 1)
    blk_e = jnp.where(jnp.arange(nblk) < n_active, blk_e, blk_e[n_active - 1]).astype(jnp.int32)
    return buf_tok, dest.reshape(TOP_K, n_tokens), blk_e, n_active.astype(jnp.int32).reshape(1)


def _split_bf16(w):
    hi = w.astype(BF16)
    return jnp.stack([hi, (w - hi.astype(F32)).astype(BF16)])


def kernel(x, c, ctx, c_ctx, w_ada, b_ada, norm1_g, w_in, attn_q_norm, attn_k_norm, ret_decay_fwd,
           ret_decay_bwd, ret_gn_g, ret_gn_b, w_out, norm2_g, moe_w_grp, moe_b_grp, moe_w_exp, moe_b_exp,
           moe_w_gate, moe_w_up, moe_w_down, final_norm_g):
    b, l, d = x.shape
    cl = ctx.shape[1]
    assert b == 1 and w_ada.shape[0] == 1, "single batch element, single layer"
    x2, ctx2 = x[0], ctx[0]

    cvecs = jnp.zeros((SUBLANES, d), F32).at[0].set(c[0]).at[1].set(c_ctx)
    mod = _ada(cvecs, w_ada[0], b_ada[0])
    sh1, sc1, gt1, sh2, sc2, gt2 = [mod[0:1, i * d:(i + 1) * d] for i in range(6)]
    csh1, csc1 = mod[1:2, 0:d], mod[1:2, d:2 * d]

    w_in_bf = w_in[0].astype(BF16)
    head_of = jnp.arange(ATTN_WIDTH) // HEAD_DIM
    bd = jnp.where(head_of[:, None] == head_of[None, :], 1.0 / HEAD_DIM, 0.0).astype(BF16)
    gq = jnp.tile(attn_q_norm[0], ATTN_HEADS)[None, :]
    gk = jnp.tile(attn_k_norm[0], ATTN_KV_HEADS)[None, :]
    g1 = norm1_g[0][None, :]
    cos, sin = _rope_tables(l // GRID_W)
    bound = (HEAD_DIM ** 0.5 * LOG2E * 1.01) * jnp.max(jnp.abs(attn_q_norm[0])) * jnp.max(jnp.abs(attn_k_norm[0]))
    bound = bound.astype(BF16).astype(F32)
    neg_bound = jnp.full((1, LANES), -1.0, F32) * bound
    q_t, ka, va, qr, kr, vr, gr = _inproj(x2, sh1, sc1, g1, w_in_bf, bd, gq, gk, cos, sin, neg_bound)
    _, ka_c, va_c, _, kr_c, vr_c, _ = _inproj(ctx2, csh1, csc1, g1, w_in_bf, bd, gq, gk,
                                              jnp.ones((cl, LANES), F32), jnp.zeros((cl, LANES), F32), neg_bound)

    nk = cl + l
    tk = _key_chunk(nk)
    k_all = jnp.concatenate([ka_c, ka], axis=0).reshape(nk // tk, tk, ATTN_KV_HEADS, HEAD_DIM)
    k_aug = jnp.zeros(k_all.shape[:3] + (ATTN_AUG_DIM - HEAD_DIM,), BF16).at[..., 0].set(1.0)
    k_all = jnp.concatenate([k_all, k_aug], axis=-1)
    v_all = jnp.concatenate([va_c, va], axis=0).reshape(nk // tk, tk, ATTN_KV_HEADS, HEAD_DIM)
    attn_args = (q_t, k_all.transpose(2, 0, 1, 3), v_all.transpose(2, 0, 3, 1))
    o_t = lax.cond(bound <= MAX_UNSHIFTED_BOUND,
                   functools.partial(_attention, running_max=False),
                   functools.partial(_attention, running_max=True), *attn_args)

    log_gf = jax.nn.log_sigmoid(ret_decay_fwd[0].astype(F32))
    log_gb = jax.nn.log_sigmoid(ret_decay_bwd[0].astype(F32))
    tab_f, gc_f = _retention_tables(log_gf, False)
    tab_b, gc_b = _retention_tables(log_gb, True)
    o_f = _retention_pass(gc_f, qr, kr, vr, kr_c, vr_c, tab_f, None, reverse=False)
    ret = _retention_pass(gc_b, qr, kr, vr, kr_c, vr_c, tab_b,
                          (o_f, gr, ret_gn_g[0][None, :], ret_gn_b[0][None, :]), reverse=True)

    wr = jnp.zeros((ROUTER_ROWS, d), F32)
    wr = wr.at[:N_GROUPS].set(moe_w_grp[0].T).at[EXPERT_ROW0:EXPERT_ROW0 + N_EXPERTS].set(moe_w_exp[0].T)
    br = jnp.zeros((ROUTER_ROWS, 1), F32)
    br = br.at[:N_GROUPS, 0].set(moe_b_grp[0]).at[EXPERT_ROW0:EXPERT_ROW0 + N_EXPERTS, 0].set(moe_b_exp[0])
    x_new, h2, ids, wts = _outproj_router(x2, o_t, ret, w_out[0].astype(BF16), gt1, norm2_g[0][None, :],
                                          sh2, sc2, _split_bf16(wr), br)

    buf_tok, dest, blk_e, n_active = _dispatch(ids, l)
    yb = _moe_blocks(blk_e, n_active, buf_tok.reshape(-1, 1, MOE_BLOCK), h2, moe_w_gate[0], moe_w_up[0],
                     moe_w_down[0])
    tf = min(FINAL_TILE, l)
    dest_t = dest.reshape(TOP_K, l // tf, tf).transpose(1, 0, 2)
    out = _final(x_new, yb, dest_t, wts[:TOP_K].T, gt2, final_norm_g[None, :])
    return out[None]
```

```python
import functools

import jax
import jax.numpy as jnp
import numpy as np
from jax import lax
from jax.experimental import pallas as pl
from jax.experimental.pallas import tpu as pltpu

F32 = jnp.float32
BF16 = jnp.bfloat16

GRID_W = 64
HEAD_DIM = 64
ATTN_HEADS = 8
ATTN_KV_HEADS = 2
GQA_GROUP = ATTN_HEADS // ATTN_KV_HEADS
RET_HEADS = 4
RET_QK_DIM = 64
RET_V_DIM = 128
RET_CHUNK = 256
ATTN_WIDTH = ATTN_HEADS * HEAD_DIM
KV_WIDTH = ATTN_KV_HEADS * HEAD_DIM
RET_QK_WIDTH = RET_HEADS * RET_QK_DIM
RET_WIDTH = RET_HEADS * RET_V_DIM
PROJ_SIZES = (ATTN_WIDTH, KV_WIDTH, KV_WIDTH, RET_QK_WIDTH, RET_QK_WIDTH, RET_WIDTH, RET_WIDTH)
PROJ_OFFS = tuple(int(v) for v in np.cumsum((0,) + PROJ_SIZES))
PROJ_DIM = PROJ_OFFS[-1]
ROPE_THETA = 10000.0
ROPE_AXIS_DIM = HEAD_DIM // 2
N_GROUPS = 4
EXPERTS_PER_GROUP = 8
N_EXPERTS = N_GROUPS * EXPERTS_PER_GROUP
TOP_K = 2
D_EXPERT = 512
EPS = 1e-6
LOG2E = 1.4426950408889634
ATTN_AUG_DIM = 128
MAX_UNSHIFTED_BOUND = 60.0

LANES = 128
SUBLANES = 8
VMEM_LIMIT_BYTES = 56 * 1024 * 1024

ROW_TILE = 512
INPROJ_SUBTILE = 256
ATTN_Q_TILE = 512
RET_ROW_TILE = 1024
RET_UNROLL = 4
MOE_BLOCK = 256
MOE_SLOTS = 3
FINAL_TILE = 512
ROUTER_ROWS = 128
EXPERT_ROW0 = 8


def _cparams(*sem):
    return pltpu.CompilerParams(dimension_semantics=sem, vmem_limit_bytes=VMEM_LIMIT_BYTES)


def _const_spec(shape):
    nd = len(shape)
    return pl.BlockSpec(shape, lambda *_: (0,) * nd)


def _store_row_tiles(ref, idx, rows):
    n, w = rows.shape[0], rows.shape[1] // SUBLANES
    for s in range(SUBLANES):
        ref[idx + (pl.ds(s, n, stride=SUBLANES), slice(None))] = rows[:, s * w:(s + 1) * w]


def _load_row_tiles(ref, idx, n):
    return jnp.concatenate([ref[idx + (pl.ds(s, n, stride=SUBLANES), slice(None))] for s in range(SUBLANES)],
                           axis=1)


def _ada_kernel(s_ref, w_ref, b_ref, o_ref):
    s = s_ref[...]
    s = s / (1.0 + jnp.exp(-s))
    o_ref[...] = jnp.dot(s, w_ref[...], preferred_element_type=F32,
                         precision=lax.Precision.HIGHEST) + b_ref[...]


def _ada(cvecs, w_ada, b_ada):
    d, n = w_ada.shape
    tn = 1536
    return pl.pallas_call(
        _ada_kernel,
        grid=(n // tn,),
        in_specs=[_const_spec((SUBLANES, d)),
                  pl.BlockSpec((d, tn), lambda j: (0, j)),
                  pl.BlockSpec((1, tn), lambda j: (0, j))],
        out_specs=pl.BlockSpec((SUBLANES, tn), lambda j: (0, j)),
        out_shape=jax.ShapeDtypeStruct((SUBLANES, n), F32),
        compiler_params=_cparams("arbitrary"),
        name="ada_mod",
    )(cvecs, w_ada, b_ada.reshape(1, n))


def _rms_modulate(x, g, sh, sc):
    ms = jnp.mean(x * x, axis=-1, keepdims=True)
    return (x * lax.rsqrt(ms + EPS)) * g * (1.0 + sc) + sh


def _head_mean_sq(v, bd):
    sq = v * v
    hi = sq.astype(BF16)
    lo = (sq - hi.astype(F32)).astype(BF16)
    return (jnp.dot(hi, bd, preferred_element_type=F32) + jnp.dot(lo, bd, preferred_element_type=F32))


def _rope_chunks(v, cos, sin, even):
    outs = []
    for c in range(v.shape[1] // LANES):
        xc = v[:, c * LANES:(c + 1) * LANES]
        nxt = pltpu.roll(xc, LANES - 1, 1)
        prv = pltpu.roll(xc, 1, 1)
        outs.append(xc * cos + jnp.where(even, nxt, prv) * sin)
    return outs


def _inproj_kernel(x_ref, sh_ref, sc_ref, g_ref, w_ref, bd_ref, gq_ref, gk_ref, cos_ref, sin_ref, nb_ref,
                   qt_ref, ka_ref, va_ref, qr_ref, kr_ref, vr_ref, gr_ref):
    tm = x_ref.shape[0]
    sub = min(INPROJ_SUBTILE, tm)
    bd = bd_ref[...]
    scale = HEAD_DIM ** -0.5 * LOG2E
    rscale = RET_QK_DIM ** -0.5
    o = PROJ_OFFS
    per = LANES // HEAD_DIM
    for j0 in range(0, tm, sub):
        rows = slice(j0, j0 + sub)
        h = _rms_modulate(x_ref[rows, :], g_ref[...], sh_ref[...], sc_ref[...])
        proj = jnp.dot(h.astype(BF16), w_ref[...], preferred_element_type=F32)
        qa, ka, va = proj[:, o[0]:o[1]], proj[:, o[1]:o[2]], proj[:, o[2]:o[3]]
        qr, kr, vr, gr = proj[:, o[3]:o[4]], proj[:, o[4]:o[5]], proj[:, o[5]:o[6]], proj[:, o[6]:o[7]]
        cos, sin = cos_ref[rows, :], sin_ref[rows, :]
        even = (lax.broadcasted_iota(jnp.int32, cos.shape, 1) % 2) == 0

        qn = qa * lax.rsqrt(_head_mean_sq(qa, bd) + EPS) * gq_ref[...]
        for c, blk in enumerate(_rope_chunks(qn, cos, sin, even)):
            blk_t = (blk * scale).T
            for j in range(per):
                kvh, grp = divmod(c * per + j, GQA_GROUP)
                qt_ref[kvh, 0:HEAD_DIM, grp * tm + j0:grp * tm + j0 + sub] = (
                    blk_t[j * HEAD_DIM:(j + 1) * HEAD_DIM, :].astype(BF16))
        kn = ka * lax.rsqrt(_head_mean_sq(ka, bd[:KV_WIDTH, :KV_WIDTH]) + EPS) * gk_ref[...]
        k_rot = _rope_chunks(kn, cos, sin, even)[0].astype(BF16)
        pad_shape = (sub, ATTN_AUG_DIM - HEAD_DIM)
        k_pad = jnp.where(lax.broadcasted_iota(jnp.int32, pad_shape, 1) == 0, 1.0, 0.0).astype(BF16)
        for kvh in range(ATTN_KV_HEADS):
            c0 = kvh * ATTN_AUG_DIM
            ka_ref[rows, c0:c0 + HEAD_DIM] = k_rot[:, kvh * HEAD_DIM:(kvh + 1) * HEAD_DIM]
            ka_ref[rows, c0 + HEAD_DIM:c0 + ATTN_AUG_DIM] = k_pad
        va_ref[rows, :] = va.astype(BF16)
        for c, blk in enumerate(_rope_chunks(qr, cos, sin, even)):
            qr_ref[rows, c * LANES:(c + 1) * LANES] = blk.astype(BF16)
        for c, blk in enumerate(_rope_chunks(kr, cos, sin, even)):
            kr_ref[rows, c * LANES:(c + 1) * LANES] = (blk * rscale).astype(BF16)
        vr_ref[rows, :] = vr.astype(BF16)
        gr_ref[rows, :] = gr
    aug_shape = (ATTN_AUG_DIM - HEAD_DIM, GQA_GROUP * tm)
    aug_row = lax.broadcasted_iota(jnp.int32, aug_shape, 0)
    aug = jnp.where(aug_row == 0, nb_ref[0:1, 0:1], 0.0).astype(BF16)
    for kvh in range(ATTN_KV_HEADS):
        qt_ref[kvh, HEAD_DIM:ATTN_AUG_DIM, :] = aug


def _inproj(x2, sh, sc, g1, w_bf, bd, gq, gk, cos, sin, neg_bound):
    rows, d = x2.shape
    tm = min(ROW_TILE, rows)
    row = lambda w: pl.BlockSpec((tm, w), lambda i: (i, 0))
    widths = (ATTN_KV_HEADS * ATTN_AUG_DIM,) + PROJ_SIZES[2:]
    dtypes = (BF16,) * 5 + (F32,)
    qt_shape = (ATTN_KV_HEADS, ATTN_AUG_DIM, GQA_GROUP * rows)
    return pl.pallas_call(
        _inproj_kernel,
        grid=(rows // tm,),
        in_specs=[row(d), _const_spec((1, d)), _const_spec((1, d)), _const_spec((1, d)),
                  _const_spec(w_bf.shape), _const_spec(bd.shape),
                  _const_spec((1, ATTN_WIDTH)), _const_spec((1, KV_WIDTH)),
                  row(LANES), row(LANES), _const_spec((1, LANES))],
        out_specs=[pl.BlockSpec((ATTN_KV_HEADS, ATTN_AUG_DIM, GQA_GROUP * tm), lambda i: (0, 0, i))]
        + [row(w) for w in widths],
        out_shape=[jax.ShapeDtypeStruct(qt_shape, BF16)]
        + [jax.ShapeDtypeStruct((rows, w), dt) for w, dt in zip(widths, dtypes)],
        compiler_params=_cparams("arbitrary"),
        name="norm_inproj",
    )(x2, sh, sc, g1, w_bf, bd, gq, gk, cos, sin, neg_bound)


def _attn_kernel(q_ref, k_ref, v_ref, o_ref, m_ref, l_ref, acc_ref, *, n_chunks, running_max):
    m_ref[...] = jnp.full(m_ref.shape, -jnp.inf, F32)
    l_ref[...] = jnp.zeros(l_ref.shape, F32)
    acc_ref[...] = jnp.zeros(acc_ref.shape, F32)

    tk = v_ref.shape[3]

    def body(c, carry):
        kc = k_ref[pl.ds(pl.multiple_of(c * tk, tk), tk), :]
        vc = v_ref[0, c]
        s = jnp.dot(kc, q_ref[0], preferred_element_type=F32)
        if running_max:
            m_old = m_ref[...]
            m_new = jnp.maximum(m_old, jnp.max(s, axis=0, keepdims=True))
            alpha = jnp.exp2(m_old - m_new)
            p = jnp.exp2(s - m_new)
            l_ref[...] = alpha * l_ref[...] + jnp.sum(p, axis=0, keepdims=True)
            acc_ref[...] = acc_ref[...] * alpha + jnp.dot(vc, p.astype(BF16), preferred_element_type=F32)
            m_ref[...] = m_new
        else:
            p = jnp.exp2(s)
            l_ref[...] += jnp.sum(p, axis=0, keepdims=True)
            acc_ref[...] += jnp.dot(vc, p.astype(BF16), preferred_element_type=F32)
        return carry

    lax.fori_loop(0, n_chunks, body, 0)
    o_ref[0] = (acc_ref[...] / l_ref[...]).astype(o_ref.dtype)


def _key_chunk(nk):
    for tk in (3328, 1280, 640, 256, 128):
        if nk % tk == 0:
            return tk
    raise ValueError(f"unsupported key count {nk}")


def _attention(q_t, k_c, v_t, *, running_max):
    kv, da, gl = q_t.shape
    _, n_chunks, d, tk = v_t.shape
    nq = GQA_GROUP * min(ATTN_Q_TILE, gl // GQA_GROUP)
    return pl.pallas_call(
        functools.partial(_attn_kernel, n_chunks=n_chunks, running_max=running_max),
        grid=(kv, gl // nq),
        in_specs=[pl.BlockSpec((1, da, nq), lambda b, i: (b, 0, i)),
                  pl.BlockSpec((n_chunks * tk, da), lambda b, i: (0, b)),
                  pl.BlockSpec((1, n_chunks, d, tk), lambda b, i: (b, 0, 0, 0))],
        out_specs=pl.BlockSpec((1, d, nq), lambda b, i: (b, 0, i)),
        out_shape=jax.ShapeDtypeStruct((kv, d, gl), BF16),
        scratch_shapes=[pltpu.VMEM((1, nq), F32), pltpu.VMEM((1, nq), F32), pltpu.VMEM((d, nq), F32)],
        compiler_params=_cparams("arbitrary", "arbitrary"),
        name="flash_attn",
    )(q_t, k_c, v_t)


def _ret_kernel(gc_ref, q_ref, k_ref, v_ref, kc_ref, vc_ref, tab_ref, *rest, reverse, final):
    if final:
        of_ref, gate_ref, gng_ref, gnb_ref, out_ref, s_ref = rest
    else:
        out_ref, s_ref = rest
    c_rows = RET_CHUNK
    n_local = q_ref.shape[0] // c_rows
    n_ctx = kc_ref.shape[0] // c_rows
    lane_head = lax.broadcasted_iota(jnp.int32, (c_rows, RET_QK_WIDTH), 1) // RET_QK_DIM

    def state_update(kch, vch):
        for h in range(RET_HEADS):
            vz = (vch[:, h * RET_V_DIM:(h + 1) * RET_V_DIM].astype(F32)
                  * tab_ref[1, h, :, 0:RET_V_DIM]).astype(BF16)
            u = lax.dot_general(kch, vz, (((0,), (0,)), ((), ())), preferred_element_type=F32)
            s_ref[h] = gc_ref[h] * s_ref[h] + u

    @pl.when(pl.program_id(0) == 0)
    def _():
        s_ref[...] = jnp.zeros(s_ref.shape, F32)
        order = range(n_ctx - 1, -1, -1) if reverse else range(n_ctx)
        for cc in order:
            state_update(kc_ref[cc * c_rows:(cc + 1) * c_rows, :], vc_ref[cc * c_rows:(cc + 1) * c_rows, :])

    def chunk(ci, carry):
        c = (n_local - 1 - ci) if reverse else ci
        rows = pl.ds(pl.multiple_of(c * c_rows, c_rows), c_rows)
        q, k, v = q_ref[rows, :], k_ref[rows, :], v_ref[rows, :]
        for h in range(RET_HEADS):
            cols = slice(h * RET_V_DIM, (h + 1) * RET_V_DIM)
            qm = jnp.where(lane_head == h, q, jnp.zeros_like(q))
            sc = lax.dot_general(qm, k, (((1,), (1,)), ((), ())), preferred_element_type=F32)
            sc = sc * tab_ref[0, h]
            o = jnp.dot(sc.astype(BF16), v[:, cols], preferred_element_type=F32)
            o = o + (jnp.dot(qm, s_ref[h].astype(BF16), preferred_element_type=F32)
                     * tab_ref[2, h, :, 0:RET_V_DIM])
            if final:
                o = o + of_ref[rows, cols]
                mu = jnp.mean(o, axis=-1, keepdims=True)
                oc = o - mu
                var = jnp.mean(oc * oc, axis=-1, keepdims=True)
                on = oc * lax.rsqrt(var + EPS)
                gate = gate_ref[rows, cols]
                y = (on * gng_ref[:, cols] + gnb_ref[:, cols]) * (gate / (1.0 + jnp.exp(-gate)))
                out_ref[rows, cols] = y.astype(out_ref.dtype)
            else:
                out_ref[rows, cols] = o
        state_update(k, v)
        return carry

    lax.fori_loop(0, n_local, chunk, 0, unroll=RET_UNROLL)


def _retention_pass(gc, qr, kr, vr, kr_c, vr_c, tabs, extra, *, reverse):
    l = qr.shape[0]
    tr = min(RET_ROW_TILE, l)
    nt = l // tr
    final = extra is not None
    idx = (lambda j, gc_: (nt - 1 - j, 0)) if reverse else (lambda j, gc_: (j, 0))
    const = lambda shape: pl.BlockSpec(shape, lambda j, gc_: (0,) * len(shape))
    in_specs = [pl.BlockSpec((tr, RET_QK_WIDTH), idx), pl.BlockSpec((tr, RET_QK_WIDTH), idx),
                pl.BlockSpec((tr, RET_WIDTH), idx),
                const(kr_c.shape), const(vr_c.shape), const(tabs.shape)]
    args = [qr, kr, vr, kr_c, vr_c, tabs]
    if final:
        of, gate, gng, gnb = extra
        in_specs += [pl.BlockSpec((tr, RET_WIDTH), idx), pl.BlockSpec((tr, RET_WIDTH), idx),
                     const(gng.shape), const(gnb.shape)]
        args += [of, gate, gng, gnb]
    return pl.pallas_call(
        functools.partial(_ret_kernel, reverse=reverse, final=final),
        grid_spec=pltpu.PrefetchScalarGridSpec(
            num_scalar_prefetch=1, grid=(nt,), in_specs=in_specs,
            out_specs=pl.BlockSpec((tr, RET_WIDTH), idx),
            scratch_shapes=[pltpu.VMEM((RET_HEADS, RET_QK_WIDTH, RET_V_DIM), F32)]),
        out_shape=jax.ShapeDtypeStruct((l, RET_WIDTH), BF16 if final else F32),
        compiler_params=_cparams("arbitrary"),
        name="retention_bwd" if reverse else "retention_fwd",
    )(gc, *args)


def _retention_tables(log_g, reverse):
    c = RET_CHUNK
    pos = jnp.arange(c, dtype=F32)
    diff = (pos[None, :] - pos[:, None]) if reverse else (pos[:, None] - pos[None, :])
    mask = (diff > 0) if reverse else (diff >= 0)
    lg = log_g[:, None, None]
    dmat = jnp.where(mask, jnp.exp(lg * jnp.where(mask, diff, 0.0)[None]), 0.0)
    zeta = jnp.exp(log_g[:, None] * (pos if reverse else (c - 1 - pos))[None, :])
    xi = jnp.exp(log_g[:, None] * ((c - pos) if reverse else (pos + 1.0))[None, :])
    ones = jnp.ones((1, 1, c), F32)
    tabs = jnp.stack([dmat, zeta[:, :, None] * ones, xi[:, :, None] * ones])
    return tabs, jnp.exp(log_g * c)


def _outproj_kernel(x_ref, a_ref, r_ref, w_ref, gt_ref, g2_ref, sh_ref, sc_ref, wr_ref, br_ref,
                    xn_ref, h2_ref, ids_ref, wts_ref):
    tm = x_ref.shape[0]
    mix = jnp.dot(r_ref[...], w_ref[ATTN_WIDTH:, :], preferred_element_type=F32)
    per = LANES // HEAD_DIM
    for pair in range(ATTN_HEADS // per):
        pieces = []
        for j in range(per):
            kvh, grp = divmod(pair * per + j, GQA_GROUP)
            pieces.append(a_ref[kvh, :, grp * tm:(grp + 1) * tm])
        o_t = jnp.concatenate(pieces, axis=0)
        mix = mix + lax.dot_general(o_t, w_ref[pair * LANES:(pair + 1) * LANES, :],
                                    (((0,), (0,)), ((), ())), preferred_element_type=F32)
    x_new = x_ref[...] + gt_ref[...] * mix
    xn_ref[...] = x_new
    h2 = _rms_modulate(x_new, g2_ref[...], sh_ref[...], sc_ref[...])
    _store_row_tiles(h2_ref, (), h2)
    hi = h2.astype(BF16)
    lo = (h2 - hi.astype(F32)).astype(BF16)

    nt = (((1,), (1,)), ((), ()))
    logits = (lax.dot_general(wr_ref[0], hi, nt, preferred_element_type=F32)
              + lax.dot_general(wr_ref[0], lo, nt, preferred_element_type=F32)
              + lax.dot_general(wr_ref[1], hi, nt, preferred_element_type=F32)) + br_ref[...]
    rows = [logits[g:g + 1, :] for g in range(N_GROUPS)]
    gmax = functools.reduce(jnp.maximum, rows)
    gidx = jnp.full(gmax.shape, N_GROUPS - 1, jnp.int32)
    for g in range(N_GROUPS - 2, -1, -1):
        gidx = jnp.where(rows[g] == gmax, g, gidx)
    p_g = 1.0 / functools.reduce(jnp.add, [jnp.exp(r - gmax) for r in rows])
    sel = logits[EXPERT_ROW0 + (N_GROUPS - 1) * EXPERTS_PER_GROUP:EXPERT_ROW0 + N_GROUPS * EXPERTS_PER_GROUP, :]
    for g in range(N_GROUPS - 2, -1, -1):
        blk = logits[EXPERT_ROW0 + g * EXPERTS_PER_GROUP:EXPERT_ROW0 + (g + 1) * EXPERTS_PER_GROUP, :]
        sel = jnp.where(gidx == g, blk, sel)
    ridx = lax.broadcasted_iota(jnp.int32, sel.shape, 0)
    m1 = jnp.max(sel, axis=0, keepdims=True)
    i1 = jnp.min(jnp.where(sel == m1, ridx, EXPERTS_PER_GROUP), axis=0, keepdims=True)
    sel2 = jnp.where(ridx == i1, -jnp.inf, sel)
    m2 = jnp.max(sel2, axis=0, keepdims=True)
    i2 = jnp.min(jnp.where(sel2 == m2, ridx, EXPERTS_PER_GROUP), axis=0, keepdims=True)
    e2 = jnp.exp(m2 - m1)
    w1 = p_g / (1.0 + e2)
    w2 = w1 * e2
    out_rows = lax.broadcasted_iota(jnp.int32, ids_ref.shape, 0)
    ids_ref[...] = jnp.where(out_rows == 0, gidx * EXPERTS_PER_GROUP + i1,
                             jnp.where(out_rows == 1, gidx * EXPERTS_PER_GROUP + i2, 0))
    wts_ref[...] = jnp.where(out_rows == 0, w1, jnp.where(out_rows == 1, w2, 0.0))


def _outproj_router(x2, attn, ret, w_out_bf, gt1, g2, sh2, sc2, wr, br):
    l, d = x2.shape
    tm = min(ROW_TILE, l)
    row = lambda w: pl.BlockSpec((tm, w), lambda i: (i, 0))
    col = pl.BlockSpec((SUBLANES, tm), lambda i: (0, i))
    return pl.pallas_call(
        _outproj_kernel,
        grid=(l // tm,),
        in_specs=[row(d),
                  pl.BlockSpec((ATTN_KV_HEADS, HEAD_DIM, GQA_GROUP * tm), lambda i: (0, 0, i)),
                  row(ret.shape[1]), _const_spec(w_out_bf.shape),
                  _const_spec((1, d)), _const_spec((1, d)), _const_spec((1, d)), _const_spec((1, d)),
                  _const_spec(wr.shape), _const_spec(br.shape)],
        out_specs=[row(d), pl.BlockSpec((tm * SUBLANES, d // SUBLANES), lambda i: (i, 0)), col, col],
        out_shape=[jax.ShapeDtypeStruct((l, d), F32), jax.ShapeDtypeStruct((l * SUBLANES, d // SUBLANES), F32),
                   jax.ShapeDtypeStruct((SUBLANES, l), jnp.int32),
                   jax.ShapeDtypeStruct((SUBLANES, l), F32)],
        compiler_params=_cparams("arbitrary"),
        name="outproj_router",
    )(x2, attn, ret, w_out_bf, gt1, g2, sh2, sc2, wr, br)


def _row_copy(src_hbm, src_row, dst_ref, sem):
    return pltpu.make_async_copy(src_hbm.at[pl.ds(src_row * SUBLANES, SUBLANES), :], dst_ref, sem)


def _moe_kernel(be_ref, na_ref, tok0_ref, tok1_ref, tokn_ref, h_hbm, wg_ref, wu_ref, wd_ref, y_ref,
                wg_s, wu_s, wd_s, xg_s, sem):
    b = pl.program_id(0)
    slot = lax.rem(b, MOE_SLOTS)
    nxt = lax.rem(b + 2, MOE_SLOTS)
    n_active = na_ref[0]
    active = b < n_active
    last = be_ref.shape[0] - 1
    changed = jnp.logical_or(b == 0, be_ref[jnp.minimum(b, last)] != be_ref[jnp.clip(b - 1, 0, last)])

    def wait_rows():
        rows = pl.ds(0, MOE_BLOCK * SUBLANES)
        pltpu.make_async_copy(h_hbm.at[rows, :], xg_s.at[slot, rows, :], sem.at[slot]).wait()

    @pl.when(b == 0)
    def _():
        def body(r, carry):
            _row_copy(h_hbm, tok0_ref[0, 0, r], xg_s.at[0, pl.ds(r * SUBLANES, SUBLANES), :], sem.at[0]).start()
            _row_copy(h_hbm, tok1_ref[0, 0, r], xg_s.at[1, pl.ds(r * SUBLANES, SUBLANES), :], sem.at[1]).start()
            return carry

        lax.fori_loop(0, MOE_BLOCK, body, 0, unroll=8)

    @pl.when(jnp.logical_and(active, changed))
    def _():
        wg_s[...] = wg_ref[0].astype(BF16)
        wu_s[...] = wu_ref[0].astype(BF16)
        wd_s[...] = wd_ref[0].astype(BF16)

    @pl.when(active)
    def _():
        wait_rows()
        x = _load_row_tiles(xg_s, (slot,), MOE_BLOCK).astype(BF16)
        hg = jnp.dot(x, wg_s[...], preferred_element_type=F32)
        hu = jnp.dot(x, wu_s[...], preferred_element_type=F32)
        a = (hg / (1.0 + jnp.exp(-hg))) * hu
        _store_row_tiles(y_ref, (), jnp.dot(a.astype(BF16), wd_s[...], preferred_element_type=F32))
        for r in range(MOE_BLOCK):
            _row_copy(h_hbm, tokn_ref[0, 0, r], xg_s.at[nxt, pl.ds(r * SUBLANES, SUBLANES), :],
                      sem.at[nxt]).start(priority=r % 2)

    @pl.when(jnp.logical_or(b == n_active, b == n_active + 1))
    def _():
        wait_rows()

    @pl.when(jnp.logical_not(active))
    def _():
        y_ref[...] = jnp.zeros(y_ref.shape, y_ref.dtype)


def _moe_blocks(blk_e, n_active, buf_tok, h2, w_gate, w_up, w_down):
    ne, d, de = w_gate.shape
    tiled = (MOE_BLOCK * SUBLANES, d // SUBLANES)
    nblk = buf_tok.shape[0]
    blk = lambda b: jnp.minimum(b, nblk - 1)
    wmap = lambda b, be, na: (be[blk(b)], 0, 0)
    tokmap = lambda off: (lambda b, be, na: (blk(b + off), 0, 0))
    smem_tok = lambda off: pl.BlockSpec((1, 1, MOE_BLOCK), tokmap(off), memory_space=pltpu.SMEM)
    return pl.pallas_call(
        _moe_kernel,
        grid_spec=pltpu.PrefetchScalarGridSpec(
            num_scalar_prefetch=2, grid=(nblk + 1,),
            in_specs=[pl.BlockSpec((1, 1, MOE_BLOCK), lambda b, be, na: (0, 0, 0), memory_space=pltpu.SMEM),
                      pl.BlockSpec((1, 1, MOE_BLOCK), lambda b, be, na: (1, 0, 0), memory_space=pltpu.SMEM),
                      smem_tok(2),
                      pl.BlockSpec(memory_space=pl.ANY),
                      pl.BlockSpec((1, d, de), wmap), pl.BlockSpec((1, d, de), wmap),
                      pl.BlockSpec((1, de, d), wmap)],
            out_specs=pl.BlockSpec(tiled, lambda b, be, na: (blk(b), 0)),
            scratch_shapes=[pltpu.VMEM((d, de), BF16), pltpu.VMEM((d, de), BF16),
                            pltpu.VMEM((de, d), BF16), pltpu.VMEM((MOE_SLOTS,) + tiled, F32),
                            pltpu.SemaphoreType.DMA((MOE_SLOTS,))]),
        out_shape=jax.ShapeDtypeStruct((nblk * tiled[0], tiled[1]), F32),
        compiler_params=_cparams("arbitrary"),
        name="moe_experts",
    )(blk_e, n_active, buf_tok, buf_tok, buf_tok, h2, w_gate, w_up, w_down)


def _final_kernel(d0_ref, dn_ref, x_ref, w_ref, gt_ref, g_ref, y_hbm, o_ref, yg_s, sem):
    i = pl.program_id(0)
    slot = i % 2
    tm = x_ref.shape[0]

    def start_tile(d_ref, slot_):
        def body(r, carry):
            for k in range(TOP_K):
                _row_copy(y_hbm, d_ref[0, k, r], yg_s.at[slot_, k, pl.ds(r * SUBLANES, SUBLANES), :],
                          sem.at[slot_]).start(priority=k)
            return carry

        lax.fori_loop(0, tm, body, 0, unroll=8)

    @pl.when(i == 0)
    def _():
        start_tile(d0_ref, 0)

    @pl.when(i + 1 < pl.num_programs(0))
    def _():
        start_tile(dn_ref, 1 - slot)

    for k in range(TOP_K):
        pltpu.make_async_copy(y_hbm.at[pl.ds(0, tm * SUBLANES), :], yg_s.at[slot, k], sem.at[slot]).wait()
    w = w_ref[...]
    y = _load_row_tiles(yg_s, (slot, 0), tm) * w[:, 0:1] + _load_row_tiles(yg_s, (slot, 1), tm) * w[:, 1:2]
    x = x_ref[...] + gt_ref[...] * y
    ms = jnp.mean(x * x, axis=-1, keepdims=True)
    o_ref[...] = (x * lax.rsqrt(ms + EPS)) * g_ref[...]


def _final(x_new, yb, dest, wts, gt2, gfin):
    l, d = x_new.shape
    n, _, tm = dest.shape
    row = lambda w: pl.BlockSpec((tm, w), lambda i: (i, 0))
    return pl.pallas_call(
        _final_kernel,
        grid=(n,),
        in_specs=[pl.BlockSpec((1, TOP_K, tm), lambda i: (0, 0, 0), memory_space=pltpu.SMEM),
                  pl.BlockSpec((1, TOP_K, tm), lambda i: (jnp.minimum(i + 1, n - 1), 0, 0),
                               memory_space=pltpu.SMEM),
                  row(d), row(wts.shape[1]), _const_spec((1, d)), _const_spec((1, d)),
                  pl.BlockSpec(memory_space=pl.ANY)],
        out_specs=row(d),
        out_shape=jax.ShapeDtypeStruct((l, d), F32),
        scratch_shapes=[pltpu.VMEM((2, TOP_K, tm * SUBLANES, d // SUBLANES), F32), pltpu.SemaphoreType.DMA((2,))],
        compiler_params=_cparams("arbitrary"),
        name="combine_final_norm",
    )(dest, dest, x_new, wts, gt2, gfin, yb)


def _rope_tables(rows_count):
    row = jnp.repeat(jnp.arange(rows_count), GRID_W).astype(F32)
    col = jnp.tile(jnp.arange(GRID_W), rows_count).astype(F32)
    freqs = ROPE_THETA ** (-jnp.arange(0, ROPE_AXIS_DIM, 2, dtype=F32) / ROPE_AXIS_DIM)
    ang = jnp.concatenate([row[:, None] * freqs, col[:, None] * freqs], axis=-1)
    ang = jnp.repeat(ang, 2, axis=-1)
    sign = jnp.where(jnp.arange(HEAD_DIM) % 2 == 0, -1.0, 1.0).astype(F32)
    cos = jnp.tile(jnp.cos(ang), (1, LANES // HEAD_DIM))
    sin = jnp.tile(jnp.sin(ang) * sign, (1, LANES // HEAD_DIM))
    return cos, sin


def _dispatch(ids, n_tokens):
    a = TOP_K * n_tokens
    eid_f = ids[:TOP_K].reshape(a)
    tok_f = jnp.tile(jnp.arange(n_tokens, dtype=jnp.int32), TOP_K)
    onehot = (eid_f[:, None] == jnp.arange(N_EXPERTS, dtype=jnp.int32)[None, :]).astype(jnp.int32)
    incl = jnp.cumsum(onehot, axis=0)
    counts = incl[-1]
    rank = jnp.sum((incl - onehot) * onehot, axis=1)
    pcounts = (counts + MOE_BLOCK - 1) // MOE_BLOCK * MOE_BLOCK
    pends = jnp.cumsum(pcounts)
    pstarts = pends - pcounts
    dest = jnp.sum(onehot * pstarts[None, :], axis=1) + rank
    p = a + N_EXPERTS * MOE_BLOCK
    nblk = p // MOE_BLOCK
    n_fill = p - a
    fill_ends = jnp.cumsum(pcounts - counts)
    fill_idx = jnp.arange(n_fill, dtype=jnp.int32)
    fill_e = jnp.sum(fill_idx[:, None] >= fill_ends[None, :], axis=1).astype(jnp.int32)
    idx_bits = max(a, n_fill).bit_length()
    real_keys = (eid_f << (idx_bits + 1)) | jnp.arange(a, dtype=jnp.int32)
    fill_keys = (fill_e << (idx_bits + 1)) | (1 << idx_bits) | fill_idx
    keys = jnp.sort(jnp.concatenate([real_keys, fill_keys]))
    is_real = ((keys >> idx_bits) & 1) == 0
    buf_tok = jnp.where(is_real, (keys & ((1 << idx_bits) - 1)) % n_tokens, 0)
    del tok_f
    n_active = pends[-1] // MOE_BLOCK
    blk_start = jnp.arange(nblk, dtype=jnp.int32) * MOE_BLOCK
    blk_e = jnp.minimum(jnp.sum(blk_start[:, None] >= pends[None, :], axis=1), N_EXPERTS - 1)
    blk_e = jnp.where(jnp.arange(nblk) < n_active, blk_e, blk_e[n_active - 1]).astype(jnp.int32)
    return buf_tok, dest.reshape(TOP_K, n_tokens), blk_e, n_active.astype(jnp.int32).reshape(1)


def _split_bf16(w):
    hi = w.astype(BF16)
    return jnp.stack([hi, (w - hi.astype(F32)).astype(BF16)])


def kernel(x, c, ctx, c_ctx, w_ada, b_ada, norm1_g, w_in, attn_q_norm, attn_k_norm, ret_decay_fwd,
           ret_decay_bwd, ret_gn_g, ret_gn_b, w_out, norm2_g, moe_w_grp, moe_b_grp, moe_w_exp, moe_b_exp,
           moe_w_gate, moe_w_up, moe_w_down, final_norm_g):
    b, l, d = x.shape
    cl = ctx.shape[1]
    assert b == 1 and w_ada.shape[0] == 1, "single batch element, single layer"
    x2, ctx2 = x[0], ctx[0]

    cvecs = jnp.zeros((SUBLANES, d), F32).at[0].set(c[0]).at[1].set(c_ctx)
    mod = _ada(cvecs, w_ada[0], b_ada[0])
    sh1, sc1, gt1, sh2, sc2, gt2 = [mod[0:1, i * d:(i + 1) * d] for i in range(6)]
    csh1, csc1 = mod[1:2, 0:d], mod[1:2, d:2 * d]

    w_in_bf = w_in[0].astype(BF16)
    head_of = jnp.arange(ATTN_WIDTH) // HEAD_DIM
    bd = jnp.where(head_of[:, None] == head_of[None, :], 1.0 / HEAD_DIM, 0.0).astype(BF16)
    gq = jnp.tile(attn_q_norm[0], ATTN_HEADS)[None, :]
    gk = jnp.tile(attn_k_norm[0], ATTN_KV_HEADS)[None, :]
    g1 = norm1_g[0][None, :]
    cos, sin = _rope_tables(l // GRID_W)
    bound = (HEAD_DIM ** 0.5 * LOG2E * 1.01) * jnp.max(jnp.abs(attn_q_norm[0])) * jnp.max(jnp.abs(attn_k_norm[0]))
    bound = bound.astype(BF16).astype(F32)
    neg_bound = jnp.full((1, LANES), -1.0, F32) * bound
    q_t, ka, va, qr, kr, vr, gr = _inproj(x2, sh1, sc1, g1, w_in_bf, bd, gq, gk, cos, sin, neg_bound)
    _, ka_c, va_c, _, kr_c, vr_c, _ = _inproj(ctx2, csh1, csc1, g1, w_in_bf, bd, gq, gk,
                                              jnp.ones((cl, LANES), F32), jnp.zeros((cl, LANES), F32), neg_bound)

    nk = cl + l
    tk = _key_chunk(nk)
    k_all = jnp.concatenate([ka_c, ka], axis=0)
    v_all = jnp.concatenate([va_c, va], axis=0).reshape(nk // tk, tk, ATTN_KV_HEADS, HEAD_DIM)
    attn_args = (q_t, k_all, v_all.transpose(2, 0, 3, 1))
    o_t = lax.cond(bound <= MAX_UNSHIFTED_BOUND,
                   functools.partial(_attention, running_max=False),
                   functools.partial(_attention, running_max=True), *attn_args)

    log_gf = jax.nn.log_sigmoid(ret_decay_fwd[0].astype(F32))
    log_gb = jax.nn.log_sigmoid(ret_decay_bwd[0].astype(F32))
    tab_f, gc_f = _retention_tables(log_gf, False)
    tab_b, gc_b = _retention_tables(log_gb, True)
    o_f = _retention_pass(gc_f, qr, kr, vr, kr_c, vr_c, tab_f, None, reverse=False)
    ret = _retention_pass(gc_b, qr, kr, vr, kr_c, vr_c, tab_b,
                          (o_f, gr, ret_gn_g[0][None, :], ret_gn_b[0][None, :]), reverse=True)

    wr = jnp.zeros((ROUTER_ROWS, d), F32)
    wr = wr.at[:N_GROUPS].set(moe_w_grp[0].T).at[EXPERT_ROW0:EXPERT_ROW0 + N_EXPERTS].set(moe_w_exp[0].T)
    br = jnp.zeros((ROUTER_ROWS, 1), F32)
    br = br.at[:N_GROUPS, 0].set(moe_b_grp[0]).at[EXPERT_ROW0:EXPERT_ROW0 + N_EXPERTS, 0].set(moe_b_exp[0])
    x_new, h2, ids, wts = _outproj_router(x2, o_t, ret, w_out[0].astype(BF16), gt1, norm2_g[0][None, :],
                                          sh2, sc2, _split_bf16(wr), br)

    buf_tok, dest, blk_e, n_active = _dispatch(ids, l)
    yb = _moe_blocks(blk_e, n_active, buf_tok.reshape(-1, 1, MOE_BLOCK), h2, moe_w_gate[0], moe_w_up[0],
                     moe_w_down[0])
    tf = min(FINAL_TILE, l)
    dest_t = dest.reshape(TOP_K, l // tf, tf).transpose(1, 0, 2)
    out = _final(x_new, yb, dest_t, wts[:TOP_K].T, gt2, final_norm_g[None, :])
    return out[None]
```

```python
import functools

import jax
import jax.numpy as jnp
import numpy as np
from jax import lax
from jax.experimental import pallas as pl
from jax.experimental.pallas import tpu as pltpu

F32 = jnp.float32
BF16 = jnp.bfloat16

GRID_W = 64
HEAD_DIM = 64
ATTN_HEADS = 8
ATTN_KV_HEADS = 2
GQA_GROUP = ATTN_HEADS // ATTN_KV_HEADS
RET_HEADS = 4
RET_QK_DIM = 64
RET_V_DIM = 128
RET_CHUNK = 256
ATTN_WIDTH = ATTN_HEADS * HEAD_DIM
KV_WIDTH = ATTN_KV_HEADS * HEAD_DIM
RET_QK_WIDTH = RET_HEADS * RET_QK_DIM
RET_WIDTH = RET_HEADS * RET_V_DIM
PROJ_SIZES = (ATTN_WIDTH, KV_WIDTH, KV_WIDTH, RET_QK_WIDTH, RET_QK_WIDTH, RET_WIDTH, RET_WIDTH)
PROJ_OFFS = tuple(int(v) for v in np.cumsum((0,) + PROJ_SIZES))
PROJ_DIM = PROJ_OFFS[-1]
ROPE_THETA = 10000.0
ROPE_AXIS_DIM = HEAD_DIM // 2
N_GROUPS = 4
EXPERTS_PER_GROUP = 8
N_EXPERTS = N_GROUPS * EXPERTS_PER_GROUP
TOP_K = 2
D_EXPERT = 512
EPS = 1e-6
LOG2E = 1.4426950408889634
ATTN_AUG_DIM = 128
MAX_UNSHIFTED_BOUND = 60.0

LANES = 128
SUBLANES = 8
VMEM_LIMIT_BYTES = 56 * 1024 * 1024

ROW_TILE = 512
INPROJ_SUBTILE = 256
ATTN_Q_TILE = 512
RET_ROW_TILE = 1024
RET_UNROLL = 4
MOE_BLOCK = 256
MOE_SLOTS = 3
FINAL_TILE = 512
ROUTER_ROWS = 128
EXPERT_ROW0 = 8


def _cparams(*sem):
    return pltpu.CompilerParams(dimension_semantics=sem, vmem_limit_bytes=VMEM_LIMIT_BYTES)


def _const_spec(shape):
    nd = len(shape)
    return pl.BlockSpec(shape, lambda *_: (0,) * nd)


def _store_row_tiles(ref, idx, rows):
    n, w = rows.shape[0], rows.shape[1] // SUBLANES
    for s in range(SUBLANES):
        ref[idx + (pl.ds(s, n, stride=SUBLANES), slice(None))] = rows[:, s * w:(s + 1) * w]


def _load_row_tiles(ref, idx, n):
    return jnp.concatenate([ref[idx + (pl.ds(s, n, stride=SUBLANES), slice(None))] for s in range(SUBLANES)],
                           axis=1)


def _ada_kernel(s_ref, w_ref, b_ref, o_ref):
    s = s_ref[...]
    s = s / (1.0 + jnp.exp(-s))
    o_ref[...] = jnp.dot(s, w_ref[...], preferred_element_type=F32,
                         precision=lax.Precision.HIGHEST) + b_ref[...]


def _ada(cvecs, w_ada, b_ada):
    d, n = w_ada.shape
    tn = 1536
    return pl.pallas_call(
        _ada_kernel,
        grid=(n // tn,),
        in_specs=[_const_spec((SUBLANES, d)),
                  pl.BlockSpec((d, tn), lambda j: (0, j)),
                  pl.BlockSpec((1, tn), lambda j: (0, j))],
        out_specs=pl.BlockSpec((SUBLANES, tn), lambda j: (0, j)),
        out_shape=jax.ShapeDtypeStruct((SUBLANES, n), F32),
        compiler_params=_cparams("arbitrary"),
        name="ada_mod",
    )(cvecs, w_ada, b_ada.reshape(1, n))


def _rms_modulate(x, g, sh, sc):
    ms = jnp.mean(x * x, axis=-1, keepdims=True)
    return (x * lax.rsqrt(ms + EPS)) * g * (1.0 + sc) + sh


def _head_mean_sq(v, bd):
    sq = v * v
    hi = sq.astype(BF16)
    lo = (sq - hi.astype(F32)).astype(BF16)
    return (jnp.dot(hi, bd, preferred_element_type=F32) + jnp.dot(lo, bd, preferred_element_type=F32))


def _rope_chunks(v, cos, sin, even):
    outs = []
    for c in range(v.shape[1] // LANES):
        xc = v[:, c * LANES:(c + 1) * LANES]
        nxt = pltpu.roll(xc, LANES - 1, 1)
        prv = pltpu.roll(xc, 1, 1)
        outs.append(xc * cos + jnp.where(even, nxt, prv) * sin)
    return outs


def _inproj_kernel(x_ref, sh_ref, sc_ref, g_ref, w_ref, bd_ref, gq_ref, gk_ref, cos_ref, sin_ref, nb_ref,
                   qt_ref, ka_ref, va_ref, qr_ref, kr_ref, vr_ref, gr_ref):
    tm = x_ref.shape[0]
    sub = min(INPROJ_SUBTILE, tm)
    bd = bd_ref[...]
    scale = HEAD_DIM ** -0.5 * LOG2E
    rscale = RET_QK_DIM ** -0.5
    o = PROJ_OFFS
    per = LANES // HEAD_DIM
    for j0 in range(0, tm, sub):
        rows = slice(j0, j0 + sub)
        h = _rms_modulate(x_ref[rows, :], g_ref[...], sh_ref[...], sc_ref[...])
        proj = jnp.dot(h.astype(BF16), w_ref[...], preferred_element_type=F32)
        qa, ka, va = proj[:, o[0]:o[1]], proj[:, o[1]:o[2]], proj[:, o[2]:o[3]]
        qr, kr, vr, gr = proj[:, o[3]:o[4]], proj[:, o[4]:o[5]], proj[:, o[5]:o[6]], proj[:, o[6]:o[7]]
        cos, sin = cos_ref[rows, :], sin_ref[rows, :]
        even = (lax.broadcasted_iota(jnp.int32, cos.shape, 1) % 2) == 0

        qn = qa * lax.rsqrt(_head_mean_sq(qa, bd) + EPS) * gq_ref[...]
        for c, blk in enumerate(_rope_chunks(qn, cos, sin, even)):
            blk_t = (blk * scale).T
            for j in range(per):
                kvh, grp = divmod(c * per + j, GQA_GROUP)
                qt_ref[kvh, 0:HEAD_DIM, grp * tm + j0:grp * tm + j0 + sub] = (
                    blk_t[j * HEAD_DIM:(j + 1) * HEAD_DIM, :].astype(BF16))
        kn = ka * lax.rsqrt(_head_mean_sq(ka, bd[:KV_WIDTH, :KV_WIDTH]) + EPS) * gk_ref[...]
        k_rot = _rope_chunks(kn, cos, sin, even)[0].astype(BF16)
        pad_shape = (sub, ATTN_AUG_DIM - HEAD_DIM)
        k_pad = jnp.where(lax.broadcasted_iota(jnp.int32, pad_shape, 1) == 0, 1.0, 0.0).astype(BF16)
        for kvh in range(ATTN_KV_HEADS):
            c0 = kvh * ATTN_AUG_DIM
            ka_ref[rows, c0:c0 + HEAD_DIM] = k_rot[:, kvh * HEAD_DIM:(kvh + 1) * HEAD_DIM]
            ka_ref[rows, c0 + HEAD_DIM:c0 + ATTN_AUG_DIM] = k_pad
        va_ref[:, rows] = va.T.astype(BF16)
        for c, blk in enumerate(_rope_chunks(qr, cos, sin, even)):
            qr_ref[rows, c * LANES:(c + 1) * LANES] = blk.astype(BF16)
        for c, blk in enumerate(_rope_chunks(kr, cos, sin, even)):
            kr_ref[rows, c * LANES:(c + 1) * LANES] = (blk * rscale).astype(BF16)
        vr_ref[rows, :] = vr.astype(BF16)
        gr_ref[rows, :] = gr
    aug_shape = (ATTN_AUG_DIM - HEAD_DIM, GQA_GROUP * tm)
    aug_row = lax.broadcasted_iota(jnp.int32, aug_shape, 0)
    aug = jnp.where(aug_row == 0, nb_ref[0:1, 0:1], 0.0).astype(BF16)
    for kvh in range(ATTN_KV_HEADS):
        qt_ref[kvh, HEAD_DIM:ATTN_AUG_DIM, :] = aug


def _inproj(x2, sh, sc, g1, w_bf, bd, gq, gk, cos, sin, neg_bound):
    rows, d = x2.shape
    tm = min(ROW_TILE, rows)
    row = lambda w: pl.BlockSpec((tm, w), lambda i: (i, 0))
    row_specs = [row(ATTN_KV_HEADS * ATTN_AUG_DIM), pl.BlockSpec((KV_WIDTH, tm), lambda i: (0, i))]
    row_specs += [row(w) for w in PROJ_SIZES[3:]]
    row_shapes = [(rows, ATTN_KV_HEADS * ATTN_AUG_DIM), (KV_WIDTH, rows)] + [(rows, w) for w in PROJ_SIZES[3:]]
    dtypes = (BF16,) * 5 + (F32,)
    qt_shape = (ATTN_KV_HEADS, ATTN_AUG_DIM, GQA_GROUP * rows)
    return pl.pallas_call(
        _inproj_kernel,
        grid=(rows // tm,),
        in_specs=[row(d), _const_spec((1, d)), _const_spec((1, d)), _const_spec((1, d)),
                  _const_spec(w_bf.shape), _const_spec(bd.shape),
                  _const_spec((1, ATTN_WIDTH)), _const_spec((1, KV_WIDTH)),
                  row(LANES), row(LANES), _const_spec((1, LANES))],
        out_specs=[pl.BlockSpec((ATTN_KV_HEADS, ATTN_AUG_DIM, GQA_GROUP * tm), lambda i: (0, 0, i))] + row_specs,
        out_shape=[jax.ShapeDtypeStruct(qt_shape, BF16)]
        + [jax.ShapeDtypeStruct(shp, dt) for shp, dt in zip(row_shapes, dtypes)],
        compiler_params=_cparams("arbitrary"),
        name="norm_inproj",
    )(x2, sh, sc, g1, w_bf, bd, gq, gk, cos, sin, neg_bound)


def _attn_kernel(q_ref, k_ref, v_ref, o_ref, m_ref, l_ref, acc_ref, *, tk, running_max):
    m_ref[...] = jnp.full(m_ref.shape, -jnp.inf, F32)
    l_ref[...] = jnp.zeros(l_ref.shape, F32)
    acc_ref[...] = jnp.zeros(acc_ref.shape, F32)

    n_chunks = k_ref.shape[0] // tk

    def body(c, carry):
        keys = pl.ds(pl.multiple_of(c * tk, tk), tk)
        kc = k_ref[keys, :]
        vc = v_ref[:, keys]
        s = jnp.dot(kc, q_ref[0], preferred_element_type=F32)
        if running_max:
            m_old = m_ref[...]
            m_new = jnp.maximum(m_old, jnp.max(s, axis=0, keepdims=True))
            alpha = jnp.exp2(m_old - m_new)
            p = jnp.exp2(s - m_new)
            l_ref[...] = alpha * l_ref[...] + jnp.sum(p, axis=0, keepdims=True)
            acc_ref[...] = acc_ref[...] * alpha + jnp.dot(vc, p.astype(BF16), preferred_element_type=F32)
            m_ref[...] = m_new
        else:
            p = jnp.exp2(s)
            l_ref[...] += jnp.sum(p, axis=0, keepdims=True)
            acc_ref[...] += jnp.dot(vc, p.astype(BF16), preferred_element_type=F32)
        return carry

    lax.fori_loop(0, n_chunks, body, 0)
    o_ref[0] = (acc_ref[...] / l_ref[...]).astype(o_ref.dtype)


def _key_chunk(nk):
    for tk in (3328, 1280, 640, 256, 128):
        if nk % tk == 0:
            return tk
    raise ValueError(f"unsupported key count {nk}")


def _attention(q_t, k_c, v_t, *, running_max):
    kv, da, gl = q_t.shape
    nk = k_c.shape[0]
    d = v_t.shape[0] // kv
    nq = GQA_GROUP * min(ATTN_Q_TILE, gl // GQA_GROUP)
    return pl.pallas_call(
        functools.partial(_attn_kernel, tk=_key_chunk(nk), running_max=running_max),
        grid=(kv, gl // nq),
        in_specs=[pl.BlockSpec((1, da, nq), lambda b, i: (b, 0, i)),
                  pl.BlockSpec((nk, da), lambda b, i: (0, b)),
                  pl.BlockSpec((d, nk), lambda b, i: (b, 0))],
        out_specs=pl.BlockSpec((1, d, nq), lambda b, i: (b, 0, i)),
        out_shape=jax.ShapeDtypeStruct((kv, d, gl), BF16),
        scratch_shapes=[pltpu.VMEM((1, nq), F32), pltpu.VMEM((1, nq), F32), pltpu.VMEM((d, nq), F32)],
        compiler_params=_cparams("arbitrary", "arbitrary"),
        name="flash_attn",
    )(q_t, k_c, v_t)


def _ret_kernel(gc_ref, q_ref, k_ref, v_ref, kc_ref, vc_ref, tab_ref, *rest, reverse, final):
    if final:
        of_ref, gate_ref, gng_ref, gnb_ref, out_ref, s_ref = rest
    else:
        out_ref, s_ref = rest
    c_rows = RET_CHUNK
    n_local = q_ref.shape[0] // c_rows
    n_ctx = kc_ref.shape[0] // c_rows
    lane_head = lax.broadcasted_iota(jnp.int32, (c_rows, RET_QK_WIDTH), 1) // RET_QK_DIM

    def state_update(kch, vch):
        for h in range(RET_HEADS):
            vz = (vch[:, h * RET_V_DIM:(h + 1) * RET_V_DIM].astype(F32)
                  * tab_ref[1, h, :, 0:RET_V_DIM]).astype(BF16)
            u = lax.dot_general(kch, vz, (((0,), (0,)), ((), ())), preferred_element_type=F32)
            s_ref[h] = gc_ref[h] * s_ref[h] + u

    @pl.when(pl.program_id(0) == 0)
    def _():
        s_ref[...] = jnp.zeros(s_ref.shape, F32)
        order = range(n_ctx - 1, -1, -1) if reverse else range(n_ctx)
        for cc in order:
            state_update(kc_ref[cc * c_rows:(cc + 1) * c_rows, :], vc_ref[cc * c_rows:(cc + 1) * c_rows, :])

    def chunk(ci, carry):
        c = (n_local - 1 - ci) if reverse else ci
        rows = pl.ds(pl.multiple_of(c * c_rows, c_rows), c_rows)
        q, k, v = q_ref[rows, :], k_ref[rows, :], v_ref[rows, :]
        for h in range(RET_HEADS):
            cols = slice(h * RET_V_DIM, (h + 1) * RET_V_DIM)
            qm = jnp.where(lane_head == h, q, jnp.zeros_like(q))
            sc = lax.dot_general(qm, k, (((1,), (1,)), ((), ())), preferred_element_type=F32)
            sc = sc * tab_ref[0, h]
            o = jnp.dot(sc.astype(BF16), v[:, cols], preferred_element_type=F32)
            o = o + (jnp.dot(qm, s_ref[h].astype(BF16), preferred_element_type=F32)
                     * tab_ref[2, h, :, 0:RET_V_DIM])
            if final:
                o = o + of_ref[rows, cols]
                mu = jnp.mean(o, axis=-1, keepdims=True)
                oc = o - mu
                var = jnp.mean(oc * oc, axis=-1, keepdims=True)
                on = oc * lax.rsqrt(var + EPS)
                gate = gate_ref[rows, cols]
                y = (on * gng_ref[:, cols] + gnb_ref[:, cols]) * (gate / (1.0 + jnp.exp(-gate)))
                out_ref[rows, cols] = y.astype(out_ref.dtype)
            else:
                out_ref[rows, cols] = o
        state_update(k, v)
        return carry

    lax.fori_loop(0, n_local, chunk, 0, unroll=RET_UNROLL)


def _retention_pass(gc, qr, kr, vr, kr_c, vr_c, tabs, extra, *, reverse):
    l = qr.shape[0]
    tr = min(RET_ROW_TILE, l)
    nt = l // tr
    final = extra is not None
    idx = (lambda j, gc_: (nt - 1 - j, 0)) if reverse else (lambda j, gc_: (j, 0))
    const = lambda shape: pl.BlockSpec(shape, lambda j, gc_: (0,) * len(shape))
    in_specs = [pl.BlockSpec((tr, RET_QK_WIDTH), idx), pl.BlockSpec((tr, RET_QK_WIDTH), idx),
                pl.BlockSpec((tr, RET_WIDTH), idx),
                const(kr_c.shape), const(vr_c.shape), const(tabs.shape)]
    args = [qr, kr, vr, kr_c, vr_c, tabs]
    if final:
        of, gate, gng, gnb = extra
        in_specs += [pl.BlockSpec((tr, RET_WIDTH), idx), pl.BlockSpec((tr, RET_WIDTH), idx),
                     const(gng.shape), const(gnb.shape)]
        args += [of, gate, gng, gnb]
    return pl.pallas_call(
        functools.partial(_ret_kernel, reverse=reverse, final=final),
        grid_spec=pltpu.PrefetchScalarGridSpec(
            num_scalar_prefetch=1, grid=(nt,), in_specs=in_specs,
            out_specs=pl.BlockSpec((tr, RET_WIDTH), idx),
            scratch_shapes=[pltpu.VMEM((RET_HEADS, RET_QK_WIDTH, RET_V_DIM), F32)]),
        out_shape=jax.ShapeDtypeStruct((l, RET_WIDTH), BF16 if final else F32),
        compiler_params=_cparams("arbitrary"),
        name="retention_bwd" if reverse else "retention_fwd",
    )(gc, *args)


def _retention_tables(log_g, reverse):
    c = RET_CHUNK
    pos = jnp.arange(c, dtype=F32)
    diff = (pos[None, :] - pos[:, None]) if reverse else (pos[:, None] - pos[None, :])
    mask = (diff > 0) if reverse else (diff >= 0)
    lg = log_g[:, None, None]
    dmat = jnp.where(mask, jnp.exp(lg * jnp.where(mask, diff, 0.0)[None]), 0.0)
    zeta = jnp.exp(log_g[:, None] * (pos if reverse else (c - 1 - pos))[None, :])
    xi = jnp.exp(log_g[:, None] * ((c - pos) if reverse else (pos + 1.0))[None, :])
    ones = jnp.ones((1, 1, c), F32)
    tabs = jnp.stack([dmat, zeta[:, :, None] * ones, xi[:, :, None] * ones])
    return tabs, jnp.exp(log_g * c)


def _outproj_kernel(x_ref, a_ref, r_ref, w_ref, gt_ref, g2_ref, sh_ref, sc_ref, wr_ref, br_ref,
                    xn_ref, h2_ref, ids_ref, wts_ref):
    tm = x_ref.shape[0]
    mix = jnp.dot(r_ref[...], w_ref[ATTN_WIDTH:, :], preferred_element_type=F32)
    per = LANES // HEAD_DIM
    for pair in range(ATTN_HEADS // per):
        pieces = []
        for j in range(per):
            kvh, grp = divmod(pair * per + j, GQA_GROUP)
            pieces.append(a_ref[kvh, :, grp * tm:(grp + 1) * tm])
        o_t = jnp.concatenate(pieces, axis=0)
        mix = mix + lax.dot_general(o_t, w_ref[pair * LANES:(pair + 1) * LANES, :],
                                    (((0,), (0,)), ((), ())), preferred_element_type=F32)
    x_new = x_ref[...] + gt_ref[...] * mix
    xn_ref[...] = x_new
    h2 = _rms_modulate(x_new, g2_ref[...], sh_ref[...], sc_ref[...])
    _store_row_tiles(h2_ref, (), h2)
    hi = h2.astype(BF16)
    lo = (h2 - hi.astype(F32)).astype(BF16)

    nt = (((1,), (1,)), ((), ()))
    logits = (lax.dot_general(wr_ref[0], hi, nt, preferred_element_type=F32)
              + lax.dot_general(wr_ref[0], lo, nt, preferred_element_type=F32)
              + lax.dot_general(wr_ref[1], hi, nt, preferred_element_type=F32)) + br_ref[...]
    rows = [logits[g:g + 1, :] for g in range(N_GROUPS)]
    gmax = functools.reduce(jnp.maximum, rows)
    gidx = jnp.full(gmax.shape, N_GROUPS - 1, jnp.int32)
    for g in range(N_GROUPS - 2, -1, -1):
        gidx = jnp.where(rows[g] == gmax, g, gidx)
    p_g = 1.0 / functools.reduce(jnp.add, [jnp.exp(r - gmax) for r in rows])
    sel = logits[EXPERT_ROW0 + (N_GROUPS - 1) * EXPERTS_PER_GROUP:EXPERT_ROW0 + N_GROUPS * EXPERTS_PER_GROUP, :]
    for g in range(N_GROUPS - 2, -1, -1):
        blk = logits[EXPERT_ROW0 + g * EXPERTS_PER_GROUP:EXPERT_ROW0 + (g + 1) * EXPERTS_PER_GROUP, :]
        sel = jnp.where(gidx == g, blk, sel)
    ridx = lax.broadcasted_iota(jnp.int32, sel.shape, 0)
    m1 = jnp.max(sel, axis=0, keepdims=True)
    i1 = jnp.min(jnp.where(sel == m1, ridx, EXPERTS_PER_GROUP), axis=0, keepdims=True)
    sel2 = jnp.where(ridx == i1, -jnp.inf, sel)
    m2 = jnp.max(sel2, axis=0, keepdims=True)
    i2 = jnp.min(jnp.where(sel2 == m2, ridx, EXPERTS_PER_GROUP), axis=0, keepdims=True)
    e2 = jnp.exp(m2 - m1)
    w1 = p_g / (1.0 + e2)
    w2 = w1 * e2
    out_rows = lax.broadcasted_iota(jnp.int32, ids_ref.shape, 0)
    ids_ref[...] = jnp.where(out_rows == 0, gidx * EXPERTS_PER_GROUP + i1,
                             jnp.where(out_rows == 1, gidx * EXPERTS_PER_GROUP + i2, 0))
    wts_ref[...] = jnp.where(out_rows == 0, w1, jnp.where(out_rows == 1, w2, 0.0))


def _outproj_router(x2, attn, ret, w_out_bf, gt1, g2, sh2, sc2, wr, br):
    l, d = x2.shape
    tm = min(ROW_TILE, l)
    row = lambda w: pl.BlockSpec((tm, w), lambda i: (i, 0))
    col = pl.BlockSpec((SUBLANES, tm), lambda i: (0, i))
    return pl.pallas_call(
        _outproj_kernel,
        grid=(l // tm,),
        in_specs=[row(d),
                  pl.BlockSpec((ATTN_KV_HEADS, HEAD_DIM, GQA_GROUP * tm), lambda i: (0, 0, i)),
                  row(ret.shape[1]), _const_spec(w_out_bf.shape),
                  _const_spec((1, d)), _const_spec((1, d)), _const_spec((1, d)), _const_spec((1, d)),
                  _const_spec(wr.shape), _const_spec(br.shape)],
        out_specs=[row(d), pl.BlockSpec((tm * SUBLANES, d // SUBLANES), lambda i: (i, 0)), col, col],
        out_shape=[jax.ShapeDtypeStruct((l, d), F32), jax.ShapeDtypeStruct((l * SUBLANES, d // SUBLANES), F32),
                   jax.ShapeDtypeStruct((SUBLANES, l), jnp.int32),
                   jax.ShapeDtypeStruct((SUBLANES, l), F32)],
        compiler_params=_cparams("arbitrary"),
        name="outproj_router",
    )(x2, attn, ret, w_out_bf, gt1, g2, sh2, sc2, wr, br)


def _row_copy(src_hbm, src_row, dst_ref, sem):
    return pltpu.make_async_copy(src_hbm.at[pl.ds(src_row * SUBLANES, SUBLANES), :], dst_ref, sem)


def _moe_kernel(be_ref, na_ref, tok0_ref, tok1_ref, tokn_ref, h_hbm, wg_ref, wu_ref, wd_ref, y_ref,
                wg_s, wu_s, wd_s, xg_s, sem):
    b = pl.program_id(0)
    slot = lax.rem(b, MOE_SLOTS)
    nxt = lax.rem(b + 2, MOE_SLOTS)
    n_active = na_ref[0]
    active = b < n_active
    last = be_ref.shape[0] - 1
    changed = jnp.logical_or(b == 0, be_ref[jnp.minimum(b, last)] != be_ref[jnp.clip(b - 1, 0, last)])

    def wait_rows():
        rows = pl.ds(0, MOE_BLOCK * SUBLANES)
        pltpu.make_async_copy(h_hbm.at[rows, :], xg_s.at[slot, rows, :], sem.at[slot]).wait()

    @pl.when(b == 0)
    def _():
        def body(r, carry):
            _row_copy(h_hbm, tok0_ref[0, 0, r], xg_s.at[0, pl.ds(r * SUBLANES, SUBLANES), :], sem.at[0]).start()
            _row_copy(h_hbm, tok1_ref[0, 0, r], xg_s.at[1, pl.ds(r * SUBLANES, SUBLANES), :], sem.at[1]).start()
            return carry

        lax.fori_loop(0, MOE_BLOCK, body, 0, unroll=8)

    @pl.when(jnp.logical_and(active, changed))
    def _():
        wg_s[...] = wg_ref[0].astype(BF16)
        wu_s[...] = wu_ref[0].astype(BF16)
        wd_s[...] = wd_ref[0].astype(BF16)

    @pl.when(active)
    def _():
        wait_rows()
        x = _load_row_tiles(xg_s, (slot,), MOE_BLOCK).astype(BF16)
        hg = jnp.dot(x, wg_s[...], preferred_element_type=F32)
        hu = jnp.dot(x, wu_s[...], preferred_element_type=F32)
        a = (hg / (1.0 + jnp.exp(-hg))) * hu
        _store_row_tiles(y_ref, (), jnp.dot(a.astype(BF16), wd_s[...], preferred_element_type=F32))
        for r in range(MOE_BLOCK):
            _row_copy(h_hbm, tokn_ref[0, 0, r], xg_s.at[nxt, pl.ds(r * SUBLANES, SUBLANES), :],
                      sem.at[nxt]).start(priority=r % 2)

    @pl.when(jnp.logical_or(b == n_active, b == n_active + 1))
    def _():
        wait_rows()

    @pl.when(jnp.logical_not(active))
    def _():
        y_ref[...] = jnp.zeros(y_ref.shape, y_ref.dtype)


def _moe_blocks(blk_e, n_active, buf_tok, h2, w_gate, w_up, w_down):
    ne, d, de = w_gate.shape
    tiled = (MOE_BLOCK * SUBLANES, d // SUBLANES)
    nblk = buf_tok.shape[0]
    blk = lambda b: jnp.minimum(b, nblk - 1)
    wmap = lambda b, be, na: (be[blk(b)], 0, 0)
    tokmap = lambda off: (lambda b, be, na: (blk(b + off), 0, 0))
    smem_tok = lambda off: pl.BlockSpec((1, 1, MOE_BLOCK), tokmap(off), memory_space=pltpu.SMEM)
    return pl.pallas_call(
        _moe_kernel,
        grid_spec=pltpu.PrefetchScalarGridSpec(
            num_scalar_prefetch=2, grid=(nblk + 1,),
            in_specs=[pl.BlockSpec((1, 1, MOE_BLOCK), lambda b, be, na: (0, 0, 0), memory_space=pltpu.SMEM),
                      pl.BlockSpec((1, 1, MOE_BLOCK), lambda b, be, na: (1, 0, 0), memory_space=pltpu.SMEM),
                      smem_tok(2),
                      pl.BlockSpec(memory_space=pl.ANY),
                      pl.BlockSpec((1, d, de), wmap), pl.BlockSpec((1, d, de), wmap),
                      pl.BlockSpec((1, de, d), wmap)],
            out_specs=pl.BlockSpec(tiled, lambda b, be, na: (blk(b), 0)),
            scratch_shapes=[pltpu.VMEM((d, de), BF16), pltpu.VMEM((d, de), BF16),
                            pltpu.VMEM((de, d), BF16), pltpu.VMEM((MOE_SLOTS,) + tiled, F32),
                            pltpu.SemaphoreType.DMA((MOE_SLOTS,))]),
        out_shape=jax.ShapeDtypeStruct((nblk * tiled[0], tiled[1]), F32),
        compiler_params=_cparams("arbitrary"),
        name="moe_experts",
    )(blk_e, n_active, buf_tok, buf_tok, buf_tok, h2, w_gate, w_up, w_down)


def _final_kernel(d0_ref, dn_ref, x_ref, w_ref, gt_ref, g_ref, y_hbm, o_ref, yg_s, sem):
    i = pl.program_id(0)
    slot = i % 2
    tm = x_ref.shape[0]

    def start_tile(d_ref, slot_):
        def body(r, carry):
            for k in range(TOP_K):
                _row_copy(y_hbm, d_ref[0, k, r], yg_s.at[slot_, k, pl.ds(r * SUBLANES, SUBLANES), :],
                          sem.at[slot_]).start(priority=k)
            return carry

        lax.fori_loop(0, tm, body, 0, unroll=8)

    @pl.when(i == 0)
    def _():
        start_tile(d0_ref, 0)

    @pl.when(i + 1 < pl.num_programs(0))
    def _():
        start_tile(dn_ref, 1 - slot)

    for k in range(TOP_K):
        pltpu.make_async_copy(y_hbm.at[pl.ds(0, tm * SUBLANES), :], yg_s.at[slot, k], sem.at[slot]).wait()
    w = w_ref[...]
    y = _load_row_tiles(yg_s, (slot, 0), tm) * w[:, 0:1] + _load_row_tiles(yg_s, (slot, 1), tm) * w[:, 1:2]
    x = x_ref[...] + gt_ref[...] * y
    ms = jnp.mean(x * x, axis=-1, keepdims=True)
    o_ref[...] = (x * lax.rsqrt(ms + EPS)) * g_ref[...]


def _final(x_new, yb, dest, wts, gt2, gfin):
    l, d = x_new.shape
    n, _, tm = dest.shape
    row = lambda w: pl.BlockSpec((tm, w), lambda i: (i, 0))
    return pl.pallas_call(
        _final_kernel,
        grid=(n,),
        in_specs=[pl.BlockSpec((1, TOP_K, tm), lambda i: (0, 0, 0), memory_space=pltpu.SMEM),
                  pl.BlockSpec((1, TOP_K, tm), lambda i: (jnp.minimum(i + 1, n - 1), 0, 0),
                               memory_space=pltpu.SMEM),
                  row(d), row(wts.shape[1]), _const_spec((1, d)), _const_spec((1, d)),
                  pl.BlockSpec(memory_space=pl.ANY)],
        out_specs=row(d),
        out_shape=jax.ShapeDtypeStruct((l, d), F32),
        scratch_shapes=[pltpu.VMEM((2, TOP_K, tm * SUBLANES, d // SUBLANES), F32), pltpu.SemaphoreType.DMA((2,))],
        compiler_params=_cparams("arbitrary"),
        name="combine_final_norm",
    )(dest, dest, x_new, wts, gt2, gfin, yb)


def _rope_tables(rows_count):
    shape = (rows_count * GRID_W, LANES)
    tok = lax.broadcasted_iota(jnp.int32, shape, 0)
    lane = lax.broadcasted_iota(jnp.int32, shape, 1) % HEAD_DIM
    pair = lane // 2
    n_freq = ROPE_AXIS_DIM // 2
    by_row = pair < n_freq
    pos = jnp.where(by_row, tok // GRID_W, tok % GRID_W).astype(F32)
    fidx = jnp.where(by_row, pair, pair - n_freq).astype(F32)
    ang = pos * ROPE_THETA ** (-(2.0 * fidx) / ROPE_AXIS_DIM)
    sign = jnp.where(lane % 2 == 0, -1.0, 1.0).astype(F32)
    return jnp.cos(ang), jnp.sin(ang) * sign


def _dispatch(ids, n_tokens):
    a = TOP_K * n_tokens
    eid_f = ids[:TOP_K].reshape(a)
    tok_f = jnp.tile(jnp.arange(n_tokens, dtype=jnp.int32), TOP_K)
    onehot = (eid_f[:, None] == jnp.arange(N_EXPERTS, dtype=jnp.int32)[None, :]).astype(jnp.int32)
    incl = jnp.cumsum(onehot, axis=0)
    counts = incl[-1]
    rank = jnp.sum((incl - onehot) * onehot, axis=1)
    pcounts = (counts + MOE_BLOCK - 1) // MOE_BLOCK * MOE_BLOCK
    pends = jnp.cumsum(pcounts)
    pstarts = pends - pcounts
    dest = jnp.sum(onehot * pstarts[None, :], axis=1) + rank
    p = a + N_EXPERTS * MOE_BLOCK
    nblk = p // MOE_BLOCK
    n_fill = p - a
    fill_ends = jnp.cumsum(pcounts - counts)
    fill_idx = jnp.arange(n_fill, dtype=jnp.int32)
    fill_e = jnp.sum(fill_idx[:, None] >= fill_ends[None, :], axis=1).astype(jnp.int32)
    idx_bits = max(a, n_fill).bit_length()
    real_keys = (eid_f << (idx_bits + 1)) | jnp.arange(a, dtype=jnp.int32)
    fill_keys = (fill_e << (idx_bits + 1)) | (1 << idx_bits) | fill_idx
    keys = jnp.sort(jnp.concatenate([real_keys, fill_keys]))
    is_real = ((keys >> idx_bits) & 1) == 0
    buf_tok = jnp.where(is_real, (keys & ((1 << idx_bits) - 1)) % n_tokens, 0)
    del tok_f
    n_active = pends[-1] // MOE_BLOCK
    blk_start = jnp.arange(nblk, dtype=jnp.int32) * MOE_BLOCK
    blk_e = jnp.minimum(jnp.sum(blk_start[:, None] >= pends[None, :], axis=1), N_EXPERTS - 1)
    blk_e = jnp.where(jnp.arange(nblk) < n_active, blk_e, blk_e[n_active - 1]).astype(jnp.int32)
    return buf_tok, dest.reshape(TOP_K, n_tokens), blk_e, n_active.astype(jnp.int32).reshape(1)


def _split_bf16(w):
    hi = w.astype(BF16)
    return jnp.stack([hi, (w - hi.astype(F32)).astype(BF16)])


def kernel(x, c, ctx, c_ctx, w_ada, b_ada, norm1_g, w_in, attn_q_norm, attn_k_norm, ret_decay_fwd,
           ret_decay_bwd, ret_gn_g, ret_gn_b, w_out, norm2_g, moe_w_grp, moe_b_grp, moe_w_exp, moe_b_exp,
           moe_w_gate, moe_w_up, moe_w_down, final_norm_g):
    b, l, d = x.shape
    cl = ctx.shape[1]
    assert b == 1 and w_ada.shape[0] == 1, "single batch element, single layer"
    x2, ctx2 = x[0], ctx[0]

    cvecs = jnp.zeros((SUBLANES, d), F32).at[0].set(c[0]).at[1].set(c_ctx)
    mod = _ada(cvecs, w_ada[0], b_ada[0])
    sh1, sc1, gt1, sh2, sc2, gt2 = [mod[0:1, i * d:(i + 1) * d] for i in range(6)]
    csh1, csc1 = mod[1:2, 0:d], mod[1:2, d:2 * d]

    w_in_bf = w_in[0].astype(BF16)
    head_of = jnp.arange(ATTN_WIDTH) // HEAD_DIM
    bd = jnp.where(head_of[:, None] == head_of[None, :], 1.0 / HEAD_DIM, 0.0).astype(BF16)
    gq = jnp.tile(attn_q_norm[0], ATTN_HEADS)[None, :]
    gk = jnp.tile(attn_k_norm[0], ATTN_KV_HEADS)[None, :]
    g1 = norm1_g[0][None, :]
    cos, sin = _rope_tables(l // GRID_W)
    bound = (HEAD_DIM ** 0.5 * LOG2E * 1.01) * jnp.max(jnp.abs(attn_q_norm[0])) * jnp.max(jnp.abs(attn_k_norm[0]))
    bound = bound.astype(BF16).astype(F32)
    neg_bound = jnp.full((1, LANES), -1.0, F32) * bound
    q_t, ka, va, qr, kr, vr, gr = _inproj(x2, sh1, sc1, g1, w_in_bf, bd, gq, gk, cos, sin, neg_bound)
    _, ka_c, va_c, _, kr_c, vr_c, _ = _inproj(ctx2, csh1, csc1, g1, w_in_bf, bd, gq, gk,
                                              jnp.ones((cl, LANES), F32), jnp.zeros((cl, LANES), F32), neg_bound)

    k_all = jnp.concatenate([ka_c, ka], axis=0)
    attn_args = (q_t, k_all, jnp.concatenate([va_c, va], axis=1))
    o_t = lax.cond(bound <= MAX_UNSHIFTED_BOUND,
                   functools.partial(_attention, running_max=False),
                   functools.partial(_attention, running_max=True), *attn_args)

    log_gf = jax.nn.log_sigmoid(ret_decay_fwd[0].astype(F32))
    log_gb = jax.nn.log_sigmoid(ret_decay_bwd[0].astype(F32))
    tab_f, gc_f = _retention_tables(log_gf, False)
    tab_b, gc_b = _retention_tables(log_gb, True)
    o_f = _retention_pass(gc_f, qr, kr, vr, kr_c, vr_c, tab_f, None, reverse=False)
    ret = _retention_pass(gc_b, qr, kr, vr, kr_c, vr_c, tab_b,
                          (o_f, gr, ret_gn_g[0][None, :], ret_gn_b[0][None, :]), reverse=True)

    wr = jnp.zeros((ROUTER_ROWS, d), F32)
    wr = wr.at[:N_GROUPS].set(moe_w_grp[0].T).at[EXPERT_ROW0:EXPERT_ROW0 + N_EXPERTS].set(moe_w_exp[0].T)
    br = jnp.zeros((ROUTER_ROWS, 1), F32)
    br = br.at[:N_GROUPS, 0].set(moe_b_grp[0]).at[EXPERT_ROW0:EXPERT_ROW0 + N_EXPERTS, 0].set(moe_b_exp[0])
    x_new, h2, ids, wts = _outproj_router(x2, o_t, ret, w_out[0].astype(BF16), gt1, norm2_g[0][None, :],
                                          sh2, sc2, _split_bf16(wr), br)

    buf_tok, dest, blk_e, n_active = _dispatch(ids, l)
    yb = _moe_blocks(blk_e, n_active, buf_tok.reshape(-1, 1, MOE_BLOCK), h2, moe_w_gate[0], moe_w_up[0],
                     moe_w_down[0])
    tf = min(FINAL_TILE, l)
    dest_t = dest.reshape(TOP_K, l // tf, tf).transpose(1, 0, 2)
    out = _final(x_new, yb, dest_t, wts[:TOP_K].T, gt2, final_norm_g[None, :])
    return out[None]
```

```python
import functools

import jax
import jax.numpy as jnp
import numpy as np
from jax import lax
from jax.experimental import pallas as pl
from jax.experimental.pallas import tpu as pltpu

F32 = jnp.float32
BF16 = jnp.bfloat16

GRID_W = 64
HEAD_DIM = 64
ATTN_HEADS = 8
ATTN_KV_HEADS = 2
GQA_GROUP = ATTN_HEADS // ATTN_KV_HEADS
RET_HEADS = 4
RET_QK_DIM = 64
RET_V_DIM = 128
RET_CHUNK = 256
ATTN_WIDTH = ATTN_HEADS * HEAD_DIM
KV_WIDTH = ATTN_KV_HEADS * HEAD_DIM
RET_QK_WIDTH = RET_HEADS * RET_QK_DIM
RET_WIDTH = RET_HEADS * RET_V_DIM
PROJ_SIZES = (ATTN_WIDTH, KV_WIDTH, KV_WIDTH, RET_QK_WIDTH, RET_QK_WIDTH, RET_WIDTH, RET_WIDTH)
PROJ_OFFS = tuple(int(v) for v in np.cumsum((0,) + PROJ_SIZES))
PROJ_DIM = PROJ_OFFS[-1]
ROPE_THETA = 10000.0
ROPE_AXIS_DIM = HEAD_DIM // 2
N_GROUPS = 4
EXPERTS_PER_GROUP = 8
N_EXPERTS = N_GROUPS * EXPERTS_PER_GROUP
TOP_K = 2
D_EXPERT = 512
EPS = 1e-6
LOG2E = 1.4426950408889634
ATTN_AUG_DIM = 128
MAX_UNSHIFTED_BOUND = 60.0

LANES = 128
SUBLANES = 8
VMEM_LIMIT_BYTES = 56 * 1024 * 1024

ROW_TILE = 512
INPROJ_SUBTILE = 256
ATTN_Q_TILE = 512
RET_ROW_TILE = 1024
RET_UNROLL = 4
MOE_BLOCK = 256
MOE_SLOTS = 3
FINAL_TILE = 1024
ROUTER_ROWS = 128
EXPERT_ROW0 = 8


def _cparams(*sem):
    return pltpu.CompilerParams(dimension_semantics=sem, vmem_limit_bytes=VMEM_LIMIT_BYTES)


def _const_spec(shape):
    nd = len(shape)
    return pl.BlockSpec(shape, lambda *_: (0,) * nd)


def _store_row_tiles(ref, idx, rows):
    n, w = rows.shape[0], rows.shape[1] // SUBLANES
    for s in range(SUBLANES):
        ref[idx + (pl.ds(s, n, stride=SUBLANES), slice(None))] = rows[:, s * w:(s + 1) * w]


def _load_row_tiles(ref, idx, n):
    return jnp.concatenate([ref[idx + (pl.ds(s, n, stride=SUBLANES), slice(None))] for s in range(SUBLANES)],
                           axis=1)


def _ada_kernel(s_ref, w_ref, b_ref, o_ref):
    s = s_ref[...]
    s = s / (1.0 + jnp.exp(-s))
    o_ref[...] = jnp.dot(s, w_ref[...], preferred_element_type=F32,
                         precision=lax.Precision.HIGHEST) + b_ref[...]


def _ada(cvecs, w_ada, b_ada):
    d, n = w_ada.shape
    tn = 1536
    return pl.pallas_call(
        _ada_kernel,
        grid=(n // tn,),
        in_specs=[_const_spec((SUBLANES, d)),
                  pl.BlockSpec((d, tn), lambda j: (0, j)),
                  pl.BlockSpec((1, tn), lambda j: (0, j))],
        out_specs=pl.BlockSpec((SUBLANES, tn), lambda j: (0, j)),
        out_shape=jax.ShapeDtypeStruct((SUBLANES, n), F32),
        compiler_params=_cparams("arbitrary"),
        name="ada_mod",
    )(cvecs, w_ada, b_ada.reshape(1, n))


def _rms_modulate(x, g, sh, sc):
    ms = jnp.mean(x * x, axis=-1, keepdims=True)
    return (x * lax.rsqrt(ms + EPS)) * g * (1.0 + sc) + sh


def _head_mean_sq(v, bd):
    sq = v * v
    hi = sq.astype(BF16)
    lo = (sq - hi.astype(F32)).astype(BF16)
    return (jnp.dot(hi, bd, preferred_element_type=F32) + jnp.dot(lo, bd, preferred_element_type=F32))


def _rope_chunks(v, cos, sin, even):
    outs = []
    for c in range(v.shape[1] // LANES):
        xc = v[:, c * LANES:(c + 1) * LANES]
        nxt = pltpu.roll(xc, LANES - 1, 1)
        prv = pltpu.roll(xc, 1, 1)
        outs.append(xc * cos + jnp.where(even, nxt, prv) * sin)
    return outs


def _inproj_kernel(x_ref, sh_ref, sc_ref, g_ref, w_ref, bd_ref, gq_ref, gk_ref, cos_ref, sin_ref, nb_ref,
                   qt_ref, ka_ref, va_ref, qr_ref, kr_ref, vr_ref, gr_ref):
    tm = x_ref.shape[0]
    sub = min(INPROJ_SUBTILE, tm)
    bd = bd_ref[...]
    scale = HEAD_DIM ** -0.5 * LOG2E
    rscale = RET_QK_DIM ** -0.5
    o = PROJ_OFFS
    per = LANES // HEAD_DIM
    for j0 in range(0, tm, sub):
        rows = slice(j0, j0 + sub)
        h = _rms_modulate(x_ref[rows, :], g_ref[...], sh_ref[...], sc_ref[...])
        proj = jnp.dot(h.astype(BF16), w_ref[...], preferred_element_type=F32)
        qa, ka, va = proj[:, o[0]:o[1]], proj[:, o[1]:o[2]], proj[:, o[2]:o[3]]
        qr, kr, vr, gr = proj[:, o[3]:o[4]], proj[:, o[4]:o[5]], proj[:, o[5]:o[6]], proj[:, o[6]:o[7]]
        cos, sin = cos_ref[rows, :], sin_ref[rows, :]
        even = (lax.broadcasted_iota(jnp.int32, cos.shape, 1) % 2) == 0

        qn = qa * lax.rsqrt(_head_mean_sq(qa, bd) + EPS) * gq_ref[...]
        for c, blk in enumerate(_rope_chunks(qn, cos, sin, even)):
            blk_t = (blk * scale).T
            for j in range(per):
                kvh, grp = divmod(c * per + j, GQA_GROUP)
                qt_ref[kvh, 0:HEAD_DIM, grp * tm + j0:grp * tm + j0 + sub] = (
                    blk_t[j * HEAD_DIM:(j + 1) * HEAD_DIM, :].astype(BF16))
        kn = ka * lax.rsqrt(_head_mean_sq(ka, bd[:KV_WIDTH, :KV_WIDTH]) + EPS) * gk_ref[...]
        k_rot = _rope_chunks(kn, cos, sin, even)[0].astype(BF16)
        pad_shape = (sub, ATTN_AUG_DIM - HEAD_DIM)
        k_pad = jnp.where(lax.broadcasted_iota(jnp.int32, pad_shape, 1) == 0, 1.0, 0.0).astype(BF16)
        for kvh in range(ATTN_KV_HEADS):
            c0 = kvh * ATTN_AUG_DIM
            ka_ref[rows, c0:c0 + HEAD_DIM] = k_rot[:, kvh * HEAD_DIM:(kvh + 1) * HEAD_DIM]
            ka_ref[rows, c0 + HEAD_DIM:c0 + ATTN_AUG_DIM] = k_pad
        va_ref[:, rows] = va.T.astype(BF16)
        for c, blk in enumerate(_rope_chunks(qr, cos, sin, even)):
            qr_ref[rows, c * LANES:(c + 1) * LANES] = blk.astype(BF16)
        for c, blk in enumerate(_rope_chunks(kr, cos, sin, even)):
            kr_ref[rows, c * LANES:(c + 1) * LANES] = (blk * rscale).astype(BF16)
        vr_ref[rows, :] = vr.astype(BF16)
        gr_ref[rows, :] = gr
    aug_shape = (ATTN_AUG_DIM - HEAD_DIM, GQA_GROUP * tm)
    aug_row = lax.broadcasted_iota(jnp.int32, aug_shape, 0)
    aug = jnp.where(aug_row == 0, nb_ref[0:1, 0:1], 0.0).astype(BF16)
    for kvh in range(ATTN_KV_HEADS):
        qt_ref[kvh, HEAD_DIM:ATTN_AUG_DIM, :] = aug


def _inproj(x2, sh, sc, g1, w_bf, bd, gq, gk, cos, sin, neg_bound):
    rows, d = x2.shape
    tm = min(ROW_TILE, rows)
    row = lambda w: pl.BlockSpec((tm, w), lambda i: (i, 0))
    row_specs = [row(ATTN_KV_HEADS * ATTN_AUG_DIM), pl.BlockSpec((KV_WIDTH, tm), lambda i: (0, i))]
    row_specs += [row(w) for w in PROJ_SIZES[3:]]
    row_shapes = [(rows, ATTN_KV_HEADS * ATTN_AUG_DIM), (KV_WIDTH, rows)] + [(rows, w) for w in PROJ_SIZES[3:]]
    dtypes = (BF16,) * 5 + (F32,)
    qt_shape = (ATTN_KV_HEADS, ATTN_AUG_DIM, GQA_GROUP * rows)
    return pl.pallas_call(
        _inproj_kernel,
        grid=(rows // tm,),
        in_specs=[row(d), _const_spec((1, d)), _const_spec((1, d)), _const_spec((1, d)),
                  _const_spec(w_bf.shape), _const_spec(bd.shape),
                  _const_spec((1, ATTN_WIDTH)), _const_spec((1, KV_WIDTH)),
                  row(LANES), row(LANES), _const_spec((1, LANES))],
        out_specs=[pl.BlockSpec((ATTN_KV_HEADS, ATTN_AUG_DIM, GQA_GROUP * tm), lambda i: (0, 0, i))] + row_specs,
        out_shape=[jax.ShapeDtypeStruct(qt_shape, BF16)]
        + [jax.ShapeDtypeStruct(shp, dt) for shp, dt in zip(row_shapes, dtypes)],
        compiler_params=_cparams("arbitrary"),
        name="norm_inproj",
    )(x2, sh, sc, g1, w_bf, bd, gq, gk, cos, sin, neg_bound)


def _attn_kernel(q_ref, k_ref, v_ref, o_ref, m_ref, l_ref, acc_ref, *, tk, running_max):
    m_ref[...] = jnp.full(m_ref.shape, -jnp.inf, F32)
    l_ref[...] = jnp.zeros(l_ref.shape, F32)
    acc_ref[...] = jnp.zeros(acc_ref.shape, F32)

    n_chunks = k_ref.shape[0] // tk

    def body(c, carry):
        keys = pl.ds(pl.multiple_of(c * tk, tk), tk)
        kc = k_ref[keys, :]
        vc = v_ref[:, keys]
        s = jnp.dot(kc, q_ref[0], preferred_element_type=F32)
        if running_max:
            m_old = m_ref[...]
            m_new = jnp.maximum(m_old, jnp.max(s, axis=0, keepdims=True))
            alpha = jnp.exp2(m_old - m_new)
            p = jnp.exp2(s - m_new)
            l_ref[...] = alpha * l_ref[...] + jnp.sum(p, axis=0, keepdims=True)
            acc_ref[...] = acc_ref[...] * alpha + jnp.dot(vc, p.astype(BF16), preferred_element_type=F32)
            m_ref[...] = m_new
        else:
            p = jnp.exp2(s)
            l_ref[...] += jnp.sum(p, axis=0, keepdims=True)
            acc_ref[...] += jnp.dot(vc, p.astype(BF16), preferred_element_type=F32)
        return carry

    lax.fori_loop(0, n_chunks, body, 0)
    o_ref[0] = (acc_ref[...] / l_ref[...]).astype(o_ref.dtype)


def _key_chunk(nk):
    for tk in (3328, 1280, 640, 256, 128):
        if nk % tk == 0:
            return tk
    raise ValueError(f"unsupported key count {nk}")


def _attention(q_t, k_c, v_t, *, running_max):
    kv, da, gl = q_t.shape
    nk = k_c.shape[0]
    d = v_t.shape[0] // kv
    nq = GQA_GROUP * min(ATTN_Q_TILE, gl // GQA_GROUP)
    return pl.pallas_call(
        functools.partial(_attn_kernel, tk=_key_chunk(nk), running_max=running_max),
        grid=(kv, gl // nq),
        in_specs=[pl.BlockSpec((1, da, nq), lambda b, i: (b, 0, i)),
                  pl.BlockSpec((nk, da), lambda b, i: (0, b)),
                  pl.BlockSpec((d, nk), lambda b, i: (b, 0))],
        out_specs=pl.BlockSpec((1, d, nq), lambda b, i: (b, 0, i)),
        out_shape=jax.ShapeDtypeStruct((kv, d, gl), BF16),
        scratch_shapes=[pltpu.VMEM((1, nq), F32), pltpu.VMEM((1, nq), F32), pltpu.VMEM((d, nq), F32)],
        compiler_params=_cparams("arbitrary", "arbitrary"),
        name="flash_attn",
    )(q_t, k_c, v_t)


def _ret_kernel(gc_ref, q_ref, k_ref, v_ref, kc_ref, vc_ref, tab_ref, *rest, reverse, final):
    if final:
        of_ref, gate_ref, gng_ref, gnb_ref, out_ref, s_ref = rest
    else:
        out_ref, s_ref = rest
    c_rows = RET_CHUNK
    n_local = q_ref.shape[0] // c_rows
    n_ctx = kc_ref.shape[0] // c_rows
    lane_head = lax.broadcasted_iota(jnp.int32, (c_rows, RET_QK_WIDTH), 1) // RET_QK_DIM

    def state_update(kch, vch):
        for h in range(RET_HEADS):
            vz = (vch[:, h * RET_V_DIM:(h + 1) * RET_V_DIM].astype(F32)
                  * tab_ref[1, h, :, 0:RET_V_DIM]).astype(BF16)
            u = lax.dot_general(kch, vz, (((0,), (0,)), ((), ())), preferred_element_type=F32)
            s_ref[h] = gc_ref[h] * s_ref[h] + u

    @pl.when(pl.program_id(0) == 0)
    def _():
        s_ref[...] = jnp.zeros(s_ref.shape, F32)
        order = range(n_ctx - 1, -1, -1) if reverse else range(n_ctx)
        for cc in order:
            state_update(kc_ref[cc * c_rows:(cc + 1) * c_rows, :], vc_ref[cc * c_rows:(cc + 1) * c_rows, :])

    def chunk(ci, carry):
        c = (n_local - 1 - ci) if reverse else ci
        rows = pl.ds(pl.multiple_of(c * c_rows, c_rows), c_rows)
        q, k, v = q_ref[rows, :], k_ref[rows, :], v_ref[rows, :]
        for h in range(RET_HEADS):
            cols = slice(h * RET_V_DIM, (h + 1) * RET_V_DIM)
            qm = jnp.where(lane_head == h, q, jnp.zeros_like(q))
            sc = lax.dot_general(qm, k, (((1,), (1,)), ((), ())), preferred_element_type=F32)
            sc = sc * tab_ref[0, h]
            o = jnp.dot(sc.astype(BF16), v[:, cols], preferred_element_type=F32)
            o = o + (jnp.dot(qm, s_ref[h].astype(BF16), preferred_element_type=F32)
                     * tab_ref[2, h, :, 0:RET_V_DIM])
            if final:
                o = o + of_ref[rows, cols]
                mu = jnp.mean(o, axis=-1, keepdims=True)
                oc = o - mu
                var = jnp.mean(oc * oc, axis=-1, keepdims=True)
                on = oc * lax.rsqrt(var + EPS)
                gate = gate_ref[rows, cols]
                y = (on * gng_ref[:, cols] + gnb_ref[:, cols]) * (gate / (1.0 + jnp.exp(-gate)))
                out_ref[rows, cols] = y.astype(out_ref.dtype)
            else:
                out_ref[rows, cols] = o
        state_update(k, v)
        return carry

    lax.fori_loop(0, n_local, chunk, 0, unroll=RET_UNROLL)


def _retention_pass(gc, qr, kr, vr, kr_c, vr_c, tabs, extra, *, reverse):
    l = qr.shape[0]
    tr = min(RET_ROW_TILE, l)
    nt = l // tr
    final = extra is not None
    idx = (lambda j, gc_: (nt - 1 - j, 0)) if reverse else (lambda j, gc_: (j, 0))
    const = lambda shape: pl.BlockSpec(shape, lambda j, gc_: (0,) * len(shape))
    in_specs = [pl.BlockSpec((tr, RET_QK_WIDTH), idx), pl.BlockSpec((tr, RET_QK_WIDTH), idx),
                pl.BlockSpec((tr, RET_WIDTH), idx),
                const(kr_c.shape), const(vr_c.shape), const(tabs.shape)]
    args = [qr, kr, vr, kr_c, vr_c, tabs]
    if final:
        of, gate, gng, gnb = extra
        in_specs += [pl.BlockSpec((tr, RET_WIDTH), idx), pl.BlockSpec((tr, RET_WIDTH), idx),
                     const(gng.shape), const(gnb.shape)]
        args += [of, gate, gng, gnb]
    return pl.pallas_call(
        functools.partial(_ret_kernel, reverse=reverse, final=final),
        grid_spec=pltpu.PrefetchScalarGridSpec(
            num_scalar_prefetch=1, grid=(nt,), in_specs=in_specs,
            out_specs=pl.BlockSpec((tr, RET_WIDTH), idx),
            scratch_shapes=[pltpu.VMEM((RET_HEADS, RET_QK_WIDTH, RET_V_DIM), F32)]),
        out_shape=jax.ShapeDtypeStruct((l, RET_WIDTH), BF16 if final else F32),
        compiler_params=_cparams("arbitrary"),
        name="retention_bwd" if reverse else "retention_fwd",
    )(gc, *args)


def _retention_tables(log_g, reverse):
    c = RET_CHUNK
    pos = jnp.arange(c, dtype=F32)
    diff = (pos[None, :] - pos[:, None]) if reverse else (pos[:, None] - pos[None, :])
    mask = (diff > 0) if reverse else (diff >= 0)
    lg = log_g[:, None, None]
    dmat = jnp.where(mask, jnp.exp(lg * jnp.where(mask, diff, 0.0)[None]), 0.0)
    zeta = jnp.exp(log_g[:, None] * (pos if reverse else (c - 1 - pos))[None, :])
    xi = jnp.exp(log_g[:, None] * ((c - pos) if reverse else (pos + 1.0))[None, :])
    ones = jnp.ones((1, 1, c), F32)
    tabs = jnp.stack([dmat, zeta[:, :, None] * ones, xi[:, :, None] * ones])
    return tabs, jnp.exp(log_g * c)


def _outproj_kernel(x_ref, a_ref, r_ref, w_ref, gt_ref, g2_ref, sh_ref, sc_ref, wr_ref, br_ref,
                    xn_ref, h2_ref, ids_ref, wts_ref):
    tm = x_ref.shape[0]
    mix = jnp.dot(r_ref[...], w_ref[ATTN_WIDTH:, :], preferred_element_type=F32)
    per = LANES // HEAD_DIM
    for pair in range(ATTN_HEADS // per):
        pieces = []
        for j in range(per):
            kvh, grp = divmod(pair * per + j, GQA_GROUP)
            pieces.append(a_ref[kvh, :, grp * tm:(grp + 1) * tm])
        o_t = jnp.concatenate(pieces, axis=0)
        mix = mix + lax.dot_general(o_t, w_ref[pair * LANES:(pair + 1) * LANES, :],
                                    (((0,), (0,)), ((), ())), preferred_element_type=F32)
    x_new = x_ref[...] + gt_ref[...] * mix
    xn_ref[...] = x_new
    h2 = _rms_modulate(x_new, g2_ref[...], sh_ref[...], sc_ref[...])
    _store_row_tiles(h2_ref, (), h2)
    hi = h2.astype(BF16)
    lo = (h2 - hi.astype(F32)).astype(BF16)

    nt = (((1,), (1,)), ((), ()))
    logits = (lax.dot_general(wr_ref[0], hi, nt, preferred_element_type=F32)
              + lax.dot_general(wr_ref[0], lo, nt, preferred_element_type=F32)
              + lax.dot_general(wr_ref[1], hi, nt, preferred_element_type=F32)) + br_ref[...]
    rows = [logits[g:g + 1, :] for g in range(N_GROUPS)]
    gmax = functools.reduce(jnp.maximum, rows)
    gidx = jnp.full(gmax.shape, N_GROUPS - 1, jnp.int32)
    for g in range(N_GROUPS - 2, -1, -1):
        gidx = jnp.where(rows[g] == gmax, g, gidx)
    p_g = 1.0 / functools.reduce(jnp.add, [jnp.exp(r - gmax) for r in rows])
    sel = logits[EXPERT_ROW0 + (N_GROUPS - 1) * EXPERTS_PER_GROUP:EXPERT_ROW0 + N_GROUPS * EXPERTS_PER_GROUP, :]
    for g in range(N_GROUPS - 2, -1, -1):
        blk = logits[EXPERT_ROW0 + g * EXPERTS_PER_GROUP:EXPERT_ROW0 + (g + 1) * EXPERTS_PER_GROUP, :]
        sel = jnp.where(gidx == g, blk, sel)
    ridx = lax.broadcasted_iota(jnp.int32, sel.shape, 0)
    m1 = jnp.max(sel, axis=0, keepdims=True)
    i1 = jnp.min(jnp.where(sel == m1, ridx, EXPERTS_PER_GROUP), axis=0, keepdims=True)
    sel2 = jnp.where(ridx == i1, -jnp.inf, sel)
    m2 = jnp.max(sel2, axis=0, keepdims=True)
    i2 = jnp.min(jnp.where(sel2 == m2, ridx, EXPERTS_PER_GROUP), axis=0, keepdims=True)
    e2 = jnp.exp(m2 - m1)
    w1 = p_g / (1.0 + e2)
    w2 = w1 * e2
    out_rows = lax.broadcasted_iota(jnp.int32, ids_ref.shape, 0)
    ids_ref[...] = jnp.where(out_rows == 0, gidx * EXPERTS_PER_GROUP + i1,
                             jnp.where(out_rows == 1, gidx * EXPERTS_PER_GROUP + i2, 0))
    wts_ref[...] = jnp.where(out_rows == 0, w1, jnp.where(out_rows == 1, w2, 0.0))


def _outproj_router(x2, attn, ret, w_out_bf, gt1, g2, sh2, sc2, wr, br):
    l, d = x2.shape
    tm = min(ROW_TILE, l)
    row = lambda w: pl.BlockSpec((tm, w), lambda i: (i, 0))
    col = pl.BlockSpec((SUBLANES, tm), lambda i: (0, i))
    return pl.pallas_call(
        _outproj_kernel,
        grid=(l // tm,),
        in_specs=[row(d),
                  pl.BlockSpec((ATTN_KV_HEADS, HEAD_DIM, GQA_GROUP * tm), lambda i: (0, 0, i)),
                  row(ret.shape[1]), _const_spec(w_out_bf.shape),
                  _const_spec((1, d)), _const_spec((1, d)), _const_spec((1, d)), _const_spec((1, d)),
                  _const_spec(wr.shape), _const_spec(br.shape)],
        out_specs=[row(d), pl.BlockSpec((tm * SUBLANES, d // SUBLANES), lambda i: (i, 0)), col, col],
        out_shape=[jax.ShapeDtypeStruct((l, d), F32), jax.ShapeDtypeStruct((l * SUBLANES, d // SUBLANES), F32),
                   jax.ShapeDtypeStruct((SUBLANES, l), jnp.int32),
                   jax.ShapeDtypeStruct((SUBLANES, l), F32)],
        compiler_params=_cparams("arbitrary"),
        name="outproj_router",
    )(x2, attn, ret, w_out_bf, gt1, g2, sh2, sc2, wr, br)


def _row_copy(src_hbm, src_row, dst_ref, sem):
    return pltpu.make_async_copy(src_hbm.at[pl.ds(src_row * SUBLANES, SUBLANES), :], dst_ref, sem)


def _moe_kernel(be_ref, na_ref, tok0_ref, tok1_ref, tokn_ref, h_hbm, wg_ref, wu_ref, wd_ref, y_ref,
                wg_s, wu_s, wd_s, xg_s, sem):
    b = pl.program_id(0)
    slot = lax.rem(b, MOE_SLOTS)
    nxt = lax.rem(b + 2, MOE_SLOTS)
    n_active = na_ref[0]
    active = b < n_active
    last = be_ref.shape[0] - 1
    changed = jnp.logical_or(b == 0, be_ref[jnp.minimum(b, last)] != be_ref[jnp.clip(b - 1, 0, last)])

    def wait_rows():
        rows = pl.ds(0, MOE_BLOCK * SUBLANES)
        pltpu.make_async_copy(h_hbm.at[rows, :], xg_s.at[slot, rows, :], sem.at[slot]).wait()

    @pl.when(b == 0)
    def _():
        def body(r, carry):
            _row_copy(h_hbm, tok0_ref[0, 0, r], xg_s.at[0, pl.ds(r * SUBLANES, SUBLANES), :], sem.at[0]).start()
            _row_copy(h_hbm, tok1_ref[0, 0, r], xg_s.at[1, pl.ds(r * SUBLANES, SUBLANES), :], sem.at[1]).start()
            return carry

        lax.fori_loop(0, MOE_BLOCK, body, 0, unroll=8)

    @pl.when(jnp.logical_and(active, changed))
    def _():
        wg_s[...] = wg_ref[0].astype(BF16)
        wu_s[...] = wu_ref[0].astype(BF16)
        wd_s[...] = wd_ref[0].astype(BF16)

    @pl.when(active)
    def _():
        wait_rows()
        x = _load_row_tiles(xg_s, (slot,), MOE_BLOCK).astype(BF16)
        hg = jnp.dot(x, wg_s[...], preferred_element_type=F32)
        hu = jnp.dot(x, wu_s[...], preferred_element_type=F32)
        a = (hg / (1.0 + jnp.exp(-hg))) * hu
        _store_row_tiles(y_ref, (), jnp.dot(a.astype(BF16), wd_s[...], preferred_element_type=F32))
        for r in range(MOE_BLOCK):
            _row_copy(h_hbm, tokn_ref[0, 0, r], xg_s.at[nxt, pl.ds(r * SUBLANES, SUBLANES), :],
                      sem.at[nxt]).start(priority=r % 2)

    @pl.when(jnp.logical_or(b == n_active, b == n_active + 1))
    def _():
        wait_rows()

    @pl.when(jnp.logical_not(active))
    def _():
        y_ref[...] = jnp.zeros(y_ref.shape, y_ref.dtype)


def _moe_blocks(blk_e, n_active, buf_tok, h2, w_gate, w_up, w_down):
    ne, d, de = w_gate.shape
    tiled = (MOE_BLOCK * SUBLANES, d // SUBLANES)
    nblk = buf_tok.shape[0]
    blk = lambda b: jnp.minimum(b, nblk - 1)
    wmap = lambda b, be, na: (be[blk(b)], 0, 0)
    tokmap = lambda off: (lambda b, be, na: (blk(b + off), 0, 0))
    smem_tok = lambda off: pl.BlockSpec((1, 1, MOE_BLOCK), tokmap(off), memory_space=pltpu.SMEM)
    return pl.pallas_call(
        _moe_kernel,
        grid_spec=pltpu.PrefetchScalarGridSpec(
            num_scalar_prefetch=2, grid=(nblk + 1,),
            in_specs=[pl.BlockSpec((1, 1, MOE_BLOCK), lambda b, be, na: (0, 0, 0), memory_space=pltpu.SMEM),
                      pl.BlockSpec((1, 1, MOE_BLOCK), lambda b, be, na: (1, 0, 0), memory_space=pltpu.SMEM),
                      smem_tok(2),
                      pl.BlockSpec(memory_space=pl.ANY),
                      pl.BlockSpec((1, d, de), wmap), pl.BlockSpec((1, d, de), wmap),
                      pl.BlockSpec((1, de, d), wmap)],
            out_specs=pl.BlockSpec(tiled, lambda b, be, na: (blk(b), 0)),
            scratch_shapes=[pltpu.VMEM((d, de), BF16), pltpu.VMEM((d, de), BF16),
                            pltpu.VMEM((de, d), BF16), pltpu.VMEM((MOE_SLOTS,) + tiled, F32),
                            pltpu.SemaphoreType.DMA((MOE_SLOTS,))]),
        out_shape=jax.ShapeDtypeStruct((nblk * tiled[0], tiled[1]), F32),
        compiler_params=_cparams("arbitrary"),
        name="moe_experts",
    )(blk_e, n_active, buf_tok, buf_tok, buf_tok, h2, w_gate, w_up, w_down)


def _final_kernel(d0_ref, dn_ref, x_ref, w_ref, gt_ref, g_ref, y_hbm, o_ref, yg_s, sem):
    i = pl.program_id(0)
    slot = i % 2
    tm = x_ref.shape[0]

    def start_tile(d_ref, slot_):
        def body(r, carry):
            for k in range(TOP_K):
                _row_copy(y_hbm, d_ref[0, k, r], yg_s.at[slot_, k, pl.ds(r * SUBLANES, SUBLANES), :],
                          sem.at[slot_]).start(priority=k)
            return carry

        lax.fori_loop(0, tm, body, 0, unroll=8)

    @pl.when(i == 0)
    def _():
        start_tile(d0_ref, 0)

    @pl.when(i + 1 < pl.num_programs(0))
    def _():
        start_tile(dn_ref, 1 - slot)

    for k in range(TOP_K):
        pltpu.make_async_copy(y_hbm.at[pl.ds(0, tm * SUBLANES), :], yg_s.at[slot, k], sem.at[slot]).wait()
    w = w_ref[...]
    y = _load_row_tiles(yg_s, (slot, 0), tm) * w[:, 0:1] + _load_row_tiles(yg_s, (slot, 1), tm) * w[:, 1:2]
    x = x_ref[...] + gt_ref[...] * y
    ms = jnp.mean(x * x, axis=-1, keepdims=True)
    o_ref[...] = (x * lax.rsqrt(ms + EPS)) * g_ref[...]


def _final(x_new, yb, dest, wts, gt2, gfin):
    l, d = x_new.shape
    n, _, tm = dest.shape
    row = lambda w: pl.BlockSpec((tm, w), lambda i: (i, 0))
    return pl.pallas_call(
        _final_kernel,
        grid=(n,),
        in_specs=[pl.BlockSpec((1, TOP_K, tm), lambda i: (0, 0, 0), memory_space=pltpu.SMEM),
                  pl.BlockSpec((1, TOP_K, tm), lambda i: (jnp.minimum(i + 1, n - 1), 0, 0),
                               memory_space=pltpu.SMEM),
                  row(d), row(wts.shape[1]), _const_spec((1, d)), _const_spec((1, d)),
                  pl.BlockSpec(memory_space=pl.ANY)],
        out_specs=row(d),
        out_shape=jax.ShapeDtypeStruct((l, d), F32),
        scratch_shapes=[pltpu.VMEM((2, TOP_K, tm * SUBLANES, d // SUBLANES), F32), pltpu.SemaphoreType.DMA((2,))],
        compiler_params=_cparams("arbitrary"),
        name="combine_final_norm",
    )(dest, dest, x_new, wts, gt2, gfin, yb)


def _rope_tables(rows_count):
    lane = jnp.arange(LANES, dtype=jnp.int32) % HEAD_DIM
    pair = lane // 2
    n_freq = ROPE_AXIS_DIM // 2
    by_row = pair < n_freq
    fidx = jnp.where(by_row, pair, pair - n_freq).astype(F32)
    freq = ROPE_THETA ** (-(2.0 * fidx) / ROPE_AXIS_DIM)
    sign = jnp.where(lane % 2 == 0, -1.0, 1.0).astype(F32)
    ang_row = jnp.arange(rows_count, dtype=F32)[:, None] * freq
    ang_col = jnp.arange(GRID_W, dtype=F32)[:, None] * freq
    small = lax.optimization_barrier((jnp.cos(ang_row), jnp.cos(ang_col), jnp.sin(ang_row) * sign,
                                      jnp.sin(ang_col) * sign))
    cos_r, cos_c, sin_r, sin_c = small
    full = (rows_count * GRID_W, LANES)
    cos = jnp.where(by_row, cos_r[:, None, :], cos_c[None, :, :]).reshape(full)
    sin = jnp.where(by_row, sin_r[:, None, :], sin_c[None, :, :]).reshape(full)
    return cos, sin


def _dispatch(ids, n_tokens):
    a = TOP_K * n_tokens
    eid_f = ids[:TOP_K].reshape(a)
    tok_f = jnp.tile(jnp.arange(n_tokens, dtype=jnp.int32), TOP_K)
    onehot = (eid_f[:, None] == jnp.arange(N_EXPERTS, dtype=jnp.int32)[None, :]).astype(jnp.int32)
    incl = jnp.cumsum(onehot, axis=0)
    counts = incl[-1]
    rank = jnp.sum((incl - onehot) * onehot, axis=1)
    pcounts = (counts + MOE_BLOCK - 1) // MOE_BLOCK * MOE_BLOCK
    pends = jnp.cumsum(pcounts)
    pstarts = pends - pcounts
    dest = jnp.sum(onehot * pstarts[None, :], axis=1) + rank
    p = a + N_EXPERTS * MOE_BLOCK
    nblk = p // MOE_BLOCK
    n_fill = p - a
    fill_ends = jnp.cumsum(pcounts - counts)
    fill_idx = jnp.arange(n_fill, dtype=jnp.int32)
    fill_e = jnp.sum(fill_idx[:, None] >= fill_ends[None, :], axis=1).astype(jnp.int32)
    idx_bits = max(a, n_fill).bit_length()
    real_keys = (eid_f << (idx_bits + 1)) | jnp.arange(a, dtype=jnp.int32)
    fill_keys = (fill_e << (idx_bits + 1)) | (1 << idx_bits) | fill_idx
    keys = jnp.sort(jnp.concatenate([real_keys, fill_keys]))
    is_real = ((keys >> idx_bits) & 1) == 0
    buf_tok = jnp.where(is_real, (keys & ((1 << idx_bits) - 1)) % n_tokens, 0)
    del tok_f
    n_active = pends[-1] // MOE_BLOCK
    blk_start = jnp.arange(nblk, dtype=jnp.int32) * MOE_BLOCK
    blk_e = jnp.minimum(jnp.sum(blk_start[:, None] >= pends[None, :], axis=1), N_EXPERTS - 1)
    blk_e = jnp.where(jnp.arange(nblk) < n_active, blk_e, blk_e[n_active - 1]).astype(jnp.int32)
    return buf_tok, dest.reshape(TOP_K, n_tokens), blk_e, n_active.astype(jnp.int32).reshape(1)


def _split_bf16(w):
    hi = w.astype(BF16)
    return jnp.stack([hi, (w - hi.astype(F32)).astype(BF16)])


def kernel(x, c, ctx, c_ctx, w_ada, b_ada, norm1_g, w_in, attn_q_norm, attn_k_norm, ret_decay_fwd,
           ret_decay_bwd, ret_gn_g, ret_gn_b, w_out, norm2_g, moe_w_grp, moe_b_grp, moe_w_exp, moe_b_exp,
           moe_w_gate, moe_w_up, moe_w_down, final_norm_g):
    b, l, d = x.shape
    cl = ctx.shape[1]
    assert b == 1 and w_ada.shape[0] == 1, "single batch element, single layer"
    x2, ctx2 = x[0], ctx[0]

    cvecs = jnp.zeros((SUBLANES, d), F32).at[0].set(c[0]).at[1].set(c_ctx)
    mod = _ada(cvecs, w_ada[0], b_ada[0])
    sh1, sc1, gt1, sh2, sc2, gt2 = [mod[0:1, i * d:(i + 1) * d] for i in range(6)]
    csh1, csc1 = mod[1:2, 0:d], mod[1:2, d:2 * d]

    w_in_bf = w_in[0].astype(BF16)
    head_of = jnp.arange(ATTN_WIDTH) // HEAD_DIM
    bd = jnp.where(head_of[:, None] == head_of[None, :], 1.0 / HEAD_DIM, 0.0).astype(BF16)
    gq = jnp.tile(attn_q_norm[0], ATTN_HEADS)[None, :]
    gk = jnp.tile(attn_k_norm[0], ATTN_KV_HEADS)[None, :]
    g1 = norm1_g[0][None, :]
    cos, sin = _rope_tables(l // GRID_W)
    bound = (HEAD_DIM ** 0.5 * LOG2E * 1.01) * jnp.max(jnp.abs(attn_q_norm[0])) * jnp.max(jnp.abs(attn_k_norm[0]))
    bound = bound.astype(BF16).astype(F32)
    neg_bound = jnp.full((1, LANES), -1.0, F32) * bound
    q_t, ka, va, qr, kr, vr, gr = _inproj(x2, sh1, sc1, g1, w_in_bf, bd, gq, gk, cos, sin, neg_bound)
    _, ka_c, va_c, _, kr_c, vr_c, _ = _inproj(ctx2, csh1, csc1, g1, w_in_bf, bd, gq, gk,
                                              jnp.ones((cl, LANES), F32), jnp.zeros((cl, LANES), F32), neg_bound)

    k_all = jnp.concatenate([ka_c, ka], axis=0)
    attn_args = (q_t, k_all, jnp.concatenate([va_c, va], axis=1))
    o_t = lax.cond(bound <= MAX_UNSHIFTED_BOUND,
                   functools.partial(_attention, running_max=False),
                   functools.partial(_attention, running_max=True), *attn_args)

    log_gf = jax.nn.log_sigmoid(ret_decay_fwd[0].astype(F32))
    log_gb = jax.nn.log_sigmoid(ret_decay_bwd[0].astype(F32))
    tab_f, gc_f = _retention_tables(log_gf, False)
    tab_b, gc_b = _retention_tables(log_gb, True)
    o_f = _retention_pass(gc_f, qr, kr, vr, kr_c, vr_c, tab_f, None, reverse=False)
    ret = _retention_pass(gc_b, qr, kr, vr, kr_c, vr_c, tab_b,
                          (o_f, gr, ret_gn_g[0][None, :], ret_gn_b[0][None, :]), reverse=True)

    wr = jnp.zeros((ROUTER_ROWS, d), F32)
    wr = wr.at[:N_GROUPS].set(moe_w_grp[0].T).at[EXPERT_ROW0:EXPERT_ROW0 + N_EXPERTS].set(moe_w_exp[0].T)
    br = jnp.zeros((ROUTER_ROWS, 1), F32)
    br = br.at[:N_GROUPS, 0].set(moe_b_grp[0]).at[EXPERT_ROW0:EXPERT_ROW0 + N_EXPERTS, 0].set(moe_b_exp[0])
    x_new, h2, ids, wts = _outproj_router(x2, o_t, ret, w_out[0].astype(BF16), gt1, norm2_g[0][None, :],
                                          sh2, sc2, _split_bf16(wr), br)

    buf_tok, dest, blk_e, n_active = _dispatch(ids, l)
    yb = _moe_blocks(blk_e, n_active, buf_tok.reshape(-1, 1, MOE_BLOCK), h2, moe_w_gate[0], moe_w_up[0],
                     moe_w_down[0])
    tf = min(FINAL_TILE, l)
    dest_t = dest.reshape(TOP_K, l // tf, tf).transpose(1, 0, 2)
    out = _final(x_new, yb, dest_t, wts[:TOP_K].T, gt2, final_norm_g[None, :])
    return out[None]
```

```python
import functools

import jax
import jax.numpy as jnp
import numpy as np
from jax import lax
from jax.experimental import pallas as pl
from jax.experimental.pallas import tpu as pltpu

F32 = jnp.float32
BF16 = jnp.bfloat16

GRID_W = 64
HEAD_DIM = 64
ATTN_HEADS = 8
ATTN_KV_HEADS = 2
GQA_GROUP = ATTN_HEADS // ATTN_KV_HEADS
RET_HEADS = 4
RET_QK_DIM = 64
RET_V_DIM = 128
RET_CHUNK = 256
ATTN_WIDTH = ATTN_HEADS * HEAD_DIM
KV_WIDTH = ATTN_KV_HEADS * HEAD_DIM
RET_QK_WIDTH = RET_HEADS * RET_QK_DIM
RET_WIDTH = RET_HEADS * RET_V_DIM
PROJ_SIZES = (ATTN_WIDTH, KV_WIDTH, KV_WIDTH, RET_QK_WIDTH, RET_QK_WIDTH, RET_WIDTH, RET_WIDTH)
PROJ_OFFS = tuple(int(v) for v in np.cumsum((0,) + PROJ_SIZES))
PROJ_DIM = PROJ_OFFS[-1]
ROPE_THETA = 10000.0
ROPE_AXIS_DIM = HEAD_DIM // 2
N_GROUPS = 4
EXPERTS_PER_GROUP = 8
N_EXPERTS = N_GROUPS * EXPERTS_PER_GROUP
TOP_K = 2
D_EXPERT = 512
EPS = 1e-6
LOG2E = 1.4426950408889634
ATTN_AUG_DIM = 128
MAX_UNSHIFTED_BOUND = 60.0

LANES = 128
SUBLANES = 8
VMEM_LIMIT_BYTES = 56 * 1024 * 1024

ROW_TILE = 512
INPROJ_SUBTILE = 256
ATTN_Q_TILE = 512
ATTN_CHUNK_UNROLL = 5
RET_ROW_TILE = 1024
RET_UNROLL = 4
MOE_BLOCK = 256
MOE_SLOTS = 3
FINAL_TILE = 512
ROUTER_ROWS = 128
EXPERT_ROW0 = 8


def _cparams(*sem):
    return pltpu.CompilerParams(dimension_semantics=sem, vmem_limit_bytes=VMEM_LIMIT_BYTES)


def _const_spec(shape):
    nd = len(shape)
    return pl.BlockSpec(shape, lambda *_: (0,) * nd)


def _store_row_tiles(ref, idx, rows):
    n, w = rows.shape[0], rows.shape[1] // SUBLANES
    for s in range(SUBLANES):
        ref[idx + (pl.ds(s, n, stride=SUBLANES), slice(None))] = rows[:, s * w:(s + 1) * w]


def _load_row_tiles(ref, idx, n):
    return jnp.concatenate([ref[idx + (pl.ds(s, n, stride=SUBLANES), slice(None))] for s in range(SUBLANES)],
                           axis=1)


def _ada_kernel(s_ref, w_ref, b_ref, o_ref):
    s = s_ref[...]
    s = s / (1.0 + jnp.exp(-s))
    o_ref[...] = jnp.dot(s, w_ref[...], preferred_element_type=F32,
                         precision=lax.Precision.HIGHEST) + b_ref[...]


def _ada(cvecs, w_ada, b_ada):
    d, n = w_ada.shape
    tn = 1536
    return pl.pallas_call(
        _ada_kernel,
        grid=(n // tn,),
        in_specs=[_const_spec((SUBLANES, d)),
                  pl.BlockSpec((d, tn), lambda j: (0, j)),
                  pl.BlockSpec((1, tn), lambda j: (0, j))],
        out_specs=pl.BlockSpec((SUBLANES, tn), lambda j: (0, j)),
        out_shape=jax.ShapeDtypeStruct((SUBLANES, n), F32),
        compiler_params=_cparams("arbitrary"),
        name="ada_mod",
    )(cvecs, w_ada, b_ada.reshape(1, n))


def _rms_modulate(x, g, sh, sc):
    ms = jnp.mean(x * x, axis=-1, keepdims=True)
    return (x * lax.rsqrt(ms + EPS)) * g * (1.0 + sc) + sh


def _head_mean_sq(v, bd):
    sq = v * v
    hi = sq.astype(BF16)
    lo = (sq - hi.astype(F32)).astype(BF16)
    return (jnp.dot(hi, bd, preferred_element_type=F32) + jnp.dot(lo, bd, preferred_element_type=F32))


def _rope_chunks(v, cos, sin, even):
    outs = []
    for c in range(v.shape[1] // LANES):
        xc = v[:, c * LANES:(c + 1) * LANES]
        nxt = pltpu.roll(xc, LANES - 1, 1)
        prv = pltpu.roll(xc, 1, 1)
        outs.append(xc * cos + jnp.where(even, nxt, prv) * sin)
    return outs


def _inproj_kernel(x_ref, sh_ref, sc_ref, g_ref, w_ref, bd_ref, gq_ref, gk_ref, cos_ref, sin_ref, nb_ref,
                   qt_ref, ka_ref, va_ref, qr_ref, kr_ref, vr_ref, gr_ref):
    tm = x_ref.shape[0]
    sub = min(INPROJ_SUBTILE, tm)
    bd = bd_ref[...]
    scale = HEAD_DIM ** -0.5 * LOG2E
    rscale = RET_QK_DIM ** -0.5
    o = PROJ_OFFS
    per = LANES // HEAD_DIM
    for j0 in range(0, tm, sub):
        rows = slice(j0, j0 + sub)
        h = _rms_modulate(x_ref[rows, :], g_ref[...], sh_ref[...], sc_ref[...])
        proj = jnp.dot(h.astype(BF16), w_ref[...], preferred_element_type=F32)
        qa, ka, va = proj[:, o[0]:o[1]], proj[:, o[1]:o[2]], proj[:, o[2]:o[3]]
        qr, kr, vr, gr = proj[:, o[3]:o[4]], proj[:, o[4]:o[5]], proj[:, o[5]:o[6]], proj[:, o[6]:o[7]]
        cos, sin = cos_ref[rows, :], sin_ref[rows, :]
        even = (lax.broadcasted_iota(jnp.int32, cos.shape, 1) % 2) == 0

        qn = qa * lax.rsqrt(_head_mean_sq(qa, bd) + EPS) * gq_ref[...]
        for c, blk in enumerate(_rope_chunks(qn, cos, sin, even)):
            blk_t = (blk * scale).T
            for j in range(per):
                kvh, grp = divmod(c * per + j, GQA_GROUP)
                qt_ref[kvh, 0:HEAD_DIM, grp * tm + j0:grp * tm + j0 + sub] = (
                    blk_t[j * HEAD_DIM:(j + 1) * HEAD_DIM, :].astype(BF16))
        kn = ka * lax.rsqrt(_head_mean_sq(ka, bd[:KV_WIDTH, :KV_WIDTH]) + EPS) * gk_ref[...]
        k_rot = _rope_chunks(kn, cos, sin, even)[0].astype(BF16)
        pad_shape = (sub, ATTN_AUG_DIM - HEAD_DIM)
        k_pad = jnp.where(lax.broadcasted_iota(jnp.int32, pad_shape, 1) == 0, 1.0, 0.0).astype(BF16)
        for kvh in range(ATTN_KV_HEADS):
            c0 = kvh * ATTN_AUG_DIM
            ka_ref[rows, c0:c0 + HEAD_DIM] = k_rot[:, kvh * HEAD_DIM:(kvh + 1) * HEAD_DIM]
            ka_ref[rows, c0 + HEAD_DIM:c0 + ATTN_AUG_DIM] = k_pad
        va_ref[:, rows] = va.T.astype(BF16)
        for c, blk in enumerate(_rope_chunks(qr, cos, sin, even)):
            qr_ref[rows, c * LANES:(c + 1) * LANES] = blk.astype(BF16)
        for c, blk in enumerate(_rope_chunks(kr, cos, sin, even)):
            kr_ref[rows, c * LANES:(c + 1) * LANES] = (blk * rscale).astype(BF16)
        vr_ref[rows, :] = vr.astype(BF16)
        gr_ref[rows, :] = gr
    aug_shape = (ATTN_AUG_DIM - HEAD_DIM, GQA_GROUP * tm)
    aug_row = lax.broadcasted_iota(jnp.int32, aug_shape, 0)
    aug = jnp.where(aug_row == 0, nb_ref[0:1, 0:1], 0.0).astype(BF16)
    for kvh in range(ATTN_KV_HEADS):
        qt_ref[kvh, HEAD_DIM:ATTN_AUG_DIM, :] = aug


def _inproj(x2, sh, sc, g1, w_bf, bd, gq, gk, cos, sin, neg_bound):
    rows, d = x2.shape
    tm = min(ROW_TILE, rows)
    row = lambda w: pl.BlockSpec((tm, w), lambda i: (i, 0))
    row_specs = [row(ATTN_KV_HEADS * ATTN_AUG_DIM), pl.BlockSpec((KV_WIDTH, tm), lambda i: (0, i))]
    row_specs += [row(w) for w in PROJ_SIZES[3:]]
    row_shapes = [(rows, ATTN_KV_HEADS * ATTN_AUG_DIM), (KV_WIDTH, rows)] + [(rows, w) for w in PROJ_SIZES[3:]]
    dtypes = (BF16,) * 5 + (F32,)
    qt_shape = (ATTN_KV_HEADS, ATTN_AUG_DIM, GQA_GROUP * rows)
    return pl.pallas_call(
        _inproj_kernel,
        grid=(rows // tm,),
        in_specs=[row(d), _const_spec((1, d)), _const_spec((1, d)), _const_spec((1, d)),
                  _const_spec(w_bf.shape), _const_spec(bd.shape),
                  _const_spec((1, ATTN_WIDTH)), _const_spec((1, KV_WIDTH)),
                  row(LANES), row(LANES), _const_spec((1, LANES))],
        out_specs=[pl.BlockSpec((ATTN_KV_HEADS, ATTN_AUG_DIM, GQA_GROUP * tm), lambda i: (0, 0, i))] + row_specs,
        out_shape=[jax.ShapeDtypeStruct(qt_shape, BF16)]
        + [jax.ShapeDtypeStruct(shp, dt) for shp, dt in zip(row_shapes, dtypes)],
        compiler_params=_cparams("arbitrary"),
        name="norm_inproj",
    )(x2, sh, sc, g1, w_bf, bd, gq, gk, cos, sin, neg_bound)


def _attn_kernel(q_ref, k_ref, v_ref, o_ref, m_ref, l_ref, acc_ref, *, tk, running_max):
    m_ref[...] = jnp.full(m_ref.shape, -jnp.inf, F32)
    l_ref[...] = jnp.zeros(l_ref.shape, F32)
    acc_ref[...] = jnp.zeros(acc_ref.shape, F32)

    n_chunks = k_ref.shape[0] // tk

    def body(c, carry):
        keys = pl.ds(pl.multiple_of(c * tk, tk), tk)
        kc = k_ref[keys, :]
        vc = v_ref[:, keys]
        s = jnp.dot(kc, q_ref[0], preferred_element_type=F32)
        if running_max:
            m_old = m_ref[...]
            m_new = jnp.maximum(m_old, jnp.max(s, axis=0, keepdims=True))
            alpha = jnp.exp2(m_old - m_new)
            p = jnp.exp2(s - m_new)
            l_ref[...] = alpha * l_ref[...] + jnp.sum(p, axis=0, keepdims=True)
            acc_ref[...] = acc_ref[...] * alpha + jnp.dot(vc, p.astype(BF16), preferred_element_type=F32)
            m_ref[...] = m_new
        else:
            p = jnp.exp2(s)
            l_ref[...] += jnp.sum(p, axis=0, keepdims=True)
            acc_ref[...] += jnp.dot(vc, p.astype(BF16), preferred_element_type=F32)
        return carry

    lax.fori_loop(0, n_chunks, body, 0, unroll=1 if running_max else ATTN_CHUNK_UNROLL)
    o_ref[0] = (acc_ref[...] / l_ref[...]).astype(o_ref.dtype)


def _key_chunk(nk):
    for tk in (3328, 1280, 640, 256, 128):
        if nk % tk == 0:
            return tk
    raise ValueError(f"unsupported key count {nk}")


def _attention(q_t, k_c, v_t, *, running_max):
    kv, da, gl = q_t.shape
    nk = k_c.shape[0]
    d = v_t.shape[0] // kv
    nq = GQA_GROUP * min(ATTN_Q_TILE, gl // GQA_GROUP)
    return pl.pallas_call(
        functools.partial(_attn_kernel, tk=_key_chunk(nk), running_max=running_max),
        grid=(kv, gl // nq),
        in_specs=[pl.BlockSpec((1, da, nq), lambda b, i: (b, 0, i)),
                  pl.BlockSpec((nk, da), lambda b, i: (0, b)),
                  pl.BlockSpec((d, nk), lambda b, i: (b, 0))],
        out_specs=pl.BlockSpec((1, d, nq), lambda b, i: (b, 0, i)),
        out_shape=jax.ShapeDtypeStruct((kv, d, gl), BF16),
        scratch_shapes=[pltpu.VMEM((1, nq), F32), pltpu.VMEM((1, nq), F32), pltpu.VMEM((d, nq), F32)],
        compiler_params=_cparams("arbitrary", "arbitrary"),
        name="flash_attn",
    )(q_t, k_c, v_t)


def _ret_kernel(gc_ref, q_ref, k_ref, v_ref, kc_ref, vc_ref, tab_ref, *rest, reverse, final):
    if final:
        of_ref, gate_ref, gng_ref, gnb_ref, out_ref, s_ref = rest
    else:
        out_ref, s_ref = rest
    c_rows = RET_CHUNK
    n_local = q_ref.shape[0] // c_rows
    n_ctx = kc_ref.shape[0] // c_rows
    lane_head = lax.broadcasted_iota(jnp.int32, (c_rows, RET_QK_WIDTH), 1) // RET_QK_DIM

    def state_update(kch, vch):
        for h in range(RET_HEADS):
            vz = (vch[:, h * RET_V_DIM:(h + 1) * RET_V_DIM].astype(F32)
                  * tab_ref[1, h, :, 0:RET_V_DIM]).astype(BF16)
            u = lax.dot_general(kch, vz, (((0,), (0,)), ((), ())), preferred_element_type=F32)
            s_ref[h] = gc_ref[h] * s_ref[h] + u

    @pl.when(pl.program_id(0) == 0)
    def _():
        s_ref[...] = jnp.zeros(s_ref.shape, F32)
        order = range(n_ctx - 1, -1, -1) if reverse else range(n_ctx)
        for cc in order:
            state_update(kc_ref[cc * c_rows:(cc + 1) * c_rows, :], vc_ref[cc * c_rows:(cc + 1) * c_rows, :])

    def chunk(ci, carry):
        c = (n_local - 1 - ci) if reverse else ci
        rows = pl.ds(pl.multiple_of(c * c_rows, c_rows), c_rows)
        q, k, v = q_ref[rows, :], k_ref[rows, :], v_ref[rows, :]
        for h in range(RET_HEADS):
            cols = slice(h * RET_V_DIM, (h + 1) * RET_V_DIM)
            qm = jnp.where(lane_head == h, q, jnp.zeros_like(q))
            sc = lax.dot_general(qm, k, (((1,), (1,)), ((), ())), preferred_element_type=F32)
            sc = sc * tab_ref[0, h]
            o = jnp.dot(sc.astype(BF16), v[:, cols], preferred_element_type=F32)
            o = o + (jnp.dot(qm, s_ref[h].astype(BF16), preferred_element_type=F32)
                     * tab_ref[2, h, :, 0:RET_V_DIM])
            if final:
                o = o + of_ref[rows, cols]
                mu = jnp.mean(o, axis=-1, keepdims=True)
                oc = o - mu
                var = jnp.mean(oc * oc, axis=-1, keepdims=True)
                on = oc * lax.rsqrt(var + EPS)
                gate = gate_ref[rows, cols]
                y = (on * gng_ref[:, cols] + gnb_ref[:, cols]) * (gate / (1.0 + jnp.exp(-gate)))
                out_ref[rows, cols] = y.astype(out_ref.dtype)
            else:
                out_ref[rows, cols] = o
        state_update(k, v)
        return carry

    lax.fori_loop(0, n_local, chunk, 0, unroll=RET_UNROLL)


def _retention_pass(gc, qr, kr, vr, kr_c, vr_c, tabs, extra, *, reverse):
    l = qr.shape[0]
    tr = min(RET_ROW_TILE, l)
    nt = l // tr
    final = extra is not None
    idx = (lambda j, gc_: (nt - 1 - j, 0)) if reverse else (lambda j, gc_: (j, 0))
    const = lambda shape: pl.BlockSpec(shape, lambda j, gc_: (0,) * len(shape))
    in_specs = [pl.BlockSpec((tr, RET_QK_WIDTH), idx), pl.BlockSpec((tr, RET_QK_WIDTH), idx),
                pl.BlockSpec((tr, RET_WIDTH), idx),
                const(kr_c.shape), const(vr_c.shape), const(tabs.shape)]
    args = [qr, kr, vr, kr_c, vr_c, tabs]
    if final:
        of, gate, gng, gnb = extra
        in_specs += [pl.BlockSpec((tr, RET_WIDTH), idx), pl.BlockSpec((tr, RET_WIDTH), idx),
                     const(gng.shape), const(gnb.shape)]
        args += [of, gate, gng, gnb]
    return pl.pallas_call(
        functools.partial(_ret_kernel, reverse=reverse, final=final),
        grid_spec=pltpu.PrefetchScalarGridSpec(
            num_scalar_prefetch=1, grid=(nt,), in_specs=in_specs,
            out_specs=pl.BlockSpec((tr, RET_WIDTH), idx),
            scratch_shapes=[pltpu.VMEM((RET_HEADS, RET_QK_WIDTH, RET_V_DIM), F32)]),
        out_shape=jax.ShapeDtypeStruct((l, RET_WIDTH), BF16 if final else F32),
        compiler_params=_cparams("arbitrary"),
        name="retention_bwd" if reverse else "retention_fwd",
    )(gc, *args)


def _retention_tables(log_g, reverse):
    c = RET_CHUNK
    pos = jnp.arange(c, dtype=F32)
    diff = (pos[None, :] - pos[:, None]) if reverse else (pos[:, None] - pos[None, :])
    mask = (diff > 0) if reverse else (diff >= 0)
    lg = log_g[:, None, None]
    dmat = jnp.where(mask, jnp.exp(lg * jnp.where(mask, diff, 0.0)[None]), 0.0)
    zeta = jnp.exp(log_g[:, None] * (pos if reverse else (c - 1 - pos))[None, :])
    xi = jnp.exp(log_g[:, None] * ((c - pos) if reverse else (pos + 1.0))[None, :])
    ones = jnp.ones((1, 1, c), F32)
    tabs = jnp.stack([dmat, zeta[:, :, None] * ones, xi[:, :, None] * ones])
    return tabs, jnp.exp(log_g * c)


def _outproj_kernel(x_ref, a_ref, r_ref, w_ref, gt_ref, g2_ref, sh_ref, sc_ref, wr_ref, br_ref,
                    xn_ref, h2_ref, ids_ref, wts_ref):
    tm = x_ref.shape[0]
    mix = jnp.dot(r_ref[...], w_ref[ATTN_WIDTH:, :], preferred_element_type=F32)
    per = LANES // HEAD_DIM
    for pair in range(ATTN_HEADS // per):
        pieces = []
        for j in range(per):
            kvh, grp = divmod(pair * per + j, GQA_GROUP)
            pieces.append(a_ref[kvh, :, grp * tm:(grp + 1) * tm])
        o_t = jnp.concatenate(pieces, axis=0)
        mix = mix + lax.dot_general(o_t, w_ref[pair * LANES:(pair + 1) * LANES, :],
                                    (((0,), (0,)), ((), ())), preferred_element_type=F32)
    x_new = x_ref[...] + gt_ref[...] * mix
    xn_ref[...] = x_new
    h2 = _rms_modulate(x_new, g2_ref[...], sh_ref[...], sc_ref[...])
    _store_row_tiles(h2_ref, (), h2)
    hi = h2.astype(BF16)
    lo = (h2 - hi.astype(F32)).astype(BF16)

    nt = (((1,), (1,)), ((), ()))
    logits = (lax.dot_general(wr_ref[0], hi, nt, preferred_element_type=F32)
              + lax.dot_general(wr_ref[0], lo, nt, preferred_element_type=F32)
              + lax.dot_general(wr_ref[1], hi, nt, preferred_element_type=F32)) + br_ref[...]
    rows = [logits[g:g + 1, :] for g in range(N_GROUPS)]
    gmax = functools.reduce(jnp.maximum, rows)
    gidx = jnp.full(gmax.shape, N_GROUPS - 1, jnp.int32)
    for g in range(N_GROUPS - 2, -1, -1):
        gidx = jnp.where(rows[g] == gmax, g, gidx)
    p_g = 1.0 / functools.reduce(jnp.add, [jnp.exp(r - gmax) for r in rows])
    sel = logits[EXPERT_ROW0 + (N_GROUPS - 1) * EXPERTS_PER_GROUP:EXPERT_ROW0 + N_GROUPS * EXPERTS_PER_GROUP, :]
    for g in range(N_GROUPS - 2, -1, -1):
        blk = logits[EXPERT_ROW0 + g * EXPERTS_PER_GROUP:EXPERT_ROW0 + (g + 1) * EXPERTS_PER_GROUP, :]
        sel = jnp.where(gidx == g, blk, sel)
    ridx = lax.broadcasted_iota(jnp.int32, sel.shape, 0)
    m1 = jnp.max(sel, axis=0, keepdims=True)
    i1 = jnp.min(jnp.where(sel == m1, ridx, EXPERTS_PER_GROUP), axis=0, keepdims=True)
    sel2 = jnp.where(ridx == i1, -jnp.inf, sel)
    m2 = jnp.max(sel2, axis=0, keepdims=True)
    i2 = jnp.min(jnp.where(sel2 == m2, ridx, EXPERTS_PER_GROUP), axis=0, keepdims=True)
    e2 = jnp.exp(m2 - m1)
    w1 = p_g / (1.0 + e2)
    w2 = w1 * e2
    out_rows = lax.broadcasted_iota(jnp.int32, ids_ref.shape, 0)
    ids_ref[...] = jnp.where(out_rows == 0, gidx * EXPERTS_PER_GROUP + i1,
                             jnp.where(out_rows == 1, gidx * EXPERTS_PER_GROUP + i2, 0))
    wts_ref[...] = jnp.where(out_rows == 0, w1, jnp.where(out_rows == 1, w2, 0.0))


def _outproj_router(x2, attn, ret, w_out_bf, gt1, g2, sh2, sc2, wr, br):
    l, d = x2.shape
    tm = min(ROW_TILE, l)
    row = lambda w: pl.BlockSpec((tm, w), lambda i: (i, 0))
    col = pl.BlockSpec((SUBLANES, tm), lambda i: (0, i))
    return pl.pallas_call(
        _outproj_kernel,
        grid=(l // tm,),
        in_specs=[row(d),
                  pl.BlockSpec((ATTN_KV_HEADS, HEAD_DIM, GQA_GROUP * tm), lambda i: (0, 0, i)),
                  row(ret.shape[1]), _const_spec(w_out_bf.shape),
                  _const_spec((1, d)), _const_spec((1, d)), _const_spec((1, d)), _const_spec((1, d)),
                  _const_spec(wr.shape), _const_spec(br.shape)],
        out_specs=[row(d), pl.BlockSpec((tm * SUBLANES, d // SUBLANES), lambda i: (i, 0)), col, col],
        out_shape=[jax.ShapeDtypeStruct((l, d), F32), jax.ShapeDtypeStruct((l * SUBLANES, d // SUBLANES), F32),
                   jax.ShapeDtypeStruct((SUBLANES, l), jnp.int32),
                   jax.ShapeDtypeStruct((SUBLANES, l), F32)],
        compiler_params=_cparams("arbitrary"),
        name="outproj_router",
    )(x2, attn, ret, w_out_bf, gt1, g2, sh2, sc2, wr, br)


def _row_copy(src_hbm, src_row, dst_ref, sem):
    return pltpu.make_async_copy(src_hbm.at[pl.ds(src_row * SUBLANES, SUBLANES), :], dst_ref, sem)


def _moe_kernel(be_ref, na_ref, tok0_ref, tok1_ref, tokn_ref, h_hbm, wg_ref, wu_ref, wd_ref, y_ref,
                wg_s, wu_s, wd_s, xg_s, sem):
    b = pl.program_id(0)
    slot = lax.rem(b, MOE_SLOTS)
    nxt = lax.rem(b + 2, MOE_SLOTS)
    n_active = na_ref[0]
    active = b < n_active
    last = be_ref.shape[0] - 1
    changed = jnp.logical_or(b == 0, be_ref[jnp.minimum(b, last)] != be_ref[jnp.clip(b - 1, 0, last)])

    def wait_rows():
        rows = pl.ds(0, MOE_BLOCK * SUBLANES)
        pltpu.make_async_copy(h_hbm.at[rows, :], xg_s.at[slot, rows, :], sem.at[slot]).wait()

    @pl.when(b == 0)
    def _():
        def body(r, carry):
            _row_copy(h_hbm, tok0_ref[0, 0, r], xg_s.at[0, pl.ds(r * SUBLANES, SUBLANES), :], sem.at[0]).start()
            _row_copy(h_hbm, tok1_ref[0, 0, r], xg_s.at[1, pl.ds(r * SUBLANES, SUBLANES), :], sem.at[1]).start()
            return carry

        lax.fori_loop(0, MOE_BLOCK, body, 0, unroll=8)

    @pl.when(jnp.logical_and(active, changed))
    def _():
        wg_s[...] = wg_ref[0].astype(BF16)
        wu_s[...] = wu_ref[0].astype(BF16)
        wd_s[...] = wd_ref[0].astype(BF16)

    @pl.when(active)
    def _():
        wait_rows()
        x = _load_row_tiles(xg_s, (slot,), MOE_BLOCK).astype(BF16)
        hg = jnp.dot(x, wg_s[...], preferred_element_type=F32)
        hu = jnp.dot(x, wu_s[...], preferred_element_type=F32)
        a = (hg / (1.0 + jnp.exp(-hg))) * hu
        _store_row_tiles(y_ref, (), jnp.dot(a.astype(BF16), wd_s[...], preferred_element_type=F32))
        for r in range(MOE_BLOCK):
            _row_copy(h_hbm, tokn_ref[0, 0, r], xg_s.at[nxt, pl.ds(r * SUBLANES, SUBLANES), :],
                      sem.at[nxt]).start(priority=r % 2)

    @pl.when(jnp.logical_or(b == n_active, b == n_active + 1))
    def _():
        wait_rows()

    @pl.when(jnp.logical_not(active))
    def _():
        y_ref[...] = jnp.zeros(y_ref.shape, y_ref.dtype)


def _moe_blocks(blk_e, n_active, buf_tok, h2, w_gate, w_up, w_down):
    ne, d, de = w_gate.shape
    tiled = (MOE_BLOCK * SUBLANES, d // SUBLANES)
    nblk = buf_tok.shape[0]
    blk = lambda b: jnp.minimum(b, nblk - 1)
    wmap = lambda b, be, na: (be[blk(b)], 0, 0)
    tokmap = lambda off: (lambda b, be, na: (blk(b + off), 0, 0))
    smem_tok = lambda off: pl.BlockSpec((1, 1, MOE_BLOCK), tokmap(off), memory_space=pltpu.SMEM)
    return pl.pallas_call(
        _moe_kernel,
        grid_spec=pltpu.PrefetchScalarGridSpec(
            num_scalar_prefetch=2, grid=(nblk + 1,),
            in_specs=[pl.BlockSpec((1, 1, MOE_BLOCK), lambda b, be, na: (0, 0, 0), memory_space=pltpu.SMEM),
                      pl.BlockSpec((1, 1, MOE_BLOCK), lambda b, be, na: (1, 0, 0), memory_space=pltpu.SMEM),
                      smem_tok(2),
                      pl.BlockSpec(memory_space=pl.ANY),
                      pl.BlockSpec((1, d, de), wmap), pl.BlockSpec((1, d, de), wmap),
                      pl.BlockSpec((1, de, d), wmap)],
            out_specs=pl.BlockSpec(tiled, lambda b, be, na: (blk(b), 0)),
            scratch_shapes=[pltpu.VMEM((d, de), BF16), pltpu.VMEM((d, de), BF16),
                            pltpu.VMEM((de, d), BF16), pltpu.VMEM((MOE_SLOTS,) + tiled, F32),
                            pltpu.SemaphoreType.DMA((MOE_SLOTS,))]),
        out_shape=jax.ShapeDtypeStruct((nblk * tiled[0], tiled[1]), F32),
        compiler_params=_cparams("arbitrary"),
        name="moe_experts",
    )(blk_e, n_active, buf_tok, buf_tok, buf_tok, h2, w_gate, w_up, w_down)


def _final_kernel(d0_ref, dn_ref, x_ref, w_ref, gt_ref, g_ref, y_hbm, o_ref, yg_s, sem):
    i = pl.program_id(0)
    slot = i % 2
    tm = x_ref.shape[0]

    def start_tile(d_ref, slot_):
        def body(r, carry):
            for k in range(TOP_K):
                _row_copy(y_hbm, d_ref[0, k, r], yg_s.at[slot_, k, pl.ds(r * SUBLANES, SUBLANES), :],
                          sem.at[slot_]).start(priority=k)
            return carry

        lax.fori_loop(0, tm, body, 0, unroll=8)

    @pl.when(i == 0)
    def _():
        start_tile(d0_ref, 0)

    @pl.when(i + 1 < pl.num_programs(0))
    def _():
        start_tile(dn_ref, 1 - slot)

    for k in range(TOP_K):
        pltpu.make_async_copy(y_hbm.at[pl.ds(0, tm * SUBLANES), :], yg_s.at[slot, k], sem.at[slot]).wait()
    w = w_ref[...]
    y = _load_row_tiles(yg_s, (slot, 0), tm) * w[:, 0:1] + _load_row_tiles(yg_s, (slot, 1), tm) * w[:, 1:2]
    x = x_ref[...] + gt_ref[...] * y
    ms = jnp.mean(x * x, axis=-1, keepdims=True)
    o_ref[...] = (x * lax.rsqrt(ms + EPS)) * g_ref[...]


def _final(x_new, yb, dest, wts, gt2, gfin):
    l, d = x_new.shape
    n, _, tm = dest.shape
    row = lambda w: pl.BlockSpec((tm, w), lambda i: (i, 0))
    return pl.pallas_call(
        _final_kernel,
        grid=(n,),
        in_specs=[pl.BlockSpec((1, TOP_K, tm), lambda i: (0, 0, 0), memory_space=pltpu.SMEM),
                  pl.BlockSpec((1, TOP_K, tm), lambda i: (jnp.minimum(i + 1, n - 1), 0, 0),
                               memory_space=pltpu.SMEM),
                  row(d), row(wts.shape[1]), _const_spec((1, d)), _const_spec((1, d)),
                  pl.BlockSpec(memory_space=pl.ANY)],
        out_specs=row(d),
        out_shape=jax.ShapeDtypeStruct((l, d), F32),
        scratch_shapes=[pltpu.VMEM((2, TOP_K, tm * SUBLANES, d // SUBLANES), F32), pltpu.SemaphoreType.DMA((2,))],
        compiler_params=_cparams("arbitrary"),
        name="combine_final_norm",
    )(dest, dest, x_new, wts, gt2, gfin, yb)


def _rope_tables(rows_count):
    lane = jnp.arange(LANES, dtype=jnp.int32) % HEAD_DIM
    pair = lane // 2
    n_freq = ROPE_AXIS_DIM // 2
    by_row = pair < n_freq
    fidx = jnp.where(by_row, pair, pair - n_freq).astype(F32)
    freq = ROPE_THETA ** (-(2.0 * fidx) / ROPE_AXIS_DIM)
    sign = jnp.where(lane % 2 == 0, -1.0, 1.0).astype(F32)
    ang_row = jnp.arange(rows_count, dtype=F32)[:, None] * freq
    ang_col = jnp.arange(GRID_W, dtype=F32)[:, None] * freq
    small = lax.optimization_barrier((jnp.cos(ang_row), jnp.cos(ang_col), jnp.sin(ang_row) * sign,
                                      jnp.sin(ang_col) * sign))
    cos_r, cos_c, sin_r, sin_c = small
    full = (rows_count * GRID_W, LANES)
    cos = jnp.where(by_row, cos_r[:, None, :], cos_c[None, :, :]).reshape(full)
    sin = jnp.where(by_row, sin_r[:, None, :], sin_c[None, :, :]).reshape(full)
    return cos, sin


def _dispatch(ids, n_tokens):
    a = TOP_K * n_tokens
    eid_f = ids[:TOP_K].reshape(a)
    tok_f = jnp.tile(jnp.arange(n_tokens, dtype=jnp.int32), TOP_K)
    onehot = (eid_f[:, None] == jnp.arange(N_EXPERTS, dtype=jnp.int32)[None, :]).astype(jnp.int32)
    incl = jnp.cumsum(onehot, axis=0)
    counts = incl[-1]
    rank = jnp.sum((incl - onehot) * onehot, axis=1)
    pcounts = (counts + MOE_BLOCK - 1) // MOE_BLOCK * MOE_BLOCK
    pends = jnp.cumsum(pcounts)
    pstarts = pends - pcounts
    dest = jnp.sum(onehot * pstarts[None, :], axis=1) + rank
    p = a + N_EXPERTS * MOE_BLOCK
    nblk = p // MOE_BLOCK
    n_fill = p - a
    fill_ends = jnp.cumsum(pcounts - counts)
    fill_idx = jnp.arange(n_fill, dtype=jnp.int32)
    fill_e = jnp.sum(fill_idx[:, None] >= fill_ends[None, :], axis=1).astype(jnp.int32)
    idx_bits = max(a, n_fill).bit_length()
    real_keys = (eid_f << (idx_bits + 1)) | jnp.arange(a, dtype=jnp.int32)
    fill_keys = (fill_e << (idx_bits + 1)) | (1 << idx_bits) | fill_idx
    keys = jnp.sort(jnp.concatenate([real_keys, fill_keys]))
    is_real = ((keys >> idx_bits) & 1) == 0
    buf_tok = jnp.where(is_real, (keys & ((1 << idx_bits) - 1)) % n_tokens, 0)
    del tok_f
    n_active = pends[-1] // MOE_BLOCK
    blk_start = jnp.arange(nblk, dtype=jnp.int32) * MOE_BLOCK
    blk_e = jnp.minimum(jnp.sum(blk_start[:, None] >= pends[None, :], axis=1), N_EXPERTS - 1)
    blk_e = jnp.where(jnp.arange(nblk) < n_active, blk_e, blk_e[n_active - 1]).astype(jnp.int32)
    return buf_tok, dest.reshape(TOP_K, n_tokens), blk_e, n_active.astype(jnp.int32).reshape(1)


def _split_bf16(w):
    hi = w.astype(BF16)
    return jnp.stack([hi, (w - hi.astype(F32)).astype(BF16)])


def kernel(x, c, ctx, c_ctx, w_ada, b_ada, norm1_g, w_in, attn_q_norm, attn_k_norm, ret_decay_fwd,
           ret_decay_bwd, ret_gn_g, ret_gn_b, w_out, norm2_g, moe_w_grp, moe_b_grp, moe_w_exp, moe_b_exp,
           moe_w_gate, moe_w_up, moe_w_down, final_norm_g):
    b, l, d = x.shape
    cl = ctx.shape[1]
    assert b == 1 and w_ada.shape[0] == 1, "single batch element, single layer"
    x2, ctx2 = x[0], ctx[0]

    cvecs = jnp.zeros((SUBLANES, d), F32).at[0].set(c[0]).at[1].set(c_ctx)
    mod = _ada(cvecs, w_ada[0], b_ada[0])
    sh1, sc1, gt1, sh2, sc2, gt2 = [mod[0:1, i * d:(i + 1) * d] for i in range(6)]
    csh1, csc1 = mod[1:2, 0:d], mod[1:2, d:2 * d]

    w_in_bf = w_in[0].astype(BF16)
    head_of = jnp.arange(ATTN_WIDTH) // HEAD_DIM
    bd = jnp.where(head_of[:, None] == head_of[None, :], 1.0 / HEAD_DIM, 0.0).astype(BF16)
    gq = jnp.tile(attn_q_norm[0], ATTN_HEADS)[None, :]
    gk = jnp.tile(attn_k_norm[0], ATTN_KV_HEADS)[None, :]
    g1 = norm1_g[0][None, :]
    cos, sin = _rope_tables(l // GRID_W)
    bound = (HEAD_DIM ** 0.5 * LOG2E * 1.01) * jnp.max(jnp.abs(attn_q_norm[0])) * jnp.max(jnp.abs(attn_k_norm[0]))
    bound = bound.astype(BF16).astype(F32)
    neg_bound = jnp.full((1, LANES), -1.0, F32) * bound
    q_t, ka, va, qr, kr, vr, gr = _inproj(x2, sh1, sc1, g1, w_in_bf, bd, gq, gk, cos, sin, neg_bound)
    _, ka_c, va_c, _, kr_c, vr_c, _ = _inproj(ctx2, csh1, csc1, g1, w_in_bf, bd, gq, gk,
                                              jnp.ones((cl, LANES), F32), jnp.zeros((cl, LANES), F32), neg_bound)

    k_all = jnp.concatenate([ka_c, ka], axis=0)
    attn_args = (q_t, k_all, jnp.concatenate([va_c, va], axis=1))
    o_t = lax.cond(bound <= MAX_UNSHIFTED_BOUND,
                   functools.partial(_attention, running_max=False),
                   functools.partial(_attention, running_max=True), *attn_args)

    log_gf = jax.nn.log_sigmoid(ret_decay_fwd[0].astype(F32))
    log_gb = jax.nn.log_sigmoid(ret_decay_bwd[0].astype(F32))
    tab_f, gc_f = _retention_tables(log_gf, False)
    tab_b, gc_b = _retention_tables(log_gb, True)
    o_f = _retention_pass(gc_f, qr, kr, vr, kr_c, vr_c, tab_f, None, reverse=False)
    ret = _retention_pass(gc_b, qr, kr, vr, kr_c, vr_c, tab_b,
                          (o_f, gr, ret_gn_g[0][None, :], ret_gn_b[0][None, :]), reverse=True)

    wr = jnp.zeros((ROUTER_ROWS, d), F32)
    wr = wr.at[:N_GROUPS].set(moe_w_grp[0].T).at[EXPERT_ROW0:EXPERT_ROW0 + N_EXPERTS].set(moe_w_exp[0].T)
    br = jnp.zeros((ROUTER_ROWS, 1), F32)
    br = br.at[:N_GROUPS, 0].set(moe_b_grp[0]).at[EXPERT_ROW0:EXPERT_ROW0 + N_EXPERTS, 0].set(moe_b_exp[0])
    x_new, h2, ids, wts = _outproj_router(x2, o_t, ret, w_out[0].astype(BF16), gt1, norm2_g[0][None, :],
                                          sh2, sc2, _split_bf16(wr), br)

    buf_tok, dest, blk_e, n_active = _dispatch(ids, l)
    yb = _moe_blocks(blk_e, n_active, buf_tok.reshape(-1, 1, MOE_BLOCK), h2, moe_w_gate[0], moe_w_up[0],
                     moe_w_down[0])
    tf = min(FINAL_TILE, l)
    dest_t = dest.reshape(TOP_K, l // tf, tf).transpose(1, 0, 2)
    out = _final(x_new, yb, dest_t, wts[:TOP_K].T, gt2, final_norm_g[None, :])
    return out[None]
```

```python
import functools

import jax
import jax.numpy as jnp
import numpy as np
from jax import lax
from jax.experimental import pallas as pl
from jax.experimental.pallas import tpu as pltpu

F32 = jnp.float32
BF16 = jnp.bfloat16

GRID_W = 64
HEAD_DIM = 64
ATTN_HEADS = 8
ATTN_KV_HEADS = 2
GQA_GROUP = ATTN_HEADS // ATTN_KV_HEADS
RET_HEADS = 4
RET_QK_DIM = 64
RET_V_DIM = 128
RET_CHUNK = 256
ATTN_WIDTH = ATTN_HEADS * HEAD_DIM
KV_WIDTH = ATTN_KV_HEADS * HEAD_DIM
RET_QK_WIDTH = RET_HEADS * RET_QK_DIM
RET_WIDTH = RET_HEADS * RET_V_DIM
PROJ_SIZES = (ATTN_WIDTH, KV_WIDTH, KV_WIDTH, RET_QK_WIDTH, RET_QK_WIDTH, RET_WIDTH, RET_WIDTH)
PROJ_OFFS = tuple(int(v) for v in np.cumsum((0,) + PROJ_SIZES))
PROJ_DIM = PROJ_OFFS[-1]
ROPE_THETA = 10000.0
ROPE_AXIS_DIM = HEAD_DIM // 2
N_GROUPS = 4
EXPERTS_PER_GROUP = 8
N_EXPERTS = N_GROUPS * EXPERTS_PER_GROUP
TOP_K = 2
D_EXPERT = 512
EPS = 1e-6
LOG2E = 1.4426950408889634
ATTN_AUG_DIM = 128
MAX_UNSHIFTED_BOUND = 60.0

LANES = 128
SUBLANES = 8
VMEM_LIMIT_BYTES = 56 * 1024 * 1024

ROW_TILE = 512
INPROJ_SUBTILE = 256
ATTN_Q_TILE = 512
ATTN_CHUNK_UNROLL = 5
ATTN_MAX_KEY_CHUNK = 4096
ADA_COL_TILE = 1536
BOUND_MARGIN = 1.01
RET_ROW_TILE = 1024
RET_UNROLL = 4
MOE_BLOCK = 256
MOE_SLOTS = 3
FINAL_TILE = 512
ROUTER_ROWS = 128
EXPERT_ROW0 = 8


def _cparams(*sem):
    return pltpu.CompilerParams(dimension_semantics=sem, vmem_limit_bytes=VMEM_LIMIT_BYTES)


def _const_spec(shape):
    nd = len(shape)
    return pl.BlockSpec(shape, lambda *_: (0,) * nd)


def _store_row_tiles(ref, idx, rows):
    n, w = rows.shape[0], rows.shape[1] // SUBLANES
    for s in range(SUBLANES):
        ref[idx + (pl.ds(s, n, stride=SUBLANES), slice(None))] = rows[:, s * w:(s + 1) * w]


def _load_row_tiles(ref, idx, n):
    return jnp.concatenate([ref[idx + (pl.ds(s, n, stride=SUBLANES), slice(None))] for s in range(SUBLANES)],
                           axis=1)


def _ada_kernel(s_ref, w_ref, b_ref, o_ref):
    s = s_ref[...]
    s = s / (1.0 + jnp.exp(-s))
    o_ref[...] = jnp.dot(s, w_ref[...], preferred_element_type=F32,
                         precision=lax.Precision.HIGHEST) + b_ref[...]


def _ada(cvecs, w_ada, b_ada):
    d, n = w_ada.shape
    tn = ADA_COL_TILE
    return pl.pallas_call(
        _ada_kernel,
        grid=(n // tn,),
        in_specs=[_const_spec((SUBLANES, d)),
                  pl.BlockSpec((d, tn), lambda j: (0, j)),
                  pl.BlockSpec((1, tn), lambda j: (0, j))],
        out_specs=pl.BlockSpec((SUBLANES, tn), lambda j: (0, j)),
        out_shape=jax.ShapeDtypeStruct((SUBLANES, n), F32),
        compiler_params=_cparams("arbitrary"),
        name="ada_mod",
    )(cvecs, w_ada, b_ada.reshape(1, n))


def _rms_modulate(x, g, sh, sc):
    ms = jnp.mean(x * x, axis=-1, keepdims=True)
    return (x * lax.rsqrt(ms + EPS)) * g * (1.0 + sc) + sh


def _head_mean_sq(v, bd):
    sq = v * v
    hi = sq.astype(BF16)
    lo = (sq - hi.astype(F32)).astype(BF16)
    return (jnp.dot(hi, bd, preferred_element_type=F32) + jnp.dot(lo, bd, preferred_element_type=F32))


def _rope_chunks(v, cos, sin, even):
    outs = []
    for c in range(v.shape[1] // LANES):
        xc = v[:, c * LANES:(c + 1) * LANES]
        nxt = pltpu.roll(xc, LANES - 1, 1)
        prv = pltpu.roll(xc, 1, 1)
        outs.append(xc * cos + jnp.where(even, nxt, prv) * sin)
    return outs


def _inproj_kernel(x_ref, sh_ref, sc_ref, g_ref, w_ref, bd_ref, gq_ref, gk_ref, cos_ref, sin_ref, nb_ref,
                   qt_ref, ka_ref, va_ref, qr_ref, kr_ref, vr_ref, gr_ref):
    tm = x_ref.shape[0]
    sub = min(INPROJ_SUBTILE, tm)
    bd = bd_ref[...]
    scale = HEAD_DIM ** -0.5 * LOG2E
    rscale = RET_QK_DIM ** -0.5
    o = PROJ_OFFS
    per = LANES // HEAD_DIM
    for j0 in range(0, tm, sub):
        rows = slice(j0, j0 + sub)
        h = _rms_modulate(x_ref[rows, :], g_ref[...], sh_ref[...], sc_ref[...])
        proj = jnp.dot(h.astype(BF16), w_ref[...], preferred_element_type=F32)
        qa, ka, va = proj[:, o[0]:o[1]], proj[:, o[1]:o[2]], proj[:, o[2]:o[3]]
        qr, kr, vr, gr = proj[:, o[3]:o[4]], proj[:, o[4]:o[5]], proj[:, o[5]:o[6]], proj[:, o[6]:o[7]]
        cos, sin = cos_ref[rows, :], sin_ref[rows, :]
        even = (lax.broadcasted_iota(jnp.int32, cos.shape, 1) % 2) == 0

        qn = qa * lax.rsqrt(_head_mean_sq(qa, bd) + EPS) * gq_ref[...]
        for c, blk in enumerate(_rope_chunks(qn, cos, sin, even)):
            blk_t = (blk * scale).T
            for j in range(per):
                kvh, grp = divmod(c * per + j, GQA_GROUP)
                qt_ref[kvh, 0:HEAD_DIM, grp * tm + j0:grp * tm + j0 + sub] = (
                    blk_t[j * HEAD_DIM:(j + 1) * HEAD_DIM, :].astype(BF16))
        kn = ka * lax.rsqrt(_head_mean_sq(ka, bd[:KV_WIDTH, :KV_WIDTH]) + EPS) * gk_ref[...]
        k_rot = _rope_chunks(kn, cos, sin, even)[0].astype(BF16)
        pad_shape = (sub, ATTN_AUG_DIM - HEAD_DIM)
        k_pad = jnp.where(lax.broadcasted_iota(jnp.int32, pad_shape, 1) == 0, 1.0, 0.0).astype(BF16)
        for kvh in range(ATTN_KV_HEADS):
            c0 = kvh * ATTN_AUG_DIM
            ka_ref[rows, c0:c0 + HEAD_DIM] = k_rot[:, kvh * HEAD_DIM:(kvh + 1) * HEAD_DIM]
            ka_ref[rows, c0 + HEAD_DIM:c0 + ATTN_AUG_DIM] = k_pad
        va_ref[:, rows] = va.T.astype(BF16)
        for c, blk in enumerate(_rope_chunks(qr, cos, sin, even)):
            qr_ref[rows, c * LANES:(c + 1) * LANES] = blk.astype(BF16)
        for c, blk in enumerate(_rope_chunks(kr, cos, sin, even)):
            kr_ref[rows, c * LANES:(c + 1) * LANES] = (blk * rscale).astype(BF16)
        vr_ref[rows, :] = vr.astype(BF16)
        gr_ref[rows, :] = gr
    aug_shape = (ATTN_AUG_DIM - HEAD_DIM, GQA_GROUP * tm)
    aug_row = lax.broadcasted_iota(jnp.int32, aug_shape, 0)
    aug = jnp.where(aug_row == 0, nb_ref[0:1, 0:1], 0.0).astype(BF16)
    for kvh in range(ATTN_KV_HEADS):
        qt_ref[kvh, HEAD_DIM:ATTN_AUG_DIM, :] = aug


def _inproj(x2, sh, sc, g1, w_bf, bd, gq, gk, cos, sin, neg_bound):
    rows, d = x2.shape
    tm = min(ROW_TILE, rows)
    row = lambda w: pl.BlockSpec((tm, w), lambda i: (i, 0))
    row_specs = [row(ATTN_KV_HEADS * ATTN_AUG_DIM), pl.BlockSpec((KV_WIDTH, tm), lambda i: (0, i))]
    row_specs += [row(w) for w in PROJ_SIZES[3:]]
    row_shapes = [(rows, ATTN_KV_HEADS * ATTN_AUG_DIM), (KV_WIDTH, rows)] + [(rows, w) for w in PROJ_SIZES[3:]]
    dtypes = (BF16,) * 5 + (F32,)
    qt_shape = (ATTN_KV_HEADS, ATTN_AUG_DIM, GQA_GROUP * rows)
    return pl.pallas_call(
        _inproj_kernel,
        grid=(rows // tm,),
        in_specs=[row(d), _const_spec((1, d)), _const_spec((1, d)), _const_spec((1, d)),
                  _const_spec(w_bf.shape), _const_spec(bd.shape),
                  _const_spec((1, ATTN_WIDTH)), _const_spec((1, KV_WIDTH)),
                  row(LANES), row(LANES), _const_spec((1, LANES))],
        out_specs=[pl.BlockSpec((ATTN_KV_HEADS, ATTN_AUG_DIM, GQA_GROUP * tm), lambda i: (0, 0, i))] + row_specs,
        out_shape=[jax.ShapeDtypeStruct(qt_shape, BF16)]
        + [jax.ShapeDtypeStruct(shp, dt) for shp, dt in zip(row_shapes, dtypes)],
        compiler_params=_cparams("arbitrary"),
        name="norm_inproj",
    )(x2, sh, sc, g1, w_bf, bd, gq, gk, cos, sin, neg_bound)


def _attn_kernel(q_ref, k_ref, v_ref, o_ref, m_ref, l_ref, acc_ref, *, tk, running_max):
    m_ref[...] = jnp.full(m_ref.shape, -jnp.inf, F32)
    l_ref[...] = jnp.zeros(l_ref.shape, F32)
    acc_ref[...] = jnp.zeros(acc_ref.shape, F32)

    n_chunks = k_ref.shape[0] // tk

    def body(c, carry):
        keys = pl.ds(pl.multiple_of(c * tk, tk), tk)
        kc = k_ref[keys, :]
        vc = v_ref[:, keys]
        s = jnp.dot(kc, q_ref[0], preferred_element_type=F32)
        if running_max:
            m_old = m_ref[...]
            m_new = jnp.maximum(m_old, jnp.max(s, axis=0, keepdims=True))
            alpha = jnp.exp2(m_old - m_new)
            p = jnp.exp2(s - m_new)
            l_ref[...] = alpha * l_ref[...] + jnp.sum(p, axis=0, keepdims=True)
            acc_ref[...] = acc_ref[...] * alpha + jnp.dot(vc, p.astype(BF16), preferred_element_type=F32)
            m_ref[...] = m_new
        else:
            p = jnp.exp2(s)
            l_ref[...] += jnp.sum(p, axis=0, keepdims=True)
            acc_ref[...] += jnp.dot(vc, p.astype(BF16), preferred_element_type=F32)
        return carry

    lax.fori_loop(0, n_chunks, body, 0, unroll=1 if running_max else min(ATTN_CHUNK_UNROLL, n_chunks))
    o_ref[0] = (acc_ref[...] / l_ref[...]).astype(o_ref.dtype)


def _key_chunk(nk):
    for tk in range(min(nk, ATTN_MAX_KEY_CHUNK) // LANES * LANES, 0, -LANES):
        if nk % tk == 0:
            return tk
    raise ValueError(f"unsupported key count {nk}")


def _attention(q_t, k_c, v_t, *, running_max):
    kv, da, gl = q_t.shape
    nk = k_c.shape[0]
    d = v_t.shape[0] // kv
    nq = GQA_GROUP * min(ATTN_Q_TILE, gl // GQA_GROUP)
    return pl.pallas_call(
        functools.partial(_attn_kernel, tk=_key_chunk(nk), running_max=running_max),
        grid=(kv, gl // nq),
        in_specs=[pl.BlockSpec((1, da, nq), lambda b, i: (b, 0, i)),
                  pl.BlockSpec((nk, da), lambda b, i: (0, b)),
                  pl.BlockSpec((d, nk), lambda b, i: (b, 0))],
        out_specs=pl.BlockSpec((1, d, nq), lambda b, i: (b, 0, i)),
        out_shape=jax.ShapeDtypeStruct((kv, d, gl), BF16),
        scratch_shapes=[pltpu.VMEM((1, nq), F32), pltpu.VMEM((1, nq), F32), pltpu.VMEM((d, nq), F32)],
        compiler_params=_cparams("arbitrary", "arbitrary"),
        name="flash_attn",
    )(q_t, k_c, v_t)


def _ret_kernel(gc_ref, q_ref, k_ref, v_ref, kc_ref, vc_ref, tab_ref, *rest, reverse, final):
    if final:
        of_ref, gate_ref, gng_ref, gnb_ref, out_ref, s_ref = rest
    else:
        out_ref, s_ref = rest
    c_rows = RET_CHUNK
    n_local = q_ref.shape[0] // c_rows
    n_ctx = kc_ref.shape[0] // c_rows
    lane_head = lax.broadcasted_iota(jnp.int32, (c_rows, RET_QK_WIDTH), 1) // RET_QK_DIM

    def state_update(kch, vch):
        for h in range(RET_HEADS):
            vz = (vch[:, h * RET_V_DIM:(h + 1) * RET_V_DIM].astype(F32)
                  * tab_ref[1, h, :, 0:RET_V_DIM]).astype(BF16)
            u = lax.dot_general(kch, vz, (((0,), (0,)), ((), ())), preferred_element_type=F32)
            s_ref[h] = gc_ref[h] * s_ref[h] + u

    @pl.when(pl.program_id(0) == 0)
    def _():
        s_ref[...] = jnp.zeros(s_ref.shape, F32)
        order = range(n_ctx - 1, -1, -1) if reverse else range(n_ctx)
        for cc in order:
            state_update(kc_ref[cc * c_rows:(cc + 1) * c_rows, :], vc_ref[cc * c_rows:(cc + 1) * c_rows, :])

    def chunk(ci, carry):
        c = (n_local - 1 - ci) if reverse else ci
        rows = pl.ds(pl.multiple_of(c * c_rows, c_rows), c_rows)
        q, k, v = q_ref[rows, :], k_ref[rows, :], v_ref[rows, :]
        for h in range(RET_HEADS):
            cols = slice(h * RET_V_DIM, (h + 1) * RET_V_DIM)
            qm = jnp.where(lane_head == h, q, jnp.zeros_like(q))
            sc = lax.dot_general(qm, k, (((1,), (1,)), ((), ())), preferred_element_type=F32)
            sc = sc * tab_ref[0, h]
            o = jnp.dot(sc.astype(BF16), v[:, cols], preferred_element_type=F32)
            o = o + (jnp.dot(qm, s_ref[h].astype(BF16), preferred_element_type=F32)
                     * tab_ref[2, h, :, 0:RET_V_DIM])
            if final:
                o = o + of_ref[rows, cols]
                mu = jnp.mean(o, axis=-1, keepdims=True)
                oc = o - mu
                var = jnp.mean(oc * oc, axis=-1, keepdims=True)
                on = oc * lax.rsqrt(var + EPS)
                gate = gate_ref[rows, cols]
                y = (on * gng_ref[:, cols] + gnb_ref[:, cols]) * (gate / (1.0 + jnp.exp(-gate)))
                out_ref[rows, cols] = y.astype(out_ref.dtype)
            else:
                out_ref[rows, cols] = o
        state_update(k, v)
        return carry

    lax.fori_loop(0, n_local, chunk, 0, unroll=RET_UNROLL)


def _retention_pass(gc, qr, kr, vr, kr_c, vr_c, tabs, extra, *, reverse):
    l = qr.shape[0]
    tr = min(RET_ROW_TILE, l)
    nt = l // tr
    final = extra is not None
    idx = (lambda j, gc_: (nt - 1 - j, 0)) if reverse else (lambda j, gc_: (j, 0))
    const = lambda shape: pl.BlockSpec(shape, lambda j, gc_: (0,) * len(shape))
    in_specs = [pl.BlockSpec((tr, RET_QK_WIDTH), idx), pl.BlockSpec((tr, RET_QK_WIDTH), idx),
                pl.BlockSpec((tr, RET_WIDTH), idx),
                const(kr_c.shape), const(vr_c.shape), const(tabs.shape)]
    args = [qr, kr, vr, kr_c, vr_c, tabs]
    if final:
        of, gate, gng, gnb = extra
        in_specs += [pl.BlockSpec((tr, RET_WIDTH), idx), pl.BlockSpec((tr, RET_WIDTH), idx),
                     const(gng.shape), const(gnb.shape)]
        args += [of, gate, gng, gnb]
    return pl.pallas_call(
        functools.partial(_ret_kernel, reverse=reverse, final=final),
        grid_spec=pltpu.PrefetchScalarGridSpec(
            num_scalar_prefetch=1, grid=(nt,), in_specs=in_specs,
            out_specs=pl.BlockSpec((tr, RET_WIDTH), idx),
            scratch_shapes=[pltpu.VMEM((RET_HEADS, RET_QK_WIDTH, RET_V_DIM), F32)]),
        out_shape=jax.ShapeDtypeStruct((l, RET_WIDTH), BF16 if final else F32),
        compiler_params=_cparams("arbitrary"),
        name="retention_bwd" if reverse else "retention_fwd",
    )(gc, *args)


def _retention_tables(log_g, reverse):
    c = RET_CHUNK
    pos = jnp.arange(c, dtype=F32)
    diff = (pos[None, :] - pos[:, None]) if reverse else (pos[:, None] - pos[None, :])
    mask = (diff > 0) if reverse else (diff >= 0)
    lg = log_g[:, None, None]
    dmat = jnp.where(mask, jnp.exp(lg * jnp.where(mask, diff, 0.0)[None]), 0.0)
    zeta = jnp.exp(log_g[:, None] * (pos if reverse else (c - 1 - pos))[None, :])
    xi = jnp.exp(log_g[:, None] * ((c - pos) if reverse else (pos + 1.0))[None, :])
    ones = jnp.ones((1, 1, c), F32)
    tabs = jnp.stack([dmat, zeta[:, :, None] * ones, xi[:, :, None] * ones])
    return tabs, jnp.exp(log_g * c)


def _outproj_kernel(x_ref, a_ref, r_ref, w_ref, gt_ref, g2_ref, sh_ref, sc_ref, wr_ref, br_ref,
                    xn_ref, h2_ref, ids_ref, wts_ref):
    tm = x_ref.shape[0]
    mix = jnp.dot(r_ref[...], w_ref[ATTN_WIDTH:, :], preferred_element_type=F32)
    per = LANES // HEAD_DIM
    for pair in range(ATTN_HEADS // per):
        pieces = []
        for j in range(per):
            kvh, grp = divmod(pair * per + j, GQA_GROUP)
            pieces.append(a_ref[kvh, :, grp * tm:(grp + 1) * tm])
        o_t = jnp.concatenate(pieces, axis=0)
        mix = mix + lax.dot_general(o_t, w_ref[pair * LANES:(pair + 1) * LANES, :],
                                    (((0,), (0,)), ((), ())), preferred_element_type=F32)
    x_new = x_ref[...] + gt_ref[...] * mix
    xn_ref[...] = x_new
    h2 = _rms_modulate(x_new, g2_ref[...], sh_ref[...], sc_ref[...])
    _store_row_tiles(h2_ref, (), h2)
    hi = h2.astype(BF16)
    lo = (h2 - hi.astype(F32)).astype(BF16)

    nt = (((1,), (1,)), ((), ()))
    logits = (lax.dot_general(wr_ref[0], hi, nt, preferred_element_type=F32)
              + lax.dot_general(wr_ref[0], lo, nt, preferred_element_type=F32)
              + lax.dot_general(wr_ref[1], hi, nt, preferred_element_type=F32)) + br_ref[...]
    rows = [logits[g:g + 1, :] for g in range(N_GROUPS)]
    gmax = functools.reduce(jnp.maximum, rows)
    gidx = jnp.full(gmax.shape, N_GROUPS - 1, jnp.int32)
    for g in range(N_GROUPS - 2, -1, -1):
        gidx = jnp.where(rows[g] == gmax, g, gidx)
    p_g = 1.0 / functools.reduce(jnp.add, [jnp.exp(r - gmax) for r in rows])
    sel = logits[EXPERT_ROW0 + (N_GROUPS - 1) * EXPERTS_PER_GROUP:EXPERT_ROW0 + N_GROUPS * EXPERTS_PER_GROUP, :]
    for g in range(N_GROUPS - 2, -1, -1):
        blk = logits[EXPERT_ROW0 + g * EXPERTS_PER_GROUP:EXPERT_ROW0 + (g + 1) * EXPERTS_PER_GROUP, :]
        sel = jnp.where(gidx == g, blk, sel)
    ridx = lax.broadcasted_iota(jnp.int32, sel.shape, 0)
    m1 = jnp.max(sel, axis=0, keepdims=True)
    i1 = jnp.min(jnp.where(sel == m1, ridx, EXPERTS_PER_GROUP), axis=0, keepdims=True)
    sel2 = jnp.where(ridx == i1, -jnp.inf, sel)
    m2 = jnp.max(sel2, axis=0, keepdims=True)
    i2 = jnp.min(jnp.where(sel2 == m2, ridx, EXPERTS_PER_GROUP), axis=0, keepdims=True)
    e2 = jnp.exp(m2 - m1)
    w1 = p_g / (1.0 + e2)
    w2 = w1 * e2
    out_rows = lax.broadcasted_iota(jnp.int32, ids_ref.shape, 0)
    ids_ref[...] = jnp.where(out_rows == 0, gidx * EXPERTS_PER_GROUP + i1,
                             jnp.where(out_rows == 1, gidx * EXPERTS_PER_GROUP + i2, 0))
    wts_ref[...] = jnp.where(out_rows == 0, w1, jnp.where(out_rows == 1, w2, 0.0))


def _outproj_router(x2, attn, ret, w_out_bf, gt1, g2, sh2, sc2, wr, br):
    l, d = x2.shape
    tm = min(ROW_TILE, l)
    row = lambda w: pl.BlockSpec((tm, w), lambda i: (i, 0))
    col = pl.BlockSpec((SUBLANES, tm), lambda i: (0, i))
    return pl.pallas_call(
        _outproj_kernel,
        grid=(l // tm,),
        in_specs=[row(d),
                  pl.BlockSpec((ATTN_KV_HEADS, HEAD_DIM, GQA_GROUP * tm), lambda i: (0, 0, i)),
                  row(ret.shape[1]), _const_spec(w_out_bf.shape),
                  _const_spec((1, d)), _const_spec((1, d)), _const_spec((1, d)), _const_spec((1, d)),
                  _const_spec(wr.shape), _const_spec(br.shape)],
        out_specs=[row(d), pl.BlockSpec((tm * SUBLANES, d // SUBLANES), lambda i: (i, 0)), col, col],
        out_shape=[jax.ShapeDtypeStruct((l, d), F32), jax.ShapeDtypeStruct((l * SUBLANES, d // SUBLANES), F32),
                   jax.ShapeDtypeStruct((SUBLANES, l), jnp.int32),
                   jax.ShapeDtypeStruct((SUBLANES, l), F32)],
        compiler_params=_cparams("arbitrary"),
        name="outproj_router",
    )(x2, attn, ret, w_out_bf, gt1, g2, sh2, sc2, wr, br)


def _row_copy(src_hbm, src_row, dst_ref, sem):
    return pltpu.make_async_copy(src_hbm.at[pl.ds(src_row * SUBLANES, SUBLANES), :], dst_ref, sem)


def _moe_kernel(be_ref, na_ref, tok0_ref, tok1_ref, tokn_ref, h_hbm, wg_ref, wu_ref, wd_ref, y_ref,
                wg_s, wu_s, wd_s, xg_s, sem):
    b = pl.program_id(0)
    slot = lax.rem(b, MOE_SLOTS)
    nxt = lax.rem(b + 2, MOE_SLOTS)
    n_active = na_ref[0]
    active = b < n_active
    last = be_ref.shape[0] - 1
    changed = jnp.logical_or(b == 0, be_ref[jnp.minimum(b, last)] != be_ref[jnp.clip(b - 1, 0, last)])

    def wait_rows():
        rows = pl.ds(0, MOE_BLOCK * SUBLANES)
        pltpu.make_async_copy(h_hbm.at[rows, :], xg_s.at[slot, rows, :], sem.at[slot]).wait()

    @pl.when(b == 0)
    def _():
        def body(r, carry):
            _row_copy(h_hbm, tok0_ref[0, 0, r], xg_s.at[0, pl.ds(r * SUBLANES, SUBLANES), :], sem.at[0]).start()
            _row_copy(h_hbm, tok1_ref[0, 0, r], xg_s.at[1, pl.ds(r * SUBLANES, SUBLANES), :], sem.at[1]).start()
            return carry

        lax.fori_loop(0, MOE_BLOCK, body, 0, unroll=8)

    @pl.when(jnp.logical_and(active, changed))
    def _():
        wg_s[...] = wg_ref[0].astype(BF16)
        wu_s[...] = wu_ref[0].astype(BF16)
        wd_s[...] = wd_ref[0].astype(BF16)

    @pl.when(active)
    def _():
        wait_rows()
        x = _load_row_tiles(xg_s, (slot,), MOE_BLOCK).astype(BF16)
        hg = jnp.dot(x, wg_s[...], preferred_element_type=F32)
        hu = jnp.dot(x, wu_s[...], preferred_element_type=F32)
        a = (hg / (1.0 + jnp.exp(-hg))) * hu
        _store_row_tiles(y_ref, (), jnp.dot(a.astype(BF16), wd_s[...], preferred_element_type=F32))
        for r in range(MOE_BLOCK):
            _row_copy(h_hbm, tokn_ref[0, 0, r], xg_s.at[nxt, pl.ds(r * SUBLANES, SUBLANES), :],
                      sem.at[nxt]).start(priority=r % 2)

    @pl.when(jnp.logical_or(b == n_active, b == n_active + 1))
    def _():
        wait_rows()

    @pl.when(jnp.logical_not(active))
    def _():
        y_ref[...] = jnp.zeros(y_ref.shape, y_ref.dtype)


def _moe_blocks(blk_e, n_active, buf_tok, h2, w_gate, w_up, w_down):
    ne, d, de = w_gate.shape
    tiled = (MOE_BLOCK * SUBLANES, d // SUBLANES)
    nblk = buf_tok.shape[0]
    blk = lambda b: jnp.minimum(b, nblk - 1)
    wmap = lambda b, be, na: (be[blk(b)], 0, 0)
    tokmap = lambda off: (lambda b, be, na: (blk(b + off), 0, 0))
    smem_tok = lambda off: pl.BlockSpec((1, 1, MOE_BLOCK), tokmap(off), memory_space=pltpu.SMEM)
    return pl.pallas_call(
        _moe_kernel,
        grid_spec=pltpu.PrefetchScalarGridSpec(
            num_scalar_prefetch=2, grid=(nblk + 1,),
            in_specs=[pl.BlockSpec((1, 1, MOE_BLOCK), lambda b, be, na: (0, 0, 0), memory_space=pltpu.SMEM),
                      pl.BlockSpec((1, 1, MOE_BLOCK), lambda b, be, na: (1, 0, 0), memory_space=pltpu.SMEM),
                      smem_tok(2),
                      pl.BlockSpec(memory_space=pl.ANY),
                      pl.BlockSpec((1, d, de), wmap), pl.BlockSpec((1, d, de), wmap),
                      pl.BlockSpec((1, de, d), wmap)],
            out_specs=pl.BlockSpec(tiled, lambda b, be, na: (blk(b), 0)),
            scratch_shapes=[pltpu.VMEM((d, de), BF16), pltpu.VMEM((d, de), BF16),
                            pltpu.VMEM((de, d), BF16), pltpu.VMEM((MOE_SLOTS,) + tiled, F32),
                            pltpu.SemaphoreType.DMA((MOE_SLOTS,))]),
        out_shape=jax.ShapeDtypeStruct((nblk * tiled[0], tiled[1]), F32),
        compiler_params=_cparams("arbitrary"),
        name="moe_experts",
    )(blk_e, n_active, buf_tok, buf_tok, buf_tok, h2, w_gate, w_up, w_down)


def _final_kernel(d0_ref, dn_ref, x_ref, w_ref, gt_ref, g_ref, y_hbm, o_ref, yg_s, sem):
    i = pl.program_id(0)
    slot = i % 2
    tm = x_ref.shape[0]

    def start_tile(d_ref, slot_):
        def body(r, carry):
            for k in range(TOP_K):
                _row_copy(y_hbm, d_ref[0, k, r], yg_s.at[slot_, k, pl.ds(r * SUBLANES, SUBLANES), :],
                          sem.at[slot_]).start(priority=k)
            return carry

        lax.fori_loop(0, tm, body, 0, unroll=8)

    @pl.when(i == 0)
    def _():
        start_tile(d0_ref, 0)

    @pl.when(i + 1 < pl.num_programs(0))
    def _():
        start_tile(dn_ref, 1 - slot)

    for k in range(TOP_K):
        pltpu.make_async_copy(y_hbm.at[pl.ds(0, tm * SUBLANES), :], yg_s.at[slot, k], sem.at[slot]).wait()
    w = w_ref[...]
    y = _load_row_tiles(yg_s, (slot, 0), tm) * w[:, 0:1] + _load_row_tiles(yg_s, (slot, 1), tm) * w[:, 1:2]
    x = x_ref[...] + gt_ref[...] * y
    ms = jnp.mean(x * x, axis=-1, keepdims=True)
    o_ref[...] = (x * lax.rsqrt(ms + EPS)) * g_ref[...]


def _final(x_new, yb, dest, wts, gt2, gfin):
    l, d = x_new.shape
    n, _, tm = dest.shape
    row = lambda w: pl.BlockSpec((tm, w), lambda i: (i, 0))
    return pl.pallas_call(
        _final_kernel,
        grid=(n,),
        in_specs=[pl.BlockSpec((1, TOP_K, tm), lambda i: (0, 0, 0), memory_space=pltpu.SMEM),
                  pl.BlockSpec((1, TOP_K, tm), lambda i: (jnp.minimum(i + 1, n - 1), 0, 0),
                               memory_space=pltpu.SMEM),
                  row(d), row(wts.shape[1]), _const_spec((1, d)), _const_spec((1, d)),
                  pl.BlockSpec(memory_space=pl.ANY)],
        out_specs=row(d),
        out_shape=jax.ShapeDtypeStruct((l, d), F32),
        scratch_shapes=[pltpu.VMEM((2, TOP_K, tm * SUBLANES, d // SUBLANES), F32), pltpu.SemaphoreType.DMA((2,))],
        compiler_params=_cparams("arbitrary"),
        name="combine_final_norm",
    )(dest, dest, x_new, wts, gt2, gfin, yb)


def _rope_tables(rows_count):
    lane = jnp.arange(LANES, dtype=jnp.int32) % HEAD_DIM
    pair = lane // 2
    n_freq = ROPE_AXIS_DIM // 2
    by_row = pair < n_freq
    fidx = jnp.where(by_row, pair, pair - n_freq).astype(F32)
    freq = ROPE_THETA ** (-(2.0 * fidx) / ROPE_AXIS_DIM)
    sign = jnp.where(lane % 2 == 0, -1.0, 1.0).astype(F32)
    ang_row = jnp.arange(rows_count, dtype=F32)[:, None] * freq
    ang_col = jnp.arange(GRID_W, dtype=F32)[:, None] * freq
    small = lax.optimization_barrier((jnp.cos(ang_row), jnp.cos(ang_col), jnp.sin(ang_row) * sign,
                                      jnp.sin(ang_col) * sign))
    cos_r, cos_c, sin_r, sin_c = small
    full = (rows_count * GRID_W, LANES)
    cos = jnp.where(by_row, cos_r[:, None, :], cos_c[None, :, :]).reshape(full)
    sin = jnp.where(by_row, sin_r[:, None, :], sin_c[None, :, :]).reshape(full)
    return cos, sin


def _dispatch(ids, n_tokens):
    a = TOP_K * n_tokens
    eid_f = ids[:TOP_K].reshape(a)
    tok_f = jnp.tile(jnp.arange(n_tokens, dtype=jnp.int32), TOP_K)
    onehot = (eid_f[:, None] == jnp.arange(N_EXPERTS, dtype=jnp.int32)[None, :]).astype(jnp.int32)
    incl = jnp.cumsum(onehot, axis=0)
    counts = incl[-1]
    rank = jnp.sum((incl - onehot) * onehot, axis=1)
    pcounts = (counts + MOE_BLOCK - 1) // MOE_BLOCK * MOE_BLOCK
    pends = jnp.cumsum(pcounts)
    pstarts = pends - pcounts
    dest = jnp.sum(onehot * pstarts[None, :], axis=1) + rank
    p = a + N_EXPERTS * MOE_BLOCK
    nblk = p // MOE_BLOCK
    n_fill = p - a
    fill_ends = jnp.cumsum(pcounts - counts)
    fill_idx = jnp.arange(n_fill, dtype=jnp.int32)
    fill_e = jnp.sum(fill_idx[:, None] >= fill_ends[None, :], axis=1).astype(jnp.int32)
    idx_bits = max(a, n_fill).bit_length()
    real_keys = (eid_f << (idx_bits + 1)) | jnp.arange(a, dtype=jnp.int32)
    fill_keys = (fill_e << (idx_bits + 1)) | (1 << idx_bits) | fill_idx
    keys = jnp.sort(jnp.concatenate([real_keys, fill_keys]))
    is_real = ((keys >> idx_bits) & 1) == 0
    buf_tok = jnp.where(is_real, (keys & ((1 << idx_bits) - 1)) % n_tokens, 0)
    del tok_f
    n_active = pends[-1] // MOE_BLOCK
    blk_start = jnp.arange(nblk, dtype=jnp.int32) * MOE_BLOCK
    blk_e = jnp.minimum(jnp.sum(blk_start[:, None] >= pends[None, :], axis=1), N_EXPERTS - 1)
    blk_e = jnp.where(jnp.arange(nblk) < n_active, blk_e, blk_e[n_active - 1]).astype(jnp.int32)
    return buf_tok, dest.reshape(TOP_K, n_tokens), blk_e, n_active.astype(jnp.int32).reshape(1)


def _split_bf16(w):
    hi = w.astype(BF16)
    return jnp.stack([hi, (w - hi.astype(F32)).astype(BF16)])


def kernel(x, c, ctx, c_ctx, w_ada, b_ada, norm1_g, w_in, attn_q_norm, attn_k_norm, ret_decay_fwd,
           ret_decay_bwd, ret_gn_g, ret_gn_b, w_out, norm2_g, moe_w_grp, moe_b_grp, moe_w_exp, moe_b_exp,
           moe_w_gate, moe_w_up, moe_w_down, final_norm_g):
    b, l, d = x.shape
    cl = ctx.shape[1]
    assert b == 1 and w_ada.shape[0] == 1, "single batch element, single layer"
    x2, ctx2 = x[0], ctx[0]

    cvecs = jnp.zeros((SUBLANES, d), F32).at[0].set(c[0]).at[1].set(c_ctx)
    mod = _ada(cvecs, w_ada[0], b_ada[0])
    sh1, sc1, gt1, sh2, sc2, gt2 = [mod[0:1, i * d:(i + 1) * d] for i in range(6)]
    csh1, csc1 = mod[1:2, 0:d], mod[1:2, d:2 * d]

    w_in_bf = w_in[0].astype(BF16)
    head_of = jnp.arange(ATTN_WIDTH) // HEAD_DIM
    bd = jnp.where(head_of[:, None] == head_of[None, :], 1.0 / HEAD_DIM, 0.0).astype(BF16)
    gq = jnp.tile(attn_q_norm[0], ATTN_HEADS)[None, :]
    gk = jnp.tile(attn_k_norm[0], ATTN_KV_HEADS)[None, :]
    g1 = norm1_g[0][None, :]
    cos, sin = _rope_tables(l // GRID_W)
    bound = (HEAD_DIM ** 0.5 * LOG2E * BOUND_MARGIN) * jnp.max(jnp.abs(attn_q_norm[0])) * jnp.max(jnp.abs(attn_k_norm[0]))
    bound = bound.astype(BF16).astype(F32)
    neg_bound = jnp.full((1, LANES), -1.0, F32) * bound
    q_t, ka, va, qr, kr, vr, gr = _inproj(x2, sh1, sc1, g1, w_in_bf, bd, gq, gk, cos, sin, neg_bound)
    _, ka_c, va_c, _, kr_c, vr_c, _ = _inproj(ctx2, csh1, csc1, g1, w_in_bf, bd, gq, gk,
                                              jnp.ones((cl, LANES), F32), jnp.zeros((cl, LANES), F32), neg_bound)

    k_all = jnp.concatenate([ka_c, ka], axis=0)
    attn_args = (q_t, k_all, jnp.concatenate([va_c, va], axis=1))
    o_t = lax.cond(bound <= MAX_UNSHIFTED_BOUND,
                   functools.partial(_attention, running_max=False),
                   functools.partial(_attention, running_max=True), *attn_args)

    log_gf = jax.nn.log_sigmoid(ret_decay_fwd[0].astype(F32))
    log_gb = jax.nn.log_sigmoid(ret_decay_bwd[0].astype(F32))
    tab_f, gc_f = _retention_tables(log_gf, False)
    tab_b, gc_b = _retention_tables(log_gb, True)
    o_f = _retention_pass(gc_f, qr, kr, vr, kr_c, vr_c, tab_f, None, reverse=False)
    ret = _retention_pass(gc_b, qr, kr, vr, kr_c, vr_c, tab_b,
                          (o_f, gr, ret_gn_g[0][None, :], ret_gn_b[0][None, :]), reverse=True)

    wr = jnp.zeros((ROUTER_ROWS, d), F32)
    wr = wr.at[:N_GROUPS].set(moe_w_grp[0].T).at[EXPERT_ROW0:EXPERT_ROW0 + N_EXPERTS].set(moe_w_exp[0].T)
    br = jnp.zeros((ROUTER_ROWS, 1), F32)
    br = br.at[:N_GROUPS, 0].set(moe_b_grp[0]).at[EXPERT_ROW0:EXPERT_ROW0 + N_EXPERTS, 0].set(moe_b_exp[0])
    x_new, h2, ids, wts = _outproj_router(x2, o_t, ret, w_out[0].astype(BF16), gt1, norm2_g[0][None, :],
                                          sh2, sc2, _split_bf16(wr), br)

    buf_tok, dest, blk_e, n_active = _dispatch(ids, l)
    yb = _moe_blocks(blk_e, n_active, buf_tok.reshape(-1, 1, MOE_BLOCK), h2, moe_w_gate[0], moe_w_up[0],
                     moe_w_down[0])
    tf = min(FINAL_TILE, l)
    dest_t = dest.reshape(TOP_K, l // tf, tf).transpose(1, 0, 2)
    out = _final(x_new, yb, dest_t, wts[:TOP_K].T, gt2, final_norm_g[None, :])
    return out[None]
```

```python
import functools

import jax
import jax.numpy as jnp
import numpy as np
from jax import lax
from jax.experimental import pallas as pl
from jax.experimental.pallas import tpu as pltpu

F32 = jnp.float32
BF16 = jnp.bfloat16

GRID_W = 64
HEAD_DIM = 64
ATTN_HEADS = 8
ATTN_KV_HEADS = 2
GQA_GROUP = ATTN_HEADS // ATTN_KV_HEADS
RET_HEADS = 4
RET_QK_DIM = 64
RET_V_DIM = 128
RET_CHUNK = 256
ATTN_WIDTH = ATTN_HEADS * HEAD_DIM
KV_WIDTH = ATTN_KV_HEADS * HEAD_DIM
RET_QK_WIDTH = RET_HEADS * RET_QK_DIM
RET_WIDTH = RET_HEADS * RET_V_DIM
PROJ_SIZES = (ATTN_WIDTH, KV_WIDTH, KV_WIDTH, RET_QK_WIDTH, RET_QK_WIDTH, RET_WIDTH, RET_WIDTH)
PROJ_OFFS = tuple(int(v) for v in np.cumsum((0,) + PROJ_SIZES))
PROJ_DIM = PROJ_OFFS[-1]
ROPE_THETA = 10000.0
ROPE_AXIS_DIM = HEAD_DIM // 2
N_GROUPS = 4
EXPERTS_PER_GROUP = 8
N_EXPERTS = N_GROUPS * EXPERTS_PER_GROUP
TOP_K = 2
D_EXPERT = 512
EPS = 1e-6
LOG2E = 1.4426950408889634
ATTN_AUG_DIM = 128
MAX_UNSHIFTED_BOUND = 60.0

LANES = 128
SUBLANES = 8
VMEM_LIMIT_BYTES = 56 * 1024 * 1024

ROW_TILE = 512
INPROJ_SUBTILE = 256
ATTN_Q_TILE = 512
ATTN_CHUNK_UNROLL = 5
ATTN_MAX_KEY_CHUNK = 4096
ADA_COL_TILE = 1536
BOUND_MARGIN = 1.01
RET_ROW_TILE = 1024
RET_UNROLL = 4
MOE_BLOCK = 256
MOE_SLOTS = 3
FINAL_TILE = 512
ROUTER_ROWS = 128
EXPERT_ROW0 = 8


def _cparams(*sem):
    return pltpu.CompilerParams(dimension_semantics=sem, vmem_limit_bytes=VMEM_LIMIT_BYTES)


def _const_spec(shape):
    nd = len(shape)
    return pl.BlockSpec(shape, lambda *_: (0,) * nd)


def _store_row_tiles(ref, idx, rows):
    n, w = rows.shape[0], rows.shape[1] // SUBLANES
    for s in range(SUBLANES):
        ref[idx + (pl.ds(s, n, stride=SUBLANES), slice(None))] = rows[:, s * w:(s + 1) * w]


def _load_row_tiles(ref, idx, n):
    return jnp.concatenate([ref[idx + (pl.ds(s, n, stride=SUBLANES), slice(None))] for s in range(SUBLANES)],
                           axis=1)


def _ada_kernel(s_ref, w_ref, b_ref, o_ref):
    s = s_ref[...]
    s = s / (1.0 + jnp.exp(-s))
    o_ref[...] = jnp.dot(s, w_ref[...], preferred_element_type=F32,
                         precision=lax.Precision.HIGHEST) + b_ref[...]


def _ada(cvecs, w_ada, b_ada):
    d, n = w_ada.shape
    tn = ADA_COL_TILE
    return pl.pallas_call(
        _ada_kernel,
        grid=(n // tn,),
        in_specs=[_const_spec((SUBLANES, d)),
                  pl.BlockSpec((d, tn), lambda j: (0, j)),
                  pl.BlockSpec((1, tn), lambda j: (0, j))],
        out_specs=pl.BlockSpec((SUBLANES, tn), lambda j: (0, j)),
        out_shape=jax.ShapeDtypeStruct((SUBLANES, n), F32),
        compiler_params=_cparams("arbitrary"),
        name="ada_mod",
    )(cvecs, w_ada, b_ada.reshape(1, n))


def _rms_modulate(x, g, sh, sc):
    ms = jnp.mean(x * x, axis=-1, keepdims=True)
    return (x * lax.rsqrt(ms + EPS)) * g * (1.0 + sc) + sh


def _head_mean_sq(v, bd):
    sq = v * v
    hi = sq.astype(BF16)
    lo = (sq - hi.astype(F32)).astype(BF16)
    return (jnp.dot(hi, bd, preferred_element_type=F32) + jnp.dot(lo, bd, preferred_element_type=F32))


def _rope_chunks(v, cos, sin, even):
    outs = []
    for c in range(v.shape[1] // LANES):
        xc = v[:, c * LANES:(c + 1) * LANES]
        nxt = pltpu.roll(xc, LANES - 1, 1)
        prv = pltpu.roll(xc, 1, 1)
        outs.append(xc * cos + jnp.where(even, nxt, prv) * sin)
    return outs


def _inproj_kernel(x_ref, sh_ref, sc_ref, g_ref, w_ref, bd_ref, gq_ref, gk_ref, cos_ref, sin_ref, nb_ref,
                   qt_ref, ka_ref, va_ref, qr_ref, kr_ref, vr_ref, gr_ref):
    tm = x_ref.shape[0]
    sub = min(INPROJ_SUBTILE, tm)
    bd = bd_ref[...]
    scale = HEAD_DIM ** -0.5 * LOG2E
    rscale = RET_QK_DIM ** -0.5
    o = PROJ_OFFS
    per = LANES // HEAD_DIM
    for j0 in range(0, tm, sub):
        rows = slice(j0, j0 + sub)
        h = _rms_modulate(x_ref[rows, :], g_ref[...], sh_ref[...], sc_ref[...])
        proj = jnp.dot(h.astype(BF16), w_ref[...], preferred_element_type=F32)
        qa, ka, va = proj[:, o[0]:o[1]], proj[:, o[1]:o[2]], proj[:, o[2]:o[3]]
        qr, kr, vr, gr = proj[:, o[3]:o[4]], proj[:, o[4]:o[5]], proj[:, o[5]:o[6]], proj[:, o[6]:o[7]]
        cos, sin = cos_ref[rows, :], sin_ref[rows, :]
        even = (lax.broadcasted_iota(jnp.int32, cos.shape, 1) % 2) == 0

        qn = qa * lax.rsqrt(_head_mean_sq(qa, bd) + EPS) * gq_ref[...]
        for c, blk in enumerate(_rope_chunks(qn, cos, sin, even)):
            blk_t = (blk * scale).T
            for j in range(per):
                kvh, grp = divmod(c * per + j, GQA_GROUP)
                qt_ref[kvh, 0:HEAD_DIM, grp * tm + j0:grp * tm + j0 + sub] = (
                    blk_t[j * HEAD_DIM:(j + 1) * HEAD_DIM, :].astype(BF16))
        kn = ka * lax.rsqrt(_head_mean_sq(ka, bd[:KV_WIDTH, :KV_WIDTH]) + EPS) * gk_ref[...]
        k_rot = _rope_chunks(kn, cos, sin, even)[0].astype(BF16)
        pad_shape = (sub, ATTN_AUG_DIM - HEAD_DIM)
        k_pad = jnp.where(lax.broadcasted_iota(jnp.int32, pad_shape, 1) == 0, 1.0, 0.0).astype(BF16)
        for kvh in range(ATTN_KV_HEADS):
            c0 = kvh * ATTN_AUG_DIM
            ka_ref[rows, c0:c0 + HEAD_DIM] = k_rot[:, kvh * HEAD_DIM:(kvh + 1) * HEAD_DIM]
            ka_ref[rows, c0 + HEAD_DIM:c0 + ATTN_AUG_DIM] = k_pad
        va_ref[:, rows] = va.T.astype(BF16)
        for c, blk in enumerate(_rope_chunks(qr, cos, sin, even)):
            qr_ref[rows, c * LANES:(c + 1) * LANES] = blk.astype(BF16)
        for c, blk in enumerate(_rope_chunks(kr, cos, sin, even)):
            kr_ref[rows, c * LANES:(c + 1) * LANES] = (blk * rscale).astype(BF16)
        vr_ref[rows, :] = vr.astype(BF16)
        gr_ref[rows, :] = gr
    aug_shape = (ATTN_AUG_DIM - HEAD_DIM, GQA_GROUP * tm)
    aug_row = lax.broadcasted_iota(jnp.int32, aug_shape, 0)
    aug = jnp.where(aug_row == 0, nb_ref[0:1, 0:1], 0.0).astype(BF16)
    for kvh in range(ATTN_KV_HEADS):
        qt_ref[kvh, HEAD_DIM:ATTN_AUG_DIM, :] = aug


def _inproj(x2, sh, sc, g1, w_bf, bd, gq, gk, cos, sin, neg_bound):
    rows, d = x2.shape
    tm = min(ROW_TILE, rows)
    row = lambda w: pl.BlockSpec((tm, w), lambda i: (i, 0))
    row_specs = [row(ATTN_KV_HEADS * ATTN_AUG_DIM), pl.BlockSpec((KV_WIDTH, tm), lambda i: (0, i))]
    row_specs += [row(w) for w in PROJ_SIZES[3:]]
    row_shapes = [(rows, ATTN_KV_HEADS * ATTN_AUG_DIM), (KV_WIDTH, rows)] + [(rows, w) for w in PROJ_SIZES[3:]]
    dtypes = (BF16,) * 5 + (F32,)
    qt_shape = (ATTN_KV_HEADS, ATTN_AUG_DIM, GQA_GROUP * rows)
    return pl.pallas_call(
        _inproj_kernel,
        grid=(rows // tm,),
        in_specs=[row(d), _const_spec((1, d)), _const_spec((1, d)), _const_spec((1, d)),
                  _const_spec(w_bf.shape), _const_spec(bd.shape),
                  _const_spec((1, ATTN_WIDTH)), _const_spec((1, KV_WIDTH)),
                  row(LANES), row(LANES), _const_spec((1, LANES))],
        out_specs=[pl.BlockSpec((ATTN_KV_HEADS, ATTN_AUG_DIM, GQA_GROUP * tm), lambda i: (0, 0, i))] + row_specs,
        out_shape=[jax.ShapeDtypeStruct(qt_shape, BF16)]
        + [jax.ShapeDtypeStruct(shp, dt) for shp, dt in zip(row_shapes, dtypes)],
        compiler_params=_cparams("arbitrary"),
        name="norm_inproj",
    )(x2, sh, sc, g1, w_bf, bd, gq, gk, cos, sin, neg_bound)


def _attn_kernel(q_ref, k_ref, v_ref, o_ref, m_ref, l_ref, acc_ref, *, tk, running_max):
    m_ref[...] = jnp.full(m_ref.shape, -jnp.inf, F32)
    l_ref[...] = jnp.zeros(l_ref.shape, F32)
    acc_ref[...] = jnp.zeros(acc_ref.shape, F32)

    n_chunks = k_ref.shape[0] // tk

    def body(c, carry):
        keys = pl.ds(pl.multiple_of(c * tk, tk), tk)
        kc = k_ref[keys, :]
        vc = v_ref[:, keys]
        s = jnp.dot(kc, q_ref[0], preferred_element_type=F32)
        if running_max:
            m_old = m_ref[...]
            m_new = jnp.maximum(m_old, jnp.max(s, axis=0, keepdims=True))
            alpha = jnp.exp2(m_old - m_new)
            p = jnp.exp2(s - m_new)
            l_ref[...] = alpha * l_ref[...] + jnp.sum(p, axis=0, keepdims=True)
            acc_ref[...] = acc_ref[...] * alpha + jnp.dot(vc, p.astype(BF16), preferred_element_type=F32)
            m_ref[...] = m_new
        else:
            p = jnp.exp2(s)
            l_ref[...] += jnp.sum(p, axis=0, keepdims=True)
            acc_ref[...] += jnp.dot(vc, p.astype(BF16), preferred_element_type=F32)
        return carry

    lax.fori_loop(0, n_chunks, body, 0, unroll=1 if running_max else min(ATTN_CHUNK_UNROLL, n_chunks))
    o_ref[0] = (acc_ref[...] / l_ref[...]).astype(o_ref.dtype)


def _key_chunk(nk):
    for tk in range(min(nk, ATTN_MAX_KEY_CHUNK) // LANES * LANES, 0, -LANES):
        if nk % tk == 0:
            return tk
    raise ValueError(f"unsupported key count {nk}")


def _attention(q_t, k_c, v_t, *, running_max):
    kv, da, gl = q_t.shape
    nk = k_c.shape[0]
    d = v_t.shape[0] // kv
    nq = GQA_GROUP * min(ATTN_Q_TILE, gl // GQA_GROUP)
    return pl.pallas_call(
        functools.partial(_attn_kernel, tk=_key_chunk(nk), running_max=running_max),
        grid=(kv, gl // nq),
        in_specs=[pl.BlockSpec((1, da, nq), lambda b, i: (b, 0, i)),
                  pl.BlockSpec((nk, da), lambda b, i: (0, b)),
                  pl.BlockSpec((d, nk), lambda b, i: (b, 0))],
        out_specs=pl.BlockSpec((1, d, nq), lambda b, i: (b, 0, i)),
        out_shape=jax.ShapeDtypeStruct((kv, d, gl), BF16),
        scratch_shapes=[pltpu.VMEM((1, nq), F32), pltpu.VMEM((1, nq), F32), pltpu.VMEM((d, nq), F32)],
        compiler_params=_cparams("arbitrary", "arbitrary"),
        name="flash_attn",
    )(q_t, k_c, v_t)


def _ret_kernel(gc_ref, q_ref, k_ref, v_ref, kc_ref, vc_ref, tab_ref, *rest, reverse, final):
    if final:
        of_ref, gate_ref, gng_ref, gnb_ref, out_ref, s_ref = rest
    else:
        out_ref, s_ref = rest
    c_rows = RET_CHUNK
    n_local = q_ref.shape[0] // c_rows
    n_ctx = kc_ref.shape[0] // c_rows
    lane_head = lax.broadcasted_iota(jnp.int32, (c_rows, RET_QK_WIDTH), 1) // RET_QK_DIM

    def state_update(kch, vch):
        for h in range(RET_HEADS):
            vz = (vch[:, h * RET_V_DIM:(h + 1) * RET_V_DIM].astype(F32)
                  * tab_ref[1, h, :, 0:RET_V_DIM]).astype(BF16)
            u = lax.dot_general(kch, vz, (((0,), (0,)), ((), ())), preferred_element_type=F32)
            s_ref[h] = gc_ref[h] * s_ref[h] + u

    @pl.when(pl.program_id(0) == 0)
    def _():
        s_ref[...] = jnp.zeros(s_ref.shape, F32)
        order = range(n_ctx - 1, -1, -1) if reverse else range(n_ctx)
        for cc in order:
            state_update(kc_ref[cc * c_rows:(cc + 1) * c_rows, :], vc_ref[cc * c_rows:(cc + 1) * c_rows, :])

    def chunk(ci, carry):
        c = (n_local - 1 - ci) if reverse else ci
        rows = pl.ds(pl.multiple_of(c * c_rows, c_rows), c_rows)
        q, k, v = q_ref[rows, :], k_ref[rows, :], v_ref[rows, :]
        for h in range(RET_HEADS):
            cols = slice(h * RET_V_DIM, (h + 1) * RET_V_DIM)
            qm = jnp.where(lane_head == h, q, jnp.zeros_like(q))
            sc = lax.dot_general(qm, k, (((1,), (1,)), ((), ())), preferred_element_type=F32)
            sc = sc * tab_ref[0, h]
            o = jnp.dot(sc.astype(BF16), v[:, cols], preferred_element_type=F32)
            o = o + (jnp.dot(qm, s_ref[h].astype(BF16), preferred_element_type=F32)
                     * tab_ref[2, h, :, 0:RET_V_DIM])
            if final:
                o = o + of_ref[rows, cols]
                mu = jnp.mean(o, axis=-1, keepdims=True)
                oc = o - mu
                var = jnp.mean(oc * oc, axis=-1, keepdims=True)
                on = oc * lax.rsqrt(var + EPS)
                gate = gate_ref[rows, cols]
                y = (on * gng_ref[:, cols] + gnb_ref[:, cols]) * (gate / (1.0 + jnp.exp(-gate)))
                out_ref[rows, cols] = y.astype(out_ref.dtype)
            else:
                out_ref[rows, cols] = o
        state_update(k, v)
        return carry

    lax.fori_loop(0, n_local, chunk, 0, unroll=RET_UNROLL)


def _retention_pass(gc, qr, kr, vr, kr_c, vr_c, tabs, extra, *, reverse):
    l = qr.shape[0]
    tr = min(RET_ROW_TILE, l)
    nt = l // tr
    final = extra is not None
    idx = (lambda j, gc_: (nt - 1 - j, 0)) if reverse else (lambda j, gc_: (j, 0))
    const = lambda shape: pl.BlockSpec(shape, lambda j, gc_: (0,) * len(shape))
    in_specs = [pl.BlockSpec((tr, RET_QK_WIDTH), idx), pl.BlockSpec((tr, RET_QK_WIDTH), idx),
                pl.BlockSpec((tr, RET_WIDTH), idx),
                const(kr_c.shape), const(vr_c.shape), const(tabs.shape)]
    args = [qr, kr, vr, kr_c, vr_c, tabs]
    if final:
        of, gate, gng, gnb = extra
        in_specs += [pl.BlockSpec((tr, RET_WIDTH), idx), pl.BlockSpec((tr, RET_WIDTH), idx),
                     const(gng.shape), const(gnb.shape)]
        args += [of, gate, gng, gnb]
    return pl.pallas_call(
        functools.partial(_ret_kernel, reverse=reverse, final=final),
        grid_spec=pltpu.PrefetchScalarGridSpec(
            num_scalar_prefetch=1, grid=(nt,), in_specs=in_specs,
            out_specs=pl.BlockSpec((tr, RET_WIDTH), idx),
            scratch_shapes=[pltpu.VMEM((RET_HEADS, RET_QK_WIDTH, RET_V_DIM), F32)]),
        out_shape=jax.ShapeDtypeStruct((l, RET_WIDTH), BF16 if final else F32),
        compiler_params=_cparams("arbitrary"),
        name="retention_bwd" if reverse else "retention_fwd",
    )(gc, *args)


def _retention_tables(log_g, reverse):
    c = RET_CHUNK
    pos = jnp.arange(c, dtype=F32)
    diff = (pos[None, :] - pos[:, None]) if reverse else (pos[:, None] - pos[None, :])
    mask = (diff > 0) if reverse else (diff >= 0)
    lg = log_g[:, None, None]
    dmat = jnp.where(mask, jnp.exp(lg * jnp.where(mask, diff, 0.0)[None]), 0.0)
    zeta = jnp.exp(log_g[:, None] * (pos if reverse else (c - 1 - pos))[None, :])
    xi = jnp.exp(log_g[:, None] * ((c - pos) if reverse else (pos + 1.0))[None, :])
    ones = jnp.ones((1, 1, c), F32)
    tabs = jnp.stack([dmat, zeta[:, :, None] * ones, xi[:, :, None] * ones])
    return tabs, jnp.exp(log_g * c)


def _outproj_kernel(x_ref, a_ref, r_ref, w_ref, gt_ref, g2_ref, sh_ref, sc_ref, wr_ref, br_ref,
                    xn_ref, h2_ref, ids_ref, wts_ref):
    tm = x_ref.shape[0]
    pieces = []
    for head in range(ATTN_HEADS):
        kvh, grp = divmod(head, GQA_GROUP)
        pieces.append(a_ref[kvh, :, grp * tm:(grp + 1) * tm])
    o_t = jnp.concatenate(pieces, axis=0)
    mix = (jnp.dot(r_ref[...], w_ref[ATTN_WIDTH:, :], preferred_element_type=F32)
           + lax.dot_general(o_t, w_ref[:ATTN_WIDTH, :], (((0,), (0,)), ((), ())),
                             preferred_element_type=F32))
    x_new = x_ref[...] + gt_ref[...] * mix
    xn_ref[...] = x_new
    h2 = _rms_modulate(x_new, g2_ref[...], sh_ref[...], sc_ref[...])
    _store_row_tiles(h2_ref, (), h2)
    hi = h2.astype(BF16)
    lo = (h2 - hi.astype(F32)).astype(BF16)

    nt = (((1,), (1,)), ((), ()))
    logits = (lax.dot_general(wr_ref[0], hi, nt, preferred_element_type=F32)
              + lax.dot_general(wr_ref[0], lo, nt, preferred_element_type=F32)
              + lax.dot_general(wr_ref[1], hi, nt, preferred_element_type=F32)) + br_ref[...]
    rows = [logits[g:g + 1, :] for g in range(N_GROUPS)]
    gmax = functools.reduce(jnp.maximum, rows)
    gidx = jnp.full(gmax.shape, N_GROUPS - 1, jnp.int32)
    for g in range(N_GROUPS - 2, -1, -1):
        gidx = jnp.where(rows[g] == gmax, g, gidx)
    p_g = 1.0 / functools.reduce(jnp.add, [jnp.exp(r - gmax) for r in rows])
    sel = logits[EXPERT_ROW0 + (N_GROUPS - 1) * EXPERTS_PER_GROUP:EXPERT_ROW0 + N_GROUPS * EXPERTS_PER_GROUP, :]
    for g in range(N_GROUPS - 2, -1, -1):
        blk = logits[EXPERT_ROW0 + g * EXPERTS_PER_GROUP:EXPERT_ROW0 + (g + 1) * EXPERTS_PER_GROUP, :]
        sel = jnp.where(gidx == g, blk, sel)
    ridx = lax.broadcasted_iota(jnp.int32, sel.shape, 0)
    m1 = jnp.max(sel, axis=0, keepdims=True)
    i1 = jnp.min(jnp.where(sel == m1, ridx, EXPERTS_PER_GROUP), axis=0, keepdims=True)
    sel2 = jnp.where(ridx == i1, -jnp.inf, sel)
    m2 = jnp.max(sel2, axis=0, keepdims=True)
    i2 = jnp.min(jnp.where(sel2 == m2, ridx, EXPERTS_PER_GROUP), axis=0, keepdims=True)
    e2 = jnp.exp(m2 - m1)
    w1 = p_g / (1.0 + e2)
    w2 = w1 * e2
    out_rows = lax.broadcasted_iota(jnp.int32, ids_ref.shape, 0)
    ids_ref[...] = jnp.where(out_rows == 0, gidx * EXPERTS_PER_GROUP + i1,
                             jnp.where(out_rows == 1, gidx * EXPERTS_PER_GROUP + i2, 0))
    wts_ref[...] = jnp.where(out_rows == 0, w1, jnp.where(out_rows == 1, w2, 0.0))


def _outproj_router(x2, attn, ret, w_out_bf, gt1, g2, sh2, sc2, wr, br):
    l, d = x2.shape
    tm = min(ROW_TILE, l)
    row = lambda w: pl.BlockSpec((tm, w), lambda i: (i, 0))
    col = pl.BlockSpec((SUBLANES, tm), lambda i: (0, i))
    return pl.pallas_call(
        _outproj_kernel,
        grid=(l // tm,),
        in_specs=[row(d),
                  pl.BlockSpec((ATTN_KV_HEADS, HEAD_DIM, GQA_GROUP * tm), lambda i: (0, 0, i)),
                  row(ret.shape[1]), _const_spec(w_out_bf.shape),
                  _const_spec((1, d)), _const_spec((1, d)), _const_spec((1, d)), _const_spec((1, d)),
                  _const_spec(wr.shape), _const_spec(br.shape)],
        out_specs=[row(d), pl.BlockSpec((tm * SUBLANES, d // SUBLANES), lambda i: (i, 0)), col, col],
        out_shape=[jax.ShapeDtypeStruct((l, d), F32), jax.ShapeDtypeStruct((l * SUBLANES, d // SUBLANES), F32),
                   jax.ShapeDtypeStruct((SUBLANES, l), jnp.int32),
                   jax.ShapeDtypeStruct((SUBLANES, l), F32)],
        compiler_params=_cparams("arbitrary"),
        name="outproj_router",
    )(x2, attn, ret, w_out_bf, gt1, g2, sh2, sc2, wr, br)


def _row_copy(src_hbm, src_row, dst_ref, sem):
    return pltpu.make_async_copy(src_hbm.at[pl.ds(src_row * SUBLANES, SUBLANES), :], dst_ref, sem)


def _moe_kernel(be_ref, na_ref, tok0_ref, tok1_ref, tokn_ref, h_hbm, wg_ref, wu_ref, wd_ref, y_ref,
                wg_s, wu_s, wd_s, xg_s, sem):
    b = pl.program_id(0)
    slot = lax.rem(b, MOE_SLOTS)
    nxt = lax.rem(b + 2, MOE_SLOTS)
    n_active = na_ref[0]
    active = b < n_active
    last = be_ref.shape[0] - 1
    changed = jnp.logical_or(b == 0, be_ref[jnp.minimum(b, last)] != be_ref[jnp.clip(b - 1, 0, last)])

    def wait_rows():
        rows = pl.ds(0, MOE_BLOCK * SUBLANES)
        pltpu.make_async_copy(h_hbm.at[rows, :], xg_s.at[slot, rows, :], sem.at[slot]).wait()

    @pl.when(b == 0)
    def _():
        def body(r, carry):
            _row_copy(h_hbm, tok0_ref[0, 0, r], xg_s.at[0, pl.ds(r * SUBLANES, SUBLANES), :], sem.at[0]).start()
            _row_copy(h_hbm, tok1_ref[0, 0, r], xg_s.at[1, pl.ds(r * SUBLANES, SUBLANES), :], sem.at[1]).start()
            return carry

        lax.fori_loop(0, MOE_BLOCK, body, 0, unroll=8)

    @pl.when(jnp.logical_and(active, changed))
    def _():
        wg_s[...] = wg_ref[0].astype(BF16)
        wu_s[...] = wu_ref[0].astype(BF16)
        wd_s[...] = wd_ref[0].astype(BF16)

    @pl.when(active)
    def _():
        wait_rows()
        x = _load_row_tiles(xg_s, (slot,), MOE_BLOCK).astype(BF16)
        hg = jnp.dot(x, wg_s[...], preferred_element_type=F32)
        hu = jnp.dot(x, wu_s[...], preferred_element_type=F32)
        a = (hg / (1.0 + jnp.exp(-hg))) * hu
        _store_row_tiles(y_ref, (), jnp.dot(a.astype(BF16), wd_s[...], preferred_element_type=F32))
        for r in range(MOE_BLOCK):
            _row_copy(h_hbm, tokn_ref[0, 0, r], xg_s.at[nxt, pl.ds(r * SUBLANES, SUBLANES), :],
                      sem.at[nxt]).start(priority=r % 2)

    @pl.when(jnp.logical_or(b == n_active, b == n_active + 1))
    def _():
        wait_rows()

    @pl.when(jnp.logical_not(active))
    def _():
        y_ref[...] = jnp.zeros(y_ref.shape, y_ref.dtype)


def _moe_blocks(blk_e, n_active, buf_tok, h2, w_gate, w_up, w_down):
    ne, d, de = w_gate.shape
    tiled = (MOE_BLOCK * SUBLANES, d // SUBLANES)
    nblk = buf_tok.shape[0]
    blk = lambda b: jnp.minimum(b, nblk - 1)
    wmap = lambda b, be, na: (be[blk(b)], 0, 0)
    tokmap = lambda off: (lambda b, be, na: (blk(b + off), 0, 0))
    smem_tok = lambda off: pl.BlockSpec((1, 1, MOE_BLOCK), tokmap(off), memory_space=pltpu.SMEM)
    return pl.pallas_call(
        _moe_kernel,
        grid_spec=pltpu.PrefetchScalarGridSpec(
            num_scalar_prefetch=2, grid=(nblk + 1,),
            in_specs=[pl.BlockSpec((1, 1, MOE_BLOCK), lambda b, be, na: (0, 0, 0), memory_space=pltpu.SMEM),
                      pl.BlockSpec((1, 1, MOE_BLOCK), lambda b, be, na: (1, 0, 0), memory_space=pltpu.SMEM),
                      smem_tok(2),
                      pl.BlockSpec(memory_space=pl.ANY),
                      pl.BlockSpec((1, d, de), wmap), pl.BlockSpec((1, d, de), wmap),
                      pl.BlockSpec((1, de, d), wmap)],
            out_specs=pl.BlockSpec(tiled, lambda b, be, na: (blk(b), 0)),
            scratch_shapes=[pltpu.VMEM((d, de), BF16), pltpu.VMEM((d, de), BF16),
                            pltpu.VMEM((de, d), BF16), pltpu.VMEM((MOE_SLOTS,) + tiled, F32),
                            pltpu.SemaphoreType.DMA((MOE_SLOTS,))]),
        out_shape=jax.ShapeDtypeStruct((nblk * tiled[0], tiled[1]), F32),
        compiler_params=_cparams("arbitrary"),
        name="moe_experts",
    )(blk_e, n_active, buf_tok, buf_tok, buf_tok, h2, w_gate, w_up, w_down)


def _final_kernel(d0_ref, dn_ref, x_ref, w_ref, gt_ref, g_ref, y_hbm, o_ref, yg_s, sem):
    i = pl.program_id(0)
    slot = i % 2
    tm = x_ref.shape[0]

    def start_tile(d_ref, slot_):
        def body(r, carry):
            for k in range(TOP_K):
                _row_copy(y_hbm, d_ref[0, k, r], yg_s.at[slot_, k, pl.ds(r * SUBLANES, SUBLANES), :],
                          sem.at[slot_]).start(priority=k)
            return carry

        lax.fori_loop(0, tm, body, 0, unroll=8)

    @pl.when(i == 0)
    def _():
        start_tile(d0_ref, 0)

    @pl.when(i + 1 < pl.num_programs(0))
    def _():
        start_tile(dn_ref, 1 - slot)

    for k in range(TOP_K):
        pltpu.make_async_copy(y_hbm.at[pl.ds(0, tm * SUBLANES), :], yg_s.at[slot, k], sem.at[slot]).wait()
    w = w_ref[...]
    y = _load_row_tiles(yg_s, (slot, 0), tm) * w[:, 0:1] + _load_row_tiles(yg_s, (slot, 1), tm) * w[:, 1:2]
    x = x_ref[...] + gt_ref[...] * y
    ms = jnp.mean(x * x, axis=-1, keepdims=True)
    o_ref[...] = (x * lax.rsqrt(ms + EPS)) * g_ref[...]


def _final(x_new, yb, dest, wts, gt2, gfin):
    l, d = x_new.shape
    n, _, tm = dest.shape
    row = lambda w: pl.BlockSpec((tm, w), lambda i: (i, 0))
    return pl.pallas_call(
        _final_kernel,
        grid=(n,),
        in_specs=[pl.BlockSpec((1, TOP_K, tm), lambda i: (0, 0, 0), memory_space=pltpu.SMEM),
                  pl.BlockSpec((1, TOP_K, tm), lambda i: (jnp.minimum(i + 1, n - 1), 0, 0),
                               memory_space=pltpu.SMEM),
                  row(d), row(wts.shape[1]), _const_spec((1, d)), _const_spec((1, d)),
                  pl.BlockSpec(memory_space=pl.ANY)],
        out_specs=row(d),
        out_shape=jax.ShapeDtypeStruct((l, d), F32),
        scratch_shapes=[pltpu.VMEM((2, TOP_K, tm * SUBLANES, d // SUBLANES), F32), pltpu.SemaphoreType.DMA((2,))],
        compiler_params=_cparams("arbitrary"),
        name="combine_final_norm",
    )(dest, dest, x_new, wts, gt2, gfin, yb)


def _rope_tables(rows_count):
    lane = jnp.arange(LANES, dtype=jnp.int32) % HEAD_DIM
    pair = lane // 2
    n_freq = ROPE_AXIS_DIM // 2
    by_row = pair < n_freq
    fidx = jnp.where(by_row, pair, pair - n_freq).astype(F32)
    freq = ROPE_THETA ** (-(2.0 * fidx) / ROPE_AXIS_DIM)
    sign = jnp.where(lane % 2 == 0, -1.0, 1.0).astype(F32)
    ang_row = jnp.arange(rows_count, dtype=F32)[:, None] * freq
    ang_col = jnp.arange(GRID_W, dtype=F32)[:, None] * freq
    small = lax.optimization_barrier((jnp.cos(ang_row), jnp.cos(ang_col), jnp.sin(ang_row) * sign,
                                      jnp.sin(ang_col) * sign))
    cos_r, cos_c, sin_r, sin_c = small
    full = (rows_count * GRID_W, LANES)
    cos = jnp.where(by_row, cos_r[:, None, :], cos_c[None, :, :]).reshape(full)
    sin = jnp.where(by_row, sin_r[:, None, :], sin_c[None, :, :]).reshape(full)
    return cos, sin


def _dispatch(ids, n_tokens):
    a = TOP_K * n_tokens
    eid_f = ids[:TOP_K].reshape(a)
    tok_f = jnp.tile(jnp.arange(n_tokens, dtype=jnp.int32), TOP_K)
    onehot = (eid_f[:, None] == jnp.arange(N_EXPERTS, dtype=jnp.int32)[None, :]).astype(jnp.int32)
    incl = jnp.cumsum(onehot, axis=0)
    counts = incl[-1]
    rank = jnp.sum((incl - onehot) * onehot, axis=1)
    pcounts = (counts + MOE_BLOCK - 1) // MOE_BLOCK * MOE_BLOCK
    pends = jnp.cumsum(pcounts)
    pstarts = pends - pcounts
    dest = jnp.sum(onehot * pstarts[None, :], axis=1) + rank
    p = a + N_EXPERTS * MOE_BLOCK
    nblk = p // MOE_BLOCK
    n_fill = p - a
    fill_ends = jnp.cumsum(pcounts - counts)
    fill_idx = jnp.arange(n_fill, dtype=jnp.int32)
    fill_e = jnp.sum(fill_idx[:, None] >= fill_ends[None, :], axis=1).astype(jnp.int32)
    idx_bits = max(a, n_fill).bit_length()
    real_keys = (eid_f << (idx_bits + 1)) | jnp.arange(a, dtype=jnp.int32)
    fill_keys = (fill_e << (idx_bits + 1)) | (1 << idx_bits) | fill_idx
    keys = jnp.sort(jnp.concatenate([real_keys, fill_keys]))
    is_real = ((keys >> idx_bits) & 1) == 0
    buf_tok = jnp.where(is_real, (keys & ((1 << idx_bits) - 1)) % n_tokens, 0)
    del tok_f
    n_active = pends[-1] // MOE_BLOCK
    blk_start = jnp.arange(nblk, dtype=jnp.int32) * MOE_BLOCK
    blk_e = jnp.minimum(jnp.sum(blk_start[:, None] >= pends[None, :], axis=1), N_EXPERTS - 1)
    blk_e = jnp.where(jnp.arange(nblk) < n_active, blk_e, blk_e[n_active - 1]).astype(jnp.int32)
    return buf_tok, dest.reshape(TOP_K, n_tokens), blk_e, n_active.astype(jnp.int32).reshape(1)


def _split_bf16(w):
    hi = w.astype(BF16)
    return jnp.stack([hi, (w - hi.astype(F32)).astype(BF16)])


def kernel(x, c, ctx, c_ctx, w_ada, b_ada, norm1_g, w_in, attn_q_norm, attn_k_norm, ret_decay_fwd,
           ret_decay_bwd, ret_gn_g, ret_gn_b, w_out, norm2_g, moe_w_grp, moe_b_grp, moe_w_exp, moe_b_exp,
           moe_w_gate, moe_w_up, moe_w_down, final_norm_g):
    b, l, d = x.shape
    cl = ctx.shape[1]
    assert b == 1 and w_ada.shape[0] == 1, "single batch element, single layer"
    x2, ctx2 = x[0], ctx[0]

    cvecs = jnp.zeros((SUBLANES, d), F32).at[0].set(c[0]).at[1].set(c_ctx)
    mod = _ada(cvecs, w_ada[0], b_ada[0])
    sh1, sc1, gt1, sh2, sc2, gt2 = [mod[0:1, i * d:(i + 1) * d] for i in range(6)]
    csh1, csc1 = mod[1:2, 0:d], mod[1:2, d:2 * d]

    w_in_bf = w_in[0].astype(BF16)
    head_of = jnp.arange(ATTN_WIDTH) // HEAD_DIM
    bd = jnp.where(head_of[:, None] == head_of[None, :], 1.0 / HEAD_DIM, 0.0).astype(BF16)
    gq = jnp.tile(attn_q_norm[0], ATTN_HEADS)[None, :]
    gk = jnp.tile(attn_k_norm[0], ATTN_KV_HEADS)[None, :]
    g1 = norm1_g[0][None, :]
    cos, sin = _rope_tables(l // GRID_W)
    bound = (HEAD_DIM ** 0.5 * LOG2E * BOUND_MARGIN) * jnp.max(jnp.abs(attn_q_norm[0])) * jnp.max(jnp.abs(attn_k_norm[0]))
    bound = bound.astype(BF16).astype(F32)
    neg_bound = jnp.full((1, LANES), -1.0, F32) * bound
    q_t, ka, va, qr, kr, vr, gr = _inproj(x2, sh1, sc1, g1, w_in_bf, bd, gq, gk, cos, sin, neg_bound)
    _, ka_c, va_c, _, kr_c, vr_c, _ = _inproj(ctx2, csh1, csc1, g1, w_in_bf, bd, gq, gk,
                                              jnp.ones((cl, LANES), F32), jnp.zeros((cl, LANES), F32), neg_bound)

    k_all = jnp.concatenate([ka_c, ka], axis=0)
    attn_args = (q_t, k_all, jnp.concatenate([va_c, va], axis=1))
    o_t = lax.cond(bound <= MAX_UNSHIFTED_BOUND,
                   functools.partial(_attention, running_max=False),
                   functools.partial(_attention, running_max=True), *attn_args)

    log_gf = jax.nn.log_sigmoid(ret_decay_fwd[0].astype(F32))
    log_gb = jax.nn.log_sigmoid(ret_decay_bwd[0].astype(F32))
    tab_f, gc_f = _retention_tables(log_gf, False)
    tab_b, gc_b = _retention_tables(log_gb, True)
    o_f = _retention_pass(gc_f, qr, kr, vr, kr_c, vr_c, tab_f, None, reverse=False)
    ret = _retention_pass(gc_b, qr, kr, vr, kr_c, vr_c, tab_b,
                          (o_f, gr, ret_gn_g[0][None, :], ret_gn_b[0][None, :]), reverse=True)

    wr = jnp.zeros((ROUTER_ROWS, d), F32)
    wr = wr.at[:N_GROUPS].set(moe_w_grp[0].T).at[EXPERT_ROW0:EXPERT_ROW0 + N_EXPERTS].set(moe_w_exp[0].T)
    br = jnp.zeros((ROUTER_ROWS, 1), F32)
    br = br.at[:N_GROUPS, 0].set(moe_b_grp[0]).at[EXPERT_ROW0:EXPERT_ROW0 + N_EXPERTS, 0].set(moe_b_exp[0])
    x_new, h2, ids, wts = _outproj_router(x2, o_t, ret, w_out[0].astype(BF16), gt1, norm2_g[0][None, :],
                                          sh2, sc2, _split_bf16(wr), br)

    buf_tok, dest, blk_e, n_active = _dispatch(ids, l)
    yb = _moe_blocks(blk_e, n_active, buf_tok.reshape(-1, 1, MOE_BLOCK), h2, moe_w_gate[0], moe_w_up[0],
                     moe_w_down[0])
    tf = min(FINAL_TILE, l)
    dest_t = dest.reshape(TOP_K, l // tf, tf).transpose(1, 0, 2)
    out = _final(x_new, yb, dest_t, wts[:TOP_K].T, gt2, final_norm_g[None, :])
    return out[None]
```

```python
import functools

import jax
import jax.numpy as jnp
import numpy as np
from jax import lax
from jax.experimental import pallas as pl
from jax.experimental.pallas import tpu as pltpu

F32 = jnp.float32
BF16 = jnp.bfloat16

GRID_W = 64
HEAD_DIM = 64
ATTN_HEADS = 8
ATTN_KV_HEADS = 2
GQA_GROUP = ATTN_HEADS // ATTN_KV_HEADS
RET_HEADS = 4
RET_QK_DIM = 64
RET_V_DIM = 128
RET_CHUNK = 256
ATTN_WIDTH = ATTN_HEADS * HEAD_DIM
KV_WIDTH = ATTN_KV_HEADS * HEAD_DIM
RET_QK_WIDTH = RET_HEADS * RET_QK_DIM
RET_WIDTH = RET_HEADS * RET_V_DIM
PROJ_SIZES = (ATTN_WIDTH, KV_WIDTH, KV_WIDTH, RET_QK_WIDTH, RET_QK_WIDTH, RET_WIDTH, RET_WIDTH)
PROJ_OFFS = tuple(int(v) for v in np.cumsum((0,) + PROJ_SIZES))
PROJ_DIM = PROJ_OFFS[-1]
ROPE_THETA = 10000.0
ROPE_AXIS_DIM = HEAD_DIM // 2
N_GROUPS = 4
EXPERTS_PER_GROUP = 8
N_EXPERTS = N_GROUPS * EXPERTS_PER_GROUP
TOP_K = 2
D_EXPERT = 512
EPS = 1e-6
LOG2E = 1.4426950408889634
ATTN_AUG_DIM = 128
MAX_UNSHIFTED_BOUND = 60.0

LANES = 128
SUBLANES = 8
VMEM_LIMIT_BYTES = 56 * 1024 * 1024

ROW_TILE = 512
INPROJ_SUBTILE = 256
ATTN_Q_TILE = 512
ATTN_CHUNK_UNROLL = 5
ATTN_MAX_KEY_CHUNK = 4096
ADA_COL_TILE = 3072
BOUND_MARGIN = 1.01
RET_ROW_TILE = 2048
RET_UNROLL = 4
MOE_BLOCK = 256
MOE_SLOTS = 3
FINAL_TILE = 512
ROUTER_ROWS = 128
EXPERT_ROW0 = 8


def _cparams(*sem):
    return pltpu.CompilerParams(dimension_semantics=sem, vmem_limit_bytes=VMEM_LIMIT_BYTES)


def _const_spec(shape):
    nd = len(shape)
    return pl.BlockSpec(shape, lambda *_: (0,) * nd)


def _store_row_tiles(ref, idx, rows):
    n, w = rows.shape[0], rows.shape[1] // SUBLANES
    for s in range(SUBLANES):
        ref[idx + (pl.ds(s, n, stride=SUBLANES), slice(None))] = rows[:, s * w:(s + 1) * w]


def _load_row_tiles(ref, idx, n):
    return jnp.concatenate([ref[idx + (pl.ds(s, n, stride=SUBLANES), slice(None))] for s in range(SUBLANES)],
                           axis=1)


def _ada_kernel(s_ref, w_ref, b_ref, o_ref):
    s = s_ref[...]
    s = s / (1.0 + jnp.exp(-s))
    o_ref[...] = jnp.dot(s, w_ref[...], preferred_element_type=F32,
                         precision=lax.Precision.HIGHEST) + b_ref[...]


def _ada(cvecs, w_ada, b_ada):
    d, n = w_ada.shape
    tn = ADA_COL_TILE
    return pl.pallas_call(
        _ada_kernel,
        grid=(n // tn,),
        in_specs=[_const_spec((SUBLANES, d)),
                  pl.BlockSpec((d, tn), lambda j: (0, j)),
                  pl.BlockSpec((1, tn), lambda j: (0, j))],
        out_specs=pl.BlockSpec((SUBLANES, tn), lambda j: (0, j)),
        out_shape=jax.ShapeDtypeStruct((SUBLANES, n), F32),
        compiler_params=_cparams("arbitrary"),
        name="ada_mod",
    )(cvecs, w_ada, b_ada.reshape(1, n))


def _rms_modulate(x, g, sh, sc):
    ms = jnp.mean(x * x, axis=-1, keepdims=True)
    return (x * lax.rsqrt(ms + EPS)) * g * (1.0 + sc) + sh


def _head_mean_sq(v, bd):
    sq = v * v
    hi = sq.astype(BF16)
    lo = (sq - hi.astype(F32)).astype(BF16)
    return (jnp.dot(hi, bd, preferred_element_type=F32) + jnp.dot(lo, bd, preferred_element_type=F32))


def _rope_chunks(v, cos, sin, even):
    outs = []
    for c in range(v.shape[1] // LANES):
        xc = v[:, c * LANES:(c + 1) * LANES]
        nxt = pltpu.roll(xc, LANES - 1, 1)
        prv = pltpu.roll(xc, 1, 1)
        outs.append(xc * cos + jnp.where(even, nxt, prv) * sin)
    return outs


def _inproj_kernel(x_ref, sh_ref, sc_ref, g_ref, w_ref, bd_ref, gq_ref, gk_ref, cos_ref, sin_ref, nb_ref,
                   qt_ref, ka_ref, va_ref, qr_ref, kr_ref, vr_ref, gr_ref):
    tm = x_ref.shape[0]
    sub = min(INPROJ_SUBTILE, tm)
    bd = bd_ref[...]
    scale = HEAD_DIM ** -0.5 * LOG2E
    rscale = RET_QK_DIM ** -0.5
    o = PROJ_OFFS
    per = LANES // HEAD_DIM
    for j0 in range(0, tm, sub):
        rows = slice(j0, j0 + sub)
        h = _rms_modulate(x_ref[rows, :], g_ref[...], sh_ref[...], sc_ref[...])
        proj = jnp.dot(h.astype(BF16), w_ref[...], preferred_element_type=F32)
        qa, ka, va = proj[:, o[0]:o[1]], proj[:, o[1]:o[2]], proj[:, o[2]:o[3]]
        qr, kr, vr, gr = proj[:, o[3]:o[4]], proj[:, o[4]:o[5]], proj[:, o[5]:o[6]], proj[:, o[6]:o[7]]
        cos, sin = cos_ref[rows, :], sin_ref[rows, :]
        even = (lax.broadcasted_iota(jnp.int32, cos.shape, 1) % 2) == 0

        qn = qa * lax.rsqrt(_head_mean_sq(qa, bd) + EPS) * gq_ref[...]
        for c, blk in enumerate(_rope_chunks(qn, cos, sin, even)):
            blk_t = (blk * scale).T
            for j in range(per):
                kvh, grp = divmod(c * per + j, GQA_GROUP)
                qt_ref[kvh, 0:HEAD_DIM, grp * tm + j0:grp * tm + j0 + sub] = (
                    blk_t[j * HEAD_DIM:(j + 1) * HEAD_DIM, :].astype(BF16))
        kn = ka * lax.rsqrt(_head_mean_sq(ka, bd[:KV_WIDTH, :KV_WIDTH]) + EPS) * gk_ref[...]
        k_rot = _rope_chunks(kn, cos, sin, even)[0].astype(BF16)
        pad_shape = (sub, ATTN_AUG_DIM - HEAD_DIM)
        k_pad = jnp.where(lax.broadcasted_iota(jnp.int32, pad_shape, 1) == 0, 1.0, 0.0).astype(BF16)
        for kvh in range(ATTN_KV_HEADS):
            c0 = kvh * ATTN_AUG_DIM
            ka_ref[rows, c0:c0 + HEAD_DIM] = k_rot[:, kvh * HEAD_DIM:(kvh + 1) * HEAD_DIM]
            ka_ref[rows, c0 + HEAD_DIM:c0 + ATTN_AUG_DIM] = k_pad
        va_ref[:, rows] = va.T.astype(BF16)
        for c, blk in enumerate(_rope_chunks(qr, cos, sin, even)):
            qr_ref[rows, c * LANES:(c + 1) * LANES] = blk.astype(BF16)
        for c, blk in enumerate(_rope_chunks(kr, cos, sin, even)):
            kr_ref[rows, c * LANES:(c + 1) * LANES] = (blk * rscale).astype(BF16)
        vr_ref[rows, :] = vr.astype(BF16)
        gr_ref[rows, :] = gr
    aug_shape = (ATTN_AUG_DIM - HEAD_DIM, GQA_GROUP * tm)
    aug_row = lax.broadcasted_iota(jnp.int32, aug_shape, 0)
    aug = jnp.where(aug_row == 0, nb_ref[0:1, 0:1], 0.0).astype(BF16)
    for kvh in range(ATTN_KV_HEADS):
        qt_ref[kvh, HEAD_DIM:ATTN_AUG_DIM, :] = aug


def _inproj(x2, sh, sc, g1, w_bf, bd, gq, gk, cos, sin, neg_bound):
    rows, d = x2.shape
    tm = min(ROW_TILE, rows)
    row = lambda w: pl.BlockSpec((tm, w), lambda i: (i, 0))
    row_specs = [row(ATTN_KV_HEADS * ATTN_AUG_DIM), pl.BlockSpec((KV_WIDTH, tm), lambda i: (0, i))]
    row_specs += [row(w) for w in PROJ_SIZES[3:]]
    row_shapes = [(rows, ATTN_KV_HEADS * ATTN_AUG_DIM), (KV_WIDTH, rows)] + [(rows, w) for w in PROJ_SIZES[3:]]
    dtypes = (BF16,) * 5 + (F32,)
    qt_shape = (ATTN_KV_HEADS, ATTN_AUG_DIM, GQA_GROUP * rows)
    return pl.pallas_call(
        _inproj_kernel,
        grid=(rows // tm,),
        in_specs=[row(d), _const_spec((1, d)), _const_spec((1, d)), _const_spec((1, d)),
                  _const_spec(w_bf.shape), _const_spec(bd.shape),
                  _const_spec((1, ATTN_WIDTH)), _const_spec((1, KV_WIDTH)),
                  row(LANES), row(LANES), _const_spec((1, LANES))],
        out_specs=[pl.BlockSpec((ATTN_KV_HEADS, ATTN_AUG_DIM, GQA_GROUP * tm), lambda i: (0, 0, i))] + row_specs,
        out_shape=[jax.ShapeDtypeStruct(qt_shape, BF16)]
        + [jax.ShapeDtypeStruct(shp, dt) for shp, dt in zip(row_shapes, dtypes)],
        compiler_params=_cparams("arbitrary"),
        name="norm_inproj",
    )(x2, sh, sc, g1, w_bf, bd, gq, gk, cos, sin, neg_bound)


def _attn_kernel(q_ref, k_ref, v_ref, o_ref, m_ref, l_ref, acc_ref, *, tk, running_max):
    m_ref[...] = jnp.full(m_ref.shape, -jnp.inf, F32)
    l_ref[...] = jnp.zeros(l_ref.shape, F32)
    acc_ref[...] = jnp.zeros(acc_ref.shape, F32)

    n_chunks = k_ref.shape[0] // tk

    def body(c, carry):
        keys = pl.ds(pl.multiple_of(c * tk, tk), tk)
        kc = k_ref[keys, :]
        vc = v_ref[:, keys]
        s = jnp.dot(kc, q_ref[0], preferred_element_type=F32)
        if running_max:
            m_old = m_ref[...]
            m_new = jnp.maximum(m_old, jnp.max(s, axis=0, keepdims=True))
            alpha = jnp.exp2(m_old - m_new)
            p = jnp.exp2(s - m_new)
            l_ref[...] = alpha * l_ref[...] + jnp.sum(p, axis=0, keepdims=True)
            acc_ref[...] = acc_ref[...] * alpha + jnp.dot(vc, p.astype(BF16), preferred_element_type=F32)
            m_ref[...] = m_new
        else:
            p = jnp.exp2(s)
            l_ref[...] += jnp.sum(p, axis=0, keepdims=True)
            acc_ref[...] += jnp.dot(vc, p.astype(BF16), preferred_element_type=F32)
        return carry

    lax.fori_loop(0, n_chunks, body, 0, unroll=1 if running_max else min(ATTN_CHUNK_UNROLL, n_chunks))
    o_ref[0] = (acc_ref[...] / l_ref[...]).astype(o_ref.dtype)


def _key_chunk(nk):
    for tk in range(min(nk, ATTN_MAX_KEY_CHUNK) // LANES * LANES, 0, -LANES):
        if nk % tk == 0:
            return tk
    raise ValueError(f"unsupported key count {nk}")


def _attention(q_t, k_c, v_t, *, running_max):
    kv, da, gl = q_t.shape
    nk = k_c.shape[0]
    d = v_t.shape[0] // kv
    nq = GQA_GROUP * min(ATTN_Q_TILE, gl // GQA_GROUP)
    return pl.pallas_call(
        functools.partial(_attn_kernel, tk=_key_chunk(nk), running_max=running_max),
        grid=(kv, gl // nq),
        in_specs=[pl.BlockSpec((1, da, nq), lambda b, i: (b, 0, i)),
                  pl.BlockSpec((nk, da), lambda b, i: (0, b)),
                  pl.BlockSpec((d, nk), lambda b, i: (b, 0))],
        out_specs=pl.BlockSpec((1, d, nq), lambda b, i: (b, 0, i)),
        out_shape=jax.ShapeDtypeStruct((kv, d, gl), BF16),
        scratch_shapes=[pltpu.VMEM((1, nq), F32), pltpu.VMEM((1, nq), F32), pltpu.VMEM((d, nq), F32)],
        compiler_params=_cparams("arbitrary", "arbitrary"),
        name="flash_attn",
    )(q_t, k_c, v_t)


def _ret_kernel(gc_ref, q_ref, k_ref, v_ref, kc_ref, vc_ref, tab_ref, *rest, reverse, final):
    if final:
        of_ref, gate_ref, gng_ref, gnb_ref, out_ref, s_ref = rest
    else:
        out_ref, s_ref = rest
    c_rows = RET_CHUNK
    n_local = q_ref.shape[0] // c_rows
    n_ctx = kc_ref.shape[0] // c_rows
    lane_head = lax.broadcasted_iota(jnp.int32, (c_rows, RET_QK_WIDTH), 1) // RET_QK_DIM

    def state_update(kch, vch):
        for h in range(RET_HEADS):
            vz = (vch[:, h * RET_V_DIM:(h + 1) * RET_V_DIM].astype(F32)
                  * tab_ref[1, h, :, 0:RET_V_DIM]).astype(BF16)
            u = lax.dot_general(kch, vz, (((0,), (0,)), ((), ())), preferred_element_type=F32)
            s_ref[h] = gc_ref[h] * s_ref[h] + u

    @pl.when(pl.program_id(0) == 0)
    def _():
        s_ref[...] = jnp.zeros(s_ref.shape, F32)
        order = range(n_ctx - 1, -1, -1) if reverse else range(n_ctx)
        for cc in order:
            state_update(kc_ref[cc * c_rows:(cc + 1) * c_rows, :], vc_ref[cc * c_rows:(cc + 1) * c_rows, :])

    def chunk(ci, carry):
        c = (n_local - 1 - ci) if reverse else ci
        rows = pl.ds(pl.multiple_of(c * c_rows, c_rows), c_rows)
        q, k, v = q_ref[rows, :], k_ref[rows, :], v_ref[rows, :]
        for h in range(RET_HEADS):
            cols = slice(h * RET_V_DIM, (h + 1) * RET_V_DIM)
            qm = jnp.where(lane_head == h, q, jnp.zeros_like(q))
            sc = lax.dot_general(qm, k, (((1,), (1,)), ((), ())), preferred_element_type=F32)
            sc = sc * tab_ref[0, h]
            o = jnp.dot(sc.astype(BF16), v[:, cols], preferred_element_type=F32)
            o = o + (jnp.dot(qm, s_ref[h].astype(BF16), preferred_element_type=F32)
                     * tab_ref[2, h, :, 0:RET_V_DIM])
            if final:
                o = o + of_ref[rows, cols]
                mu = jnp.mean(o, axis=-1, keepdims=True)
                oc = o - mu
                var = jnp.mean(oc * oc, axis=-1, keepdims=True)
                on = oc * lax.rsqrt(var + EPS)
                gate = gate_ref[rows, cols]
                y = (on * gng_ref[:, cols] + gnb_ref[:, cols]) * (gate / (1.0 + jnp.exp(-gate)))
                out_ref[rows, cols] = y.astype(out_ref.dtype)
            else:
                out_ref[rows, cols] = o
        state_update(k, v)
        return carry

    lax.fori_loop(0, n_local, chunk, 0, unroll=RET_UNROLL)


def _retention_pass(gc, qr, kr, vr, kr_c, vr_c, tabs, extra, *, reverse):
    l = qr.shape[0]
    tr = min(RET_ROW_TILE, l)
    nt = l // tr
    final = extra is not None
    idx = (lambda j, gc_: (nt - 1 - j, 0)) if reverse else (lambda j, gc_: (j, 0))
    const = lambda shape: pl.BlockSpec(shape, lambda j, gc_: (0,) * len(shape))
    in_specs = [pl.BlockSpec((tr, RET_QK_WIDTH), idx), pl.BlockSpec((tr, RET_QK_WIDTH), idx),
                pl.BlockSpec((tr, RET_WIDTH), idx),
                const(kr_c.shape), const(vr_c.shape), const(tabs.shape)]
    args = [qr, kr, vr, kr_c, vr_c, tabs]
    if final:
        of, gate, gng, gnb = extra
        in_specs += [pl.BlockSpec((tr, RET_WIDTH), idx), pl.BlockSpec((tr, RET_WIDTH), idx),
                     const(gng.shape), const(gnb.shape)]
        args += [of, gate, gng, gnb]
    return pl.pallas_call(
        functools.partial(_ret_kernel, reverse=reverse, final=final),
        grid_spec=pltpu.PrefetchScalarGridSpec(
            num_scalar_prefetch=1, grid=(nt,), in_specs=in_specs,
            out_specs=pl.BlockSpec((tr, RET_WIDTH), idx),
            scratch_shapes=[pltpu.VMEM((RET_HEADS, RET_QK_WIDTH, RET_V_DIM), F32)]),
        out_shape=jax.ShapeDtypeStruct((l, RET_WIDTH), BF16 if final else F32),
        compiler_params=_cparams("arbitrary"),
        name="retention_bwd" if reverse else "retention_fwd",
    )(gc, *args)


def _retention_tables(log_g, reverse):
    c = RET_CHUNK
    pos = jnp.arange(c, dtype=F32)
    diff = (pos[None, :] - pos[:, None]) if reverse else (pos[:, None] - pos[None, :])
    mask = (diff > 0) if reverse else (diff >= 0)
    lg = log_g[:, None, None]
    dmat = jnp.where(mask, jnp.exp(lg * jnp.where(mask, diff, 0.0)[None]), 0.0)
    zeta = jnp.exp(log_g[:, None] * (pos if reverse else (c - 1 - pos))[None, :])
    xi = jnp.exp(log_g[:, None] * ((c - pos) if reverse else (pos + 1.0))[None, :])
    ones = jnp.ones((1, 1, c), F32)
    tabs = jnp.stack([dmat, zeta[:, :, None] * ones, xi[:, :, None] * ones])
    return tabs, jnp.exp(log_g * c)


def _outproj_kernel(x_ref, a_ref, r_ref, w_ref, gt_ref, g2_ref, sh_ref, sc_ref, wr_ref, br_ref,
                    xn_ref, h2_ref, ids_ref, wts_ref):
    tm = x_ref.shape[0]
    pieces = []
    for head in range(ATTN_HEADS):
        kvh, grp = divmod(head, GQA_GROUP)
        pieces.append(a_ref[kvh, :, grp * tm:(grp + 1) * tm])
    o_t = jnp.concatenate(pieces, axis=0)
    mix = (jnp.dot(r_ref[...], w_ref[ATTN_WIDTH:, :], preferred_element_type=F32)
           + lax.dot_general(o_t, w_ref[:ATTN_WIDTH, :], (((0,), (0,)), ((), ())),
                             preferred_element_type=F32))
    x_new = x_ref[...] + gt_ref[...] * mix
    xn_ref[...] = x_new
    h2 = _rms_modulate(x_new, g2_ref[...], sh_ref[...], sc_ref[...])
    _store_row_tiles(h2_ref, (), h2)
    hi = h2.astype(BF16)
    lo = (h2 - hi.astype(F32)).astype(BF16)

    nt = (((1,), (1,)), ((), ()))
    logits = (lax.dot_general(wr_ref[0], hi, nt, preferred_element_type=F32)
              + lax.dot_general(wr_ref[0], lo, nt, preferred_element_type=F32)
              + lax.dot_general(wr_ref[1], hi, nt, preferred_element_type=F32)) + br_ref[...]
    rows = [logits[g:g + 1, :] for g in range(N_GROUPS)]
    gmax = functools.reduce(jnp.maximum, rows)
    gidx = jnp.full(gmax.shape, N_GROUPS - 1, jnp.int32)
    for g in range(N_GROUPS - 2, -1, -1):
        gidx = jnp.where(rows[g] == gmax, g, gidx)
    p_g = 1.0 / functools.reduce(jnp.add, [jnp.exp(r - gmax) for r in rows])
    sel = logits[EXPERT_ROW0 + (N_GROUPS - 1) * EXPERTS_PER_GROUP:EXPERT_ROW0 + N_GROUPS * EXPERTS_PER_GROUP, :]
    for g in range(N_GROUPS - 2, -1, -1):
        blk = logits[EXPERT_ROW0 + g * EXPERTS_PER_GROUP:EXPERT_ROW0 + (g + 1) * EXPERTS_PER_GROUP, :]
        sel = jnp.where(gidx == g, blk, sel)
    ridx = lax.broadcasted_iota(jnp.int32, sel.shape, 0)
    m1 = jnp.max(sel, axis=0, keepdims=True)
    i1 = jnp.min(jnp.where(sel == m1, ridx, EXPERTS_PER_GROUP), axis=0, keepdims=True)
    sel2 = jnp.where(ridx == i1, -jnp.inf, sel)
    m2 = jnp.max(sel2, axis=0, keepdims=True)
    i2 = jnp.min(jnp.where(sel2 == m2, ridx, EXPERTS_PER_GROUP), axis=0, keepdims=True)
    e2 = jnp.exp(m2 - m1)
    w1 = p_g / (1.0 + e2)
    w2 = w1 * e2
    out_rows = lax.broadcasted_iota(jnp.int32, ids_ref.shape, 0)
    ids_ref[...] = jnp.where(out_rows == 0, gidx * EXPERTS_PER_GROUP + i1,
                             jnp.where(out_rows == 1, gidx * EXPERTS_PER_GROUP + i2, 0))
    wts_ref[...] = jnp.where(out_rows == 0, w1, jnp.where(out_rows == 1, w2, 0.0))


def _outproj_router(x2, attn, ret, w_out_bf, gt1, g2, sh2, sc2, wr, br):
    l, d = x2.shape
    tm = min(ROW_TILE, l)
    row = lambda w: pl.BlockSpec((tm, w), lambda i: (i, 0))
    col = pl.BlockSpec((SUBLANES, tm), lambda i: (0, i))
    return pl.pallas_call(
        _outproj_kernel,
        grid=(l // tm,),
        in_specs=[row(d),
                  pl.BlockSpec((ATTN_KV_HEADS, HEAD_DIM, GQA_GROUP * tm), lambda i: (0, 0, i)),
                  row(ret.shape[1]), _const_spec(w_out_bf.shape),
                  _const_spec((1, d)), _const_spec((1, d)), _const_spec((1, d)), _const_spec((1, d)),
                  _const_spec(wr.shape), _const_spec(br.shape)],
        out_specs=[row(d), pl.BlockSpec((tm * SUBLANES, d // SUBLANES), lambda i: (i, 0)), col, col],
        out_shape=[jax.ShapeDtypeStruct((l, d), F32), jax.ShapeDtypeStruct((l * SUBLANES, d // SUBLANES), F32),
                   jax.ShapeDtypeStruct((SUBLANES, l), jnp.int32),
                   jax.ShapeDtypeStruct((SUBLANES, l), F32)],
        compiler_params=_cparams("arbitrary"),
        name="outproj_router",
    )(x2, attn, ret, w_out_bf, gt1, g2, sh2, sc2, wr, br)


def _row_copy(src_hbm, src_row, dst_ref, sem):
    return pltpu.make_async_copy(src_hbm.at[pl.ds(src_row * SUBLANES, SUBLANES), :], dst_ref, sem)


def _moe_kernel(be_ref, na_ref, tok0_ref, tok1_ref, tokn_ref, h_hbm, wg_ref, wu_ref, wd_ref, y_ref,
                wg_s, wu_s, wd_s, xg_s, sem):
    b = pl.program_id(0)
    slot = lax.rem(b, MOE_SLOTS)
    nxt = lax.rem(b + 2, MOE_SLOTS)
    n_active = na_ref[0]
    active = b < n_active
    last = be_ref.shape[0] - 1
    changed = jnp.logical_or(b == 0, be_ref[jnp.minimum(b, last)] != be_ref[jnp.clip(b - 1, 0, last)])

    def wait_rows():
        rows = pl.ds(0, MOE_BLOCK * SUBLANES)
        pltpu.make_async_copy(h_hbm.at[rows, :], xg_s.at[slot, rows, :], sem.at[slot]).wait()

    @pl.when(b == 0)
    def _():
        def body(r, carry):
            _row_copy(h_hbm, tok0_ref[0, 0, r], xg_s.at[0, pl.ds(r * SUBLANES, SUBLANES), :], sem.at[0]).start()
            _row_copy(h_hbm, tok1_ref[0, 0, r], xg_s.at[1, pl.ds(r * SUBLANES, SUBLANES), :], sem.at[1]).start()
            return carry

        lax.fori_loop(0, MOE_BLOCK, body, 0, unroll=8)

    @pl.when(jnp.logical_and(active, changed))
    def _():
        wg_s[...] = wg_ref[0].astype(BF16)
        wu_s[...] = wu_ref[0].astype(BF16)
        wd_s[...] = wd_ref[0].astype(BF16)

    @pl.when(active)
    def _():
        wait_rows()
        x = _load_row_tiles(xg_s, (slot,), MOE_BLOCK).astype(BF16)
        hg = jnp.dot(x, wg_s[...], preferred_element_type=F32)
        hu = jnp.dot(x, wu_s[...], preferred_element_type=F32)
        a = (hg / (1.0 + jnp.exp(-hg))) * hu
        _store_row_tiles(y_ref, (), jnp.dot(a.astype(BF16), wd_s[...], preferred_element_type=F32))
        for r in range(MOE_BLOCK):
            _row_copy(h_hbm, tokn_ref[0, 0, r], xg_s.at[nxt, pl.ds(r * SUBLANES, SUBLANES), :],
                      sem.at[nxt]).start(priority=r % 2)

    @pl.when(jnp.logical_or(b == n_active, b == n_active + 1))
    def _():
        wait_rows()

    @pl.when(jnp.logical_not(active))
    def _():
        y_ref[...] = jnp.zeros(y_ref.shape, y_ref.dtype)


def _moe_blocks(blk_e, n_active, buf_tok, h2, w_gate, w_up, w_down):
    ne, d, de = w_gate.shape
    tiled = (MOE_BLOCK * SUBLANES, d // SUBLANES)
    nblk = buf_tok.shape[0]
    blk = lambda b: jnp.minimum(b, nblk - 1)
    wmap = lambda b, be, na: (be[blk(b)], 0, 0)
    tokmap = lambda off: (lambda b, be, na: (blk(b + off), 0, 0))
    smem_tok = lambda off: pl.BlockSpec((1, 1, MOE_BLOCK), tokmap(off), memory_space=pltpu.SMEM)
    return pl.pallas_call(
        _moe_kernel,
        grid_spec=pltpu.PrefetchScalarGridSpec(
            num_scalar_prefetch=2, grid=(nblk + 1,),
            in_specs=[pl.BlockSpec((1, 1, MOE_BLOCK), lambda b, be, na: (0, 0, 0), memory_space=pltpu.SMEM),
                      pl.BlockSpec((1, 1, MOE_BLOCK), lambda b, be, na: (1, 0, 0), memory_space=pltpu.SMEM),
                      smem_tok(2),
                      pl.BlockSpec(memory_space=pl.ANY),
                      pl.BlockSpec((1, d, de), wmap), pl.BlockSpec((1, d, de), wmap),
                      pl.BlockSpec((1, de, d), wmap)],
            out_specs=pl.BlockSpec(tiled, lambda b, be, na: (blk(b), 0)),
            scratch_shapes=[pltpu.VMEM((d, de), BF16), pltpu.VMEM((d, de), BF16),
                            pltpu.VMEM((de, d), BF16), pltpu.VMEM((MOE_SLOTS,) + tiled, F32),
                            pltpu.SemaphoreType.DMA((MOE_SLOTS,))]),
        out_shape=jax.ShapeDtypeStruct((nblk * tiled[0], tiled[1]), F32),
        compiler_params=_cparams("arbitrary"),
        name="moe_experts",
    )(blk_e, n_active, buf_tok, buf_tok, buf_tok, h2, w_gate, w_up, w_down)


def _final_kernel(d0_ref, dn_ref, x_ref, w_ref, gt_ref, g_ref, y_hbm, o_ref, yg_s, sem):
    i = pl.program_id(0)
    slot = i % 2
    tm = x_ref.shape[0]

    def start_tile(d_ref, slot_):
        def body(r, carry):
            for k in range(TOP_K):
                _row_copy(y_hbm, d_ref[0, k, r], yg_s.at[slot_, k, pl.ds(r * SUBLANES, SUBLANES), :],
                          sem.at[slot_]).start(priority=k)
            return carry

        lax.fori_loop(0, tm, body, 0, unroll=8)

    @pl.when(i == 0)
    def _():
        start_tile(d0_ref, 0)

    @pl.when(i + 1 < pl.num_programs(0))
    def _():
        start_tile(dn_ref, 1 - slot)

    for k in range(TOP_K):
        pltpu.make_async_copy(y_hbm.at[pl.ds(0, tm * SUBLANES), :], yg_s.at[slot, k], sem.at[slot]).wait()
    w = w_ref[...]
    y = _load_row_tiles(yg_s, (slot, 0), tm) * w[:, 0:1] + _load_row_tiles(yg_s, (slot, 1), tm) * w[:, 1:2]
    x = x_ref[...] + gt_ref[...] * y
    ms = jnp.mean(x * x, axis=-1, keepdims=True)
    o_ref[...] = (x * lax.rsqrt(ms + EPS)) * g_ref[...]


def _final(x_new, yb, dest, wts, gt2, gfin):
    l, d = x_new.shape
    n, _, tm = dest.shape
    row = lambda w: pl.BlockSpec((tm, w), lambda i: (i, 0))
    return pl.pallas_call(
        _final_kernel,
        grid=(n,),
        in_specs=[pl.BlockSpec((1, TOP_K, tm), lambda i: (0, 0, 0), memory_space=pltpu.SMEM),
                  pl.BlockSpec((1, TOP_K, tm), lambda i: (jnp.minimum(i + 1, n - 1), 0, 0),
                               memory_space=pltpu.SMEM),
                  row(d), row(wts.shape[1]), _const_spec((1, d)), _const_spec((1, d)),
                  pl.BlockSpec(memory_space=pl.ANY)],
        out_specs=row(d),
        out_shape=jax.ShapeDtypeStruct((l, d), F32),
        scratch_shapes=[pltpu.VMEM((2, TOP_K, tm * SUBLANES, d // SUBLANES), F32), pltpu.SemaphoreType.DMA((2,))],
        compiler_params=_cparams("arbitrary"),
        name="combine_final_norm",
    )(dest, dest, x_new, wts, gt2, gfin, yb)


def _rope_tables(rows_count):
    lane = jnp.arange(LANES, dtype=jnp.int32) % HEAD_DIM
    pair = lane // 2
    n_freq = ROPE_AXIS_DIM // 2
    by_row = pair < n_freq
    fidx = jnp.where(by_row, pair, pair - n_freq).astype(F32)
    freq = ROPE_THETA ** (-(2.0 * fidx) / ROPE_AXIS_DIM)
    sign = jnp.where(lane % 2 == 0, -1.0, 1.0).astype(F32)
    ang_row = jnp.arange(rows_count, dtype=F32)[:, None] * freq
    ang_col = jnp.arange(GRID_W, dtype=F32)[:, None] * freq
    small = lax.optimization_barrier((jnp.cos(ang_row), jnp.cos(ang_col), jnp.sin(ang_row) * sign,
                                      jnp.sin(ang_col) * sign))
    cos_r, cos_c, sin_r, sin_c = small
    full = (rows_count * GRID_W, LANES)
    cos = jnp.where(by_row, cos_r[:, None, :], cos_c[None, :, :]).reshape(full)
    sin = jnp.where(by_row, sin_r[:, None, :], sin_c[None, :, :]).reshape(full)
    return cos, sin


def _dispatch(ids, n_tokens):
    a = TOP_K * n_tokens
    eid_f = ids[:TOP_K].reshape(a)
    tok_f = jnp.tile(jnp.arange(n_tokens, dtype=jnp.int32), TOP_K)
    onehot = (eid_f[:, None] == jnp.arange(N_EXPERTS, dtype=jnp.int32)[None, :]).astype(jnp.int32)
    incl = jnp.cumsum(onehot, axis=0)
    counts = incl[-1]
    rank = jnp.sum((incl - onehot) * onehot, axis=1)
    pcounts = (counts + MOE_BLOCK - 1) // MOE_BLOCK * MOE_BLOCK
    pends = jnp.cumsum(pcounts)
    pstarts = pends - pcounts
    dest = jnp.sum(onehot * pstarts[None, :], axis=1) + rank
    p = a + N_EXPERTS * MOE_BLOCK
    nblk = p // MOE_BLOCK
    n_fill = p - a
    fill_ends = jnp.cumsum(pcounts - counts)
    fill_idx = jnp.arange(n_fill, dtype=jnp.int32)
    fill_e = jnp.sum(fill_idx[:, None] >= fill_ends[None, :], axis=1).astype(jnp.int32)
    idx_bits = max(a, n_fill).bit_length()
    real_keys = (eid_f << (idx_bits + 1)) | jnp.arange(a, dtype=jnp.int32)
    fill_keys = (fill_e << (idx_bits + 1)) | (1 << idx_bits) | fill_idx
    keys = jnp.sort(jnp.concatenate([real_keys, fill_keys]))
    is_real = ((keys >> idx_bits) & 1) == 0
    buf_tok = jnp.where(is_real, (keys & ((1 << idx_bits) - 1)) % n_tokens, 0)
    del tok_f
    n_active = pends[-1] // MOE_BLOCK
    blk_start = jnp.arange(nblk, dtype=jnp.int32) * MOE_BLOCK
    blk_e = jnp.minimum(jnp.sum(blk_start[:, None] >= pends[None, :], axis=1), N_EXPERTS - 1)
    blk_e = jnp.where(jnp.arange(nblk) < n_active, blk_e, blk_e[n_active - 1]).astype(jnp.int32)
    return buf_tok, dest.reshape(TOP_K, n_tokens), blk_e, n_active.astype(jnp.int32).reshape(1)


def _split_bf16(w):
    hi = w.astype(BF16)
    return jnp.stack([hi, (w - hi.astype(F32)).astype(BF16)])


def kernel(x, c, ctx, c_ctx, w_ada, b_ada, norm1_g, w_in, attn_q_norm, attn_k_norm, ret_decay_fwd,
           ret_decay_bwd, ret_gn_g, ret_gn_b, w_out, norm2_g, moe_w_grp, moe_b_grp, moe_w_exp, moe_b_exp,
           moe_w_gate, moe_w_up, moe_w_down, final_norm_g):
    b, l, d = x.shape
    cl = ctx.shape[1]
    assert b == 1 and w_ada.shape[0] == 1, "single batch element, single layer"
    assert min(ROW_TILE, l) == min(ATTN_Q_TILE, l), "in-proj row tile and attention query tile share a lane order"
    assert l % min(ROW_TILE, l) == 0 and l % GRID_W == 0 and d % (SUBLANES * LANES) == 0
    assert l % RET_CHUNK == 0 and cl % RET_CHUNK == 0 and l % min(RET_ROW_TILE, l) == 0
    assert (TOP_K * l) % MOE_BLOCK == 0 and l % min(FINAL_TILE, l) == 0 and (cl + l) % LANES == 0
    x2, ctx2 = x[0], ctx[0]

    cvecs = jnp.zeros((SUBLANES, d), F32).at[0].set(c[0]).at[1].set(c_ctx)
    mod = _ada(cvecs, w_ada[0], b_ada[0])
    sh1, sc1, gt1, sh2, sc2, gt2 = [mod[0:1, i * d:(i + 1) * d] for i in range(6)]
    csh1, csc1 = mod[1:2, 0:d], mod[1:2, d:2 * d]

    w_in_bf = w_in[0].astype(BF16)
    head_of = jnp.arange(ATTN_WIDTH) // HEAD_DIM
    bd = jnp.where(head_of[:, None] == head_of[None, :], 1.0 / HEAD_DIM, 0.0).astype(BF16)
    gq = jnp.tile(attn_q_norm[0], ATTN_HEADS)[None, :]
    gk = jnp.tile(attn_k_norm[0], ATTN_KV_HEADS)[None, :]
    g1 = norm1_g[0][None, :]
    cos, sin = _rope_tables(l // GRID_W)
    bound = (HEAD_DIM ** 0.5 * LOG2E * BOUND_MARGIN) * jnp.max(jnp.abs(attn_q_norm[0])) * jnp.max(jnp.abs(attn_k_norm[0]))
    bound = bound.astype(BF16).astype(F32)
    neg_bound = jnp.full((1, LANES), -1.0, F32) * bound
    q_t, ka, va, qr, kr, vr, gr = _inproj(x2, sh1, sc1, g1, w_in_bf, bd, gq, gk, cos, sin, neg_bound)
    _, ka_c, va_c, _, kr_c, vr_c, _ = _inproj(ctx2, csh1, csc1, g1, w_in_bf, bd, gq, gk,
                                              jnp.ones((cl, LANES), F32), jnp.zeros((cl, LANES), F32), neg_bound)

    k_all = jnp.concatenate([ka_c, ka], axis=0)
    attn_args = (q_t, k_all, jnp.concatenate([va_c, va], axis=1))
    o_t = lax.cond(bound <= MAX_UNSHIFTED_BOUND,
                   functools.partial(_attention, running_max=False),
                   functools.partial(_attention, running_max=True), *attn_args)

    log_gf = jax.nn.log_sigmoid(ret_decay_fwd[0].astype(F32))
    log_gb = jax.nn.log_sigmoid(ret_decay_bwd[0].astype(F32))
    tab_f, gc_f = _retention_tables(log_gf, False)
    tab_b, gc_b = _retention_tables(log_gb, True)
    o_f = _retention_pass(gc_f, qr, kr, vr, kr_c, vr_c, tab_f, None, reverse=False)
    ret = _retention_pass(gc_b, qr, kr, vr, kr_c, vr_c, tab_b,
                          (o_f, gr, ret_gn_g[0][None, :], ret_gn_b[0][None, :]), reverse=True)

    wr = jnp.zeros((ROUTER_ROWS, d), F32)
    wr = wr.at[:N_GROUPS].set(moe_w_grp[0].T).at[EXPERT_ROW0:EXPERT_ROW0 + N_EXPERTS].set(moe_w_exp[0].T)
    br = jnp.zeros((ROUTER_ROWS, 1), F32)
    br = br.at[:N_GROUPS, 0].set(moe_b_grp[0]).at[EXPERT_ROW0:EXPERT_ROW0 + N_EXPERTS, 0].set(moe_b_exp[0])
    x_new, h2, ids, wts = _outproj_router(x2, o_t, ret, w_out[0].astype(BF16), gt1, norm2_g[0][None, :],
                                          sh2, sc2, _split_bf16(wr), br)

    buf_tok, dest, blk_e, n_active = _dispatch(ids, l)
    yb = _moe_blocks(blk_e, n_active, buf_tok.reshape(-1, 1, MOE_BLOCK), h2, moe_w_gate[0], moe_w_up[0],
                     moe_w_down[0])
    tf = min(FINAL_TILE, l)
    dest_t = dest.reshape(TOP_K, l // tf, tf).transpose(1, 0, 2)
    out = _final(x_new, yb, dest_t, wts[:TOP_K].T, gt2, final_norm_g[None, :])
    return out[None]
```

```python
import functools

import jax
import jax.numpy as jnp
import numpy as np
from jax import lax
from jax.experimental import pallas as pl
from jax.experimental.pallas import tpu as pltpu

F32 = jnp.float32
BF16 = jnp.bfloat16

GRID_W = 64
HEAD_DIM = 64
ATTN_HEADS = 8
ATTN_KV_HEADS = 2
GQA_GROUP = ATTN_HEADS // ATTN_KV_HEADS
RET_HEADS = 4
RET_QK_DIM = 64
RET_V_DIM = 128
RET_CHUNK = 256
ATTN_WIDTH = ATTN_HEADS * HEAD_DIM
KV_WIDTH = ATTN_KV_HEADS * HEAD_DIM
RET_QK_WIDTH = RET_HEADS * RET_QK_DIM
RET_WIDTH = RET_HEADS * RET_V_DIM
PROJ_SIZES = (ATTN_WIDTH, KV_WIDTH, KV_WIDTH, RET_QK_WIDTH, RET_QK_WIDTH, RET_WIDTH, RET_WIDTH)
PROJ_OFFS = tuple(int(v) for v in np.cumsum((0,) + PROJ_SIZES))
PROJ_DIM = PROJ_OFFS[-1]
ROPE_THETA = 10000.0
ROPE_AXIS_DIM = HEAD_DIM // 2
N_GROUPS = 4
EXPERTS_PER_GROUP = 8
N_EXPERTS = N_GROUPS * EXPERTS_PER_GROUP
TOP_K = 2
D_EXPERT = 512
EPS = 1e-6
LOG2E = 1.4426950408889634
ATTN_AUG_DIM = 128
MAX_UNSHIFTED_BOUND = 60.0

LANES = 128
SUBLANES = 8
VMEM_LIMIT_BYTES = 56 * 1024 * 1024

ROW_TILE = 512
INPROJ_SUBTILE = 256
ATTN_Q_TILE = 512
ATTN_CHUNK_UNROLL = 5
ATTN_MAX_KEY_CHUNK = 4096
ADA_COL_TILE = 1536
BOUND_MARGIN = 1.01
RET_ROW_TILE = 1024
RET_UNROLL = 4
MOE_BLOCK = 256
MOE_SLOTS = 3
FINAL_TILE = 512
ROUTER_ROWS = 128
EXPERT_ROW0 = 8


def _cparams(*sem):
    return pltpu.CompilerParams(dimension_semantics=sem, vmem_limit_bytes=VMEM_LIMIT_BYTES)


def _const_spec(shape):
    nd = len(shape)
    return pl.BlockSpec(shape, lambda *_: (0,) * nd)


def _store_row_tiles(ref, idx, rows):
    n, w = rows.shape[0], rows.shape[1] // SUBLANES
    for s in range(SUBLANES):
        ref[idx + (pl.ds(s, n, stride=SUBLANES), slice(None))] = rows[:, s * w:(s + 1) * w]


def _load_row_tiles(ref, idx, n):
    return jnp.concatenate([ref[idx + (pl.ds(s, n, stride=SUBLANES), slice(None))] for s in range(SUBLANES)],
                           axis=1)


def _ada_kernel(s_ref, w_ref, b_ref, o_ref):
    s = s_ref[...]
    s = s / (1.0 + jnp.exp(-s))
    o_ref[...] = jnp.dot(s, w_ref[...], preferred_element_type=F32,
                         precision=lax.Precision.HIGHEST) + b_ref[...]


def _ada(cvecs, w_ada, b_ada):
    d, n = w_ada.shape
    tn = ADA_COL_TILE
    return pl.pallas_call(
        _ada_kernel,
        grid=(n // tn,),
        in_specs=[_const_spec((SUBLANES, d)),
                  pl.BlockSpec((d, tn), lambda j: (0, j)),
                  pl.BlockSpec((1, tn), lambda j: (0, j))],
        out_specs=pl.BlockSpec((SUBLANES, tn), lambda j: (0, j)),
        out_shape=jax.ShapeDtypeStruct((SUBLANES, n), F32),
        compiler_params=_cparams("arbitrary"),
        name="ada_mod",
    )(cvecs, w_ada, b_ada.reshape(1, n))


def _rms_modulate(x, g, sh, sc):
    ms = jnp.mean(x * x, axis=-1, keepdims=True)
    return (x * lax.rsqrt(ms + EPS)) * g * (1.0 + sc) + sh


def _head_mean_sq(v, bd):
    sq = v * v
    hi = sq.astype(BF16)
    lo = (sq - hi.astype(F32)).astype(BF16)
    return (jnp.dot(hi, bd, preferred_element_type=F32) + jnp.dot(lo, bd, preferred_element_type=F32))


def _rope_chunks(v, cos, sin, even):
    outs = []
    for c in range(v.shape[1] // LANES):
        xc = v[:, c * LANES:(c + 1) * LANES]
        nxt = pltpu.roll(xc, LANES - 1, 1)
        prv = pltpu.roll(xc, 1, 1)
        outs.append(xc * cos + jnp.where(even, nxt, prv) * sin)
    return outs


def _inproj_kernel(x_ref, sh_ref, sc_ref, g_ref, w_ref, bd_ref, gq_ref, gk_ref, cos_ref, sin_ref, nb_ref,
                   qt_ref, ka_ref, va_ref, qr_ref, kr_ref, vr_ref, gr_ref):
    tm = x_ref.shape[0]
    sub = min(INPROJ_SUBTILE, tm)
    bd = bd_ref[...]
    scale = HEAD_DIM ** -0.5 * LOG2E
    rscale = RET_QK_DIM ** -0.5
    o = PROJ_OFFS
    per = LANES // HEAD_DIM
    for j0 in range(0, tm, sub):
        rows = slice(j0, j0 + sub)
        h = _rms_modulate(x_ref[rows, :], g_ref[...], sh_ref[...], sc_ref[...])
        proj = jnp.dot(h.astype(BF16), w_ref[...], preferred_element_type=F32)
        qa, ka, va = proj[:, o[0]:o[1]], proj[:, o[1]:o[2]], proj[:, o[2]:o[3]]
        qr, kr, vr, gr = proj[:, o[3]:o[4]], proj[:, o[4]:o[5]], proj[:, o[5]:o[6]], proj[:, o[6]:o[7]]
        cos, sin = cos_ref[rows, :], sin_ref[rows, :]
        even = (lax.broadcasted_iota(jnp.int32, cos.shape, 1) % 2) == 0

        qn = qa * lax.rsqrt(_head_mean_sq(qa, bd) + EPS) * gq_ref[...]
        for c, blk in enumerate(_rope_chunks(qn, cos, sin, even)):
            blk_t = (blk * scale).T
            for j in range(per):
                kvh, grp = divmod(c * per + j, GQA_GROUP)
                qt_ref[kvh, 0:HEAD_DIM, grp * tm + j0:grp * tm + j0 + sub] = (
                    blk_t[j * HEAD_DIM:(j + 1) * HEAD_DIM, :].astype(BF16))
        kn = ka * lax.rsqrt(_head_mean_sq(ka, bd[:KV_WIDTH, :KV_WIDTH]) + EPS) * gk_ref[...]
        k_rot = _rope_chunks(kn, cos, sin, even)[0].astype(BF16)
        pad_shape = (sub, ATTN_AUG_DIM - HEAD_DIM)
        k_pad = jnp.where(lax.broadcasted_iota(jnp.int32, pad_shape, 1) == 0, 1.0, 0.0).astype(BF16)
        for kvh in range(ATTN_KV_HEADS):
            c0 = kvh * ATTN_AUG_DIM
            ka_ref[rows, c0:c0 + HEAD_DIM] = k_rot[:, kvh * HEAD_DIM:(kvh + 1) * HEAD_DIM]
            ka_ref[rows, c0 + HEAD_DIM:c0 + ATTN_AUG_DIM] = k_pad
        va_ref[:, rows] = va.T.astype(BF16)
        for c, blk in enumerate(_rope_chunks(qr, cos, sin, even)):
            qr_ref[rows, c * LANES:(c + 1) * LANES] = blk.astype(BF16)
        for c, blk in enumerate(_rope_chunks(kr, cos, sin, even)):
            kr_ref[rows, c * LANES:(c + 1) * LANES] = (blk * rscale).astype(BF16)
        vr_ref[rows, :] = vr.astype(BF16)
        gr_ref[rows, :] = gr
    aug_shape = (ATTN_AUG_DIM - HEAD_DIM, GQA_GROUP * tm)
    aug_row = lax.broadcasted_iota(jnp.int32, aug_shape, 0)
    aug = jnp.where(aug_row == 0, nb_ref[0:1, 0:1], 0.0).astype(BF16)
    for kvh in range(ATTN_KV_HEADS):
        qt_ref[kvh, HEAD_DIM:ATTN_AUG_DIM, :] = aug


def _inproj(x2, sh, sc, g1, w_bf, bd, gq, gk, cos, sin, neg_bound):
    rows, d = x2.shape
    tm = min(ROW_TILE, rows)
    row = lambda w: pl.BlockSpec((tm, w), lambda i: (i, 0))
    row_specs = [row(ATTN_KV_HEADS * ATTN_AUG_DIM), pl.BlockSpec((KV_WIDTH, tm), lambda i: (0, i))]
    row_specs += [row(w) for w in PROJ_SIZES[3:]]
    row_shapes = [(rows, ATTN_KV_HEADS * ATTN_AUG_DIM), (KV_WIDTH, rows)] + [(rows, w) for w in PROJ_SIZES[3:]]
    dtypes = (BF16,) * 5 + (F32,)
    qt_shape = (ATTN_KV_HEADS, ATTN_AUG_DIM, GQA_GROUP * rows)
    return pl.pallas_call(
        _inproj_kernel,
        grid=(rows // tm,),
        in_specs=[row(d), _const_spec((1, d)), _const_spec((1, d)), _const_spec((1, d)),
                  _const_spec(w_bf.shape), _const_spec(bd.shape),
                  _const_spec((1, ATTN_WIDTH)), _const_spec((1, KV_WIDTH)),
                  row(LANES), row(LANES), _const_spec((1, LANES))],
        out_specs=[pl.BlockSpec((ATTN_KV_HEADS, ATTN_AUG_DIM, GQA_GROUP * tm), lambda i: (0, 0, i))] + row_specs,
        out_shape=[jax.ShapeDtypeStruct(qt_shape, BF16)]
        + [jax.ShapeDtypeStruct(shp, dt) for shp, dt in zip(row_shapes, dtypes)],
        compiler_params=_cparams("arbitrary"),
        name="norm_inproj",
    )(x2, sh, sc, g1, w_bf, bd, gq, gk, cos, sin, neg_bound)


def _attn_kernel(q_ref, k_ref, v_ref, o_ref, m_ref, l_ref, acc_ref, *, tk, running_max):
    m_ref[...] = jnp.full(m_ref.shape, -jnp.inf, F32)
    l_ref[...] = jnp.zeros(l_ref.shape, F32)
    acc_ref[...] = jnp.zeros(acc_ref.shape, F32)

    n_chunks = k_ref.shape[0] // tk

    def body(c, carry):
        keys = pl.ds(pl.multiple_of(c * tk, tk), tk)
        kc = k_ref[keys, :]
        vc = v_ref[:, keys]
        s = jnp.dot(kc, q_ref[0], preferred_element_type=F32)
        if running_max:
            m_old = m_ref[...]
            m_new = jnp.maximum(m_old, jnp.max(s, axis=0, keepdims=True))
            alpha = jnp.exp2(m_old - m_new)
            p = jnp.exp2(s - m_new)
            l_ref[...] = alpha * l_ref[...] + jnp.sum(p, axis=0, keepdims=True)
            acc_ref[...] = acc_ref[...] * alpha + jnp.dot(vc, p.astype(BF16), preferred_element_type=F32)
            m_ref[...] = m_new
        else:
            p = jnp.exp2(s)
            l_ref[...] += jnp.sum(p, axis=0, keepdims=True)
            acc_ref[...] += jnp.dot(vc, p.astype(BF16), preferred_element_type=F32)
        return carry

    lax.fori_loop(0, n_chunks, body, 0, unroll=1 if running_max else min(ATTN_CHUNK_UNROLL, n_chunks))
    o_ref[0] = (acc_ref[...] / l_ref[...]).astype(o_ref.dtype)


def _key_chunk(nk):
    for tk in range(min(nk, ATTN_MAX_KEY_CHUNK) // LANES * LANES, 0, -LANES):
        if nk % tk == 0:
            return tk
    raise ValueError(f"unsupported key count {nk}")


def _attention(q_t, k_c, v_t, *, running_max):
    kv, da, gl = q_t.shape
    nk = k_c.shape[0]
    d = v_t.shape[0] // kv
    nq = GQA_GROUP * min(ATTN_Q_TILE, gl // GQA_GROUP)
    return pl.pallas_call(
        functools.partial(_attn_kernel, tk=_key_chunk(nk), running_max=running_max),
        grid=(kv, gl // nq),
        in_specs=[pl.BlockSpec((1, da, nq), lambda b, i: (b, 0, i)),
                  pl.BlockSpec((nk, da), lambda b, i: (0, b)),
                  pl.BlockSpec((d, nk), lambda b, i: (b, 0))],
        out_specs=pl.BlockSpec((1, d, nq), lambda b, i: (b, 0, i)),
        out_shape=jax.ShapeDtypeStruct((kv, d, gl), BF16),
        scratch_shapes=[pltpu.VMEM((1, nq), F32), pltpu.VMEM((1, nq), F32), pltpu.VMEM((d, nq), F32)],
        compiler_params=_cparams("arbitrary", "arbitrary"),
        name="flash_attn",
    )(q_t, k_c, v_t)


def _ret_kernel(gc_ref, q_ref, k_ref, v_ref, kc_ref, vc_ref, tab_ref, *rest, reverse, final):
    if final:
        of_ref, gate_ref, gng_ref, gnb_ref, out_ref, s_ref = rest
    else:
        out_ref, s_ref = rest
    c_rows = RET_CHUNK
    n_local = q_ref.shape[0] // c_rows
    n_ctx = kc_ref.shape[0] // c_rows
    lane_head = lax.broadcasted_iota(jnp.int32, (c_rows, RET_QK_WIDTH), 1) // RET_QK_DIM

    def state_update(kch, vch):
        for h in range(RET_HEADS):
            vz = (vch[:, h * RET_V_DIM:(h + 1) * RET_V_DIM].astype(F32)
                  * tab_ref[1, h, :, 0:RET_V_DIM]).astype(BF16)
            u = lax.dot_general(kch, vz, (((0,), (0,)), ((), ())), preferred_element_type=F32)
            s_ref[h] = gc_ref[h] * s_ref[h] + u

    @pl.when(pl.program_id(0) == 0)
    def _():
        s_ref[...] = jnp.zeros(s_ref.shape, F32)
        order = range(n_ctx - 1, -1, -1) if reverse else range(n_ctx)
        for cc in order:
            state_update(kc_ref[cc * c_rows:(cc + 1) * c_rows, :], vc_ref[cc * c_rows:(cc + 1) * c_rows, :])

    def chunk(ci, carry):
        c = (n_local - 1 - ci) if reverse else ci
        rows = pl.ds(pl.multiple_of(c * c_rows, c_rows), c_rows)
        q, k, v = q_ref[rows, :], k_ref[rows, :], v_ref[rows, :]
        for h in range(RET_HEADS):
            cols = slice(h * RET_V_DIM, (h + 1) * RET_V_DIM)
            qm = jnp.where(lane_head == h, q, jnp.zeros_like(q))
            sc = lax.dot_general(qm, k, (((1,), (1,)), ((), ())), preferred_element_type=F32)
            sc = sc * tab_ref[0, h]
            o = jnp.dot(sc.astype(BF16), v[:, cols], preferred_element_type=F32)
            o = o + (jnp.dot(qm, s_ref[h].astype(BF16), preferred_element_type=F32)
                     * tab_ref[2, h, :, 0:RET_V_DIM])
            if final:
                o = o + of_ref[rows, cols]
                mu = jnp.mean(o, axis=-1, keepdims=True)
                oc = o - mu
                var = jnp.mean(oc * oc, axis=-1, keepdims=True)
                on = oc * lax.rsqrt(var + EPS)
                gate = gate_ref[rows, cols]
                y = (on * gng_ref[:, cols] + gnb_ref[:, cols]) * (gate / (1.0 + jnp.exp(-gate)))
                out_ref[rows, cols] = y.astype(out_ref.dtype)
            else:
                out_ref[rows, cols] = o
        state_update(k, v)
        return carry

    lax.fori_loop(0, n_local, chunk, 0, unroll=RET_UNROLL)


def _retention_pass(gc, qr, kr, vr, kr_c, vr_c, tabs, extra, *, reverse):
    l = qr.shape[0]
    tr = min(RET_ROW_TILE, l)
    nt = l // tr
    final = extra is not None
    idx = (lambda j, gc_: (nt - 1 - j, 0)) if reverse else (lambda j, gc_: (j, 0))
    const = lambda shape: pl.BlockSpec(shape, lambda j, gc_: (0,) * len(shape))
    in_specs = [pl.BlockSpec((tr, RET_QK_WIDTH), idx), pl.BlockSpec((tr, RET_QK_WIDTH), idx),
                pl.BlockSpec((tr, RET_WIDTH), idx),
                const(kr_c.shape), const(vr_c.shape), const(tabs.shape)]
    args = [qr, kr, vr, kr_c, vr_c, tabs]
    if final:
        of, gate, gng, gnb = extra
        in_specs += [pl.BlockSpec((tr, RET_WIDTH), idx), pl.BlockSpec((tr, RET_WIDTH), idx),
                     const(gng.shape), const(gnb.shape)]
        args += [of, gate, gng, gnb]
    return pl.pallas_call(
        functools.partial(_ret_kernel, reverse=reverse, final=final),
        grid_spec=pltpu.PrefetchScalarGridSpec(
            num_scalar_prefetch=1, grid=(nt,), in_specs=in_specs,
            out_specs=pl.BlockSpec((tr, RET_WIDTH), idx),
            scratch_shapes=[pltpu.VMEM((RET_HEADS, RET_QK_WIDTH, RET_V_DIM), F32)]),
        out_shape=jax.ShapeDtypeStruct((l, RET_WIDTH), BF16 if final else F32),
        compiler_params=_cparams("arbitrary"),
        name="retention_bwd" if reverse else "retention_fwd",
    )(gc, *args)


def _retention_tables(log_g, reverse):
    c = RET_CHUNK
    pos = jnp.arange(c, dtype=F32)
    diff = (pos[None, :] - pos[:, None]) if reverse else (pos[:, None] - pos[None, :])
    mask = (diff > 0) if reverse else (diff >= 0)
    lg = log_g[:, None, None]
    dmat = jnp.where(mask, jnp.exp(lg * jnp.where(mask, diff, 0.0)[None]), 0.0)
    zeta = jnp.exp(log_g[:, None] * (pos if reverse else (c - 1 - pos))[None, :])
    xi = jnp.exp(log_g[:, None] * ((c - pos) if reverse else (pos + 1.0))[None, :])
    ones = jnp.ones((1, 1, c), F32)
    tabs = jnp.stack([dmat, zeta[:, :, None] * ones, xi[:, :, None] * ones])
    return tabs, jnp.exp(log_g * c)


def _outproj_kernel(x_ref, a_ref, r_ref, w_ref, gt_ref, g2_ref, sh_ref, sc_ref, wr_ref, br_ref,
                    xn_ref, h2_ref, ids_ref, wts_ref):
    tm = x_ref.shape[0]
    pieces = []
    for head in range(ATTN_HEADS):
        kvh, grp = divmod(head, GQA_GROUP)
        pieces.append(a_ref[kvh, :, grp * tm:(grp + 1) * tm])
    o_t = jnp.concatenate(pieces, axis=0)
    mix = (jnp.dot(r_ref[...], w_ref[ATTN_WIDTH:, :], preferred_element_type=F32)
           + lax.dot_general(o_t, w_ref[:ATTN_WIDTH, :], (((0,), (0,)), ((), ())),
                             preferred_element_type=F32))
    x_new = x_ref[...] + gt_ref[...] * mix
    xn_ref[...] = x_new
    h2 = _rms_modulate(x_new, g2_ref[...], sh_ref[...], sc_ref[...])
    _store_row_tiles(h2_ref, (), h2)
    hi = h2.astype(BF16)
    lo = (h2 - hi.astype(F32)).astype(BF16)

    nt = (((1,), (1,)), ((), ()))
    logits = (lax.dot_general(wr_ref[0], hi, nt, preferred_element_type=F32)
              + lax.dot_general(wr_ref[0], lo, nt, preferred_element_type=F32)
              + lax.dot_general(wr_ref[1], hi, nt, preferred_element_type=F32)) + br_ref[...]
    rows = [logits[g:g + 1, :] for g in range(N_GROUPS)]
    gmax = functools.reduce(jnp.maximum, rows)
    gidx = jnp.full(gmax.shape, N_GROUPS - 1, jnp.int32)
    for g in range(N_GROUPS - 2, -1, -1):
        gidx = jnp.where(rows[g] == gmax, g, gidx)
    p_g = 1.0 / functools.reduce(jnp.add, [jnp.exp(r - gmax) for r in rows])
    sel = logits[EXPERT_ROW0 + (N_GROUPS - 1) * EXPERTS_PER_GROUP:EXPERT_ROW0 + N_GROUPS * EXPERTS_PER_GROUP, :]
    for g in range(N_GROUPS - 2, -1, -1):
        blk = logits[EXPERT_ROW0 + g * EXPERTS_PER_GROUP:EXPERT_ROW0 + (g + 1) * EXPERTS_PER_GROUP, :]
        sel = jnp.where(gidx == g, blk, sel)
    ridx = lax.broadcasted_iota(jnp.int32, sel.shape, 0)
    m1 = jnp.max(sel, axis=0, keepdims=True)
    i1 = jnp.min(jnp.where(sel == m1, ridx, EXPERTS_PER_GROUP), axis=0, keepdims=True)
    sel2 = jnp.where(ridx == i1, -jnp.inf, sel)
    m2 = jnp.max(sel2, axis=0, keepdims=True)
    i2 = jnp.min(jnp.where(sel2 == m2, ridx, EXPERTS_PER_GROUP), axis=0, keepdims=True)
    e2 = jnp.exp(m2 - m1)
    w1 = p_g / (1.0 + e2)
    w2 = w1 * e2
    out_rows = lax.broadcasted_iota(jnp.int32, ids_ref.shape, 0)
    ids_ref[...] = jnp.where(out_rows == 0, gidx * EXPERTS_PER_GROUP + i1,
                             jnp.where(out_rows == 1, gidx * EXPERTS_PER_GROUP + i2, 0))
    wts_ref[...] = jnp.where(out_rows == 0, w1, jnp.where(out_rows == 1, w2, 0.0))


def _outproj_router(x2, attn, ret, w_out_bf, gt1, g2, sh2, sc2, wr, br):
    l, d = x2.shape
    tm = min(ROW_TILE, l)
    row = lambda w: pl.BlockSpec((tm, w), lambda i: (i, 0))
    col = pl.BlockSpec((SUBLANES, tm), lambda i: (0, i))
    return pl.pallas_call(
        _outproj_kernel,
        grid=(l // tm,),
        in_specs=[row(d),
                  pl.BlockSpec((ATTN_KV_HEADS, HEAD_DIM, GQA_GROUP * tm), lambda i: (0, 0, i)),
                  row(ret.shape[1]), _const_spec(w_out_bf.shape),
                  _const_spec((1, d)), _const_spec((1, d)), _const_spec((1, d)), _const_spec((1, d)),
                  _const_spec(wr.shape), _const_spec(br.shape)],
        out_specs=[row(d), pl.BlockSpec((tm * SUBLANES, d // SUBLANES), lambda i: (i, 0)), col, col],
        out_shape=[jax.ShapeDtypeStruct((l, d), F32), jax.ShapeDtypeStruct((l * SUBLANES, d // SUBLANES), F32),
                   jax.ShapeDtypeStruct((SUBLANES, l), jnp.int32),
                   jax.ShapeDtypeStruct((SUBLANES, l), F32)],
        compiler_params=_cparams("arbitrary"),
        name="outproj_router",
    )(x2, attn, ret, w_out_bf, gt1, g2, sh2, sc2, wr, br)


def _row_copy(src_hbm, src_row, dst_ref, sem):
    return pltpu.make_async_copy(src_hbm.at[pl.ds(src_row * SUBLANES, SUBLANES), :], dst_ref, sem)


def _moe_kernel(be_ref, na_ref, tok0_ref, tok1_ref, tokn_ref, h_hbm, wg_ref, wu_ref, wd_ref, y_ref,
                wg_s, wu_s, wd_s, xg_s, sem):
    b = pl.program_id(0)
    slot = lax.rem(b, MOE_SLOTS)
    nxt = lax.rem(b + 2, MOE_SLOTS)
    n_active = na_ref[0]
    active = b < n_active
    last = be_ref.shape[0] - 1
    changed = jnp.logical_or(b == 0, be_ref[jnp.minimum(b, last)] != be_ref[jnp.clip(b - 1, 0, last)])

    def wait_rows():
        rows = pl.ds(0, MOE_BLOCK * SUBLANES)
        pltpu.make_async_copy(h_hbm.at[rows, :], xg_s.at[slot, rows, :], sem.at[slot]).wait()

    @pl.when(b == 0)
    def _():
        def body(r, carry):
            _row_copy(h_hbm, tok0_ref[0, 0, r], xg_s.at[0, pl.ds(r * SUBLANES, SUBLANES), :], sem.at[0]).start()
            _row_copy(h_hbm, tok1_ref[0, 0, r], xg_s.at[1, pl.ds(r * SUBLANES, SUBLANES), :], sem.at[1]).start()
            return carry

        lax.fori_loop(0, MOE_BLOCK, body, 0, unroll=8)

    @pl.when(jnp.logical_and(active, changed))
    def _():
        wg_s[...] = wg_ref[0].astype(BF16)
        wu_s[...] = wu_ref[0].astype(BF16)
        wd_s[...] = wd_ref[0].astype(BF16)

    @pl.when(active)
    def _():
        wait_rows()
        x = _load_row_tiles(xg_s, (slot,), MOE_BLOCK).astype(BF16)
        hg = jnp.dot(x, wg_s[...], preferred_element_type=F32)
        hu = jnp.dot(x, wu_s[...], preferred_element_type=F32)
        a = (hg / (1.0 + jnp.exp(-hg))) * hu
        _store_row_tiles(y_ref, (), jnp.dot(a.astype(BF16), wd_s[...], preferred_element_type=F32))
        for r in range(MOE_BLOCK):
            _row_copy(h_hbm, tokn_ref[0, 0, r], xg_s.at[nxt, pl.ds(r * SUBLANES, SUBLANES), :],
                      sem.at[nxt]).start(priority=r % 2)

    @pl.when(jnp.logical_or(b == n_active, b == n_active + 1))
    def _():
        wait_rows()

    @pl.when(jnp.logical_not(active))
    def _():
        y_ref[...] = jnp.zeros(y_ref.shape, y_ref.dtype)


def _moe_blocks(blk_e, n_active, buf_tok, h2, w_gate, w_up, w_down):
    ne, d, de = w_gate.shape
    tiled = (MOE_BLOCK * SUBLANES, d // SUBLANES)
    nblk = buf_tok.shape[0]
    blk = lambda b: jnp.minimum(b, nblk - 1)
    wmap = lambda b, be, na: (be[blk(b)], 0, 0)
    tokmap = lambda off: (lambda b, be, na: (blk(b + off), 0, 0))
    smem_tok = lambda off: pl.BlockSpec((1, 1, MOE_BLOCK), tokmap(off), memory_space=pltpu.SMEM)
    return pl.pallas_call(
        _moe_kernel,
        grid_spec=pltpu.PrefetchScalarGridSpec(
            num_scalar_prefetch=2, grid=(nblk + 1,),
            in_specs=[pl.BlockSpec((1, 1, MOE_BLOCK), lambda b, be, na: (0, 0, 0), memory_space=pltpu.SMEM),
                      pl.BlockSpec((1, 1, MOE_BLOCK), lambda b, be, na: (1, 0, 0), memory_space=pltpu.SMEM),
                      smem_tok(2),
                      pl.BlockSpec(memory_space=pl.ANY),
                      pl.BlockSpec((1, d, de), wmap), pl.BlockSpec((1, d, de), wmap),
                      pl.BlockSpec((1, de, d), wmap)],
            out_specs=pl.BlockSpec(tiled, lambda b, be, na: (blk(b), 0)),
            scratch_shapes=[pltpu.VMEM((d, de), BF16), pltpu.VMEM((d, de), BF16),
                            pltpu.VMEM((de, d), BF16), pltpu.VMEM((MOE_SLOTS,) + tiled, F32),
                            pltpu.SemaphoreType.DMA((MOE_SLOTS,))]),
        out_shape=jax.ShapeDtypeStruct((nblk * tiled[0], tiled[1]), F32),
        compiler_params=_cparams("arbitrary"),
        name="moe_experts",
    )(blk_e, n_active, buf_tok, buf_tok, buf_tok, h2, w_gate, w_up, w_down)


def _final_kernel(d0_ref, dn_ref, x_ref, w_ref, gt_ref, g_ref, y_hbm, o_ref, yg_s, sem):
    i = pl.program_id(0)
    slot = i % 2
    tm = x_ref.shape[0]

    def start_tile(d_ref, slot_):
        def body(r, carry):
            for k in range(TOP_K):
                _row_copy(y_hbm, d_ref[0, k, r], yg_s.at[slot_, k, pl.ds(r * SUBLANES, SUBLANES), :],
                          sem.at[slot_]).start(priority=k)
            return carry

        lax.fori_loop(0, tm, body, 0, unroll=8)

    @pl.when(i == 0)
    def _():
        start_tile(d0_ref, 0)

    @pl.when(i + 1 < pl.num_programs(0))
    def _():
        start_tile(dn_ref, 1 - slot)

    for k in range(TOP_K):
        pltpu.make_async_copy(y_hbm.at[pl.ds(0, tm * SUBLANES), :], yg_s.at[slot, k], sem.at[slot]).wait()
    w = w_ref[...]
    y = _load_row_tiles(yg_s, (slot, 0), tm) * w[:, 0:1] + _load_row_tiles(yg_s, (slot, 1), tm) * w[:, 1:2]
    x = x_ref[...] + gt_ref[...] * y
    ms = jnp.mean(x * x, axis=-1, keepdims=True)
    o_ref[...] = (x * lax.rsqrt(ms + EPS)) * g_ref[...]


def _final(x_new, yb, dest, wts, gt2, gfin):
    l, d = x_new.shape
    n, _, tm = dest.shape
    row = lambda w: pl.BlockSpec((tm, w), lambda i: (i, 0))
    return pl.pallas_call(
        _final_kernel,
        grid=(n,),
        in_specs=[pl.BlockSpec((1, TOP_K, tm), lambda i: (0, 0, 0), memory_space=pltpu.SMEM),
                  pl.BlockSpec((1, TOP_K, tm), lambda i: (jnp.minimum(i + 1, n - 1), 0, 0),
                               memory_space=pltpu.SMEM),
                  row(d), row(wts.shape[1]), _const_spec((1, d)), _const_spec((1, d)),
                  pl.BlockSpec(memory_space=pl.ANY)],
        out_specs=row(d),
        out_shape=jax.ShapeDtypeStruct((l, d), F32),
        scratch_shapes=[pltpu.VMEM((2, TOP_K, tm * SUBLANES, d // SUBLANES), F32), pltpu.SemaphoreType.DMA((2,))],
        compiler_params=_cparams("arbitrary"),
        name="combine_final_norm",
    )(dest, dest, x_new, wts, gt2, gfin, yb)


def _rope_tables(rows_count):
    lane = jnp.arange(LANES, dtype=jnp.int32) % HEAD_DIM
    pair = lane // 2
    n_freq = ROPE_AXIS_DIM // 2
    by_row = pair < n_freq
    fidx = jnp.where(by_row, pair, pair - n_freq).astype(F32)
    freq = ROPE_THETA ** (-(2.0 * fidx) / ROPE_AXIS_DIM)
    sign = jnp.where(lane % 2 == 0, -1.0, 1.0).astype(F32)
    ang_row = jnp.arange(rows_count, dtype=F32)[:, None] * freq
    ang_col = jnp.arange(GRID_W, dtype=F32)[:, None] * freq
    small = lax.optimization_barrier((jnp.cos(ang_row), jnp.cos(ang_col), jnp.sin(ang_row) * sign,
                                      jnp.sin(ang_col) * sign))
    cos_r, cos_c, sin_r, sin_c = small
    full = (rows_count * GRID_W, LANES)
    cos = jnp.where(by_row, cos_r[:, None, :], cos_c[None, :, :]).reshape(full)
    sin = jnp.where(by_row, sin_r[:, None, :], sin_c[None, :, :]).reshape(full)
    return cos, sin


def _dispatch(ids, n_tokens):
    a = TOP_K * n_tokens
    eid_f = ids[:TOP_K].reshape(a)
    tok_f = jnp.tile(jnp.arange(n_tokens, dtype=jnp.int32), TOP_K)
    onehot = (eid_f[:, None] == jnp.arange(N_EXPERTS, dtype=jnp.int32)[None, :]).astype(jnp.int32)
    incl = jnp.cumsum(onehot, axis=0)
    counts = incl[-1]
    rank = jnp.sum((incl - onehot) * onehot, axis=1)
    pcounts = (counts + MOE_BLOCK - 1) // MOE_BLOCK * MOE_BLOCK
    pends = jnp.cumsum(pcounts)
    pstarts = pends - pcounts
    dest = jnp.sum(onehot * pstarts[None, :], axis=1) + rank
    p = a + N_EXPERTS * MOE_BLOCK
    nblk = p // MOE_BLOCK
    n_fill = p - a
    fill_ends = jnp.cumsum(pcounts - counts)
    fill_idx = jnp.arange(n_fill, dtype=jnp.int32)
    fill_e = jnp.sum(fill_idx[:, None] >= fill_ends[None, :], axis=1).astype(jnp.int32)
    idx_bits = max(a, n_fill).bit_length()
    real_keys = (eid_f << (idx_bits + 1)) | jnp.arange(a, dtype=jnp.int32)
    fill_keys = (fill_e << (idx_bits + 1)) | (1 << idx_bits) | fill_idx
    keys = jnp.sort(jnp.concatenate([real_keys, fill_keys]))
    is_real = ((keys >> idx_bits) & 1) == 0
    buf_tok = jnp.where(is_real, (keys & ((1 << idx_bits) - 1)) % n_tokens, 0)
    del tok_f
    n_active = pends[-1] // MOE_BLOCK
    blk_start = jnp.arange(nblk, dtype=jnp.int32) * MOE_BLOCK
    blk_e = jnp.minimum(jnp.sum(blk_start[:, None] >= pends[None, :], axis=1), N_EXPERTS - 1)
    blk_e = jnp.where(jnp.arange(nblk) < n_active, blk_e, blk_e[n_active - 1]).astype(jnp.int32)
    return buf_tok, dest.reshape(TOP_K, n_tokens), blk_e, n_active.astype(jnp.int32).reshape(1)


def _split_bf16(w):
    hi = w.astype(BF16)
    return jnp.stack([hi, (w - hi.astype(F32)).astype(BF16)])


def kernel(x, c, ctx, c_ctx, w_ada, b_ada, norm1_g, w_in, attn_q_norm, attn_k_norm, ret_decay_fwd,
           ret_decay_bwd, ret_gn_g, ret_gn_b, w_out, norm2_g, moe_w_grp, moe_b_grp, moe_w_exp, moe_b_exp,
           moe_w_gate, moe_w_up, moe_w_down, final_norm_g):
    b, l, d = x.shape
    cl = ctx.shape[1]
    assert b == 1 and w_ada.shape[0] == 1, "single batch element, single layer"
    assert min(ROW_TILE, l) == min(ATTN_Q_TILE, l), "in-proj row tile and attention query tile share a lane order"
    assert l % min(ROW_TILE, l) == 0 and l % GRID_W == 0 and d % (SUBLANES * LANES) == 0
    assert l % RET_CHUNK == 0 and cl % RET_CHUNK == 0 and l % min(RET_ROW_TILE, l) == 0
    assert (TOP_K * l) % MOE_BLOCK == 0 and l % min(FINAL_TILE, l) == 0 and (cl + l) % LANES == 0
    x2, ctx2 = x[0], ctx[0]

    cvecs = jnp.zeros((SUBLANES, d), F32).at[0].set(c[0]).at[1].set(c_ctx)
    mod = _ada(cvecs, w_ada[0], b_ada[0])
    sh1, sc1, gt1, sh2, sc2, gt2 = [mod[0:1, i * d:(i + 1) * d] for i in range(6)]
    csh1, csc1 = mod[1:2, 0:d], mod[1:2, d:2 * d]

    w_in_bf = w_in[0].astype(BF16)
    head_of = jnp.arange(ATTN_WIDTH) // HEAD_DIM
    bd = jnp.where(head_of[:, None] == head_of[None, :], 1.0 / HEAD_DIM, 0.0).astype(BF16)
    gq = jnp.tile(attn_q_norm[0], ATTN_HEADS)[None, :]
    gk = jnp.tile(attn_k_norm[0], ATTN_KV_HEADS)[None, :]
    g1 = norm1_g[0][None, :]
    cos, sin = _rope_tables(l // GRID_W)
    bound = (HEAD_DIM ** 0.5 * LOG2E * BOUND_MARGIN) * jnp.max(jnp.abs(attn_q_norm[0])) * jnp.max(jnp.abs(attn_k_norm[0]))
    bound = bound.astype(BF16).astype(F32)
    neg_bound = jnp.full((1, LANES), -1.0, F32) * bound
    q_t, ka, va, qr, kr, vr, gr = _inproj(x2, sh1, sc1, g1, w_in_bf, bd, gq, gk, cos, sin, neg_bound)
    _, ka_c, va_c, _, kr_c, vr_c, _ = _inproj(ctx2, csh1, csc1, g1, w_in_bf, bd, gq, gk,
                                              jnp.ones((cl, LANES), F32), jnp.zeros((cl, LANES), F32), neg_bound)

    k_all = jnp.concatenate([ka_c, ka], axis=0)
    attn_args = (q_t, k_all, jnp.concatenate([va_c, va], axis=1))
    o_t = lax.cond(bound <= MAX_UNSHIFTED_BOUND,
                   functools.partial(_attention, running_max=False),
                   functools.partial(_attention, running_max=True), *attn_args)

    log_gf = jax.nn.log_sigmoid(ret_decay_fwd[0].astype(F32))
    log_gb = jax.nn.log_sigmoid(ret_decay_bwd[0].astype(F32))
    tab_f, gc_f = _retention_tables(log_gf, False)
    tab_b, gc_b = _retention_tables(log_gb, True)
    o_f = _retention_pass(gc_f, qr, kr, vr, kr_c, vr_c, tab_f, None, reverse=False)
    ret = _retention_pass(gc_b, qr, kr, vr, kr_c, vr_c, tab_b,
                          (o_f, gr, ret_gn_g[0][None, :], ret_gn_b[0][None, :]), reverse=True)

    wr = jnp.zeros((ROUTER_ROWS, d), F32)
    wr = wr.at[:N_GROUPS].set(moe_w_grp[0].T).at[EXPERT_ROW0:EXPERT_ROW0 + N_EXPERTS].set(moe_w_exp[0].T)
    br = jnp.zeros((ROUTER_ROWS, 1), F32)
    br = br.at[:N_GROUPS, 0].set(moe_b_grp[0]).at[EXPERT_ROW0:EXPERT_ROW0 + N_EXPERTS, 0].set(moe_b_exp[0])
    x_new, h2, ids, wts = _outproj_router(x2, o_t, ret, w_out[0].astype(BF16), gt1, norm2_g[0][None, :],
                                          sh2, sc2, _split_bf16(wr), br)

    buf_tok, dest, blk_e, n_active = _dispatch(ids, l)
    yb = _moe_blocks(blk_e, n_active, buf_tok.reshape(-1, 1, MOE_BLOCK), h2, moe_w_gate[0], moe_w_up[0],
                     moe_w_down[0])
    tf = min(FINAL_TILE, l)
    dest_t = dest.reshape(TOP_K, l // tf, tf).transpose(1, 0, 2)
    out = _final(x_new, yb, dest_t, wts[:TOP_K].T, gt2, final_norm_g[None, :])
    return out[None]
```
